```python
import jax, jax.numpy as jnp
from jax import lax
import numpy as np

D_MODEL = 1024
BATCH = 8
SEQ = 4096
DEPTH = 2
DEC_BATCH = 128
DEC_SEQ = 1
PAST_LEN = 16384
PAGE_SIZE = 128

N_MIXERS = 2
N_ATT_LAYERS = (DEPTH + 1) // 2
N_POOL_LAYERS = DEPTH // 2
N_DENSE_LAYERS = (DEPTH + 1) // 2
N_MOE_LAYERS = DEPTH // 2
HEAD_DIM = 64
N_HEADS = 16
N_KV_HEADS = 4
GROUP = N_HEADS // N_KV_HEADS
WINDOW = 128
BLOCK = 128
ROPE_THETA = 10000.0
POOL_WINDOWS = (2, 4, 8, 16)
N_POOL_GROUPS = len(POOL_WINDOWS)
POOL_GROUP_DIM = D_MODEL // N_POOL_GROUPS
POOL_HIST = max(POOL_WINDOWS) - 1
D_FF = 2816
N_EXPERTS = 8
TOP_K = 2
D_FF_EXPERT = 3584
PLE_DIM = 256
EPS = 1e-6
QKV_DIM = (N_HEADS + 2 * N_KV_HEADS) * HEAD_DIM

kernel_name = "hybrid_swa_sink_pool_moe_ple_step"


def rms_norm(x, g):
    xf = x.astype(jnp.float32)
    y = xf * lax.rsqrt(jnp.mean(xf * xf, axis=-1, keepdims=True) + EPS)
    return (y * g.astype(jnp.float32)).astype(x.dtype)


def rope(x, pos):
    half = HEAD_DIM // 2
    inv = 1.0 / (ROPE_THETA ** (jnp.arange(half, dtype=jnp.float32) / half))
    ang = pos.astype(jnp.float32)[:, None] * inv[None, :]
    cos = jnp.cos(ang)[:, None, :]
    sin = jnp.sin(ang)[:, None, :]
    x1 = x[..., :half].astype(jnp.float32)
    x2 = x[..., half:].astype(jnp.float32)
    out = jnp.concatenate([x1 * cos - x2 * sin, x2 * cos + x1 * sin], axis=-1)
    return out.astype(x.dtype)


def qkv_proj(h, pos, w_qkv, b_qkv):
    B, S, _ = h.shape
    qkv = h @ w_qkv + b_qkv
    q = qkv[..., :N_HEADS * HEAD_DIM].reshape(B, S, N_HEADS, HEAD_DIM)
    k = qkv[..., N_HEADS * HEAD_DIM:(N_HEADS + N_KV_HEADS) * HEAD_DIM].reshape(B, S, N_KV_HEADS, HEAD_DIM)
    v = qkv[..., (N_HEADS + N_KV_HEADS) * HEAD_DIM:].reshape(B, S, N_KV_HEADS, HEAD_DIM)
    return rope(q, pos), rope(k, pos), v


def sink_attention(q, k, v, q_pos, k_pos, sink):
    scale = HEAD_DIM ** -0.5
    s = jnp.einsum("bntkgd,bnlkd->bnkgtl", q, k).astype(jnp.float32) * scale
    diff = q_pos[:, :, None] - k_pos[:, None, :]
    mask = (diff >= 0) & (diff < WINDOW) & (k_pos[:, None, :] >= 0)
    s = jnp.where(mask[None, :, None, None, :, :], s, -jnp.inf)
    sk = sink.astype(jnp.float32).reshape(N_KV_HEADS, GROUP)[None, None, :, :, None, None]
    m = jnp.maximum(jnp.max(s, axis=-1, keepdims=True), sk)
    p = jnp.exp(s - m)
    denom = jnp.sum(p, axis=-1, keepdims=True) + jnp.exp(sk - m)
    return jnp.einsum("bnkgtl,bnlkd->bntkgd", (p / denom).astype(v.dtype), v)


def banded_window_attention(q, k, v, pos, sink):
    B, S = q.shape[:2]
    nb = S // BLOCK
    qb = q.reshape(B, nb, BLOCK, N_KV_HEADS, GROUP, HEAD_DIM)
    kb = k.reshape(B, nb, BLOCK, N_KV_HEADS, HEAD_DIM)
    vb = v.reshape(B, nb, BLOCK, N_KV_HEADS, HEAD_DIM)
    kk = jnp.concatenate([jnp.concatenate([jnp.zeros_like(kb[:, :1]), kb[:, :-1]], axis=1), kb], axis=2)
    vv = jnp.concatenate([jnp.concatenate([jnp.zeros_like(vb[:, :1]), vb[:, :-1]], axis=1), vb], axis=2)
    pos_b = pos.reshape(nb, BLOCK)
    k_pos = jnp.concatenate([pos_b - BLOCK, pos_b], axis=1)
    o = sink_attention(qb, kk, vv, pos_b, k_pos, sink)
    return o.reshape(B, S, N_HEADS * HEAD_DIM)


def cached_window_attention(q, k, v, k_hist, v_hist, sink):
    Bd, S = q.shape[:2]
    n_hist = k_hist.shape[1]
    kk = jnp.concatenate([k_hist, k], axis=1)
    vv = jnp.concatenate([v_hist, v], axis=1)
    q_pos = (PAST_LEN + jnp.arange(S, dtype=jnp.int32))[None]
    k_pos = (PAST_LEN - n_hist + jnp.arange(n_hist + S, dtype=jnp.int32))[None]
    o = sink_attention(q.reshape(Bd, 1, S, N_KV_HEADS, GROUP, HEAD_DIM), kk[:, None], vv[:, None], q_pos, k_pos, sink)
    return o.reshape(Bd, S, N_HEADS * HEAD_DIM), kk[:, -n_hist:], vv[:, -n_hist:]


def pool_mixer(h, hist, pos, w_pool, pool_scale):
    B, S, D = h.shape
    xp = jnp.concatenate([hist, h], axis=1)
    c = jnp.cumsum(xp.astype(jnp.float32), axis=1)
    c = jnp.pad(c, ((0, 0), (1, 0), (0, 0)))
    end = c[:, POOL_HIST + 1:]
    groups = []
    for g, w in enumerate(POOL_WINDOWS):
        sl = slice(g * POOL_GROUP_DIM, (g + 1) * POOL_GROUP_DIM)
        start = c[:, POOL_HIST + 1 - w:POOL_HIST + 1 - w + S, sl]
        cnt = jnp.minimum(w, pos + 1).astype(jnp.float32)[None, :, None]
        groups.append((end[..., sl] - start) / cnt)
    pooled = (jnp.concatenate(groups, axis=-1) - h.astype(jnp.float32)).astype(h.dtype)
    mixed = jnp.einsum("bsgc,gcd->bsgd", pooled.reshape(B, S, N_POOL_GROUPS, POOL_GROUP_DIM), w_pool)
    out = mixed.reshape(B, S, D) * pool_scale
    return out, xp[:, -POOL_HIST:]


def swiglu(h, w_gate, w_up, w_down):
    return (jax.nn.silu(h @ w_gate) * (h @ w_up)) @ w_down


def moe_ffn(h, w_router, b_router, w_gate, w_up, w_down):
    logits = (h @ w_router).astype(jnp.float32) + b_router.astype(jnp.float32)
    top_v, top_i = lax.top_k(logits, TOP_K)
    top_w = jax.nn.softmax(top_v, axis=-1)
    gates = jnp.sum(jax.nn.one_hot(top_i, N_EXPERTS, dtype=jnp.float32) * top_w[..., None], axis=-2)
    gates = gates.astype(h.dtype)
    out = jnp.zeros_like(h)
    for e in range(N_EXPERTS):
        out = out + gates[..., e:e + 1] * swiglu(h, w_gate[e], w_up[e], w_down[e])
    return out


def trunk(x, p, pos, hist_k, hist_v, hist_pool, prm, is_prompt):
    new_k, new_v, new_pool = [], [], []
    h = x
    for i in range(DEPTH):
        a = i // N_MIXERS
        hn = rms_norm(h, prm["norm_mix"][i])
        if i % N_MIXERS == 0:
            q, k, v = qkv_proj(hn, pos, prm["w_qkv"][a], prm["b_qkv"][a])
            if is_prompt:
                o = banded_window_attention(q, k, v, pos, prm["sinks"][a])
                k_rows, v_rows = k[:, -WINDOW:], v[:, -WINDOW:]
            else:
                o, k_rows, v_rows = cached_window_attention(q, k, v, hist_k[a], hist_v[a], prm["sinks"][a])
            mix = o @ prm["w_o"][a] + prm["b_o"][a]
            new_k.append(k_rows)
            new_v.append(v_rows)
        else:
            hist = jnp.zeros((h.shape[0], POOL_HIST, D_MODEL), hn.dtype) if is_prompt else hist_pool[a]
            mix, rows = pool_mixer(hn, hist, pos, prm["w_pool"][a], prm["pool_scale"][a])
            new_pool.append(rows)
        h = h + mix
        hn = rms_norm(h, prm["norm_ffn"][i])
        c = i // 2
        if i % 2 == 0:
            f = swiglu(hn, prm["w_ff_gate"][c], prm["w_ff_up"][c], prm["w_ff_down"][c])
        else:
            f = moe_ffn(hn, prm["w_router"][c], prm["b_router"][c], prm["w_exp_gate"][c], prm["w_exp_up"][c], prm["w_exp_down"][c])
        h = h + f
        gate = jax.nn.sigmoid(rms_norm(h, prm["norm_ple"][i]) @ prm["w_ple_gate"][i] + prm["b_ple_gate"][i])
        h = h + gate * (p[i] @ prm["w_ple"][i])
    y = rms_norm(h, prm["norm_final"])
    return y, jnp.stack(new_k), jnp.stack(new_v), jnp.stack(new_pool)


def setup_inputs(seed: int = 0) -> dict:
    key = jax.random.key(seed)
    keys = jax.random.split(key, 29)

    def nrm(i, shape, scale):
        return jax.random.normal(keys[i], shape, jnp.float32) * scale

    n_win = min(WINDOW, PAST_LEN)
    return {
        "x_prompt": nrm(0, (BATCH, SEQ, D_MODEL), 1.0),
        "x_sample": nrm(1, (DEC_BATCH, DEC_SEQ, D_MODEL), 1.0),
        "cache_k": nrm(2, (N_ATT_LAYERS, DEC_BATCH, n_win, N_KV_HEADS, HEAD_DIM), 1.0),
        "cache_v": nrm(3, (N_ATT_LAYERS, DEC_BATCH, n_win, N_KV_HEADS, HEAD_DIM), 1.0),
        "state_pool": nrm(4, (N_POOL_LAYERS, DEC_BATCH, POOL_HIST, D_MODEL), 1.0),
        "p_prompt": nrm(5, (DEPTH, BATCH, SEQ, PLE_DIM), 1.0),
        "p_sample": nrm(6, (DEPTH, DEC_BATCH, DEC_SEQ, PLE_DIM), 1.0),
        "norm_mix": 1.0 + nrm(7, (DEPTH, D_MODEL), 0.05),
        "norm_ffn": 1.0 + nrm(8, (DEPTH, D_MODEL), 0.05),
        "norm_ple": 1.0 + nrm(9, (DEPTH, D_MODEL), 0.05),
        "norm_final": 1.0 + nrm(10, (D_MODEL,), 0.05),
        "w_qkv": nrm(11, (N_ATT_LAYERS, D_MODEL, QKV_DIM), D_MODEL ** -0.5),
        "b_qkv": nrm(12, (N_ATT_LAYERS, QKV_DIM), 0.02),
        "w_o": nrm(13, (N_ATT_LAYERS, N_HEADS * HEAD_DIM, D_MODEL), (N_HEADS * HEAD_DIM) ** -0.5),
        "b_o": nrm(14, (N_ATT_LAYERS, D_MODEL), 0.02),
        "sinks": nrm(15, (N_ATT_LAYERS, N_HEADS), 0.5),
        "w_pool": nrm(16, (N_POOL_LAYERS, N_POOL_GROUPS, POOL_GROUP_DIM, POOL_GROUP_DIM), POOL_GROUP_DIM ** -0.5),
        "pool_scale": 0.5 + nrm(17, (N_POOL_LAYERS, D_MODEL), 0.05),
        "w_ff_gate": nrm(18, (N_DENSE_LAYERS, D_MODEL, D_FF), D_MODEL ** -0.5),
        "w_ff_up": nrm(19, (N_DENSE_LAYERS, D_MODEL, D_FF), D_MODEL ** -0.5),
        "w_ff_down": nrm(20, (N_DENSE_LAYERS, D_FF, D_MODEL), D_FF ** -0.5),
        "w_router": nrm(21, (N_MOE_LAYERS, D_MODEL, N_EXPERTS), D_MODEL ** -0.5),
        "b_router": nrm(22, (N_MOE_LAYERS, N_EXPERTS), 0.01),
        "w_exp_gate": nrm(23, (N_MOE_LAYERS, N_EXPERTS, D_MODEL, D_FF_EXPERT), D_MODEL ** -0.5),
        "w_exp_up": nrm(24, (N_MOE_LAYERS, N_EXPERTS, D_MODEL, D_FF_EXPERT), D_MODEL ** -0.5),
        "w_exp_down": nrm(25, (N_MOE_LAYERS, N_EXPERTS, D_FF_EXPERT, D_MODEL), D_FF_EXPERT ** -0.5),
        "w_ple": nrm(26, (DEPTH, PLE_DIM, D_MODEL), PLE_DIM ** -0.5),
        "w_ple_gate": nrm(27, (DEPTH, D_MODEL, D_MODEL), D_MODEL ** -0.5),
        "b_ple_gate": nrm(28, (DEPTH, D_MODEL), 0.02),
    }


def reference(x_prompt, x_sample, cache_k, cache_v, state_pool, p_prompt, p_sample,
              norm_mix, norm_ffn, norm_ple, norm_final, w_qkv, b_qkv, w_o, b_o, sinks,
              w_pool, pool_scale, w_ff_gate, w_ff_up, w_ff_down, w_router, b_router,
              w_exp_gate, w_exp_up, w_exp_down, w_ple, w_ple_gate, b_ple_gate):
    prm = dict(norm_mix=norm_mix, norm_ffn=norm_ffn, norm_ple=norm_ple, norm_final=norm_final,
               w_qkv=w_qkv, b_qkv=b_qkv, w_o=w_o, b_o=b_o, sinks=sinks,
               w_pool=w_pool, pool_scale=pool_scale,
               w_ff_gate=w_ff_gate, w_ff_up=w_ff_up, w_ff_down=w_ff_down,
               w_router=w_router, b_router=b_router,
               w_exp_gate=w_exp_gate, w_exp_up=w_exp_up, w_exp_down=w_exp_down,
               w_ple=w_ple, w_ple_gate=w_ple_gate, b_ple_gate=b_ple_gate)
    pos_prompt = jnp.arange(x_prompt.shape[1], dtype=jnp.int32)
    pos_sample = PAST_LEN + jnp.arange(x_sample.shape[1], dtype=jnp.int32)
    y_prompt, new_k_prompt, new_v_prompt, new_pool_prompt = trunk(
        x_prompt, p_prompt, pos_prompt, None, None, None, prm, True)
    y_sample, new_k_sample, new_v_sample, new_pool_sample = trunk(
        x_sample, p_sample, pos_sample, cache_k, cache_v, state_pool, prm, False)
    return (y_prompt, y_sample, new_k_prompt, new_v_prompt, new_pool_prompt, new_k_sample, new_v_sample, new_pool_sample)
```

```python
import functools

import jax
import jax.numpy as jnp
from jax import lax
from jax.experimental import pallas as pl
from jax.experimental.pallas import tpu as pltpu

F32 = jnp.float32
BF16 = jnp.bfloat16

D_MODEL = 1024
HEAD_DIM = 64
N_HEADS = 16
N_KV_HEADS = 4
GROUP = N_HEADS // N_KV_HEADS
KV_DIM = N_KV_HEADS * HEAD_DIM
QKV_DIM = (N_HEADS + 2 * N_KV_HEADS) * HEAD_DIM
WINDOW = 128
ROPE_THETA = 10000.0
PAST_LEN = 16384
POOL_WINDOWS = (2, 4, 8, 16)
POOL_GROUP_DIM = D_MODEL // len(POOL_WINDOWS)
POOL_HIST = max(POOL_WINDOWS) - 1
POOL_HALO = 16
D_FF = 2816
N_EXPERTS = 8
D_FF_EXPERT = 3584
PLE_DIM = 256
EPS = 1e-6

LANES = 128
SUBLANES = 8
VMEM_LIMIT = 56 * 1024 * 1024

ATTN_TQ = 512
FF_CHUNK = 1408
EXPERT_FF_CHUNK = 512
SEG_PIECES = (256, 128, 64, 32, 16, 8)
SLACK_CHUNK = 256


def _cparams(sem):
    return pltpu.CompilerParams(dimension_semantics=sem, vmem_limit_bytes=VMEM_LIMIT)


def _const_spec(shape):
    nd = len(shape)
    return pl.BlockSpec(shape, lambda *_: (0,) * nd, pipeline_mode=pl.Buffered(1))


def _rms(x, g):
    return x * lax.rsqrt(jnp.mean(x * x, axis=-1, keepdims=True) + EPS) * g


def _mm(a, b):
    return jnp.dot(a, b, preferred_element_type=F32)


def _silu(x):
    return x * jax.nn.sigmoid(x)


def _qkv_kernel(x_ref, g_ref, w_ref, b_ref, cos_ref, sin_ref, q_ref, k_ref, v_ref):
    hn = _rms(x_ref[...], g_ref[...])
    qkv = _mm(hn.astype(BF16), w_ref[...]) + b_ref[...]
    cos = cos_ref[...]
    sin = sin_ref[...]
    lane = lax.broadcasted_iota(jnp.int32, cos.shape, 1)
    first_half = (lane % HEAD_DIM) < (HEAD_DIM // 2)

    def rope(xb):
        partner = jnp.where(first_half, pltpu.roll(xb, LANES - HEAD_DIM // 2, 1),
                            pltpu.roll(xb, HEAD_DIM // 2, 1))
        return xb * cos + partner * sin

    scale = HEAD_DIM ** -0.5
    for c in range(N_HEADS * HEAD_DIM // LANES):
        sl = slice(c * LANES, (c + 1) * LANES)
        q_ref[:, sl] = (rope(qkv[:, sl]) * scale).astype(BF16)
    for c in range(KV_DIM // LANES):
        sl = slice(c * LANES, (c + 1) * LANES)
        k_ref[:, sl] = rope(qkv[:, N_HEADS * HEAD_DIM + c * LANES:N_HEADS * HEAD_DIM + (c + 1) * LANES])
    v_ref[...] = qkv[:, N_HEADS * HEAD_DIM + KV_DIM:]


def _qkv_rope(x, g, w, b, cos, sin, tm):
    t = x.shape[0]
    n_pos_tiles = cos.shape[0] // tm
    row = lambda i: (i, 0)
    return pl.pallas_call(
        _qkv_kernel,
        grid=(t // tm,),
        in_specs=[
            pl.BlockSpec((tm, D_MODEL), row),
            _const_spec((1, D_MODEL)),
            _const_spec((D_MODEL, QKV_DIM)),
            _const_spec((1, QKV_DIM)),
            pl.BlockSpec((tm, LANES), lambda i: (i % n_pos_tiles, 0)),
            pl.BlockSpec((tm, LANES), lambda i: (i % n_pos_tiles, 0)),
        ],
        out_specs=[
            pl.BlockSpec((tm, N_HEADS * HEAD_DIM), row),
            pl.BlockSpec((tm, KV_DIM), row),
            pl.BlockSpec((tm, KV_DIM), row),
        ],
        out_shape=[
            jax.ShapeDtypeStruct((t, N_HEADS * HEAD_DIM), BF16),
            jax.ShapeDtypeStruct((t, KV_DIM), F32),
            jax.ShapeDtypeStruct((t, KV_DIM), F32),
        ],
        compiler_params=_cparams(("parallel",)),
        name="qkv_rope",
    )(x, g, w, b, cos, sin)


def _softmax_sink_pv(s, sink, vw):
    m = jnp.maximum(jnp.max(s, axis=-1, keepdims=True), sink)
    p = jnp.exp(s - m)
    denom = jnp.sum(p, axis=-1, keepdims=True) + jnp.exp(sink - m)
    return _mm((p / denom).astype(BF16), vw)


def _attn_kernel(sink_ref, q_ref, kc_ref, kp_ref, vc_ref, vp_ref, o_ref, kbuf, vbuf):
    n = pl.program_id(1)
    kbuf[0:WINDOW] = kp_ref[...].astype(BF16)
    kbuf[WINDOW:] = kc_ref[...].astype(BF16)
    vbuf[0:WINDOW] = vp_ref[...].astype(BF16)
    vbuf[WINDOW:] = vc_ref[...].astype(BF16)
    row = lax.broadcasted_iota(jnp.int32, (WINDOW, 2 * WINDOW), 0)
    col = lax.broadcasted_iota(jnp.int32, (WINDOW, 2 * WINDOW), 1)
    mask_cur = (col >= WINDOW) & (col - WINDOW <= row)
    mask_prev = (col < WINDOW) & (col > row)
    for sb in range(ATTN_TQ // WINDOW):
        first_col = jnp.where(n * (ATTN_TQ // WINDOW) + sb > 0, 0, WINDOW)
        mask = mask_cur | (mask_prev & (col >= first_col))
        rows = slice(sb * WINDOW, (sb + 1) * WINDOW)
        for kv in range(N_KV_HEADS):
            cols = slice(kv * HEAD_DIM, (kv + 1) * HEAD_DIM)
            kw = kbuf[sb * WINDOW:(sb + 2) * WINDOW, cols]
            vw = vbuf[sb * WINDOW:(sb + 2) * WINDOW, cols]
            for g in range(GROUP):
                h = kv * GROUP + g
                hcols = slice(h * HEAD_DIM, (h + 1) * HEAD_DIM)
                s = lax.dot_general(q_ref[rows, hcols], kw, (((1,), (1,)), ((), ())),
                                    preferred_element_type=F32)
                s = jnp.where(mask, s, -jnp.inf)
                o_ref[rows, hcols] = _softmax_sink_pv(s, sink_ref[h], vw).astype(BF16)


def _prompt_attention(q, k, v, sinks, batch, seq):
    nq = seq // ATTN_TQ
    per = ATTN_TQ // WINDOW
    cur = lambda b, n: (b * nq + n, 0)
    prev = lambda b, n: (jnp.maximum(b * nq * per + n * per - 1, b * nq * per), 0)
    return pl.pallas_call(
        _attn_kernel,
        grid=(batch, nq),
        in_specs=[
            pl.BlockSpec(memory_space=pltpu.SMEM),
            pl.BlockSpec((ATTN_TQ, N_HEADS * HEAD_DIM), cur),
            pl.BlockSpec((ATTN_TQ, KV_DIM), cur),
            pl.BlockSpec((WINDOW, KV_DIM), prev),
            pl.BlockSpec((ATTN_TQ, KV_DIM), cur),
            pl.BlockSpec((WINDOW, KV_DIM), prev),
        ],
        out_specs=pl.BlockSpec((ATTN_TQ, N_HEADS * HEAD_DIM), cur),
        out_shape=jax.ShapeDtypeStruct(q.shape, BF16),
        scratch_shapes=[pltpu.VMEM((ATTN_TQ + WINDOW, KV_DIM), BF16),
                        pltpu.VMEM((ATTN_TQ + WINDOW, KV_DIM), BF16)],
        compiler_params=_cparams(("parallel", "parallel")),
        name="prompt_attention",
    )(sinks, q, k, k, v, v)


def _attn_sample_kernel(sink_ref, q_ref, k_ref, v_ref, o_ref):
    s = jnp.einsum("bhc,blc->bhl", q_ref[...], k_ref[...].astype(BF16),
                   preferred_element_type=F32)
    sink = sink_ref[...]
    m = jnp.maximum(jnp.max(s, axis=-1, keepdims=True), sink)
    p = jnp.exp(s - m)
    denom = jnp.sum(p, axis=-1, keepdims=True) + jnp.exp(sink - m)
    o_ref[...] = jnp.einsum("bhl,blc->bhc", (p / denom).astype(BF16), v_ref[...].astype(BF16),
                            preferred_element_type=F32)


def _sample_attention(q_blk, k_win, v_win, sinks, bt=16):
    b = q_blk.shape[0]
    blk = lambda i: (i, 0, 0)
    return pl.pallas_call(
        _attn_sample_kernel,
        grid=(b // bt,),
        in_specs=[
            _const_spec((1, N_HEADS, 1)),
            pl.BlockSpec((bt, N_HEADS, KV_DIM), blk),
            pl.BlockSpec((bt, WINDOW, KV_DIM), blk),
            pl.BlockSpec((bt, WINDOW, KV_DIM), blk),
        ],
        out_specs=pl.BlockSpec((bt, N_HEADS, KV_DIM), blk),
        out_shape=jax.ShapeDtypeStruct((b, N_HEADS, KV_DIM), F32),
        compiler_params=_cparams(("parallel",)),
        name="sample_attention",
    )(sinks.reshape(1, N_HEADS, 1), q_blk, k_win, v_win)


def _ple(h, p, g, wg, bg, wp):
    gate = jax.nn.sigmoid(_mm(_rms(h, g).astype(BF16), wg) + bg)
    return h + gate * _mm(p.astype(BF16), wp)


def _layer0_tail_kernel(x_ref, o_ref, p_ref, wo_ref, bo_ref, gf_ref, wg_ref, wu_ref, wd_ref,
                        gp_ref, wpg_ref, bpg_ref, wp_ref, gm_ref, h_ref, hn_ref):
    h = x_ref[...] + _mm(o_ref[...], wo_ref[...]) + bo_ref[...]
    hn = _rms(h, gf_ref[...]).astype(BF16)
    f = None
    for c in range(0, D_FF, FF_CHUNK):
        a = (_silu(_mm(hn, wg_ref[:, c:c + FF_CHUNK])) * _mm(hn, wu_ref[:, c:c + FF_CHUNK])).astype(BF16)
        part = _mm(a, wd_ref[c:c + FF_CHUNK, :])
        f = part if f is None else f + part
    h = h + f
    h = _ple(h, p_ref[...], gp_ref[...], wpg_ref[...], bpg_ref[...], wp_ref[...])
    h_ref[...] = h
    hn_ref[...] = _rms(h, gm_ref[...])


def _layer0_tail(x, o, p, wo, bo, gf, wg, wu, wd, gp, wpg, bpg, wp, gm, tm):
    t = x.shape[0]
    row = lambda i: (i, 0)
    consts = [wo, bo, gf, wg, wu, wd, gp, wpg, bpg, wp, gm]
    return pl.pallas_call(
        _layer0_tail_kernel,
        grid=(t // tm,),
        in_specs=[pl.BlockSpec((tm, D_MODEL), row), pl.BlockSpec((tm, D_MODEL), row),
                  pl.BlockSpec((tm, PLE_DIM), row)] + [_const_spec(c.shape) for c in consts],
        out_specs=[pl.BlockSpec((tm, D_MODEL), row), pl.BlockSpec((tm, D_MODEL), row)],
        out_shape=[jax.ShapeDtypeStruct((t, D_MODEL), F32), jax.ShapeDtypeStruct((t, D_MODEL), F32)],
        compiler_params=_cparams(("parallel",)),
        name="layer0_tail",
    )(x, o, p, *consts)


def _pool_project_route(h, hn, pooled_sum_inv, wpool_ref, ps_ref, gf_ref, wr_ref, br_ref,
                        h_out, hn_out, gates_out):
    mixed = []
    for g in range(len(POOL_WINDOWS)):
        cols = slice(g * POOL_GROUP_DIM, (g + 1) * POOL_GROUP_DIM)
        pooled = pooled_sum_inv[g] - hn[:, cols]
        mixed.append(_mm(pooled.astype(BF16), wpool_ref[g]))
    h = h + jnp.concatenate(mixed, axis=-1) * ps_ref[...]
    h_out[...] = h
    hn2 = _rms(h, gf_ref[...])
    hn_out[...] = hn2.astype(BF16)
    logits = jnp.dot(hn2, wr_ref[...], precision=lax.Precision.HIGHEST,
                     preferred_element_type=F32) + br_ref[...]
    idx = lax.broadcasted_iota(jnp.int32, logits.shape, 1)
    m1 = jnp.max(logits, axis=-1, keepdims=True)
    i1 = jnp.min(jnp.where(logits == m1, idx, N_EXPERTS), axis=-1, keepdims=True)
    rest = jnp.where(idx == i1, -jnp.inf, logits)
    m2 = jnp.max(rest, axis=-1, keepdims=True)
    i2 = jnp.min(jnp.where(rest == m2, idx, N_EXPERTS), axis=-1, keepdims=True)
    e = jnp.exp(m2 - m1)
    w1 = 1.0 / (1.0 + e)
    w2 = e / (1.0 + e)
    gates_out[...] = jnp.where(idx == i1, w1, jnp.where(idx == i2, w2, 0.0))


def _pool_prompt_kernel(h_ref, hn_ref, wpool_ref, ps_ref, gf_ref, wr_ref, br_ref,
                        h_out, hn_out, gates_out, carry, buf):
    n = pl.program_id(1)
    tm = h_ref.shape[0]

    @pl.when(n == 0)
    def _():
        carry[...] = jnp.zeros_like(carry)

    hn = hn_ref[...]
    buf[0:POOL_HALO] = carry[...]
    buf[POOL_HALO:] = hn
    carry[...] = hn[tm - POOL_HALO:]
    pos = n * tm + lax.broadcasted_iota(jnp.int32, (tm, 1), 0)
    means = []
    for g, w in enumerate(POOL_WINDOWS):
        s = buf[:, g * POOL_GROUP_DIM:(g + 1) * POOL_GROUP_DIM]
        shift = 1
        while shift < w:
            s = s + pltpu.roll(s, shift, 0)
            shift *= 2
        cnt = jnp.minimum(w, pos + 1).astype(F32)
        means.append(s[POOL_HALO:] * (1.0 / cnt))
    _pool_project_route(h_ref[...], hn, means, wpool_ref, ps_ref, gf_ref, wr_ref, br_ref,
                        h_out, hn_out, gates_out)


def _pool_prompt(h, hn, wpool, ps, gf, wr, br, batch, seq, tm):
    t = h.shape[0]
    ns = seq // tm
    row = lambda b, n: (b * ns + n, 0)
    consts = [wpool, ps, gf, wr, br]
    return pl.pallas_call(
        _pool_prompt_kernel,
        grid=(batch, ns),
        in_specs=[pl.BlockSpec((tm, D_MODEL), row), pl.BlockSpec((tm, D_MODEL), row)]
        + [_const_spec(c.shape) for c in consts],
        out_specs=[pl.BlockSpec((tm, D_MODEL), row), pl.BlockSpec((tm, D_MODEL), row),
                   pl.BlockSpec((tm, N_EXPERTS), row)],
        out_shape=[jax.ShapeDtypeStruct((t, D_MODEL), F32), jax.ShapeDtypeStruct((t, D_MODEL), BF16),
                   jax.ShapeDtypeStruct((t, N_EXPERTS), F32)],
        scratch_shapes=[pltpu.VMEM((POOL_HALO, D_MODEL), F32),
                        pltpu.VMEM((tm + POOL_HALO, D_MODEL), F32)],
        compiler_params=_cparams(("parallel", "arbitrary")),
        name="pool_prompt",
    )(h, hn, *consts)


def _pool_sample_kernel(h_ref, hn_ref, hist_ref, wpool_ref, ps_ref, gf_ref, wr_ref, br_ref,
                        h_out, hn_out, gates_out):
    hn = hn_ref[...]
    means = []
    for g, w in enumerate(POOL_WINDOWS):
        cols = slice(g * POOL_GROUP_DIM, (g + 1) * POOL_GROUP_DIM)
        s = hn[:, cols]
        for j in range(1, w):
            s = s + hist_ref[POOL_HIST - j, :, cols]
        means.append(s * (1.0 / min(w, PAST_LEN + 1)))
    _pool_project_route(h_ref[...], hn, means, wpool_ref, ps_ref, gf_ref, wr_ref, br_ref,
                        h_out, hn_out, gates_out)


def _pool_sample(h, hn, hist_t, wpool, ps, gf, wr, br):
    t = h.shape[0]
    args = [h, hn, hist_t, wpool, ps, gf, wr, br]
    return pl.pallas_call(
        _pool_sample_kernel,
        grid=(1,),
        in_specs=[_const_spec(a.shape) for a in args],
        out_specs=[_const_spec((t, D_MODEL)), _const_spec((t, D_MODEL)), _const_spec((t, N_EXPERTS))],
        out_shape=[jax.ShapeDtypeStruct((t, D_MODEL), F32), jax.ShapeDtypeStruct((t, D_MODEL), BF16),
                   jax.ShapeDtypeStruct((t, N_EXPERTS), F32)],
        compiler_params=_cparams(("arbitrary",)),
        name="pool_sample",
    )(*args)


def _segment_copies(meta_ref, i, src_of, dst_hbm, sem):
    out = []
    for e in range(N_EXPERTS):
        dst = meta_ref[0, i * N_EXPERTS + e]
        cnt = meta_ref[1, i * N_EXPERTS + e]
        for piece in SEG_PIECES:
            if piece > src_of(e).shape[0]:
                continue
            start = pl.multiple_of(cnt & ~(2 * piece - 1), SUBLANES)
            copy = pltpu.make_async_copy(
                src_of(e).at[pl.ds(start, piece)],
                dst_hbm.at[pl.ds(pl.multiple_of(dst + start, SUBLANES), piece)],
                sem.at[e])
            out.append(((cnt & piece) != 0, copy))
    return out


def _dispatch_kernel(meta_ref, tail_ref, hn_ref, lrt_ref, xs_ref, seg, zeros, sem, zsem):
    i = pl.program_id(0)
    tile = hn_ref.shape[0]
    hn = hn_ref[...]
    slot = lax.broadcasted_iota(jnp.int32, (tile, tile), 0)
    for e in range(N_EXPERTS):
        onehot = (slot == lrt_ref[e:e + 1, :]).astype(BF16)
        seg[e] = _mm(onehot, hn)
    copies = _segment_copies(meta_ref, i, lambda e: seg.at[e], xs_ref, sem)
    for pred, copy in copies:
        @pl.when(pred)
        def _(copy=copy):
            copy.start()
    for pred, copy in copies:
        @pl.when(pred)
        def _(copy=copy):
            copy.wait()

    @pl.when(i == pl.num_programs(0) - 1)
    def _():
        zeros[...] = jnp.zeros_like(zeros)
        tails = []
        for e in range(N_EXPERTS):
            start = tail_ref[0, e]
            cnt = tail_ref[1, e]
            for piece in (512,) + SEG_PIECES:
                off = pl.multiple_of(cnt & ~(2 * piece - 1), SUBLANES)
                copy = pltpu.make_async_copy(
                    zeros.at[pl.ds(0, piece)],
                    xs_ref.at[pl.ds(pl.multiple_of(start + off, SUBLANES), piece)],
                    zsem.at[e])
                tails.append(((cnt & piece) != 0, copy))
        for pred, copy in tails:
            @pl.when(pred)
            def _(copy=copy):
                copy.start()
        for pred, copy in tails:
            @pl.when(pred)
            def _(copy=copy):
                copy.wait()

        def slack_copy(c):
            row = pl.multiple_of(tail_ref[0, N_EXPERTS] + c * SLACK_CHUNK, SLACK_CHUNK)
            return pltpu.make_async_copy(zeros.at[pl.ds(0, SLACK_CHUNK)],
                                         xs_ref.at[pl.ds(row, SLACK_CHUNK)], zsem.at[0])

        n_slack = tail_ref[1, N_EXPERTS] // SLACK_CHUNK
        lax.fori_loop(0, n_slack, lambda c, _: slack_copy(c).start(), None)
        lax.fori_loop(0, n_slack, lambda c, _: slack_copy(c).wait(), None)


def _dispatch(hn, lr_t, meta, tail, rows, tile):
    t = hn.shape[0]
    return pl.pallas_call(
        _dispatch_kernel,
        grid_spec=pltpu.PrefetchScalarGridSpec(
            num_scalar_prefetch=2,
            grid=(t // tile,),
            in_specs=[pl.BlockSpec((tile, D_MODEL), lambda i, *_: (i, 0)),
                      pl.BlockSpec((N_EXPERTS, tile), lambda i, *_: (0, i))],
            out_specs=pl.BlockSpec(memory_space=pl.ANY),
            scratch_shapes=[pltpu.VMEM((N_EXPERTS, tile, D_MODEL), F32),
                            pltpu.VMEM((512, D_MODEL), F32),
                            pltpu.SemaphoreType.DMA((N_EXPERTS,)),
                            pltpu.SemaphoreType.DMA((N_EXPERTS,))]),
        out_shape=jax.ShapeDtypeStruct((rows, D_MODEL), F32),
        compiler_params=_cparams(("arbitrary",)),
        name="expert_dispatch",
    )(meta, tail, hn, lr_t)


def _expert_kernel(te_ref, tr_ref, x_ref, wg_ref, wu_ref, wd_ref, y_ref, acc):
    r = pl.program_id(0)
    j = pl.program_id(1)

    @pl.when(j == 0)
    def _():
        acc[...] = jnp.zeros_like(acc)

    @pl.when(tr_ref[r] > 0)
    def _():
        x = x_ref[...].astype(BF16)
        a = (_silu(_mm(x, wg_ref[0])) * _mm(x, wu_ref[0])).astype(BF16)
        acc[...] += _mm(a, wd_ref[0])

    @pl.when(j == pl.num_programs(1) - 1)
    def _():
        y_ref[...] = acc[...]


def _expert_ffn(xs, tile_e, tile_rows, wg, wu, wd, tmd):
    rows = xs.shape[0]
    nj = D_FF_EXPERT // EXPERT_FF_CHUNK
    live = lambda r, tr: tr[r] > 0
    jj = lambda r, j, tr: jnp.where(live(r, tr), j, nj - 1)
    return pl.pallas_call(
        _expert_kernel,
        grid_spec=pltpu.PrefetchScalarGridSpec(
            num_scalar_prefetch=2,
            grid=(rows // tmd, nj),
            in_specs=[
                pl.BlockSpec((tmd, D_MODEL), lambda r, j, te, tr: (jnp.where(live(r, tr), r, 0), 0)),
                pl.BlockSpec((1, D_MODEL, EXPERT_FF_CHUNK), lambda r, j, te, tr: (te[r], 0, jj(r, j, tr))),
                pl.BlockSpec((1, D_MODEL, EXPERT_FF_CHUNK), lambda r, j, te, tr: (te[r], 0, jj(r, j, tr))),
                pl.BlockSpec((1, EXPERT_FF_CHUNK, D_MODEL), lambda r, j, te, tr: (te[r], jj(r, j, tr), 0)),
            ],
            out_specs=pl.BlockSpec((tmd, D_MODEL), lambda r, j, te, tr: (r, 0)),
            scratch_shapes=[pltpu.VMEM((tmd, D_MODEL), F32)]),
        out_shape=jax.ShapeDtypeStruct((rows, D_MODEL), F32),
        compiler_params=_cparams(("parallel", "arbitrary")),
        name="expert_ffn",
    )(tile_e, tile_rows, xs, wg, wu, wd)


def _combine_kernel(pos_ref, h_ref, gates_ref, lr_ref, p_ref, gp_ref, wpg_ref, bpg_ref, wp_ref,
                    gfin_ref, ys_ref, out_ref, ybuf, sem):
    i = pl.program_id(0)
    n = pl.num_programs(0)
    tile = h_ref.shape[0]

    def copies(step, slot):
        return [pltpu.make_async_copy(
            ys_ref.at[pl.ds(pl.multiple_of(pos_ref[step * N_EXPERTS + e], SUBLANES), tile)],
            ybuf.at[slot, e], sem.at[slot, e]) for e in range(N_EXPERTS)]

    @pl.when(i == 0)
    def _():
        for c in copies(0, 0):
            c.start()

    slot = i % 2
    for s in range(2):
        @pl.when((slot == 1 - s) & (i + 1 < n))
        def _(s=s):
            for c in copies(i + 1, s):
                c.start()

    for s in range(2):
        @pl.when(slot == s)
        def _(s=s):
            for c in copies(i, s):
                c.wait()
    gates = gates_ref[...]
    lr = lr_ref[...]
    lane = lax.broadcasted_iota(jnp.int32, (tile, tile), 1)
    f = jnp.zeros((tile, D_MODEL), F32)
    for e in range(N_EXPERTS):
        onehot = (lane == lr[:, e:e + 1]).astype(BF16)
        f = f + gates[:, e:e + 1] * _mm(onehot, ybuf[slot, e].astype(BF16))
    h = h_ref[...] + f
    h = _ple(h, p_ref[...], gp_ref[...], wpg_ref[...], bpg_ref[...], wp_ref[...])
    out_ref[...] = _rms(h, gfin_ref[...])


def _combine(h, gates, lr, p, gp, wpg, bpg, wp, gfin, ys, pos, tile):
    t = h.shape[0]
    row = lambda i, *_: (i, 0)
    consts = [gp, wpg, bpg, wp, gfin]
    return pl.pallas_call(
        _combine_kernel,
        grid_spec=pltpu.PrefetchScalarGridSpec(
            num_scalar_prefetch=1,
            grid=(t // tile,),
            in_specs=[pl.BlockSpec((tile, D_MODEL), row), pl.BlockSpec((tile, N_EXPERTS), row),
                      pl.BlockSpec((tile, N_EXPERTS), row), pl.BlockSpec((tile, PLE_DIM), row)]
            + [_const_spec(c.shape) for c in consts] + [pl.BlockSpec(memory_space=pl.ANY)],
            out_specs=pl.BlockSpec((tile, D_MODEL), row),
            scratch_shapes=[pltpu.VMEM((2, N_EXPERTS, tile, D_MODEL), F32),
                            pltpu.SemaphoreType.DMA((2, N_EXPERTS))]),
        out_shape=jax.ShapeDtypeStruct((t, D_MODEL), F32),
        compiler_params=_cparams(("arbitrary",)),
        name="combine",
    )(pos, h, gates, lr, p, *consts, ys)


def _round_up(x, m):
    return (x + m - 1) // m * m


def _route_plan(gates, tile, tmd):
    t = gates.shape[0]
    n_tiles = t // tile
    sel = (gates > 0).astype(jnp.int32).reshape(n_tiles, tile, N_EXPERTS)
    local = jnp.cumsum(sel, axis=1) - sel
    lr = jnp.where(sel > 0, local, -1).reshape(t, N_EXPERTS)
    cnt8 = _round_up(sel.sum(axis=1), SUBLANES)
    tile_base = jnp.cumsum(cnt8, axis=0) - cnt8
    total8 = cnt8.sum(axis=0)
    region = _round_up(total8, tmd)
    region_end = jnp.cumsum(region)
    off = region_end - region
    dst = off[None, :] + tile_base
    rows = _round_up(2 * t + (SUBLANES - 1) * N_EXPERTS * n_tiles + N_EXPERTS * tmd + tile, tmd)
    n_rt = rows // tmd
    tile_start = jnp.arange(n_rt, dtype=jnp.int32) * tmd
    te = jnp.searchsorted(region_end, tile_start, side="right").astype(jnp.int32)
    tec = jnp.minimum(te, N_EXPERTS - 1)
    tile_rows = jnp.where(te < N_EXPERTS, jnp.clip(total8[tec] - (tile_start - off[tec]), 0, tmd), 0)
    last_e = jnp.max(jnp.where(tile_rows > 0, tec, 0))
    tile_e = jnp.where(tile_rows > 0, tec, last_e).astype(jnp.int32)
    meta = jnp.stack([dst.reshape(-1), cnt8.reshape(-1)]).astype(jnp.int32)
    tail = jnp.stack([jnp.append(off + total8, region_end[-1]),
                      jnp.append(region - total8, rows - region_end[-1])]).astype(jnp.int32)
    return dict(lr=lr.astype(jnp.int32), lr_t=lr.T.astype(jnp.int32), meta=meta, tail=tail,
                pos=dst.reshape(-1).astype(jnp.int32), tile_e=tile_e,
                tile_rows=tile_rows.astype(jnp.int32), rows=rows)


def _moe_and_head(h, hn, gates, p, w, tile, tmd):
    plan = _route_plan(gates, tile, tmd)
    xs = _dispatch(hn, plan["lr_t"], plan["meta"], plan["tail"], plan["rows"], tile)
    ys = _expert_ffn(xs, plan["tile_e"], plan["tile_rows"], w["exp_gate"], w["exp_up"], w["exp_down"], tmd)
    return _combine(h, gates, plan["lr"], p, w["norm_ple1"], w["ple_gate1"], w["b_ple_gate1"], w["ple1"],
                    w["norm_final"], ys, plan["pos"], tile)


def _rope_tables(pos):
    half = HEAD_DIM // 2
    inv = 1.0 / (ROPE_THETA ** (jnp.arange(half, dtype=F32) / half))
    ang = pos.astype(F32)[:, None] * inv[None, :]
    cos = jnp.tile(jnp.cos(ang), (1, LANES // half))
    sin = jnp.sin(ang)
    sin = jnp.tile(jnp.concatenate([-sin, sin], axis=-1), (1, LANES // HEAD_DIM))
    return cos, sin


def kernel(x_prompt, x_sample, cache_k, cache_v, state_pool, p_prompt, p_sample, norm_mix, norm_ffn, norm_ple, norm_final, w_qkv, b_qkv, w_o, b_o, sinks, w_pool, pool_scale, w_ff_gate, w_ff_up, w_ff_down, w_router, b_router, w_exp_gate, w_exp_up, w_exp_down, w_ple, w_ple_gate, b_ple_gate):
    batch, seq, _ = x_prompt.shape
    dec = x_sample.shape[0]
    row2 = lambda a: a.reshape(1, -1)
    w = dict(
        exp_gate=w_exp_gate[0].astype(BF16), exp_up=w_exp_up[0].astype(BF16), exp_down=w_exp_down[0].astype(BF16),
        norm_ple1=row2(norm_ple[1]), ple_gate1=w_ple_gate[1].astype(BF16), b_ple_gate1=row2(b_ple_gate[1]),
        ple1=w_ple[1].astype(BF16), norm_final=row2(norm_final))
    wqkv = w_qkv[0].astype(BF16)
    l0 = [w_o[0].astype(BF16), row2(b_o[0]), row2(norm_ffn[0]), w_ff_gate[0].astype(BF16),
          w_ff_up[0].astype(BF16), w_ff_down[0].astype(BF16), row2(norm_ple[0]),
          w_ple_gate[0].astype(BF16), row2(b_ple_gate[0]), w_ple[0].astype(BF16), row2(norm_mix[1])]
    pool_w = [w_pool[0].astype(BF16), row2(pool_scale[0]), row2(norm_ffn[1]), w_router[0], row2(b_router[0])]

    xp = x_prompt.reshape(batch * seq, D_MODEL)
    cos_p, sin_p = _rope_tables(jnp.arange(seq, dtype=jnp.int32))
    q, k, v = _qkv_rope(xp, row2(norm_mix[0]), wqkv, row2(b_qkv[0]), cos_p, sin_p, tm=512)
    o = _prompt_attention(q, k, v, sinks[0], batch, seq)
    h1, hn1 = _layer0_tail(xp, o, p_prompt[0].reshape(batch * seq, PLE_DIM), *l0, tm=512)
    h2, hn2, gates = _pool_prompt(h1, hn1, *pool_w, batch=batch, seq=seq, tm=512)
    y_prompt = _moe_and_head(h2, hn2, gates, p_prompt[1].reshape(batch * seq, PLE_DIM), w, tile=256, tmd=1024)
    kv_shape = (1, batch, WINDOW, N_KV_HEADS, HEAD_DIM)
    new_k_prompt = k.reshape(batch, seq, KV_DIM)[:, -WINDOW:].reshape(kv_shape)
    new_v_prompt = v.reshape(batch, seq, KV_DIM)[:, -WINDOW:].reshape(kv_shape)
    new_pool_prompt = hn1.reshape(batch, seq, D_MODEL)[:, -POOL_HIST:][None]

    xs_ = x_sample.reshape(dec, D_MODEL)
    cos_s, sin_s = _rope_tables(jnp.full((dec,), PAST_LEN, jnp.int32))
    qs, ks, vs = _qkv_rope(xs_, row2(norm_mix[0]), wqkv, row2(b_qkv[0]), cos_s, sin_s, tm=dec)
    n_hist = cache_k.shape[2]
    k_win = jnp.concatenate([cache_k[0].reshape(dec, n_hist, KV_DIM), ks[:, None]], axis=1)[:, -n_hist:]
    v_win = jnp.concatenate([cache_v[0].reshape(dec, n_hist, KV_DIM), vs[:, None]], axis=1)[:, -n_hist:]
    head_kv = jnp.arange(N_HEADS) // GROUP
    blk = (head_kv[:, None] == jnp.arange(N_KV_HEADS)[None, :]).astype(BF16)
    q_blk = (qs.reshape(dec, N_HEADS, 1, HEAD_DIM) * blk[None, :, :, None]).reshape(dec, N_HEADS, KV_DIM)
    o_blk = _sample_attention(q_blk, k_win, v_win, sinks[0])
    o_s = jnp.take_along_axis(o_blk.reshape(dec, N_HEADS, N_KV_HEADS, HEAD_DIM),
                              head_kv[None, :, None, None], axis=2).reshape(dec, N_HEADS * HEAD_DIM)
    h1s, hn1s = _layer0_tail(xs_, o_s.astype(BF16), p_sample[0].reshape(dec, PLE_DIM), *l0, tm=dec)
    hist_t = jnp.swapaxes(state_pool[0], 0, 1)
    h2s, hn2s, gates_s = _pool_sample(h1s, hn1s, hist_t, *pool_w)
    y_sample = _moe_and_head(h2s, hn2s, gates_s, p_sample[1].reshape(dec, PLE_DIM), w, tile=dec, tmd=256)
    new_k_sample = k_win.reshape(1, dec, n_hist, N_KV_HEADS, HEAD_DIM)
    new_v_sample = v_win.reshape(1, dec, n_hist, N_KV_HEADS, HEAD_DIM)
    new_pool_sample = jnp.concatenate([state_pool[0], hn1s[:, None]], axis=1)[:, -POOL_HIST:][None]

    return (y_prompt.reshape(batch, seq, D_MODEL), y_sample.reshape(dec, 1, D_MODEL),
            new_k_prompt, new_v_prompt, new_pool_prompt, new_k_sample, new_v_sample, new_pool_sample)
```

```python
import functools

import jax
import jax.numpy as jnp
from jax import lax
from jax.experimental import pallas as pl
from jax.experimental.pallas import tpu as pltpu

F32 = jnp.float32
BF16 = jnp.bfloat16

D_MODEL = 1024
HEAD_DIM = 64
N_HEADS = 16
N_KV_HEADS = 4
GROUP = N_HEADS // N_KV_HEADS
KV_DIM = N_KV_HEADS * HEAD_DIM
QKV_DIM = (N_HEADS + 2 * N_KV_HEADS) * HEAD_DIM
WINDOW = 128
ROPE_THETA = 10000.0
PAST_LEN = 16384
POOL_WINDOWS = (2, 4, 8, 16)
POOL_GROUP_DIM = D_MODEL // len(POOL_WINDOWS)
POOL_HIST = max(POOL_WINDOWS) - 1
POOL_HALO = 16
D_FF = 2816
N_EXPERTS = 8
D_FF_EXPERT = 3584
PLE_DIM = 256
EPS = 1e-6

LANES = 128
SUBLANES = 8
VMEM_LIMIT = 56 * 1024 * 1024

ATTN_TQ = 512
FF_CHUNK = 1408
EXPERT_FF_CHUNK = 512
SEG_PIECES = (256, 128, 64, 32, 16, 8)
SLACK_CHUNK = 256


def _cparams(sem):
    return pltpu.CompilerParams(dimension_semantics=sem, vmem_limit_bytes=VMEM_LIMIT)


def _const_spec(shape):
    nd = len(shape)
    return pl.BlockSpec(shape, lambda *_: (0,) * nd, pipeline_mode=pl.Buffered(1))


def _rms(x, g):
    return x * lax.rsqrt(jnp.mean(x * x, axis=-1, keepdims=True) + EPS) * g


def _mm(a, b):
    return jnp.dot(a, b, preferred_element_type=F32)


def _silu(x):
    return x * jax.nn.sigmoid(x)


def _qkv_kernel(x_ref, g_ref, w_ref, b_ref, cos_ref, sin_ref, q_ref, k_ref, v_ref):
    hn = _rms(x_ref[...], g_ref[...])
    qkv = _mm(hn.astype(BF16), w_ref[...]) + b_ref[...]
    cos = cos_ref[...]
    sin = sin_ref[...]
    lane = lax.broadcasted_iota(jnp.int32, cos.shape, 1)
    first_half = (lane % HEAD_DIM) < (HEAD_DIM // 2)

    def rope(xb):
        partner = jnp.where(first_half, pltpu.roll(xb, LANES - HEAD_DIM // 2, 1),
                            pltpu.roll(xb, HEAD_DIM // 2, 1))
        return xb * cos + partner * sin

    scale = HEAD_DIM ** -0.5
    for c in range(N_HEADS * HEAD_DIM // LANES):
        sl = slice(c * LANES, (c + 1) * LANES)
        q_ref[:, sl] = (rope(qkv[:, sl]) * scale).astype(BF16)
    for c in range(KV_DIM // LANES):
        sl = slice(c * LANES, (c + 1) * LANES)
        k_ref[:, sl] = rope(qkv[:, N_HEADS * HEAD_DIM + c * LANES:N_HEADS * HEAD_DIM + (c + 1) * LANES])
    v_ref[...] = qkv[:, N_HEADS * HEAD_DIM + KV_DIM:]


def _qkv_rope(x, g, w, b, cos, sin, tm):
    t = x.shape[0]
    n_pos_tiles = cos.shape[0] // tm
    row = lambda i: (i, 0)
    return pl.pallas_call(
        _qkv_kernel,
        grid=(t // tm,),
        in_specs=[
            pl.BlockSpec((tm, D_MODEL), row),
            _const_spec((1, D_MODEL)),
            _const_spec((D_MODEL, QKV_DIM)),
            _const_spec((1, QKV_DIM)),
            pl.BlockSpec((tm, LANES), lambda i: (i % n_pos_tiles, 0)),
            pl.BlockSpec((tm, LANES), lambda i: (i % n_pos_tiles, 0)),
        ],
        out_specs=[
            pl.BlockSpec((tm, N_HEADS * HEAD_DIM), row),
            pl.BlockSpec((tm, KV_DIM), row),
            pl.BlockSpec((tm, KV_DIM), row),
        ],
        out_shape=[
            jax.ShapeDtypeStruct((t, N_HEADS * HEAD_DIM), BF16),
            jax.ShapeDtypeStruct((t, KV_DIM), F32),
            jax.ShapeDtypeStruct((t, KV_DIM), F32),
        ],
        compiler_params=_cparams(("parallel",)),
        name="qkv_rope",
    )(x, g, w, b, cos, sin)


V_AUG = 4 * HEAD_DIM


def _attn_kernel(sink_ref, q_ref, kc_ref, kp_ref, vc_ref, vp_ref, o_ref, kbuf, vbuf, bias):
    n = pl.program_id(1)
    kbuf[0:WINDOW] = kp_ref[...].astype(BF16)
    kbuf[WINDOW:] = kc_ref[...].astype(BF16)
    v_all = jnp.concatenate([vp_ref[...], vc_ref[...]], axis=0).astype(BF16)
    pad0 = jnp.zeros((v_all.shape[0], HEAD_DIM), BF16)
    pad1 = jnp.ones((v_all.shape[0], 2 * HEAD_DIM), BF16)
    for kv in range(N_KV_HEADS):
        vbuf[:, kv * V_AUG:(kv + 1) * V_AUG] = jnp.concatenate(
            [v_all[:, kv * HEAD_DIM:(kv + 1) * HEAD_DIM], pad0, pad1], axis=1)
    shape = (GROUP * WINDOW, 2 * WINDOW)
    qrow = lax.broadcasted_iota(jnp.int32, shape, 0) & (WINDOW - 1)
    col = lax.broadcasted_iota(jnp.int32, shape, 1)
    mask_cur = (col >= WINDOW) & (col - WINDOW <= qrow)
    mask_prev = (col < WINDOW) & (col > qrow)
    bias[0] = jnp.where(mask_cur | mask_prev, 0.0, -jnp.inf)
    bias[1] = jnp.where(mask_cur, 0.0, -jnp.inf)
    for sb in range(ATTN_TQ // WINDOW):
        bias_sb = bias[jnp.where(n == 0, 1, 0)] if sb == 0 else bias[0]
        rows = slice(sb * WINDOW, (sb + 1) * WINDOW)
        for kv in range(N_KV_HEADS):
            kw = kbuf[sb * WINDOW:(sb + 2) * WINDOW, kv * HEAD_DIM:(kv + 1) * HEAD_DIM]
            vw = vbuf[sb * WINDOW:(sb + 2) * WINDOW, kv * V_AUG:(kv + 1) * V_AUG]
            heads = range(kv * GROUP, (kv + 1) * GROUP)
            qg = jnp.concatenate([q_ref[rows, h * HEAD_DIM:(h + 1) * HEAD_DIM] for h in heads], axis=0)
            s = lax.dot_general(qg, kw, (((1,), (1,)), ((), ())), preferred_element_type=F32)
            s = s + bias_sb
            p, sink_term = [], []
            for g, h in enumerate(heads):
                sh = s[g * WINDOW:(g + 1) * WINDOW]
                m = jnp.maximum(jnp.max(sh, axis=-1, keepdims=True), sink_ref[h])
                p.append(jnp.exp(sh - m).astype(BF16))
                sink_term.append(jnp.exp(sink_ref[h] - m))
            oa = _mm(jnp.concatenate(p, axis=0), vw)
            for g, h in enumerate(heads):
                oh = oa[g * WINDOW:(g + 1) * WINDOW]
                o = oh[:, :2 * HEAD_DIM] / (oh[:, 2 * HEAD_DIM:] + sink_term[g])
                o_ref[rows, h * HEAD_DIM:(h + 1) * HEAD_DIM] = o[:, :HEAD_DIM].astype(BF16)


def _prompt_attention(q, k, v, sinks, batch, seq):
    nq = seq // ATTN_TQ
    per = ATTN_TQ // WINDOW
    cur = lambda b, n: (b * nq + n, 0)
    prev = lambda b, n: (jnp.maximum(b * nq * per + n * per - 1, b * nq * per), 0)
    return pl.pallas_call(
        _attn_kernel,
        grid=(batch, nq),
        in_specs=[
            pl.BlockSpec(memory_space=pltpu.SMEM),
            pl.BlockSpec((ATTN_TQ, N_HEADS * HEAD_DIM), cur),
            pl.BlockSpec((ATTN_TQ, KV_DIM), cur),
            pl.BlockSpec((WINDOW, KV_DIM), prev),
            pl.BlockSpec((ATTN_TQ, KV_DIM), cur),
            pl.BlockSpec((WINDOW, KV_DIM), prev),
        ],
        out_specs=pl.BlockSpec((ATTN_TQ, N_HEADS * HEAD_DIM), cur),
        out_shape=jax.ShapeDtypeStruct(q.shape, BF16),
        scratch_shapes=[pltpu.VMEM((ATTN_TQ + WINDOW, KV_DIM), BF16),
                        pltpu.VMEM((ATTN_TQ + WINDOW, N_KV_HEADS * V_AUG), BF16),
                        pltpu.VMEM((2, GROUP * WINDOW, 2 * WINDOW), F32)],
        compiler_params=_cparams(("parallel", "parallel")),
        name="prompt_attention",
    )(sinks, q, k, k, v, v)


def _attn_sample_kernel(sink_ref, q_ref, k_ref, v_ref, o_ref):
    s = jnp.einsum("bhc,blc->bhl", q_ref[...], k_ref[...].astype(BF16),
                   preferred_element_type=F32)
    sink = sink_ref[...]
    m = jnp.maximum(jnp.max(s, axis=-1, keepdims=True), sink)
    p = jnp.exp(s - m)
    denom = jnp.sum(p, axis=-1, keepdims=True) + jnp.exp(sink - m)
    o_ref[...] = jnp.einsum("bhl,blc->bhc", (p / denom).astype(BF16), v_ref[...].astype(BF16),
                            preferred_element_type=F32)


def _sample_attention(q_blk, k_win, v_win, sinks, bt=16):
    b = q_blk.shape[0]
    blk = lambda i: (i, 0, 0)
    return pl.pallas_call(
        _attn_sample_kernel,
        grid=(b // bt,),
        in_specs=[
            _const_spec((1, N_HEADS, 1)),
            pl.BlockSpec((bt, N_HEADS, KV_DIM), blk),
            pl.BlockSpec((bt, WINDOW, KV_DIM), blk),
            pl.BlockSpec((bt, WINDOW, KV_DIM), blk),
        ],
        out_specs=pl.BlockSpec((bt, N_HEADS, KV_DIM), blk),
        out_shape=jax.ShapeDtypeStruct((b, N_HEADS, KV_DIM), F32),
        compiler_params=_cparams(("parallel",)),
        name="sample_attention",
    )(sinks.reshape(1, N_HEADS, 1), q_blk, k_win, v_win)


def _ple(h, p, g, wg, bg, wp):
    gate = jax.nn.sigmoid(_mm(_rms(h, g).astype(BF16), wg) + bg)
    return h + gate * _mm(p.astype(BF16), wp)


def _layer0_tail_kernel(x_ref, o_ref, p_ref, wo_ref, bo_ref, gf_ref, wg_ref, wu_ref, wd_ref,
                        gp_ref, wpg_ref, bpg_ref, wp_ref, gm_ref, h_ref, hn_ref):
    h = x_ref[...] + _mm(o_ref[...], wo_ref[...]) + bo_ref[...]
    hn = _rms(h, gf_ref[...]).astype(BF16)
    f = None
    for c in range(0, D_FF, FF_CHUNK):
        a = (_silu(_mm(hn, wg_ref[:, c:c + FF_CHUNK])) * _mm(hn, wu_ref[:, c:c + FF_CHUNK])).astype(BF16)
        part = _mm(a, wd_ref[c:c + FF_CHUNK, :])
        f = part if f is None else f + part
    h = h + f
    h = _ple(h, p_ref[...], gp_ref[...], wpg_ref[...], bpg_ref[...], wp_ref[...])
    h_ref[...] = h
    hn_ref[...] = _rms(h, gm_ref[...])


def _layer0_tail(x, o, p, wo, bo, gf, wg, wu, wd, gp, wpg, bpg, wp, gm, tm):
    t = x.shape[0]
    row = lambda i: (i, 0)
    consts = [wo, bo, gf, wg, wu, wd, gp, wpg, bpg, wp, gm]
    return pl.pallas_call(
        _layer0_tail_kernel,
        grid=(t // tm,),
        in_specs=[pl.BlockSpec((tm, D_MODEL), row), pl.BlockSpec((tm, D_MODEL), row),
                  pl.BlockSpec((tm, PLE_DIM), row)] + [_const_spec(c.shape) for c in consts],
        out_specs=[pl.BlockSpec((tm, D_MODEL), row), pl.BlockSpec((tm, D_MODEL), row)],
        out_shape=[jax.ShapeDtypeStruct((t, D_MODEL), F32), jax.ShapeDtypeStruct((t, D_MODEL), F32)],
        compiler_params=_cparams(("parallel",)),
        name="layer0_tail",
    )(x, o, p, *consts)


def _pool_project_route(h, hn, pooled_sum_inv, wpool_ref, ps_ref, gf_ref, wr_ref, br_ref,
                        h_out, hn_out, gates_out):
    mixed = []
    for g in range(len(POOL_WINDOWS)):
        cols = slice(g * POOL_GROUP_DIM, (g + 1) * POOL_GROUP_DIM)
        pooled = pooled_sum_inv[g] - hn[:, cols]
        mixed.append(_mm(pooled.astype(BF16), wpool_ref[g]))
    h = h + jnp.concatenate(mixed, axis=-1) * ps_ref[...]
    h_out[...] = h
    hn2 = _rms(h, gf_ref[...])
    hn_out[...] = hn2.astype(BF16)
    logits = jnp.dot(hn2, wr_ref[...], precision=lax.Precision.HIGHEST,
                     preferred_element_type=F32) + br_ref[...]
    idx = lax.broadcasted_iota(jnp.int32, logits.shape, 1)
    m1 = jnp.max(logits, axis=-1, keepdims=True)
    i1 = jnp.min(jnp.where(logits == m1, idx, N_EXPERTS), axis=-1, keepdims=True)
    rest = jnp.where(idx == i1, -jnp.inf, logits)
    m2 = jnp.max(rest, axis=-1, keepdims=True)
    i2 = jnp.min(jnp.where(rest == m2, idx, N_EXPERTS), axis=-1, keepdims=True)
    e = jnp.exp(m2 - m1)
    w1 = 1.0 / (1.0 + e)
    w2 = e / (1.0 + e)
    gates_out[...] = jnp.where(idx == i1, w1, jnp.where(idx == i2, w2, 0.0))


def _pool_prompt_kernel(h_ref, hn_ref, wpool_ref, ps_ref, gf_ref, wr_ref, br_ref,
                        h_out, hn_out, gates_out, carry, buf):
    n = pl.program_id(1)
    tm = h_ref.shape[0]

    @pl.when(n == 0)
    def _():
        carry[...] = jnp.zeros_like(carry)

    hn = hn_ref[...]
    buf[0:POOL_HALO] = carry[...]
    buf[POOL_HALO:] = hn
    carry[...] = hn[tm - POOL_HALO:]
    pos = n * tm + lax.broadcasted_iota(jnp.int32, (tm, 1), 0)
    means = []
    for g, w in enumerate(POOL_WINDOWS):
        s = buf[:, g * POOL_GROUP_DIM:(g + 1) * POOL_GROUP_DIM]
        shift = 1
        while shift < w:
            s = s + pltpu.roll(s, shift, 0)
            shift *= 2
        cnt = jnp.minimum(w, pos + 1).astype(F32)
        means.append(s[POOL_HALO:] * (1.0 / cnt))
    _pool_project_route(h_ref[...], hn, means, wpool_ref, ps_ref, gf_ref, wr_ref, br_ref,
                        h_out, hn_out, gates_out)


def _pool_prompt(h, hn, wpool, ps, gf, wr, br, batch, seq, tm):
    t = h.shape[0]
    ns = seq // tm
    row = lambda b, n: (b * ns + n, 0)
    consts = [wpool, ps, gf, wr, br]
    return pl.pallas_call(
        _pool_prompt_kernel,
        grid=(batch, ns),
        in_specs=[pl.BlockSpec((tm, D_MODEL), row), pl.BlockSpec((tm, D_MODEL), row)]
        + [_const_spec(c.shape) for c in consts],
        out_specs=[pl.BlockSpec((tm, D_MODEL), row), pl.BlockSpec((tm, D_MODEL), row),
                   pl.BlockSpec((tm, N_EXPERTS), row)],
        out_shape=[jax.ShapeDtypeStruct((t, D_MODEL), F32), jax.ShapeDtypeStruct((t, D_MODEL), BF16),
                   jax.ShapeDtypeStruct((t, N_EXPERTS), F32)],
        scratch_shapes=[pltpu.VMEM((POOL_HALO, D_MODEL), F32),
                        pltpu.VMEM((tm + POOL_HALO, D_MODEL), F32)],
        compiler_params=_cparams(("parallel", "arbitrary")),
        name="pool_prompt",
    )(h, hn, *consts)


def _pool_sample_kernel(h_ref, hn_ref, hist_ref, wpool_ref, ps_ref, gf_ref, wr_ref, br_ref,
                        h_out, hn_out, gates_out):
    hn = hn_ref[...]
    means = []
    for g, w in enumerate(POOL_WINDOWS):
        cols = slice(g * POOL_GROUP_DIM, (g + 1) * POOL_GROUP_DIM)
        s = hn[:, cols]
        for j in range(1, w):
            s = s + hist_ref[POOL_HIST - j, :, cols]
        means.append(s * (1.0 / min(w, PAST_LEN + 1)))
    _pool_project_route(h_ref[...], hn, means, wpool_ref, ps_ref, gf_ref, wr_ref, br_ref,
                        h_out, hn_out, gates_out)


def _pool_sample(h, hn, hist_t, wpool, ps, gf, wr, br):
    t = h.shape[0]
    args = [h, hn, hist_t, wpool, ps, gf, wr, br]
    return pl.pallas_call(
        _pool_sample_kernel,
        grid=(1,),
        in_specs=[_const_spec(a.shape) for a in args],
        out_specs=[_const_spec((t, D_MODEL)), _const_spec((t, D_MODEL)), _const_spec((t, N_EXPERTS))],
        out_shape=[jax.ShapeDtypeStruct((t, D_MODEL), F32), jax.ShapeDtypeStruct((t, D_MODEL), BF16),
                   jax.ShapeDtypeStruct((t, N_EXPERTS), F32)],
        compiler_params=_cparams(("arbitrary",)),
        name="pool_sample",
    )(*args)


def _segment_copies(meta_ref, i, src_of, dst_hbm, sem):
    out = []
    for e in range(N_EXPERTS):
        dst = meta_ref[0, i * N_EXPERTS + e]
        cnt = meta_ref[1, i * N_EXPERTS + e]
        for piece in SEG_PIECES:
            if piece > src_of(e).shape[0]:
                continue
            start = pl.multiple_of(cnt & ~(2 * piece - 1), SUBLANES)
            copy = pltpu.make_async_copy(
                src_of(e).at[pl.ds(start, piece)],
                dst_hbm.at[pl.ds(pl.multiple_of(dst + start, SUBLANES), piece)],
                sem.at[e])
            out.append(((cnt & piece) != 0, copy))
    return out


def _dispatch_kernel(meta_ref, tail_ref, hn_ref, lrt_ref, xs_ref, seg, zeros, sem, zsem):
    i = pl.program_id(0)
    tile = hn_ref.shape[0]
    hn = hn_ref[...]
    slot = lax.broadcasted_iota(jnp.int32, (tile, tile), 0)
    for e in range(N_EXPERTS):
        onehot = (slot == lrt_ref[e:e + 1, :]).astype(BF16)
        seg[e] = _mm(onehot, hn)
    copies = _segment_copies(meta_ref, i, lambda e: seg.at[e], xs_ref, sem)
    for pred, copy in copies:
        @pl.when(pred)
        def _(copy=copy):
            copy.start()
    for pred, copy in copies:
        @pl.when(pred)
        def _(copy=copy):
            copy.wait()

    @pl.when(i == pl.num_programs(0) - 1)
    def _():
        zeros[...] = jnp.zeros_like(zeros)
        tails = []
        for e in range(N_EXPERTS):
            start = tail_ref[0, e]
            cnt = tail_ref[1, e]
            for piece in (512,) + SEG_PIECES:
                off = pl.multiple_of(cnt & ~(2 * piece - 1), SUBLANES)
                copy = pltpu.make_async_copy(
                    zeros.at[pl.ds(0, piece)],
                    xs_ref.at[pl.ds(pl.multiple_of(start + off, SUBLANES), piece)],
                    zsem.at[e])
                tails.append(((cnt & piece) != 0, copy))
        for pred, copy in tails:
            @pl.when(pred)
            def _(copy=copy):
                copy.start()
        for pred, copy in tails:
            @pl.when(pred)
            def _(copy=copy):
                copy.wait()

        def slack_copy(c):
            row = pl.multiple_of(tail_ref[0, N_EXPERTS] + c * SLACK_CHUNK, SLACK_CHUNK)
            return pltpu.make_async_copy(zeros.at[pl.ds(0, SLACK_CHUNK)],
                                         xs_ref.at[pl.ds(row, SLACK_CHUNK)], zsem.at[0])

        n_slack = tail_ref[1, N_EXPERTS] // SLACK_CHUNK
        lax.fori_loop(0, n_slack, lambda c, _: slack_copy(c).start(), None)
        lax.fori_loop(0, n_slack, lambda c, _: slack_copy(c).wait(), None)


def _dispatch(hn, lr_t, meta, tail, rows, tile):
    t = hn.shape[0]
    return pl.pallas_call(
        _dispatch_kernel,
        grid_spec=pltpu.PrefetchScalarGridSpec(
            num_scalar_prefetch=2,
            grid=(t // tile,),
            in_specs=[pl.BlockSpec((tile, D_MODEL), lambda i, *_: (i, 0)),
                      pl.BlockSpec((N_EXPERTS, tile), lambda i, *_: (0, i))],
            out_specs=pl.BlockSpec(memory_space=pl.ANY),
            scratch_shapes=[pltpu.VMEM((N_EXPERTS, tile, D_MODEL), F32),
                            pltpu.VMEM((512, D_MODEL), F32),
                            pltpu.SemaphoreType.DMA((N_EXPERTS,)),
                            pltpu.SemaphoreType.DMA((N_EXPERTS,))]),
        out_shape=jax.ShapeDtypeStruct((rows, D_MODEL), F32),
        compiler_params=_cparams(("arbitrary",)),
        name="expert_dispatch",
    )(meta, tail, hn, lr_t)


def _expert_kernel(te_ref, tr_ref, x_ref, wg_ref, wu_ref, wd_ref, y_ref, acc):
    r = pl.program_id(0)
    j = pl.program_id(1)

    @pl.when(j == 0)
    def _():
        acc[...] = jnp.zeros_like(acc)

    @pl.when(tr_ref[r] > 0)
    def _():
        x = x_ref[...].astype(BF16)
        a = (_silu(_mm(x, wg_ref[0].astype(BF16))) * _mm(x, wu_ref[0].astype(BF16))).astype(BF16)
        acc[...] += _mm(a, wd_ref[0].astype(BF16))

    @pl.when(j == pl.num_programs(1) - 1)
    def _():
        y_ref[...] = acc[...]


def _expert_ffn(xs, tile_e, tile_rows, wg, wu, wd, tmd):
    rows = xs.shape[0]
    nj = D_FF_EXPERT // EXPERT_FF_CHUNK
    live = lambda r, tr: tr[r] > 0
    jj = lambda r, j, tr: jnp.where(live(r, tr), j, nj - 1)
    return pl.pallas_call(
        _expert_kernel,
        grid_spec=pltpu.PrefetchScalarGridSpec(
            num_scalar_prefetch=2,
            grid=(rows // tmd, nj),
            in_specs=[
                pl.BlockSpec((tmd, D_MODEL), lambda r, j, te, tr: (jnp.where(live(r, tr), r, 0), 0)),
                pl.BlockSpec((1, D_MODEL, EXPERT_FF_CHUNK), lambda r, j, te, tr: (te[r], 0, jj(r, j, tr))),
                pl.BlockSpec((1, D_MODEL, EXPERT_FF_CHUNK), lambda r, j, te, tr: (te[r], 0, jj(r, j, tr))),
                pl.BlockSpec((1, EXPERT_FF_CHUNK, D_MODEL), lambda r, j, te, tr: (te[r], jj(r, j, tr), 0)),
            ],
            out_specs=pl.BlockSpec((tmd, D_MODEL), lambda r, j, te, tr: (r, 0)),
            scratch_shapes=[pltpu.VMEM((tmd, D_MODEL), F32)]),
        out_shape=jax.ShapeDtypeStruct((rows, D_MODEL), F32),
        compiler_params=_cparams(("parallel", "arbitrary")),
        name="expert_ffn",
    )(tile_e, tile_rows, xs, wg, wu, wd)


def _combine_kernel(pos_ref, h_ref, gates_ref, lr_ref, p_ref, gp_ref, wpg_ref, bpg_ref, wp_ref,
                    gfin_ref, ys_ref, out_ref, ybuf, sem):
    i = pl.program_id(0)
    n = pl.num_programs(0)
    tile = h_ref.shape[0]

    def copies(step, slot):
        return [pltpu.make_async_copy(
            ys_ref.at[pl.ds(pl.multiple_of(pos_ref[step * N_EXPERTS + e], SUBLANES), tile)],
            ybuf.at[slot, e], sem.at[slot, e]) for e in range(N_EXPERTS)]

    @pl.when(i == 0)
    def _():
        for c in copies(0, 0):
            c.start()

    slot = i % 2
    for s in range(2):
        @pl.when((slot == 1 - s) & (i + 1 < n))
        def _(s=s):
            for c in copies(i + 1, s):
                c.start()

    for s in range(2):
        @pl.when(slot == s)
        def _(s=s):
            for c in copies(i, s):
                c.wait()
    gates = gates_ref[...]
    lr = lr_ref[...]
    lane = lax.broadcasted_iota(jnp.int32, (tile, tile), 1)
    f = jnp.zeros((tile, D_MODEL), F32)
    for e in range(N_EXPERTS):
        onehot = (lane == lr[:, e:e + 1]).astype(BF16)
        f = f + gates[:, e:e + 1] * _mm(onehot, ybuf[slot, e].astype(BF16))
    h = h_ref[...] + f
    h = _ple(h, p_ref[...], gp_ref[...], wpg_ref[...], bpg_ref[...], wp_ref[...])
    out_ref[...] = _rms(h, gfin_ref[...])


def _combine(h, gates, lr, p, gp, wpg, bpg, wp, gfin, ys, pos, tile):
    t = h.shape[0]
    row = lambda i, *_: (i, 0)
    consts = [gp, wpg, bpg, wp, gfin]
    return pl.pallas_call(
        _combine_kernel,
        grid_spec=pltpu.PrefetchScalarGridSpec(
            num_scalar_prefetch=1,
            grid=(t // tile,),
            in_specs=[pl.BlockSpec((tile, D_MODEL), row), pl.BlockSpec((tile, N_EXPERTS), row),
                      pl.BlockSpec((tile, N_EXPERTS), row), pl.BlockSpec((tile, PLE_DIM), row)]
            + [_const_spec(c.shape) for c in consts] + [pl.BlockSpec(memory_space=pl.ANY)],
            out_specs=pl.BlockSpec((tile, D_MODEL), row),
            scratch_shapes=[pltpu.VMEM((2, N_EXPERTS, tile, D_MODEL), F32),
                            pltpu.SemaphoreType.DMA((2, N_EXPERTS))]),
        out_shape=jax.ShapeDtypeStruct((t, D_MODEL), F32),
        compiler_params=_cparams(("arbitrary",)),
        name="combine",
    )(pos, h, gates, lr, p, *consts, ys)


def _round_up(x, m):
    return (x + m - 1) // m * m


def _route_plan(gates, tile, tmd):
    t = gates.shape[0]
    n_tiles = t // tile
    sel = (gates > 0).astype(jnp.int32).reshape(n_tiles, tile, N_EXPERTS)
    local = jnp.cumsum(sel, axis=1) - sel
    lr = jnp.where(sel > 0, local, -1).reshape(t, N_EXPERTS)
    cnt8 = _round_up(sel.sum(axis=1), SUBLANES)
    tile_base = jnp.cumsum(cnt8, axis=0) - cnt8
    total8 = cnt8.sum(axis=0)
    region = _round_up(total8, tmd)
    region_end = jnp.cumsum(region)
    off = region_end - region
    dst = off[None, :] + tile_base
    rows = _round_up(2 * t + (SUBLANES - 1) * N_EXPERTS * n_tiles + N_EXPERTS * tmd + tile, tmd)
    n_rt = rows // tmd
    tile_start = jnp.arange(n_rt, dtype=jnp.int32) * tmd
    te = jnp.sum(region_end[None, :] <= tile_start[:, None], axis=1).astype(jnp.int32)
    tec = jnp.minimum(te, N_EXPERTS - 1)
    tile_rows = jnp.where(te < N_EXPERTS, jnp.clip(total8[tec] - (tile_start - off[tec]), 0, tmd), 0)
    last_e = jnp.max(jnp.where(tile_rows > 0, tec, 0))
    tile_e = jnp.where(tile_rows > 0, tec, last_e).astype(jnp.int32)
    meta = jnp.stack([dst.reshape(-1), cnt8.reshape(-1)]).astype(jnp.int32)
    tail = jnp.stack([jnp.append(off + total8, region_end[-1]),
                      jnp.append(region - total8, rows - region_end[-1])]).astype(jnp.int32)
    return dict(lr=lr.astype(jnp.int32), lr_t=lr.T.astype(jnp.int32), meta=meta, tail=tail,
                pos=dst.reshape(-1).astype(jnp.int32), tile_e=tile_e,
                tile_rows=tile_rows.astype(jnp.int32), rows=rows)


def _moe_and_head(h, hn, gates, p, w, tile, tmd):
    plan = _route_plan(gates, tile, tmd)
    xs = _dispatch(hn, plan["lr_t"], plan["meta"], plan["tail"], plan["rows"], tile)
    ys = _expert_ffn(xs, plan["tile_e"], plan["tile_rows"], w["exp_gate"], w["exp_up"], w["exp_down"], tmd)
    return _combine(h, gates, plan["lr"], p, w["norm_ple1"], w["ple_gate1"], w["b_ple_gate1"], w["ple1"],
                    w["norm_final"], ys, plan["pos"], tile)


def _rope_tables(pos):
    half = HEAD_DIM // 2
    inv = 1.0 / (ROPE_THETA ** (jnp.arange(half, dtype=F32) / half))
    ang = pos.astype(F32)[:, None] * inv[None, :]
    cos = jnp.tile(jnp.cos(ang), (1, LANES // half))
    sin = jnp.sin(ang)
    sin = jnp.tile(jnp.concatenate([-sin, sin], axis=-1), (1, LANES // HEAD_DIM))
    return cos, sin


def kernel(x_prompt, x_sample, cache_k, cache_v, state_pool, p_prompt, p_sample, norm_mix, norm_ffn, norm_ple, norm_final, w_qkv, b_qkv, w_o, b_o, sinks, w_pool, pool_scale, w_ff_gate, w_ff_up, w_ff_down, w_router, b_router, w_exp_gate, w_exp_up, w_exp_down, w_ple, w_ple_gate, b_ple_gate):
    batch, seq, _ = x_prompt.shape
    dec = x_sample.shape[0]
    row2 = lambda a: a.reshape(1, -1)
    w = dict(
        exp_gate=w_exp_gate[0], exp_up=w_exp_up[0], exp_down=w_exp_down[0],
        norm_ple1=row2(norm_ple[1]), ple_gate1=w_ple_gate[1].astype(BF16), b_ple_gate1=row2(b_ple_gate[1]),
        ple1=w_ple[1].astype(BF16), norm_final=row2(norm_final))
    wqkv = w_qkv[0].astype(BF16)
    l0 = [w_o[0].astype(BF16), row2(b_o[0]), row2(norm_ffn[0]), w_ff_gate[0].astype(BF16),
          w_ff_up[0].astype(BF16), w_ff_down[0].astype(BF16), row2(norm_ple[0]),
          w_ple_gate[0].astype(BF16), row2(b_ple_gate[0]), w_ple[0].astype(BF16), row2(norm_mix[1])]
    pool_w = [w_pool[0].astype(BF16), row2(pool_scale[0]), row2(norm_ffn[1]), w_router[0], row2(b_router[0])]

    xp = x_prompt.reshape(batch * seq, D_MODEL)
    cos_p, sin_p = _rope_tables(jnp.arange(seq, dtype=jnp.int32))
    q, k, v = _qkv_rope(xp, row2(norm_mix[0]), wqkv, row2(b_qkv[0]), cos_p, sin_p, tm=512)
    o = _prompt_attention(q, k, v, sinks[0], batch, seq)
    h1, hn1 = _layer0_tail(xp, o, p_prompt[0].reshape(batch * seq, PLE_DIM), *l0, tm=512)
    h2, hn2, gates = _pool_prompt(h1, hn1, *pool_w, batch=batch, seq=seq, tm=512)
    y_prompt = _moe_and_head(h2, hn2, gates, p_prompt[1].reshape(batch * seq, PLE_DIM), w, tile=256, tmd=1024)
    kv_shape = (1, batch, WINDOW, N_KV_HEADS, HEAD_DIM)
    new_k_prompt = k.reshape(batch, seq, KV_DIM)[:, -WINDOW:].reshape(kv_shape)
    new_v_prompt = v.reshape(batch, seq, KV_DIM)[:, -WINDOW:].reshape(kv_shape)
    new_pool_prompt = hn1.reshape(batch, seq, D_MODEL)[:, -POOL_HIST:][None]

    xs_ = x_sample.reshape(dec, D_MODEL)
    cos_s, sin_s = _rope_tables(jnp.full((dec,), PAST_LEN, jnp.int32))
    qs, ks, vs = _qkv_rope(xs_, row2(norm_mix[0]), wqkv, row2(b_qkv[0]), cos_s, sin_s, tm=dec)
    n_hist = cache_k.shape[2]
    k_win = jnp.concatenate([cache_k[0].reshape(dec, n_hist, KV_DIM), ks[:, None]], axis=1)[:, -n_hist:]
    v_win = jnp.concatenate([cache_v[0].reshape(dec, n_hist, KV_DIM), vs[:, None]], axis=1)[:, -n_hist:]
    head_kv = jnp.arange(N_HEADS) // GROUP
    blk = (head_kv[:, None] == jnp.arange(N_KV_HEADS)[None, :]).astype(BF16)
    q_blk = (qs.reshape(dec, N_HEADS, 1, HEAD_DIM) * blk[None, :, :, None]).reshape(dec, N_HEADS, KV_DIM)
    o_blk = _sample_attention(q_blk, k_win, v_win, sinks[0])
    o_s = jnp.take_along_axis(o_blk.reshape(dec, N_HEADS, N_KV_HEADS, HEAD_DIM),
                              head_kv[None, :, None, None], axis=2).reshape(dec, N_HEADS * HEAD_DIM)
    h1s, hn1s = _layer0_tail(xs_, o_s.astype(BF16), p_sample[0].reshape(dec, PLE_DIM), *l0, tm=dec)
    hist_t = jnp.swapaxes(state_pool[0], 0, 1)
    h2s, hn2s, gates_s = _pool_sample(h1s, hn1s, hist_t, *pool_w)
    y_sample = _moe_and_head(h2s, hn2s, gates_s, p_sample[1].reshape(dec, PLE_DIM), w, tile=dec, tmd=256)
    new_k_sample = k_win.reshape(1, dec, n_hist, N_KV_HEADS, HEAD_DIM)
    new_v_sample = v_win.reshape(1, dec, n_hist, N_KV_HEADS, HEAD_DIM)
    new_pool_sample = jnp.concatenate([state_pool[0], hn1s[:, None]], axis=1)[:, -POOL_HIST:][None]

    return (y_prompt.reshape(batch, seq, D_MODEL), y_sample.reshape(dec, 1, D_MODEL),
            new_k_prompt, new_v_prompt, new_pool_prompt, new_k_sample, new_v_sample, new_pool_sample)
```

```python
import functools

import jax
import jax.numpy as jnp
from jax import lax
from jax.experimental import pallas as pl
from jax.experimental.pallas import tpu as pltpu

F32 = jnp.float32
BF16 = jnp.bfloat16

D_MODEL = 1024
HEAD_DIM = 64
N_HEADS = 16
N_KV_HEADS = 4
GROUP = N_HEADS // N_KV_HEADS
KV_DIM = N_KV_HEADS * HEAD_DIM
QKV_DIM = (N_HEADS + 2 * N_KV_HEADS) * HEAD_DIM
WINDOW = 128
ROPE_THETA = 10000.0
PAST_LEN = 16384
POOL_WINDOWS = (2, 4, 8, 16)
POOL_GROUP_DIM = D_MODEL // len(POOL_WINDOWS)
POOL_HIST = max(POOL_WINDOWS) - 1
POOL_HALO = 16
D_FF = 2816
N_EXPERTS = 8
D_FF_EXPERT = 3584
PLE_DIM = 256
EPS = 1e-6

LANES = 128
SUBLANES = 8
VMEM_LIMIT = 56 * 1024 * 1024

ATTN_TQ = 512
FF_CHUNK = 1408
EXPERT_FF_CHUNK = 512
TAIL_PIECES = (1024, 512, 256, 128, 64, 32, 16, 8)
SLACK_CHUNK = 256


def _cparams(sem):
    return pltpu.CompilerParams(dimension_semantics=sem, vmem_limit_bytes=VMEM_LIMIT)


def _const_spec(shape):
    nd = len(shape)
    return pl.BlockSpec(shape, lambda *_: (0,) * nd, pipeline_mode=pl.Buffered(1))


def _rms(x, g):
    return x * lax.rsqrt(jnp.mean(x * x, axis=-1, keepdims=True) + EPS) * g


def _mm(a, b):
    return jnp.dot(a, b, preferred_element_type=F32)


def _silu(x):
    return x * jax.nn.sigmoid(x)


def _qkv_kernel(x_ref, g_ref, w_ref, b_ref, cos_ref, sin_ref, q_ref, k_ref, v_ref):
    hn = _rms(x_ref[...], g_ref[...])
    qkv = _mm(hn.astype(BF16), w_ref[...]) + b_ref[...]
    cos = cos_ref[...]
    sin = sin_ref[...]
    lane = lax.broadcasted_iota(jnp.int32, cos.shape, 1)
    first_half = (lane % HEAD_DIM) < (HEAD_DIM // 2)

    def rope(xb):
        partner = jnp.where(first_half, pltpu.roll(xb, LANES - HEAD_DIM // 2, 1),
                            pltpu.roll(xb, HEAD_DIM // 2, 1))
        return xb * cos + partner * sin

    scale = HEAD_DIM ** -0.5
    for c in range(N_HEADS * HEAD_DIM // LANES):
        sl = slice(c * LANES, (c + 1) * LANES)
        q_ref[:, sl] = (rope(qkv[:, sl]) * scale).astype(BF16)
    for c in range(KV_DIM // LANES):
        sl = slice(c * LANES, (c + 1) * LANES)
        k_ref[:, sl] = rope(qkv[:, N_HEADS * HEAD_DIM + c * LANES:N_HEADS * HEAD_DIM + (c + 1) * LANES])
    v_ref[...] = qkv[:, N_HEADS * HEAD_DIM + KV_DIM:]


def _qkv_rope(x, g, w, b, cos, sin, tm):
    t = x.shape[0]
    n_pos_tiles = cos.shape[0] // tm
    row = lambda i: (i, 0)
    return pl.pallas_call(
        _qkv_kernel,
        grid=(t // tm,),
        in_specs=[
            pl.BlockSpec((tm, D_MODEL), row),
            _const_spec((1, D_MODEL)),
            _const_spec((D_MODEL, QKV_DIM)),
            _const_spec((1, QKV_DIM)),
            pl.BlockSpec((tm, LANES), lambda i: (i % n_pos_tiles, 0)),
            pl.BlockSpec((tm, LANES), lambda i: (i % n_pos_tiles, 0)),
        ],
        out_specs=[
            pl.BlockSpec((tm, N_HEADS * HEAD_DIM), row),
            pl.BlockSpec((tm, KV_DIM), row),
            pl.BlockSpec((tm, KV_DIM), row),
        ],
        out_shape=[
            jax.ShapeDtypeStruct((t, N_HEADS * HEAD_DIM), BF16),
            jax.ShapeDtypeStruct((t, KV_DIM), F32),
            jax.ShapeDtypeStruct((t, KV_DIM), F32),
        ],
        compiler_params=_cparams(("parallel",)),
        name="qkv_rope",
    )(x, g, w, b, cos, sin)


V_AUG = 4 * HEAD_DIM


def _attn_kernel(sink_ref, q_ref, kc_ref, kp_ref, vc_ref, vp_ref, o_ref, kbuf, vbuf, bias):
    n = pl.program_id(1)
    kbuf[0:WINDOW] = kp_ref[...].astype(BF16)
    kbuf[WINDOW:] = kc_ref[...].astype(BF16)
    v_all = jnp.concatenate([vp_ref[...], vc_ref[...]], axis=0).astype(BF16)
    pad0 = jnp.zeros((v_all.shape[0], HEAD_DIM), BF16)
    pad1 = jnp.ones((v_all.shape[0], 2 * HEAD_DIM), BF16)
    for kv in range(N_KV_HEADS):
        vbuf[:, kv * V_AUG:(kv + 1) * V_AUG] = jnp.concatenate(
            [v_all[:, kv * HEAD_DIM:(kv + 1) * HEAD_DIM], pad0, pad1], axis=1)
    shape = (GROUP * WINDOW, 2 * WINDOW)
    qrow = lax.broadcasted_iota(jnp.int32, shape, 0) & (WINDOW - 1)
    col = lax.broadcasted_iota(jnp.int32, shape, 1)
    mask_cur = (col >= WINDOW) & (col - WINDOW <= qrow)
    mask_prev = (col < WINDOW) & (col > qrow)
    bias[0] = jnp.where(mask_cur | mask_prev, 0.0, -jnp.inf)
    bias[1] = jnp.where(mask_cur, 0.0, -jnp.inf)
    for sb in range(ATTN_TQ // WINDOW):
        bias_sb = bias[jnp.where(n == 0, 1, 0)] if sb == 0 else bias[0]
        rows = slice(sb * WINDOW, (sb + 1) * WINDOW)
        for kv in range(N_KV_HEADS):
            kw = kbuf[sb * WINDOW:(sb + 2) * WINDOW, kv * HEAD_DIM:(kv + 1) * HEAD_DIM]
            vw = vbuf[sb * WINDOW:(sb + 2) * WINDOW, kv * V_AUG:(kv + 1) * V_AUG]
            heads = range(kv * GROUP, (kv + 1) * GROUP)
            qg = jnp.concatenate([q_ref[rows, h * HEAD_DIM:(h + 1) * HEAD_DIM] for h in heads], axis=0)
            s = lax.dot_general(qg, kw, (((1,), (1,)), ((), ())), preferred_element_type=F32)
            s = s + bias_sb
            p, sink_term = [], []
            for g, h in enumerate(heads):
                sh = s[g * WINDOW:(g + 1) * WINDOW]
                m = jnp.maximum(jnp.max(sh, axis=-1, keepdims=True), sink_ref[h])
                p.append(jnp.exp(sh - m).astype(BF16))
                sink_term.append(jnp.exp(sink_ref[h] - m))
            oa = _mm(jnp.concatenate(p, axis=0), vw)
            for g, h in enumerate(heads):
                oh = oa[g * WINDOW:(g + 1) * WINDOW]
                o = oh[:, :2 * HEAD_DIM] / (oh[:, 2 * HEAD_DIM:] + sink_term[g])
                o_ref[rows, h * HEAD_DIM:(h + 1) * HEAD_DIM] = o[:, :HEAD_DIM].astype(BF16)


def _prompt_attention(q, k, v, sinks, batch, seq):
    nq = seq // ATTN_TQ
    per = ATTN_TQ // WINDOW
    cur = lambda b, n: (b * nq + n, 0)
    prev = lambda b, n: (jnp.maximum(b * nq * per + n * per - 1, b * nq * per), 0)
    return pl.pallas_call(
        _attn_kernel,
        grid=(batch, nq),
        in_specs=[
            pl.BlockSpec(memory_space=pltpu.SMEM),
            pl.BlockSpec((ATTN_TQ, N_HEADS * HEAD_DIM), cur),
            pl.BlockSpec((ATTN_TQ, KV_DIM), cur),
            pl.BlockSpec((WINDOW, KV_DIM), prev),
            pl.BlockSpec((ATTN_TQ, KV_DIM), cur),
            pl.BlockSpec((WINDOW, KV_DIM), prev),
        ],
        out_specs=pl.BlockSpec((ATTN_TQ, N_HEADS * HEAD_DIM), cur),
        out_shape=jax.ShapeDtypeStruct(q.shape, BF16),
        scratch_shapes=[pltpu.VMEM((ATTN_TQ + WINDOW, KV_DIM), BF16),
                        pltpu.VMEM((ATTN_TQ + WINDOW, N_KV_HEADS * V_AUG), BF16),
                        pltpu.VMEM((2, GROUP * WINDOW, 2 * WINDOW), F32)],
        compiler_params=_cparams(("parallel", "parallel")),
        name="prompt_attention",
    )(sinks, q, k, k, v, v)


def _attn_sample_kernel(sink_ref, q_ref, k_ref, v_ref, o_ref):
    s = jnp.einsum("bhc,blc->bhl", q_ref[...], k_ref[...].astype(BF16),
                   preferred_element_type=F32)
    sink = sink_ref[...]
    m = jnp.maximum(jnp.max(s, axis=-1, keepdims=True), sink)
    p = jnp.exp(s - m)
    denom = jnp.sum(p, axis=-1, keepdims=True) + jnp.exp(sink - m)
    o_ref[...] = jnp.einsum("bhl,blc->bhc", (p / denom).astype(BF16), v_ref[...].astype(BF16),
                            preferred_element_type=F32)


def _sample_attention(q_blk, k_win, v_win, sinks, bt=16):
    b = q_blk.shape[0]
    blk = lambda i: (i, 0, 0)
    return pl.pallas_call(
        _attn_sample_kernel,
        grid=(b // bt,),
        in_specs=[
            _const_spec((1, N_HEADS, 1)),
            pl.BlockSpec((bt, N_HEADS, KV_DIM), blk),
            pl.BlockSpec((bt, WINDOW, KV_DIM), blk),
            pl.BlockSpec((bt, WINDOW, KV_DIM), blk),
        ],
        out_specs=pl.BlockSpec((bt, N_HEADS, KV_DIM), blk),
        out_shape=jax.ShapeDtypeStruct((b, N_HEADS, KV_DIM), F32),
        compiler_params=_cparams(("parallel",)),
        name="sample_attention",
    )(sinks.reshape(1, N_HEADS, 1), q_blk, k_win, v_win)


def _ple(h, p, g, wg, bg, wp):
    gate = jax.nn.sigmoid(_mm(_rms(h, g).astype(BF16), wg) + bg)
    return h + gate * _mm(p.astype(BF16), wp)


def _layer0_tail_kernel(x_ref, o_ref, p_ref, wo_ref, bo_ref, gf_ref, wg_ref, wu_ref, wd_ref,
                        gp_ref, wpg_ref, bpg_ref, wp_ref, gm_ref, h_ref, hn_ref):
    h = x_ref[...] + _mm(o_ref[...], wo_ref[...]) + bo_ref[...]
    hn = _rms(h, gf_ref[...]).astype(BF16)
    f = None
    for c in range(0, D_FF, FF_CHUNK):
        a = (_silu(_mm(hn, wg_ref[:, c:c + FF_CHUNK])) * _mm(hn, wu_ref[:, c:c + FF_CHUNK])).astype(BF16)
        part = _mm(a, wd_ref[c:c + FF_CHUNK, :])
        f = part if f is None else f + part
    h = h + f
    h = _ple(h, p_ref[...], gp_ref[...], wpg_ref[...], bpg_ref[...], wp_ref[...])
    h_ref[...] = h
    hn_ref[...] = _rms(h, gm_ref[...])


def _layer0_tail(x, o, p, wo, bo, gf, wg, wu, wd, gp, wpg, bpg, wp, gm, tm):
    t = x.shape[0]
    row = lambda i: (i, 0)
    consts = [wo, bo, gf, wg, wu, wd, gp, wpg, bpg, wp, gm]
    return pl.pallas_call(
        _layer0_tail_kernel,
        grid=(t // tm,),
        in_specs=[pl.BlockSpec((tm, D_MODEL), row), pl.BlockSpec((tm, D_MODEL), row),
                  pl.BlockSpec((tm, PLE_DIM), row)] + [_const_spec(c.shape) for c in consts],
        out_specs=[pl.BlockSpec((tm, D_MODEL), row), pl.BlockSpec((tm, D_MODEL), row)],
        out_shape=[jax.ShapeDtypeStruct((t, D_MODEL), F32), jax.ShapeDtypeStruct((t, D_MODEL), F32)],
        compiler_params=_cparams(("parallel",)),
        name="layer0_tail",
    )(x, o, p, *consts)


def _segment_ranks(sel, tile, lr_out, lrt_out, cnt_out):
    r = lax.broadcasted_iota(jnp.int32, (tile, tile), 0)
    c = lax.broadcasted_iota(jnp.int32, (tile, tile), 1)
    earlier = (c < r).astype(BF16)
    pad = jnp.full((tile, LANES - N_EXPERTS), -1.0, F32)
    for k in range(sel.shape[0] // tile):
        rows = slice(k * tile, (k + 1) * tile)
        sel_k = sel[rows]
        rank = _mm(earlier, sel_k.astype(BF16))
        lr = jnp.where(sel_k, rank, -1.0)
        lr_out[rows, :] = lr.astype(jnp.int32)
        lrt_out[:, rows] = jnp.concatenate([lr, pad], axis=1).T[:N_EXPERTS].astype(jnp.int32)
        cnt_out[k] = jnp.sum(sel_k.astype(F32), axis=0, keepdims=True).astype(jnp.int32)


def _pool_project_route(h, hn, pooled_sum_inv, wpool_ref, ps_ref, gf_ref, wr_ref, br_ref,
                        h_out, hn_out, gates_out, lr_out, lrt_out, cnt_out, tile):
    mixed = []
    for g in range(len(POOL_WINDOWS)):
        cols = slice(g * POOL_GROUP_DIM, (g + 1) * POOL_GROUP_DIM)
        pooled = pooled_sum_inv[g] - hn[:, cols]
        mixed.append(_mm(pooled.astype(BF16), wpool_ref[g]))
    h = h + jnp.concatenate(mixed, axis=-1) * ps_ref[...]
    h_out[...] = h
    hn2 = _rms(h, gf_ref[...])
    hn_out[...] = hn2.astype(BF16)
    logits = jnp.dot(hn2, wr_ref[...], precision=lax.Precision.HIGHEST,
                     preferred_element_type=F32) + br_ref[...]
    idx = lax.broadcasted_iota(jnp.int32, logits.shape, 1)
    m1 = jnp.max(logits, axis=-1, keepdims=True)
    i1 = jnp.min(jnp.where(logits == m1, idx, N_EXPERTS), axis=-1, keepdims=True)
    rest = jnp.where(idx == i1, -jnp.inf, logits)
    m2 = jnp.max(rest, axis=-1, keepdims=True)
    i2 = jnp.min(jnp.where(rest == m2, idx, N_EXPERTS), axis=-1, keepdims=True)
    e = jnp.exp(m2 - m1)
    w1 = 1.0 / (1.0 + e)
    w2 = e / (1.0 + e)
    gates_out[...] = jnp.where(idx == i1, w1, jnp.where(idx == i2, w2, 0.0))
    _segment_ranks((idx == i1) | (idx == i2), tile, lr_out, lrt_out, cnt_out)


def _route_out_specs(t, tm, tile, blk):
    specs = [pl.BlockSpec((tm, N_EXPERTS), lambda *g: (blk(*g), 0)),
             pl.BlockSpec((tm, N_EXPERTS), lambda *g: (blk(*g), 0)),
             pl.BlockSpec((N_EXPERTS, tm), lambda *g: (0, blk(*g))),
             pl.BlockSpec((tm // tile, 1, N_EXPERTS), lambda *g: (blk(*g), 0, 0))]
    shapes = [jax.ShapeDtypeStruct((t, N_EXPERTS), F32),
              jax.ShapeDtypeStruct((t, N_EXPERTS), jnp.int32),
              jax.ShapeDtypeStruct((N_EXPERTS, t), jnp.int32),
              jax.ShapeDtypeStruct((t // tile, 1, N_EXPERTS), jnp.int32)]
    return specs, shapes


def _pool_prompt_kernel(h_ref, hn_ref, wpool_ref, ps_ref, gf_ref, wr_ref, br_ref,
                        h_out, hn_out, gates_out, lr_out, lrt_out, cnt_out, carry, buf, *, tile):
    n = pl.program_id(1)
    tm = h_ref.shape[0]

    @pl.when(n == 0)
    def _():
        carry[...] = jnp.zeros_like(carry)

    hn = hn_ref[...]
    buf[0:POOL_HALO] = carry[...]
    buf[POOL_HALO:] = hn
    carry[...] = hn[tm - POOL_HALO:]
    pos = n * tm + lax.broadcasted_iota(jnp.int32, (tm, 1), 0)
    means = []
    for g, w in enumerate(POOL_WINDOWS):
        s = buf[:, g * POOL_GROUP_DIM:(g + 1) * POOL_GROUP_DIM]
        shift = 1
        while shift < w:
            s = s + pltpu.roll(s, shift, 0)
            shift *= 2
        cnt = jnp.minimum(w, pos + 1).astype(F32)
        means.append(s[POOL_HALO:] * (1.0 / cnt))
    _pool_project_route(h_ref[...], hn, means, wpool_ref, ps_ref, gf_ref, wr_ref, br_ref,
                        h_out, hn_out, gates_out, lr_out, lrt_out, cnt_out, tile)


def _pool_prompt(h, hn, wpool, ps, gf, wr, br, batch, seq, tm, tile):
    t = h.shape[0]
    ns = seq // tm
    row = lambda b, n: (b * ns + n, 0)
    consts = [wpool, ps, gf, wr, br]
    route_specs, route_shapes = _route_out_specs(t, tm, tile, lambda b, n: b * ns + n)
    return pl.pallas_call(
        functools.partial(_pool_prompt_kernel, tile=tile),
        grid=(batch, ns),
        in_specs=[pl.BlockSpec((tm, D_MODEL), row), pl.BlockSpec((tm, D_MODEL), row)]
        + [_const_spec(c.shape) for c in consts],
        out_specs=[pl.BlockSpec((tm, D_MODEL), row), pl.BlockSpec((tm, D_MODEL), row)] + route_specs,
        out_shape=[jax.ShapeDtypeStruct((t, D_MODEL), F32), jax.ShapeDtypeStruct((t, D_MODEL), BF16)]
        + route_shapes,
        scratch_shapes=[pltpu.VMEM((POOL_HALO, D_MODEL), F32),
                        pltpu.VMEM((tm + POOL_HALO, D_MODEL), F32)],
        compiler_params=_cparams(("parallel", "arbitrary")),
        name="pool_prompt",
    )(h, hn, *consts)


def _pool_sample_kernel(h_ref, hn_ref, hist_ref, wpool_ref, ps_ref, gf_ref, wr_ref, br_ref,
                        h_out, hn_out, gates_out, lr_out, lrt_out, cnt_out):
    hn = hn_ref[...]
    means = []
    for g, w in enumerate(POOL_WINDOWS):
        cols = slice(g * POOL_GROUP_DIM, (g + 1) * POOL_GROUP_DIM)
        s = hn[:, cols]
        for j in range(1, w):
            s = s + hist_ref[POOL_HIST - j, :, cols]
        means.append(s * (1.0 / min(w, PAST_LEN + 1)))
    _pool_project_route(h_ref[...], hn, means, wpool_ref, ps_ref, gf_ref, wr_ref, br_ref,
                        h_out, hn_out, gates_out, lr_out, lrt_out, cnt_out, hn.shape[0])


def _pool_sample(h, hn, hist_t, wpool, ps, gf, wr, br):
    t = h.shape[0]
    args = [h, hn, hist_t, wpool, ps, gf, wr, br]
    route_specs, route_shapes = _route_out_specs(t, t, t, lambda i: 0)
    return pl.pallas_call(
        _pool_sample_kernel,
        grid=(1,),
        in_specs=[_const_spec(a.shape) for a in args],
        out_specs=[_const_spec((t, D_MODEL)), _const_spec((t, D_MODEL))] + route_specs,
        out_shape=[jax.ShapeDtypeStruct((t, D_MODEL), F32), jax.ShapeDtypeStruct((t, D_MODEL), BF16)]
        + route_shapes,
        compiler_params=_cparams(("arbitrary",)),
        name="pool_sample",
    )(*args)


def _segment_copies(meta_ref, step, tile, vmem_of, hbm, sem_of, to_hbm):
    half = tile // 2
    out = []
    for e in range(N_EXPERTS):
        pos = meta_ref[0, step * N_EXPERTS + e]
        cnt = meta_ref[1, step * N_EXPERTS + e]
        for k, pred in ((0, cnt > 0), (1, cnt > half)):
            v = vmem_of(e).at[pl.ds(k * half, half)]
            h = hbm.at[pl.ds(pl.multiple_of(pos + k * half, SUBLANES), half)]
            copy = pltpu.make_async_copy(v, h, sem_of(e)) if to_hbm else pltpu.make_async_copy(h, v, sem_of(e))
            out.append((pred, copy))
    return out


def _for_slot(slot, fn):
    for s in range(2):
        @pl.when(slot == s)
        def _(s=s):
            fn(s)


def _start_all(copies):
    for pred, copy in copies:
        @pl.when(pred)
        def _(copy=copy):
            copy.start()


def _wait_all(copies):
    for pred, copy in copies:
        @pl.when(pred)
        def _(copy=copy):
            copy.wait()


def _dispatch_kernel(meta_ref, tail_ref, hn_ref, lrt_ref, xs_ref, seg, zeros, sem, zsem):
    i = pl.program_id(0)
    last = pl.num_programs(0) - 1
    tile = hn_ref.shape[0]
    slot = i % 2
    copies = lambda step, s: _segment_copies(meta_ref, step, tile, lambda e: seg.at[s, e], xs_ref,
                                             lambda e: sem.at[s, e], True)
    hn = hn_ref[...]
    place = lax.broadcasted_iota(jnp.int32, (tile, tile), 0)
    for e in range(N_EXPERTS):
        onehot = (place == lrt_ref[e:e + 1, :]).astype(BF16)
        seg[slot, e] = _mm(onehot, hn)

    @pl.when(i > 0)
    def _():
        _for_slot(1 - slot, lambda s: _wait_all(copies(i - 1, s)))

    _for_slot(slot, lambda s: _start_all(copies(i, s)))

    @pl.when(i == last)
    def _():
        _for_slot(slot, lambda s: _wait_all(copies(i, s)))
        zeros[...] = jnp.zeros_like(zeros)
        tails = []
        for e in range(N_EXPERTS):
            start = tail_ref[0, e]
            cnt = tail_ref[1, e]
            for piece in TAIL_PIECES:
                off = pl.multiple_of(cnt & ~(2 * piece - 1), SUBLANES)
                copy = pltpu.make_async_copy(
                    zeros.at[pl.ds(0, piece)],
                    xs_ref.at[pl.ds(pl.multiple_of(start + off, SUBLANES), piece)],
                    zsem.at[e])
                tails.append(((cnt & piece) != 0, copy))
        _start_all(tails)
        _wait_all(tails)

        def slack_copy(c):
            row = pl.multiple_of(tail_ref[0, N_EXPERTS] + c * SLACK_CHUNK, SLACK_CHUNK)
            return pltpu.make_async_copy(zeros.at[pl.ds(0, SLACK_CHUNK)],
                                         xs_ref.at[pl.ds(row, SLACK_CHUNK)], zsem.at[0])

        n_slack = tail_ref[1, N_EXPERTS] // SLACK_CHUNK
        lax.fori_loop(0, n_slack, lambda c, _: slack_copy(c).start(), None)
        lax.fori_loop(0, n_slack, lambda c, _: slack_copy(c).wait(), None)


def _dispatch(hn, lr_t, meta, tail, rows, tile):
    t = hn.shape[0]
    return pl.pallas_call(
        _dispatch_kernel,
        grid_spec=pltpu.PrefetchScalarGridSpec(
            num_scalar_prefetch=2,
            grid=(t // tile,),
            in_specs=[pl.BlockSpec((tile, D_MODEL), lambda i, *_: (i, 0)),
                      pl.BlockSpec((N_EXPERTS, tile), lambda i, *_: (0, i))],
            out_specs=pl.BlockSpec(memory_space=pl.ANY),
            scratch_shapes=[pltpu.VMEM((2, N_EXPERTS, tile, D_MODEL), F32),
                            pltpu.VMEM((TAIL_PIECES[0], D_MODEL), F32),
                            pltpu.SemaphoreType.DMA((2, N_EXPERTS)),
                            pltpu.SemaphoreType.DMA((N_EXPERTS,))]),
        out_shape=jax.ShapeDtypeStruct((rows, D_MODEL), F32),
        compiler_params=_cparams(("arbitrary",)),
        name="expert_dispatch",
    )(meta, tail, hn, lr_t)


def _expert_kernel(te_ref, tr_ref, x_ref, wg_ref, wu_ref, wd_ref, y_ref, acc):
    r = pl.program_id(0)
    j = pl.program_id(1)

    @pl.when(j == 0)
    def _():
        acc[...] = jnp.zeros_like(acc)

    @pl.when(tr_ref[r] > 0)
    def _():
        x = x_ref[...].astype(BF16)
        a = (_silu(_mm(x, wg_ref[0].astype(BF16))) * _mm(x, wu_ref[0].astype(BF16))).astype(BF16)
        acc[...] += _mm(a, wd_ref[0].astype(BF16))

    @pl.when(j == pl.num_programs(1) - 1)
    def _():
        y_ref[...] = acc[...]


def _expert_ffn(xs, tile_e, tile_rows, wg, wu, wd, tmd):
    rows = xs.shape[0]
    nj = D_FF_EXPERT // EXPERT_FF_CHUNK
    live = lambda r, tr: tr[r] > 0
    jj = lambda r, j, tr: jnp.where(live(r, tr), j, nj - 1)
    return pl.pallas_call(
        _expert_kernel,
        grid_spec=pltpu.PrefetchScalarGridSpec(
            num_scalar_prefetch=2,
            grid=(rows // tmd, nj),
            in_specs=[
                pl.BlockSpec((tmd, D_MODEL), lambda r, j, te, tr: (jnp.where(live(r, tr), r, 0), 0)),
                pl.BlockSpec((1, D_MODEL, EXPERT_FF_CHUNK), lambda r, j, te, tr: (te[r], 0, jj(r, j, tr))),
                pl.BlockSpec((1, D_MODEL, EXPERT_FF_CHUNK), lambda r, j, te, tr: (te[r], 0, jj(r, j, tr))),
                pl.BlockSpec((1, EXPERT_FF_CHUNK, D_MODEL), lambda r, j, te, tr: (te[r], jj(r, j, tr), 0)),
            ],
            out_specs=pl.BlockSpec((tmd, D_MODEL), lambda r, j, te, tr: (r, 0)),
            scratch_shapes=[pltpu.VMEM((tmd, D_MODEL), F32)]),
        out_shape=jax.ShapeDtypeStruct((rows, D_MODEL), F32),
        compiler_params=_cparams(("parallel", "arbitrary")),
        name="expert_ffn",
    )(tile_e, tile_rows, xs, wg, wu, wd)


def _combine_kernel(meta_ref, h_ref, gates_ref, lr_ref, p_ref, gp_ref, wpg_ref, bpg_ref, wp_ref,
                    gfin_ref, ys_ref, out_ref, ybuf, sem):
    i = pl.program_id(0)
    n = pl.num_programs(0)
    tile = h_ref.shape[0]
    slot = i % 2
    copies = lambda step, s: _segment_copies(meta_ref, step, tile, lambda e: ybuf.at[s, e], ys_ref,
                                             lambda e: sem.at[s, e], False)

    @pl.when(i == 0)
    def _():
        ybuf[...] = jnp.zeros_like(ybuf)
        _start_all(copies(0, 0))

    @pl.when(i + 1 < n)
    def _():
        _for_slot(1 - slot, lambda s: _start_all(copies(i + 1, s)))

    _for_slot(slot, lambda s: _wait_all(copies(i, s)))
    gates = gates_ref[...]
    lr = lr_ref[...]
    place = lax.broadcasted_iota(jnp.int32, (tile, tile), 1)
    f = jnp.zeros((tile, D_MODEL), F32)
    for e in range(N_EXPERTS):
        onehot = (place == lr[:, e:e + 1]).astype(BF16)
        f = f + gates[:, e:e + 1] * _mm(onehot, ybuf[slot, e].astype(BF16))
    h = h_ref[...] + f
    h = _ple(h, p_ref[...], gp_ref[...], wpg_ref[...], bpg_ref[...], wp_ref[...])
    out_ref[...] = _rms(h, gfin_ref[...])


def _combine(h, gates, lr, p, gp, wpg, bpg, wp, gfin, ys, meta, tile):
    t = h.shape[0]
    row = lambda i, *_: (i, 0)
    consts = [gp, wpg, bpg, wp, gfin]
    return pl.pallas_call(
        _combine_kernel,
        grid_spec=pltpu.PrefetchScalarGridSpec(
            num_scalar_prefetch=1,
            grid=(t // tile,),
            in_specs=[pl.BlockSpec((tile, D_MODEL), row), pl.BlockSpec((tile, N_EXPERTS), row),
                      pl.BlockSpec((tile, N_EXPERTS), row), pl.BlockSpec((tile, PLE_DIM), row)]
            + [_const_spec(c.shape) for c in consts] + [pl.BlockSpec(memory_space=pl.ANY)],
            out_specs=pl.BlockSpec((tile, D_MODEL), row),
            scratch_shapes=[pltpu.VMEM((2, N_EXPERTS, tile, D_MODEL), F32),
                            pltpu.SemaphoreType.DMA((2, N_EXPERTS))]),
        out_shape=jax.ShapeDtypeStruct((t, D_MODEL), F32),
        compiler_params=_cparams(("arbitrary",)),
        name="combine",
    )(meta, h, gates, lr, p, *consts, ys)


def _round_up(x, m):
    return (x + m - 1) // m * m


def _route_plan(cnt, t, tile, tmd):
    n_tiles = t // tile
    guard = tile // 2
    cnt8 = _round_up(cnt.reshape(n_tiles, N_EXPERTS), SUBLANES)
    tile_base = jnp.cumsum(cnt8, axis=0) - cnt8
    total8 = cnt8.sum(axis=0)
    region = _round_up(total8 + guard, tmd)
    region_end = jnp.cumsum(region)
    off = region_end - region
    pos = off[None, :] + tile_base
    rows = _round_up(2 * t + (SUBLANES - 1) * N_EXPERTS * n_tiles + N_EXPERTS * (tmd + guard), tmd)
    tile_start = jnp.arange(rows // tmd, dtype=jnp.int32) * tmd
    te = jnp.sum(region_end[None, :] <= tile_start[:, None], axis=1).astype(jnp.int32)
    tec = jnp.minimum(te, N_EXPERTS - 1)
    tile_rows = jnp.where(te < N_EXPERTS, jnp.clip(total8[tec] - (tile_start - off[tec]), 0, tmd), 0)
    last_e = jnp.max(jnp.where(tile_rows > 0, tec, 0))
    tile_e = jnp.where(tile_rows > 0, tec, last_e).astype(jnp.int32)
    meta = jnp.stack([pos.reshape(-1), cnt8.reshape(-1)]).astype(jnp.int32)
    tail = jnp.stack([jnp.append(off + total8, region_end[-1]),
                      jnp.append(region - total8, rows - region_end[-1])]).astype(jnp.int32)
    return dict(meta=meta, tail=tail, tile_e=tile_e, tile_rows=tile_rows.astype(jnp.int32), rows=rows)


def _moe_and_head(h, hn, route, p, w, tile, tmd):
    gates, lr, lr_t, cnt = route
    plan = _route_plan(cnt, h.shape[0], tile, tmd)
    xs = _dispatch(hn, lr_t, plan["meta"], plan["tail"], plan["rows"], tile)
    ys = _expert_ffn(xs, plan["tile_e"], plan["tile_rows"], w["exp_gate"], w["exp_up"], w["exp_down"], tmd)
    return _combine(h, gates, lr, p, w["norm_ple1"], w["ple_gate1"], w["b_ple_gate1"], w["ple1"],
                    w["norm_final"], ys, plan["meta"], tile)


def _rope_tables(pos):
    half = HEAD_DIM // 2
    inv = 1.0 / (ROPE_THETA ** (jnp.arange(half, dtype=F32) / half))
    ang = pos.astype(F32)[:, None] * inv[None, :]
    cos = jnp.tile(jnp.cos(ang), (1, LANES // half))
    sin = jnp.sin(ang)
    sin = jnp.tile(jnp.concatenate([-sin, sin], axis=-1), (1, LANES // HEAD_DIM))
    return cos, sin


def kernel(x_prompt, x_sample, cache_k, cache_v, state_pool, p_prompt, p_sample, norm_mix, norm_ffn, norm_ple, norm_final, w_qkv, b_qkv, w_o, b_o, sinks, w_pool, pool_scale, w_ff_gate, w_ff_up, w_ff_down, w_router, b_router, w_exp_gate, w_exp_up, w_exp_down, w_ple, w_ple_gate, b_ple_gate):
    batch, seq, _ = x_prompt.shape
    dec = x_sample.shape[0]
    row2 = lambda a: a.reshape(1, -1)
    w = dict(
        exp_gate=w_exp_gate[0], exp_up=w_exp_up[0], exp_down=w_exp_down[0],
        norm_ple1=row2(norm_ple[1]), ple_gate1=w_ple_gate[1].astype(BF16), b_ple_gate1=row2(b_ple_gate[1]),
        ple1=w_ple[1].astype(BF16), norm_final=row2(norm_final))
    wqkv = w_qkv[0].astype(BF16)
    l0 = [w_o[0].astype(BF16), row2(b_o[0]), row2(norm_ffn[0]), w_ff_gate[0].astype(BF16),
          w_ff_up[0].astype(BF16), w_ff_down[0].astype(BF16), row2(norm_ple[0]),
          w_ple_gate[0].astype(BF16), row2(b_ple_gate[0]), w_ple[0].astype(BF16), row2(norm_mix[1])]
    pool_w = [w_pool[0].astype(BF16), row2(pool_scale[0]), row2(norm_ffn[1]), w_router[0], row2(b_router[0])]

    xp = x_prompt.reshape(batch * seq, D_MODEL)
    cos_p, sin_p = _rope_tables(jnp.arange(seq, dtype=jnp.int32))
    q, k, v = _qkv_rope(xp, row2(norm_mix[0]), wqkv, row2(b_qkv[0]), cos_p, sin_p, tm=512)
    o = _prompt_attention(q, k, v, sinks[0], batch, seq)
    h1, hn1 = _layer0_tail(xp, o, p_prompt[0].reshape(batch * seq, PLE_DIM), *l0, tm=512)
    h2, hn2, *route = _pool_prompt(h1, hn1, *pool_w, batch=batch, seq=seq, tm=512, tile=256)
    y_prompt = _moe_and_head(h2, hn2, route, p_prompt[1].reshape(batch * seq, PLE_DIM), w, tile=256, tmd=1024)
    kv_shape = (1, batch, WINDOW, N_KV_HEADS, HEAD_DIM)
    new_k_prompt = k.reshape(batch, seq, KV_DIM)[:, -WINDOW:].reshape(kv_shape)
    new_v_prompt = v.reshape(batch, seq, KV_DIM)[:, -WINDOW:].reshape(kv_shape)
    new_pool_prompt = hn1.reshape(batch, seq, D_MODEL)[:, -POOL_HIST:][None]

    xs_ = x_sample.reshape(dec, D_MODEL)
    cos_s, sin_s = _rope_tables(jnp.full((dec,), PAST_LEN, jnp.int32))
    qs, ks, vs = _qkv_rope(xs_, row2(norm_mix[0]), wqkv, row2(b_qkv[0]), cos_s, sin_s, tm=dec)
    n_hist = cache_k.shape[2]
    k_win = jnp.concatenate([cache_k[0].reshape(dec, n_hist, KV_DIM), ks[:, None]], axis=1)[:, -n_hist:]
    v_win = jnp.concatenate([cache_v[0].reshape(dec, n_hist, KV_DIM), vs[:, None]], axis=1)[:, -n_hist:]
    head_kv = jnp.arange(N_HEADS) // GROUP
    blk = (head_kv[:, None] == jnp.arange(N_KV_HEADS)[None, :]).astype(BF16)
    q_blk = (qs.reshape(dec, N_HEADS, 1, HEAD_DIM) * blk[None, :, :, None]).reshape(dec, N_HEADS, KV_DIM)
    o_blk = _sample_attention(q_blk, k_win, v_win, sinks[0])
    o_s = jnp.take_along_axis(o_blk.reshape(dec, N_HEADS, N_KV_HEADS, HEAD_DIM),
                              head_kv[None, :, None, None], axis=2).reshape(dec, N_HEADS * HEAD_DIM)
    h1s, hn1s = _layer0_tail(xs_, o_s.astype(BF16), p_sample[0].reshape(dec, PLE_DIM), *l0, tm=dec)
    hist_t = jnp.swapaxes(state_pool[0], 0, 1)
    h2s, hn2s, *route_s = _pool_sample(h1s, hn1s, hist_t, *pool_w)
    y_sample = _moe_and_head(h2s, hn2s, route_s, p_sample[1].reshape(dec, PLE_DIM), w, tile=dec, tmd=256)
    new_k_sample = k_win.reshape(1, dec, n_hist, N_KV_HEADS, HEAD_DIM)
    new_v_sample = v_win.reshape(1, dec, n_hist, N_KV_HEADS, HEAD_DIM)
    new_pool_sample = jnp.concatenate([state_pool[0], hn1s[:, None]], axis=1)[:, -POOL_HIST:][None]

    return (y_prompt.reshape(batch, seq, D_MODEL), y_sample.reshape(dec, 1, D_MODEL),
            new_k_prompt, new_v_prompt, new_pool_prompt, new_k_sample, new_v_sample, new_pool_sample)
```

```python
import functools

import jax
import jax.numpy as jnp
from jax import lax
from jax.experimental import pallas as pl
from jax.experimental.pallas import tpu as pltpu

F32 = jnp.float32
BF16 = jnp.bfloat16

D_MODEL = 1024
HEAD_DIM = 64
N_HEADS = 16
N_KV_HEADS = 4
GROUP = N_HEADS // N_KV_HEADS
KV_DIM = N_KV_HEADS * HEAD_DIM
QKV_DIM = (N_HEADS + 2 * N_KV_HEADS) * HEAD_DIM
WINDOW = 128
ROPE_THETA = 10000.0
PAST_LEN = 16384
POOL_WINDOWS = (2, 4, 8, 16)
POOL_GROUP_DIM = D_MODEL // len(POOL_WINDOWS)
POOL_HIST = max(POOL_WINDOWS) - 1
POOL_HALO = 16
D_FF = 2816
N_EXPERTS = 8
D_FF_EXPERT = 3584
PLE_DIM = 256
EPS = 1e-6

LANES = 128
SUBLANES = 8
VMEM_LIMIT = 56 * 1024 * 1024

ATTN_TQ = 512
FF_CHUNK = 1408
EXPERT_FF_CHUNK = 512
TAIL_PIECES = (1024, 512, 256, 128, 64, 32, 16, 8)
SLACK_CHUNK = 256


def _cparams(sem):
    return pltpu.CompilerParams(dimension_semantics=sem, vmem_limit_bytes=VMEM_LIMIT)


def _const_spec(shape):
    nd = len(shape)
    return pl.BlockSpec(shape, lambda *_: (0,) * nd, pipeline_mode=pl.Buffered(1))


def _rms(x, g):
    return x * lax.rsqrt(jnp.mean(x * x, axis=-1, keepdims=True) + EPS) * g


def _mm(a, b):
    return jnp.dot(a, b, preferred_element_type=F32)


def _silu(x):
    return x * jax.nn.sigmoid(x)


def _qkv_kernel(x_ref, g_ref, w_ref, b_ref, cos_ref, sin_ref, q_ref, k_ref, v_ref):
    hn = _rms(x_ref[...], g_ref[...])
    qkv = _mm(hn.astype(BF16), w_ref[...]) + b_ref[...]
    cos = cos_ref[...]
    sin = sin_ref[...]
    lane = lax.broadcasted_iota(jnp.int32, cos.shape, 1)
    first_half = (lane % HEAD_DIM) < (HEAD_DIM // 2)

    def rope(xb):
        partner = jnp.where(first_half, pltpu.roll(xb, LANES - HEAD_DIM // 2, 1),
                            pltpu.roll(xb, HEAD_DIM // 2, 1))
        return xb * cos + partner * sin

    scale = HEAD_DIM ** -0.5
    for c in range(N_HEADS * HEAD_DIM // LANES):
        sl = slice(c * LANES, (c + 1) * LANES)
        q_ref[:, sl] = (rope(qkv[:, sl]) * scale).astype(BF16)
    for c in range(KV_DIM // LANES):
        sl = slice(c * LANES, (c + 1) * LANES)
        k_ref[:, sl] = rope(qkv[:, N_HEADS * HEAD_DIM + c * LANES:N_HEADS * HEAD_DIM + (c + 1) * LANES])
    v_ref[...] = qkv[:, N_HEADS * HEAD_DIM + KV_DIM:]


def _qkv_rope(x, g, w, b, cos, sin, tm):
    t = x.shape[0]
    n_pos_tiles = cos.shape[0] // tm
    row = lambda i: (i, 0)
    return pl.pallas_call(
        _qkv_kernel,
        grid=(t // tm,),
        in_specs=[
            pl.BlockSpec((tm, D_MODEL), row),
            _const_spec((1, D_MODEL)),
            _const_spec((D_MODEL, QKV_DIM)),
            _const_spec((1, QKV_DIM)),
            pl.BlockSpec((tm, LANES), lambda i: (i % n_pos_tiles, 0)),
            pl.BlockSpec((tm, LANES), lambda i: (i % n_pos_tiles, 0)),
        ],
        out_specs=[
            pl.BlockSpec((tm, N_HEADS * HEAD_DIM), row),
            pl.BlockSpec((tm, KV_DIM), row),
            pl.BlockSpec((tm, KV_DIM), row),
        ],
        out_shape=[
            jax.ShapeDtypeStruct((t, N_HEADS * HEAD_DIM), BF16),
            jax.ShapeDtypeStruct((t, KV_DIM), F32),
            jax.ShapeDtypeStruct((t, KV_DIM), F32),
        ],
        compiler_params=_cparams(("parallel",)),
        name="qkv_rope",
    )(x, g, w, b, cos, sin)


V_AUG = 4 * HEAD_DIM


def _attn_kernel(sink_ref, q_ref, kc_ref, kp_ref, vc_ref, vp_ref, o_ref, kbuf, vbuf, bias):
    n = pl.program_id(1)
    kbuf[0:WINDOW] = kp_ref[...].astype(BF16)
    kbuf[WINDOW:] = kc_ref[...].astype(BF16)
    v_all = jnp.concatenate([vp_ref[...], vc_ref[...]], axis=0).astype(BF16)
    pad0 = jnp.zeros((v_all.shape[0], HEAD_DIM), BF16)
    pad1 = jnp.ones((v_all.shape[0], 2 * HEAD_DIM), BF16)
    for kv in range(N_KV_HEADS):
        vbuf[:, kv * V_AUG:(kv + 1) * V_AUG] = jnp.concatenate(
            [v_all[:, kv * HEAD_DIM:(kv + 1) * HEAD_DIM], pad0, pad1], axis=1)
    shape = (GROUP * WINDOW, 2 * WINDOW)
    qrow = lax.broadcasted_iota(jnp.int32, shape, 0) & (WINDOW - 1)
    col = lax.broadcasted_iota(jnp.int32, shape, 1)
    mask_cur = (col >= WINDOW) & (col - WINDOW <= qrow)
    mask_prev = (col < WINDOW) & (col > qrow)
    bias[0] = jnp.where(mask_cur | mask_prev, 0.0, -jnp.inf)
    bias[1] = jnp.where(mask_cur, 0.0, -jnp.inf)
    for sb in range(ATTN_TQ // WINDOW):
        bias_sb = bias[jnp.where(n == 0, 1, 0)] if sb == 0 else bias[0]
        rows = slice(sb * WINDOW, (sb + 1) * WINDOW)
        for kv in range(N_KV_HEADS):
            kw = kbuf[sb * WINDOW:(sb + 2) * WINDOW, kv * HEAD_DIM:(kv + 1) * HEAD_DIM]
            vw = vbuf[sb * WINDOW:(sb + 2) * WINDOW, kv * V_AUG:(kv + 1) * V_AUG]
            heads = range(kv * GROUP, (kv + 1) * GROUP)
            qg = jnp.concatenate([q_ref[rows, h * HEAD_DIM:(h + 1) * HEAD_DIM] for h in heads], axis=0)
            s = lax.dot_general(qg, kw, (((1,), (1,)), ((), ())), preferred_element_type=F32)
            s = s + bias_sb
            p, sink_term = [], []
            for g, h in enumerate(heads):
                sh = s[g * WINDOW:(g + 1) * WINDOW]
                m = jnp.maximum(jnp.max(sh, axis=-1, keepdims=True), sink_ref[h])
                p.append(jnp.exp(sh - m).astype(BF16))
                sink_term.append(jnp.exp(sink_ref[h] - m))
            oa = _mm(jnp.concatenate(p, axis=0), vw)
            for g, h in enumerate(heads):
                oh = oa[g * WINDOW:(g + 1) * WINDOW]
                o = oh[:, :2 * HEAD_DIM] / (oh[:, 2 * HEAD_DIM:] + sink_term[g])
                o_ref[rows, h * HEAD_DIM:(h + 1) * HEAD_DIM] = o[:, :HEAD_DIM].astype(BF16)


def _prompt_attention(q, k, v, sinks, batch, seq):
    nq = seq // ATTN_TQ
    per = ATTN_TQ // WINDOW
    cur = lambda b, n: (b * nq + n, 0)
    prev = lambda b, n: (jnp.maximum(b * nq * per + n * per - 1, b * nq * per), 0)
    return pl.pallas_call(
        _attn_kernel,
        grid=(batch, nq),
        in_specs=[
            pl.BlockSpec(memory_space=pltpu.SMEM),
            pl.BlockSpec((ATTN_TQ, N_HEADS * HEAD_DIM), cur),
            pl.BlockSpec((ATTN_TQ, KV_DIM), cur),
            pl.BlockSpec((WINDOW, KV_DIM), prev),
            pl.BlockSpec((ATTN_TQ, KV_DIM), cur),
            pl.BlockSpec((WINDOW, KV_DIM), prev),
        ],
        out_specs=pl.BlockSpec((ATTN_TQ, N_HEADS * HEAD_DIM), cur),
        out_shape=jax.ShapeDtypeStruct(q.shape, BF16),
        scratch_shapes=[pltpu.VMEM((ATTN_TQ + WINDOW, KV_DIM), BF16),
                        pltpu.VMEM((ATTN_TQ + WINDOW, N_KV_HEADS * V_AUG), BF16),
                        pltpu.VMEM((2, GROUP * WINDOW, 2 * WINDOW), F32)],
        compiler_params=_cparams(("parallel", "parallel")),
        name="prompt_attention",
    )(sinks, q, k, k, v, v)


def _attn_sample_kernel(sink_ref, q_ref, k_ref, v_ref, o_ref):
    s = jnp.einsum("bhc,blc->bhl", q_ref[...], k_ref[...].astype(BF16),
                   preferred_element_type=F32)
    sink = sink_ref[...]
    m = jnp.maximum(jnp.max(s, axis=-1, keepdims=True), sink)
    p = jnp.exp(s - m)
    denom = jnp.sum(p, axis=-1, keepdims=True) + jnp.exp(sink - m)
    o_ref[...] = jnp.einsum("bhl,blc->bhc", (p / denom).astype(BF16), v_ref[...].astype(BF16),
                            preferred_element_type=F32)


def _sample_attention(q_blk, k_win, v_win, sinks, bt=16):
    b = q_blk.shape[0]
    blk = lambda i: (i, 0, 0)
    return pl.pallas_call(
        _attn_sample_kernel,
        grid=(b // bt,),
        in_specs=[
            _const_spec((1, N_HEADS, 1)),
            pl.BlockSpec((bt, N_HEADS, KV_DIM), blk),
            pl.BlockSpec((bt, WINDOW, KV_DIM), blk),
            pl.BlockSpec((bt, WINDOW, KV_DIM), blk),
        ],
        out_specs=pl.BlockSpec((bt, N_HEADS, KV_DIM), blk),
        out_shape=jax.ShapeDtypeStruct((b, N_HEADS, KV_DIM), F32),
        compiler_params=_cparams(("parallel",)),
        name="sample_attention",
    )(sinks.reshape(1, N_HEADS, 1), q_blk, k_win, v_win)


def _ple(h, p, g, wg, bg, wp):
    gate = jax.nn.sigmoid(_mm(_rms(h, g).astype(BF16), wg) + bg)
    return h + gate * _mm(p.astype(BF16), wp)


def _layer0_tail_math(x_ref, o_ref, p, wo_ref, bo_ref, gf_ref, wg_ref, wu_ref, wd_ref,
                      gp_ref, wpg_ref, bpg_ref, wp_ref, gm_ref):
    h = x_ref[...] + _mm(o_ref[...], wo_ref[...]) + bo_ref[...]
    hn = _rms(h, gf_ref[...]).astype(BF16)
    f = None
    for c in range(0, D_FF, FF_CHUNK):
        a = (_silu(_mm(hn, wg_ref[:, c:c + FF_CHUNK])) * _mm(hn, wu_ref[:, c:c + FF_CHUNK])).astype(BF16)
        part = _mm(a, wd_ref[c:c + FF_CHUNK, :])
        f = part if f is None else f + part
    h = h + f
    h = _ple(h, p, gp_ref[...], wpg_ref[...], bpg_ref[...], wp_ref[...])
    return h, _rms(h, gm_ref[...])


def _layer0_tail_kernel(x_ref, o_ref, p_ref, wo_ref, bo_ref, gf_ref, wg_ref, wu_ref, wd_ref,
                        gp_ref, wpg_ref, bpg_ref, wp_ref, gm_ref, h_ref, hn_ref):
    h_ref[...], hn_ref[...] = _layer0_tail_math(x_ref, o_ref, p_ref[0], wo_ref, bo_ref, gf_ref, wg_ref,
                                                wu_ref, wd_ref, gp_ref, wpg_ref, bpg_ref, wp_ref, gm_ref)


def _layer0_tail(x, o, p, wo, bo, gf, wg, wu, wd, gp, wpg, bpg, wp, gm, tm):
    t = x.shape[0]
    row = lambda i: (i, 0)
    consts = [wo, bo, gf, wg, wu, wd, gp, wpg, bpg, wp, gm]
    return pl.pallas_call(
        _layer0_tail_kernel,
        grid=(t // tm,),
        in_specs=[pl.BlockSpec((tm, D_MODEL), row), pl.BlockSpec((tm, D_MODEL), row),
                  pl.BlockSpec((1, tm, PLE_DIM), lambda i: (0, i, 0))] + [_const_spec(c.shape) for c in consts],
        out_specs=[pl.BlockSpec((tm, D_MODEL), row), pl.BlockSpec((tm, D_MODEL), row)],
        out_shape=[jax.ShapeDtypeStruct((t, D_MODEL), F32), jax.ShapeDtypeStruct((t, D_MODEL), F32)],
        compiler_params=_cparams(("parallel",)),
        name="layer0_tail",
    )(x, o, p, *consts)


def _segment_ranks(sel, tile, lr_out, lrt_out, cnt_out):
    r = lax.broadcasted_iota(jnp.int32, (tile, tile), 0)
    c = lax.broadcasted_iota(jnp.int32, (tile, tile), 1)
    earlier = (c < r).astype(BF16)
    pad = jnp.full((tile, LANES - N_EXPERTS), -1.0, F32)
    for k in range(sel.shape[0] // tile):
        rows = slice(k * tile, (k + 1) * tile)
        sel_k = sel[rows]
        rank = _mm(earlier, sel_k.astype(BF16))
        lr = jnp.where(sel_k, rank, -1.0)
        lr_out[rows, :] = lr.astype(jnp.int32)
        lrt_out[:, rows] = jnp.concatenate([lr, pad], axis=1).T[:N_EXPERTS].astype(jnp.int32)
        cnt_out[k] = jnp.sum(sel_k.astype(F32), axis=0, keepdims=True).astype(jnp.int32)


def _pool_project_route(h, hn, pooled_sum_inv, wpool_ref, ps_ref, gf_ref, wr_ref, br_ref,
                        h_out, hn_out, gates_out, lr_out, lrt_out, cnt_out, tile):
    mixed = []
    for g in range(len(POOL_WINDOWS)):
        cols = slice(g * POOL_GROUP_DIM, (g + 1) * POOL_GROUP_DIM)
        pooled = pooled_sum_inv[g] - hn[:, cols]
        mixed.append(_mm(pooled.astype(BF16), wpool_ref[g]))
    h = h + jnp.concatenate(mixed, axis=-1) * ps_ref[...]
    h_out[...] = h
    hn2 = _rms(h, gf_ref[...])
    hn_out[...] = hn2.astype(BF16)
    logits = jnp.dot(hn2, wr_ref[...], precision=lax.Precision.HIGHEST,
                     preferred_element_type=F32) + br_ref[...]
    idx = lax.broadcasted_iota(jnp.int32, logits.shape, 1)
    m1 = jnp.max(logits, axis=-1, keepdims=True)
    i1 = jnp.min(jnp.where(logits == m1, idx, N_EXPERTS), axis=-1, keepdims=True)
    rest = jnp.where(idx == i1, -jnp.inf, logits)
    m2 = jnp.max(rest, axis=-1, keepdims=True)
    i2 = jnp.min(jnp.where(rest == m2, idx, N_EXPERTS), axis=-1, keepdims=True)
    e = jnp.exp(m2 - m1)
    w1 = 1.0 / (1.0 + e)
    w2 = e / (1.0 + e)
    gates_out[...] = jnp.where(idx == i1, w1, jnp.where(idx == i2, w2, 0.0))
    _segment_ranks((idx == i1) | (idx == i2), tile, lr_out, lrt_out, cnt_out)


def _route_out_specs(t, tm, tile, blk):
    specs = [pl.BlockSpec((tm, N_EXPERTS), lambda *g: (blk(*g), 0)),
             pl.BlockSpec((tm, N_EXPERTS), lambda *g: (blk(*g), 0)),
             pl.BlockSpec((N_EXPERTS, tm), lambda *g: (0, blk(*g))),
             pl.BlockSpec((tm // tile, 1, N_EXPERTS), lambda *g: (blk(*g), 0, 0))]
    shapes = [jax.ShapeDtypeStruct((t, N_EXPERTS), F32),
              jax.ShapeDtypeStruct((t, N_EXPERTS), jnp.int32),
              jax.ShapeDtypeStruct((N_EXPERTS, t), jnp.int32),
              jax.ShapeDtypeStruct((t // tile, 1, N_EXPERTS), jnp.int32)]
    return specs, shapes


def _pool_prompt_kernel(h_ref, hn_ref, wpool_ref, ps_ref, gf_ref, wr_ref, br_ref,
                        h_out, hn_out, gates_out, lr_out, lrt_out, cnt_out, last_out, carry, buf, *, tile):
    b = pl.program_id(0)
    n = pl.program_id(1)
    tm = h_ref.shape[0]
    n_seq = pl.num_programs(0) - 1

    @pl.when(b < n_seq)
    def _():
        @pl.when(n == 0)
        def _():
            carry[...] = jnp.zeros_like(carry)

        hn = hn_ref[...]
        buf[0:POOL_HALO] = carry[...]
        buf[POOL_HALO:] = hn
        carry[...] = hn[tm - POOL_HALO:]
        last_out[0] = hn[tm - POOL_HALO:]
        pos = n * tm + lax.broadcasted_iota(jnp.int32, (tm, 1), 0)
        means = []
        for g, w in enumerate(POOL_WINDOWS):
            s = buf[:, g * POOL_GROUP_DIM:(g + 1) * POOL_GROUP_DIM]
            shift = 1
            while shift < w:
                s = s + pltpu.roll(s, shift, 0)
                shift *= 2
            cnt = jnp.minimum(w, pos + 1).astype(F32)
            means.append(s[POOL_HALO:] * (1.0 / cnt))
        _pool_project_route(h_ref[...], hn, means, wpool_ref, ps_ref, gf_ref, wr_ref, br_ref,
                            h_out, hn_out, gates_out, lr_out, lrt_out, cnt_out, tile)

    @pl.when(b == n_seq)
    def _():
        h_out[...] = jnp.zeros_like(h_out)
        hn_out[...] = jnp.zeros_like(hn_out)
        gates_out[...] = jnp.zeros_like(gates_out)
        lr_out[...] = jnp.full(lr_out.shape, -1, jnp.int32)
        lrt_out[...] = jnp.full(lrt_out.shape, -1, jnp.int32)
        cnt_out[...] = jnp.zeros_like(cnt_out)


def _pool_prompt(h, hn, wpool, ps, gf, wr, br, batch, seq, tm, tile, t_pad):
    ns = seq // tm
    pad_blk = batch * ns
    blk = lambda b, n: jnp.where(b < batch, b * ns + n, pad_blk)
    src = lambda b, n: (jnp.minimum(b * ns + n, pad_blk - 1), 0)
    consts = [wpool, ps, gf, wr, br]
    route_specs, route_shapes = _route_out_specs(t_pad, tm, tile, blk)
    row = lambda b, n: (blk(b, n), 0)
    return pl.pallas_call(
        functools.partial(_pool_prompt_kernel, tile=tile),
        grid=(batch + 1, ns),
        in_specs=[pl.BlockSpec((tm, D_MODEL), src), pl.BlockSpec((tm, D_MODEL), src)]
        + [_const_spec(c.shape) for c in consts],
        out_specs=[pl.BlockSpec((tm, D_MODEL), row), pl.BlockSpec((tm, D_MODEL), row)] + route_specs
        + [pl.BlockSpec((1, POOL_HALO, D_MODEL), lambda b, n: (jnp.minimum(b, batch - 1), 0, 0))],
        out_shape=[jax.ShapeDtypeStruct((t_pad, D_MODEL), F32), jax.ShapeDtypeStruct((t_pad, D_MODEL), BF16)]
        + route_shapes + [jax.ShapeDtypeStruct((batch, POOL_HALO, D_MODEL), F32)],
        scratch_shapes=[pltpu.VMEM((POOL_HALO, D_MODEL), F32),
                        pltpu.VMEM((tm + POOL_HALO, D_MODEL), F32)],
        compiler_params=_cparams(("arbitrary", "arbitrary")),
        name="pool_prompt",
    )(h, hn, *consts)


def _pool_sample_kernel(h_ref, hn_ref, hist_ref, wpool_ref, ps_ref, gf_ref, wr_ref, br_ref, *rest):
    outs = rest[len(rest) // 2:]
    hn = hn_ref[...]
    means = []
    for g, w in enumerate(POOL_WINDOWS):
        cols = slice(g * POOL_GROUP_DIM, (g + 1) * POOL_GROUP_DIM)
        s = hn[:, cols]
        for j in range(1, w):
            s = s + hist_ref[POOL_HIST - j, :, cols]
        means.append(s * (1.0 / min(w, PAST_LEN + 1)))
    _pool_project_route(h_ref[...], hn, means, wpool_ref, ps_ref, gf_ref, wr_ref, br_ref,
                        *outs, hn.shape[0])


def _pool_sample(h, hn, hist_t, wpool, ps, gf, wr, br, padded, row0, tile):
    t = h.shape[0]
    args = [h, hn, hist_t, wpool, ps, gf, wr, br]
    blk = row0 // t
    out_specs = [pl.BlockSpec((t, D_MODEL), lambda i: (blk, 0)), pl.BlockSpec((t, D_MODEL), lambda i: (blk, 0)),
                 pl.BlockSpec((t, N_EXPERTS), lambda i: (blk, 0)), pl.BlockSpec((t, N_EXPERTS), lambda i: (blk, 0)),
                 pl.BlockSpec((N_EXPERTS, t), lambda i: (0, blk)),
                 pl.BlockSpec((1, 1, N_EXPERTS), lambda i: (row0 // tile, 0, 0))]
    return pl.pallas_call(
        _pool_sample_kernel,
        grid=(1,),
        in_specs=[_const_spec(a.shape) for a in args] + [pl.BlockSpec(memory_space=pl.ANY)] * len(padded),
        out_specs=out_specs,
        out_shape=[jax.ShapeDtypeStruct(a.shape, a.dtype) for a in padded],
        input_output_aliases={len(args) + k: k for k in range(len(padded))},
        compiler_params=_cparams(("arbitrary",)),
        name="pool_sample",
    )(*args, *padded)


def _segment_copies(meta_ref, step, tile, vmem_of, hbm, sem_of, to_hbm):
    half = tile // 2
    out = []
    for e in range(N_EXPERTS):
        pos = meta_ref[0, step * N_EXPERTS + e]
        cnt = meta_ref[1, step * N_EXPERTS + e]
        for k, pred in ((0, cnt > 0), (1, cnt > half)):
            v = vmem_of(e).at[pl.ds(k * half, half)]
            h = hbm.at[pl.ds(pl.multiple_of(pos + k * half, SUBLANES), half)]
            copy = pltpu.make_async_copy(v, h, sem_of(e)) if to_hbm else pltpu.make_async_copy(h, v, sem_of(e))
            out.append((pred, copy))
    return out


def _for_slot(slot, fn):
    for s in range(2):
        @pl.when(slot == s)
        def _(s=s):
            fn(s)


def _start_all(copies):
    for pred, copy in copies:
        @pl.when(pred)
        def _(copy=copy):
            copy.start()


def _wait_all(copies):
    for pred, copy in copies:
        @pl.when(pred)
        def _(copy=copy):
            copy.wait()


def _dispatch_kernel(meta_ref, tail_ref, hn_ref, lrt_ref, xs_ref, seg, zeros, sem, zsem):
    i = pl.program_id(0)
    last = pl.num_programs(0) - 1
    tile = hn_ref.shape[0]
    slot = i % 2
    copies = lambda step, s: _segment_copies(meta_ref, step, tile, lambda e: seg.at[s, e], xs_ref,
                                             lambda e: sem.at[s, e], True)
    hn = hn_ref[...]
    half = tile // 2
    place = lax.broadcasted_iota(jnp.int32, (half, tile), 0)
    for e0 in range(0, N_EXPERTS, 4):
        first = jnp.concatenate([(place == lrt_ref[e:e + 1, :]).astype(BF16) for e in range(e0, e0 + 4)], axis=0)
        rows = _mm(first, hn)
        for k in range(4):
            seg[slot, e0 + k, 0:half] = rows[k * half:(k + 1) * half]
    for e in range(N_EXPERTS):
        @pl.when(meta_ref[1, i * N_EXPERTS + e] > half)
        def _(e=e):
            seg[slot, e, half:] = _mm((place + half == lrt_ref[e:e + 1, :]).astype(BF16), hn_ref[...])

    @pl.when(i > 0)
    def _():
        _for_slot(1 - slot, lambda s: _wait_all(copies(i - 1, s)))

    _for_slot(slot, lambda s: _start_all(copies(i, s)))

    @pl.when(i == last)
    def _():
        _for_slot(slot, lambda s: _wait_all(copies(i, s)))
        zeros[...] = jnp.zeros_like(zeros)
        tails = []
        for e in range(N_EXPERTS):
            start = tail_ref[0, e]
            cnt = tail_ref[1, e]
            for piece in TAIL_PIECES:
                off = pl.multiple_of(cnt & ~(2 * piece - 1), SUBLANES)
                copy = pltpu.make_async_copy(
                    zeros.at[pl.ds(0, piece)],
                    xs_ref.at[pl.ds(pl.multiple_of(start + off, SUBLANES), piece)],
                    zsem.at[e])
                tails.append(((cnt & piece) != 0, copy))
        _start_all(tails)
        _wait_all(tails)

        def slack_copy(c):
            row = pl.multiple_of(tail_ref[0, N_EXPERTS] + c * SLACK_CHUNK, SLACK_CHUNK)
            return pltpu.make_async_copy(zeros.at[pl.ds(0, SLACK_CHUNK)],
                                         xs_ref.at[pl.ds(row, SLACK_CHUNK)], zsem.at[0])

        n_slack = tail_ref[1, N_EXPERTS] // SLACK_CHUNK
        lax.fori_loop(0, n_slack, lambda c, _: slack_copy(c).start(), None)
        lax.fori_loop(0, n_slack, lambda c, _: slack_copy(c).wait(), None)


def _dispatch(hn, lr_t, meta, tail, rows, tile):
    t = hn.shape[0]
    return pl.pallas_call(
        _dispatch_kernel,
        grid_spec=pltpu.PrefetchScalarGridSpec(
            num_scalar_prefetch=2,
            grid=(t // tile,),
            in_specs=[pl.BlockSpec((tile, D_MODEL), lambda i, *_: (i, 0)),
                      pl.BlockSpec((N_EXPERTS, tile), lambda i, *_: (0, i))],
            out_specs=pl.BlockSpec(memory_space=pl.ANY),
            scratch_shapes=[pltpu.VMEM((2, N_EXPERTS, tile, D_MODEL), F32),
                            pltpu.VMEM((TAIL_PIECES[0], D_MODEL), F32),
                            pltpu.SemaphoreType.DMA((2, N_EXPERTS)),
                            pltpu.SemaphoreType.DMA((N_EXPERTS,))]),
        out_shape=jax.ShapeDtypeStruct((rows, D_MODEL), F32),
        compiler_params=_cparams(("arbitrary",)),
        name="expert_dispatch",
    )(meta, tail, hn, lr_t)


def _expert_kernel(te_ref, tr_ref, x_ref, wg_ref, wu_ref, wd_ref, y_ref, acc):
    r = pl.program_id(0)
    j = pl.program_id(1)

    @pl.when(j == 0)
    def _():
        acc[...] = jnp.zeros_like(acc)

    @pl.when(tr_ref[r] > 0)
    def _():
        x = x_ref[...].astype(BF16)
        a = (_silu(_mm(x, wg_ref[0].astype(BF16))) * _mm(x, wu_ref[0].astype(BF16))).astype(BF16)
        acc[...] += _mm(a, wd_ref[0].astype(BF16))

    @pl.when(j == pl.num_programs(1) - 1)
    def _():
        y_ref[...] = acc[...]


def _expert_ffn(xs, tile_e, tile_rows, wg, wu, wd, tmd):
    rows = xs.shape[0]
    nj = D_FF_EXPERT // EXPERT_FF_CHUNK
    live = lambda r, tr: tr[r] > 0
    jj = lambda r, j, tr: jnp.where(live(r, tr), j, nj - 1)
    return pl.pallas_call(
        _expert_kernel,
        grid_spec=pltpu.PrefetchScalarGridSpec(
            num_scalar_prefetch=2,
            grid=(rows // tmd, nj),
            in_specs=[
                pl.BlockSpec((tmd, D_MODEL), lambda r, j, te, tr: (jnp.where(live(r, tr), r, 0), 0)),
                pl.BlockSpec((1, D_MODEL, EXPERT_FF_CHUNK), lambda r, j, te, tr: (te[r], 0, jj(r, j, tr))),
                pl.BlockSpec((1, D_MODEL, EXPERT_FF_CHUNK), lambda r, j, te, tr: (te[r], 0, jj(r, j, tr))),
                pl.BlockSpec((1, EXPERT_FF_CHUNK, D_MODEL), lambda r, j, te, tr: (te[r], jj(r, j, tr), 0)),
            ],
            out_specs=pl.BlockSpec((tmd, D_MODEL), lambda r, j, te, tr: (r, 0)),
            scratch_shapes=[pltpu.VMEM((tmd, D_MODEL), F32)]),
        out_shape=jax.ShapeDtypeStruct((rows, D_MODEL), F32),
        compiler_params=_cparams(("parallel", "arbitrary")),
        name="expert_ffn",
    )(tile_e, tile_rows, xs, wg, wu, wd)


def _combine_kernel(meta_ref, h_ref, gates_ref, lr_ref, pp_ref, ps_ref, gp_ref, wpg_ref, bpg_ref, wp_ref,
                    gfin_ref, ys_ref, outp_ref, outs_ref, ybuf, f_ref, lo_ref, sem, *, n_prompt_tiles):
    i = pl.program_id(0)
    n = pl.num_programs(0)
    tile = h_ref.shape[0]
    slot = i % 2
    copies = lambda step, s: _segment_copies(meta_ref, step, tile, lambda e: ybuf.at[s, e], ys_ref,
                                             lambda e: sem.at[s, e], False)

    @pl.when(i == 0)
    def _():
        ybuf[...] = jnp.zeros_like(ybuf)
        _start_all(copies(0, 0))

    @pl.when(i + 1 < n)
    def _():
        _for_slot(1 - slot, lambda s: _start_all(copies(i + 1, s)))

    _for_slot(slot, lambda s: _wait_all(copies(i, s)))
    gates = gates_ref[...]
    lr = lr_ref[...]
    half = tile // 2
    place = lax.broadcasted_iota(jnp.int32, (tile, half), 1)

    def gathered(offset):
        g = jnp.concatenate([jnp.where(place + offset == lr[:, e:e + 1], gates[:, e:e + 1], 0.0)
                             for e in range(N_EXPERTS)], axis=1)
        g_hi = g.astype(BF16)
        g_lo = (g - g_hi.astype(F32)).astype(BF16)
        y = ybuf[slot, :, offset:offset + half, :].reshape(N_EXPERTS * half, D_MODEL).astype(BF16)
        return _mm(jnp.concatenate([g_hi, g_lo], axis=1), jnp.concatenate([y, y], axis=0))

    f_ref[...] = gathered(0)
    longest = meta_ref[1, i * N_EXPERTS]
    for e in range(1, N_EXPERTS):
        longest = jnp.maximum(longest, meta_ref[1, i * N_EXPERTS + e])

    @pl.when(longest > half)
    def _():
        f_ref[...] += gathered(half)

    h = h_ref[...] + f_ref[...]

    def head(h, p):
        h = _ple(h, p, gp_ref[...], wpg_ref[...], bpg_ref[...], wp_ref[...])
        return _rms(h, gfin_ref[...])

    @pl.when(i < n_prompt_tiles)
    def _():
        outp_ref[...] = head(h, pp_ref[0])

    @pl.when(i == n_prompt_tiles)
    def _():
        n_s = outs_ref.shape[0]
        outs_ref[...] = head(h[:n_s], ps_ref[0])


def _combine(h, gates, lr, pp, ps, gp, wpg, bpg, wp, gfin, ys, meta, tile, n_prompt):
    t = h.shape[0]
    n_pt = n_prompt // tile
    n_s = ps.shape[1]
    row = lambda i, *_: (i, 0)
    prow = lambda i, *_: (jnp.minimum(i, n_pt - 1), 0)
    consts = [gp, wpg, bpg, wp, gfin]
    return pl.pallas_call(
        functools.partial(_combine_kernel, n_prompt_tiles=n_pt),
        grid_spec=pltpu.PrefetchScalarGridSpec(
            num_scalar_prefetch=1,
            grid=(t // tile,),
            in_specs=[pl.BlockSpec((tile, D_MODEL), row), pl.BlockSpec((tile, N_EXPERTS), row),
                      pl.BlockSpec((tile, N_EXPERTS), row),
                      pl.BlockSpec((1, tile, PLE_DIM), lambda i, *_: (1,) + prow(i)),
                      pl.BlockSpec((1, n_s, PLE_DIM), lambda i, *_: (1, 0, 0))]
            + [_const_spec(c.shape) for c in consts] + [pl.BlockSpec(memory_space=pl.ANY)],
            out_specs=[pl.BlockSpec((tile, D_MODEL), prow),
                       pl.BlockSpec((n_s, D_MODEL), lambda i, *_: (0, 0))],
            scratch_shapes=[pltpu.VMEM((2, N_EXPERTS, tile, D_MODEL), F32),
                            pltpu.VMEM((tile, D_MODEL), F32),
                            pltpu.VMEM((tile, D_MODEL), F32),
                            pltpu.SemaphoreType.DMA((2, N_EXPERTS))]),
        out_shape=[jax.ShapeDtypeStruct((n_prompt, D_MODEL), F32),
                   jax.ShapeDtypeStruct((n_s, D_MODEL), F32)],
        compiler_params=_cparams(("arbitrary",)),
        name="combine",
    )(meta, h, gates, lr, pp, ps, *consts, ys)


def _round_up(x, m):
    return (x + m - 1) // m * m


def _route_plan(cnt, t, tile, tmd):
    n_tiles = t // tile
    guard = tile // 2
    cnt8 = _round_up(cnt.reshape(n_tiles, N_EXPERTS), SUBLANES)
    tile_base = jnp.cumsum(cnt8, axis=0) - cnt8
    total8 = cnt8.sum(axis=0)
    region = _round_up(total8 + guard, tmd)
    region_end = jnp.cumsum(region)
    off = region_end - region
    pos = off[None, :] + tile_base
    rows = _round_up(2 * t + (SUBLANES - 1) * N_EXPERTS * n_tiles + N_EXPERTS * (tmd + guard), tmd)
    tile_start = jnp.arange(rows // tmd, dtype=jnp.int32) * tmd
    te = jnp.sum(region_end[None, :] <= tile_start[:, None], axis=1).astype(jnp.int32)
    tec = jnp.minimum(te, N_EXPERTS - 1)
    tile_rows = jnp.where(te < N_EXPERTS, jnp.clip(total8[tec] - (tile_start - off[tec]), 0, tmd), 0)
    last_e = jnp.max(jnp.where(tile_rows > 0, tec, 0))
    tile_e = jnp.where(tile_rows > 0, tec, last_e).astype(jnp.int32)
    meta = jnp.stack([pos.reshape(-1), cnt8.reshape(-1)]).astype(jnp.int32)
    tail = jnp.stack([jnp.append(off + total8, region_end[-1]),
                      jnp.append(region - total8, rows - region_end[-1])]).astype(jnp.int32)
    return dict(meta=meta, tail=tail, tile_e=tile_e, tile_rows=tile_rows.astype(jnp.int32), rows=rows)


def _moe_and_head(h, hn, gates, lr, lr_t, cnt, pp, ps, w, tile, tmd, n_prompt):
    plan = _route_plan(cnt, h.shape[0], tile, tmd)
    xs = _dispatch(hn, lr_t, plan["meta"], plan["tail"], plan["rows"], tile)
    ys = _expert_ffn(xs, plan["tile_e"], plan["tile_rows"], w["exp_gate"], w["exp_up"], w["exp_down"], tmd)
    return _combine(h, gates, lr, pp, ps, w["norm_ple1"], w["ple_gate1"], w["b_ple_gate1"], w["ple1"],
                    w["norm_final"], ys, plan["meta"], tile, n_prompt)


def _rope_tables(pos):
    half = HEAD_DIM // 2
    inv = 1.0 / (ROPE_THETA ** (jnp.arange(half, dtype=F32) / half))
    ang = pos.astype(F32)[:, None] * inv[None, :]
    cos = jnp.tile(jnp.cos(ang), (1, LANES // half))
    sin = jnp.sin(ang)
    sin = jnp.tile(jnp.concatenate([-sin, sin], axis=-1), (1, LANES // HEAD_DIM))
    return cos, sin


def kernel(x_prompt, x_sample, cache_k, cache_v, state_pool, p_prompt, p_sample, norm_mix, norm_ffn, norm_ple, norm_final, w_qkv, b_qkv, w_o, b_o, sinks, w_pool, pool_scale, w_ff_gate, w_ff_up, w_ff_down, w_router, b_router, w_exp_gate, w_exp_up, w_exp_down, w_ple, w_ple_gate, b_ple_gate):
    batch, seq, _ = x_prompt.shape
    dec = x_sample.shape[0]
    row2 = lambda a: a.reshape(1, -1)
    w = dict(
        exp_gate=w_exp_gate[0], exp_up=w_exp_up[0], exp_down=w_exp_down[0],
        norm_ple1=row2(norm_ple[1]), ple_gate1=w_ple_gate[1].astype(BF16), b_ple_gate1=row2(b_ple_gate[1]),
        ple1=w_ple[1].astype(BF16), norm_final=row2(norm_final))
    wqkv = w_qkv[0].astype(BF16)
    l0 = [w_o[0].astype(BF16), row2(b_o[0]), row2(norm_ffn[0]), w_ff_gate[0].astype(BF16),
          w_ff_up[0].astype(BF16), w_ff_down[0].astype(BF16), row2(norm_ple[0]),
          w_ple_gate[0].astype(BF16), row2(b_ple_gate[0]), w_ple[0].astype(BF16), row2(norm_mix[1])]
    pool_w = [w_pool[0].astype(BF16), row2(pool_scale[0]), row2(norm_ffn[1]), w_router[0], row2(b_router[0])]

    tm, tile = 512, 256
    n_prompt = batch * seq
    t_pad = n_prompt + tm

    xp = x_prompt.reshape(n_prompt, D_MODEL)
    pp = p_prompt.reshape(p_prompt.shape[0], n_prompt, PLE_DIM)
    cos_p, sin_p = _rope_tables(jnp.arange(seq, dtype=jnp.int32))
    q, k, v = _qkv_rope(xp, row2(norm_mix[0]), wqkv, row2(b_qkv[0]), cos_p, sin_p, tm=tm)
    o = _prompt_attention(q, k, v, sinks[0], batch, seq)
    h1, hn1 = _layer0_tail(xp, o, pp, *l0, tm=tm)
    *padded, last = _pool_prompt(h1, hn1, *pool_w, batch=batch, seq=seq, tm=tm, tile=tile, t_pad=t_pad)
    kv_shape = (1, batch, WINDOW, N_KV_HEADS, HEAD_DIM)
    new_k_prompt = k.reshape(batch, seq, KV_DIM)[:, -WINDOW:].reshape(kv_shape)
    new_v_prompt = v.reshape(batch, seq, KV_DIM)[:, -WINDOW:].reshape(kv_shape)
    new_pool_prompt = last[:, -POOL_HIST:][None]

    xs_ = x_sample.reshape(dec, D_MODEL)
    ps = p_sample.reshape(p_sample.shape[0], dec, PLE_DIM)
    cos_s, sin_s = _rope_tables(jnp.full((dec,), PAST_LEN, jnp.int32))
    qs, ks, vs = _qkv_rope(xs_, row2(norm_mix[0]), wqkv, row2(b_qkv[0]), cos_s, sin_s, tm=dec)
    n_hist = cache_k.shape[2]
    k_win = jnp.concatenate([cache_k[0].reshape(dec, n_hist, KV_DIM), ks[:, None]], axis=1)[:, -n_hist:]
    v_win = jnp.concatenate([cache_v[0].reshape(dec, n_hist, KV_DIM), vs[:, None]], axis=1)[:, -n_hist:]
    head_kv = jnp.arange(N_HEADS) // GROUP
    blk = (head_kv[:, None] == jnp.arange(N_KV_HEADS)[None, :]).astype(BF16)
    q_blk = (qs.reshape(dec, N_HEADS, 1, HEAD_DIM) * blk[None, :, :, None]).reshape(dec, N_HEADS, KV_DIM)
    o_blk = _sample_attention(q_blk, k_win, v_win, sinks[0])
    o_s = jnp.take_along_axis(o_blk.reshape(dec, N_HEADS, N_KV_HEADS, HEAD_DIM),
                              head_kv[None, :, None, None], axis=2).reshape(dec, N_HEADS * HEAD_DIM)
    h1s, hn1s = _layer0_tail(xs_, o_s.astype(BF16), ps, *l0, tm=dec)
    hist_t = jnp.swapaxes(state_pool[0], 0, 1)
    padded = _pool_sample(h1s, hn1s, hist_t, *pool_w, padded=padded, row0=n_prompt, tile=tile)
    new_k_sample = k_win.reshape(1, dec, n_hist, N_KV_HEADS, HEAD_DIM)
    new_v_sample = v_win.reshape(1, dec, n_hist, N_KV_HEADS, HEAD_DIM)
    new_pool_sample = jnp.concatenate([state_pool[0], hn1s[:, None]], axis=1)[:, -POOL_HIST:][None]

    y_prompt, y_sample = _moe_and_head(*padded, pp, ps, w, tile=tile, tmd=1024, n_prompt=n_prompt)

    return (y_prompt.reshape(batch, seq, D_MODEL), y_sample.reshape(dec, 1, D_MODEL),
            new_k_prompt, new_v_prompt, new_pool_prompt, new_k_sample, new_v_sample, new_pool_sample)
```

```python
import functools

import jax
import jax.numpy as jnp
from jax import lax
from jax.experimental import pallas as pl
from jax.experimental.pallas import tpu as pltpu

F32 = jnp.float32
BF16 = jnp.bfloat16

D_MODEL = 1024
HEAD_DIM = 64
N_HEADS = 16
N_KV_HEADS = 4
GROUP = N_HEADS // N_KV_HEADS
KV_DIM = N_KV_HEADS * HEAD_DIM
QKV_DIM = (N_HEADS + 2 * N_KV_HEADS) * HEAD_DIM
WINDOW = 128
ROPE_THETA = 10000.0
PAST_LEN = 16384
POOL_WINDOWS = (2, 4, 8, 16)
POOL_GROUP_DIM = D_MODEL // len(POOL_WINDOWS)
POOL_HIST = max(POOL_WINDOWS) - 1
POOL_HALO = 16
D_FF = 2816
N_EXPERTS = 8
D_FF_EXPERT = 3584
PLE_DIM = 256
EPS = 1e-6

LANES = 128
SUBLANES = 8
VMEM_LIMIT = 56 * 1024 * 1024

ATTN_TQ = 512
ATTN_STAGE = 2
FF_CHUNK = 1408
EXPERT_FF_CHUNK = 512
TAIL_PIECES = (1024, 512, 256, 128, 64, 32, 16, 8)
SLACK_CHUNK = 256


def _cparams(sem):
    return pltpu.CompilerParams(dimension_semantics=sem, vmem_limit_bytes=VMEM_LIMIT)


def _const_spec(shape):
    nd = len(shape)
    return pl.BlockSpec(shape, lambda *_: (0,) * nd, pipeline_mode=pl.Buffered(1))


def _rms(x, g):
    return x * lax.rsqrt(jnp.mean(x * x, axis=-1, keepdims=True) + EPS) * g


def _mm(a, b):
    return jnp.dot(a, b, preferred_element_type=F32)


def _silu(x):
    return x * jax.nn.sigmoid(x)


def _qkv_kernel(x_ref, g_ref, w_ref, b_ref, cos_ref, sin_ref, q_ref, k_ref, v_ref):
    hn = _rms(x_ref[...], g_ref[...])
    qkv = _mm(hn.astype(BF16), w_ref[...]) + b_ref[...]
    cos = cos_ref[...]
    sin = sin_ref[...]
    lane = lax.broadcasted_iota(jnp.int32, cos.shape, 1)
    first_half = (lane % HEAD_DIM) < (HEAD_DIM // 2)

    def rope(xb):
        partner = jnp.where(first_half, pltpu.roll(xb, LANES - HEAD_DIM // 2, 1),
                            pltpu.roll(xb, HEAD_DIM // 2, 1))
        return xb * cos + partner * sin

    scale = HEAD_DIM ** -0.5
    for c in range(N_HEADS * HEAD_DIM // LANES):
        sl = slice(c * LANES, (c + 1) * LANES)
        q_ref[:, sl] = (rope(qkv[:, sl]) * scale).astype(BF16)
    for c in range(KV_DIM // LANES):
        sl = slice(c * LANES, (c + 1) * LANES)
        k_ref[:, sl] = rope(qkv[:, N_HEADS * HEAD_DIM + c * LANES:N_HEADS * HEAD_DIM + (c + 1) * LANES])
    v_ref[...] = qkv[:, N_HEADS * HEAD_DIM + KV_DIM:]


def _qkv_rope(x, g, w, b, cos, sin, tm):
    t = x.shape[0]
    n_pos_tiles = cos.shape[0] // tm
    row = lambda i: (i, 0)
    return pl.pallas_call(
        _qkv_kernel,
        grid=(t // tm,),
        in_specs=[
            pl.BlockSpec((tm, D_MODEL), row),
            _const_spec((1, D_MODEL)),
            _const_spec((D_MODEL, QKV_DIM)),
            _const_spec((1, QKV_DIM)),
            pl.BlockSpec((tm, LANES), lambda i: (i % n_pos_tiles, 0)),
            pl.BlockSpec((tm, LANES), lambda i: (i % n_pos_tiles, 0)),
        ],
        out_specs=[
            pl.BlockSpec((tm, N_HEADS * HEAD_DIM), row),
            pl.BlockSpec((tm, KV_DIM), row),
            pl.BlockSpec((tm, KV_DIM), row),
        ],
        out_shape=[
            jax.ShapeDtypeStruct((t, N_HEADS * HEAD_DIM), BF16),
            jax.ShapeDtypeStruct((t, KV_DIM), F32),
            jax.ShapeDtypeStruct((t, KV_DIM), F32),
        ],
        compiler_params=_cparams(("parallel",)),
        name="qkv_rope",
    )(x, g, w, b, cos, sin)


V_AUG = 4 * HEAD_DIM


def _attn_kernel(sink_ref, q_ref, kc_ref, kp_ref, vc_ref, vp_ref, o_ref, kbuf, vbuf, bias):
    n = pl.program_id(1)
    kbuf[0:WINDOW] = kp_ref[...].astype(BF16)
    kbuf[WINDOW:] = kc_ref[...].astype(BF16)
    v_all = jnp.concatenate([vp_ref[...], vc_ref[...]], axis=0).astype(BF16)
    pad0 = jnp.zeros((v_all.shape[0], HEAD_DIM), BF16)
    pad1 = jnp.ones((v_all.shape[0], 2 * HEAD_DIM), BF16)
    for kv in range(N_KV_HEADS):
        vbuf[:, kv * V_AUG:(kv + 1) * V_AUG] = jnp.concatenate(
            [v_all[:, kv * HEAD_DIM:(kv + 1) * HEAD_DIM], pad0, pad1], axis=1)
    shape = (GROUP * WINDOW, 2 * WINDOW)
    qrow = lax.broadcasted_iota(jnp.int32, shape, 0) & (WINDOW - 1)
    col = lax.broadcasted_iota(jnp.int32, shape, 1)
    mask_cur = (col >= WINDOW) & (col - WINDOW <= qrow)
    mask_prev = (col < WINDOW) & (col > qrow)
    bias[0] = jnp.where(mask_cur | mask_prev, 0.0, -jnp.inf)
    bias[1] = jnp.where(mask_cur, 0.0, -jnp.inf)
    rows = lambda sb: slice(sb * WINDOW, (sb + 1) * WINDOW)
    window = lambda sb: slice(sb * WINDOW, (sb + 2) * WINDOW)
    for first in range(0, ATTN_TQ // WINDOW, ATTN_STAGE):
        blocks = range(first, first + ATTN_STAGE)
        scores, probs, sink_term, outs = {}, {}, {}, {}
        for sb in blocks:
            for kv in range(N_KV_HEADS):
                qg = jnp.concatenate([q_ref[rows(sb), h * HEAD_DIM:(h + 1) * HEAD_DIM]
                                      for h in range(kv * GROUP, (kv + 1) * GROUP)], axis=0)
                scores[sb, kv] = lax.dot_general(qg, kbuf[window(sb), kv * HEAD_DIM:(kv + 1) * HEAD_DIM],
                                                 (((1,), (1,)), ((), ())), preferred_element_type=F32)
        for sb in blocks:
            bias_sb = bias[jnp.where(n == 0, 1, 0)] if sb == 0 else bias[0]
            for kv in range(N_KV_HEADS):
                s = scores[sb, kv] + bias_sb
                p = []
                for g in range(GROUP):
                    h = kv * GROUP + g
                    sh = s[g * WINDOW:(g + 1) * WINDOW]
                    m = jnp.maximum(jnp.max(sh, axis=-1, keepdims=True), sink_ref[h])
                    p.append(jnp.exp(sh - m).astype(BF16))
                    sink_term[sb, h] = jnp.exp(sink_ref[h] - m)
                probs[sb, kv] = jnp.concatenate(p, axis=0)
        for sb in blocks:
            for kv in range(N_KV_HEADS):
                outs[sb, kv] = _mm(probs[sb, kv], vbuf[window(sb), kv * V_AUG:(kv + 1) * V_AUG])
        for sb in blocks:
            for h in range(N_HEADS):
                oh = outs[sb, h // GROUP][(h % GROUP) * WINDOW:(h % GROUP + 1) * WINDOW]
                o = oh[:, :2 * HEAD_DIM] / (oh[:, 2 * HEAD_DIM:] + sink_term[sb, h])
                o_ref[rows(sb), h * HEAD_DIM:(h + 1) * HEAD_DIM] = o[:, :HEAD_DIM].astype(BF16)


def _prompt_attention(q, k, v, sinks, batch, seq):
    nq = seq // ATTN_TQ
    per = ATTN_TQ // WINDOW
    cur = lambda b, n: (b * nq + n, 0)
    prev = lambda b, n: (jnp.maximum(b * nq * per + n * per - 1, b * nq * per), 0)
    return pl.pallas_call(
        _attn_kernel,
        grid=(batch, nq),
        in_specs=[
            pl.BlockSpec(memory_space=pltpu.SMEM),
            pl.BlockSpec((ATTN_TQ, N_HEADS * HEAD_DIM), cur),
            pl.BlockSpec((ATTN_TQ, KV_DIM), cur),
            pl.BlockSpec((WINDOW, KV_DIM), prev),
            pl.BlockSpec((ATTN_TQ, KV_DIM), cur),
            pl.BlockSpec((WINDOW, KV_DIM), prev),
        ],
        out_specs=pl.BlockSpec((ATTN_TQ, N_HEADS * HEAD_DIM), cur),
        out_shape=jax.ShapeDtypeStruct(q.shape, BF16),
        scratch_shapes=[pltpu.VMEM((ATTN_TQ + WINDOW, KV_DIM), BF16),
                        pltpu.VMEM((ATTN_TQ + WINDOW, N_KV_HEADS * V_AUG), BF16),
                        pltpu.VMEM((2, GROUP * WINDOW, 2 * WINDOW), F32)],
        compiler_params=_cparams(("parallel", "parallel")),
        name="prompt_attention",
    )(sinks, q, k, k, v, v)


def _attn_sample_kernel(sink_ref, q_ref, k_ref, v_ref, o_ref):
    s = jnp.einsum("bhc,blc->bhl", q_ref[...], k_ref[...].astype(BF16),
                   preferred_element_type=F32)
    sink = sink_ref[...]
    m = jnp.maximum(jnp.max(s, axis=-1, keepdims=True), sink)
    p = jnp.exp(s - m)
    denom = jnp.sum(p, axis=-1, keepdims=True) + jnp.exp(sink - m)
    o_ref[...] = jnp.einsum("bhl,blc->bhc", (p / denom).astype(BF16), v_ref[...].astype(BF16),
                            preferred_element_type=F32)


def _sample_attention(q_blk, k_win, v_win, sinks, bt=16):
    b = q_blk.shape[0]
    blk = lambda i: (i, 0, 0)
    return pl.pallas_call(
        _attn_sample_kernel,
        grid=(b // bt,),
        in_specs=[
            _const_spec((1, N_HEADS, 1)),
            pl.BlockSpec((bt, N_HEADS, KV_DIM), blk),
            pl.BlockSpec((bt, WINDOW, KV_DIM), blk),
            pl.BlockSpec((bt, WINDOW, KV_DIM), blk),
        ],
        out_specs=pl.BlockSpec((bt, N_HEADS, KV_DIM), blk),
        out_shape=jax.ShapeDtypeStruct((b, N_HEADS, KV_DIM), F32),
        compiler_params=_cparams(("parallel",)),
        name="sample_attention",
    )(sinks.reshape(1, N_HEADS, 1), q_blk, k_win, v_win)


def _ple(h, p, g, wg, bg, wp):
    gate = jax.nn.sigmoid(_mm(_rms(h, g).astype(BF16), wg) + bg)
    return h + gate * _mm(p.astype(BF16), wp)


def _layer0_tail_math(x_ref, o_ref, p, wo_ref, bo_ref, gf_ref, wg_ref, wu_ref, wd_ref,
                      gp_ref, wpg_ref, bpg_ref, wp_ref, gm_ref):
    h = x_ref[...] + _mm(o_ref[...], wo_ref[...]) + bo_ref[...]
    hn = _rms(h, gf_ref[...]).astype(BF16)
    f = None
    for c in range(0, D_FF, FF_CHUNK):
        a = (_silu(_mm(hn, wg_ref[:, c:c + FF_CHUNK])) * _mm(hn, wu_ref[:, c:c + FF_CHUNK])).astype(BF16)
        part = _mm(a, wd_ref[c:c + FF_CHUNK, :])
        f = part if f is None else f + part
    h = h + f
    h = _ple(h, p, gp_ref[...], wpg_ref[...], bpg_ref[...], wp_ref[...])
    return h, _rms(h, gm_ref[...])


def _layer0_tail_kernel(x_ref, o_ref, p_ref, wo_ref, bo_ref, gf_ref, wg_ref, wu_ref, wd_ref,
                        gp_ref, wpg_ref, bpg_ref, wp_ref, gm_ref, h_ref, hn_ref):
    h_ref[...], hn_ref[...] = _layer0_tail_math(x_ref, o_ref, p_ref[0], wo_ref, bo_ref, gf_ref, wg_ref,
                                                wu_ref, wd_ref, gp_ref, wpg_ref, bpg_ref, wp_ref, gm_ref)


def _layer0_tail(x, o, p, wo, bo, gf, wg, wu, wd, gp, wpg, bpg, wp, gm, tm):
    t = x.shape[0]
    row = lambda i: (i, 0)
    consts = [wo, bo, gf, wg, wu, wd, gp, wpg, bpg, wp, gm]
    return pl.pallas_call(
        _layer0_tail_kernel,
        grid=(t // tm,),
        in_specs=[pl.BlockSpec((tm, D_MODEL), row), pl.BlockSpec((tm, D_MODEL), row),
                  pl.BlockSpec((1, tm, PLE_DIM), lambda i: (0, i, 0))] + [_const_spec(c.shape) for c in consts],
        out_specs=[pl.BlockSpec((tm, D_MODEL), row), pl.BlockSpec((tm, D_MODEL), row)],
        out_shape=[jax.ShapeDtypeStruct((t, D_MODEL), F32), jax.ShapeDtypeStruct((t, D_MODEL), F32)],
        compiler_params=_cparams(("parallel",)),
        name="layer0_tail",
    )(x, o, p, *consts)


def _segment_ranks(sel, tile, lr_out, lrt_out, cnt_out):
    r = lax.broadcasted_iota(jnp.int32, (tile, tile), 0)
    c = lax.broadcasted_iota(jnp.int32, (tile, tile), 1)
    earlier = (c < r).astype(BF16)
    pad = jnp.full((tile, LANES - N_EXPERTS), -1.0, F32)
    for k in range(sel.shape[0] // tile):
        rows = slice(k * tile, (k + 1) * tile)
        sel_k = sel[rows]
        rank = _mm(earlier, sel_k.astype(BF16))
        lr = jnp.where(sel_k, rank, -1.0)
        lr_out[rows, :] = lr.astype(jnp.int32)
        lrt_out[:, rows] = jnp.concatenate([lr, pad], axis=1).T[:N_EXPERTS].astype(jnp.int32)
        cnt_out[k] = jnp.sum(sel_k.astype(F32), axis=0, keepdims=True).astype(jnp.int32)


def _pool_project_route(h, hn, pooled_sum_inv, wpool_ref, ps_ref, gf_ref, wr_ref, br_ref,
                        h_out, hn_out, gates_out, lr_out, lrt_out, cnt_out, tile):
    mixed = []
    for g in range(len(POOL_WINDOWS)):
        cols = slice(g * POOL_GROUP_DIM, (g + 1) * POOL_GROUP_DIM)
        pooled = pooled_sum_inv[g] - hn[:, cols]
        mixed.append(_mm(pooled.astype(BF16), wpool_ref[g]))
    h = h + jnp.concatenate(mixed, axis=-1) * ps_ref[...]
    h_out[...] = h
    hn2 = _rms(h, gf_ref[...])
    hi = hn2.astype(BF16)
    hn_out[...] = hi
    lo = (hn2 - hi.astype(F32)).astype(BF16)
    by_hi = _mm(hi, wr_ref[...])
    logits = (by_hi[:, :N_EXPERTS] + by_hi[:, N_EXPERTS:] + _mm(lo, wr_ref[:, :N_EXPERTS])) + br_ref[...]
    idx = lax.broadcasted_iota(jnp.int32, logits.shape, 1)
    m1 = jnp.max(logits, axis=-1, keepdims=True)
    i1 = jnp.min(jnp.where(logits == m1, idx, N_EXPERTS), axis=-1, keepdims=True)
    rest = jnp.where(idx == i1, -jnp.inf, logits)
    m2 = jnp.max(rest, axis=-1, keepdims=True)
    i2 = jnp.min(jnp.where(rest == m2, idx, N_EXPERTS), axis=-1, keepdims=True)
    e = jnp.exp(m2 - m1)
    w1 = 1.0 / (1.0 + e)
    w2 = e / (1.0 + e)
    gates_out[...] = jnp.where(idx == i1, w1, jnp.where(idx == i2, w2, 0.0))
    _segment_ranks((idx == i1) | (idx == i2), tile, lr_out, lrt_out, cnt_out)


def _route_out_specs(t, tm, tile, blk):
    specs = [pl.BlockSpec((tm, N_EXPERTS), lambda *g: (blk(*g), 0)),
             pl.BlockSpec((tm, N_EXPERTS), lambda *g: (blk(*g), 0)),
             pl.BlockSpec((N_EXPERTS, tm), lambda *g: (0, blk(*g))),
             pl.BlockSpec((tm // tile, 1, N_EXPERTS), lambda *g: (blk(*g), 0, 0))]
    shapes = [jax.ShapeDtypeStruct((t, N_EXPERTS), F32),
              jax.ShapeDtypeStruct((t, N_EXPERTS), jnp.int32),
              jax.ShapeDtypeStruct((N_EXPERTS, t), jnp.int32),
              jax.ShapeDtypeStruct((t // tile, 1, N_EXPERTS), jnp.int32)]
    return specs, shapes


def _pool_prompt_kernel(h_ref, hn_ref, wpool_ref, ps_ref, gf_ref, wr_ref, br_ref,
                        h_out, hn_out, gates_out, lr_out, lrt_out, cnt_out, last_out, carry, buf, *, tile):
    b = pl.program_id(0)
    n = pl.program_id(1)
    tm = h_ref.shape[0]
    n_seq = pl.num_programs(0) - 1

    @pl.when(b < n_seq)
    def _():
        @pl.when(n == 0)
        def _():
            carry[...] = jnp.zeros_like(carry)

        hn = hn_ref[...]
        buf[0:POOL_HALO] = carry[...]
        buf[POOL_HALO:] = hn
        carry[...] = hn[tm - POOL_HALO:]
        last_out[0] = hn[tm - POOL_HALO:]
        pos = n * tm + lax.broadcasted_iota(jnp.int32, (tm, 1), 0)
        means = []
        for g, w in enumerate(POOL_WINDOWS):
            s = buf[:, g * POOL_GROUP_DIM:(g + 1) * POOL_GROUP_DIM]
            shift = 1
            while shift < w:
                s = s + pltpu.roll(s, shift, 0)
                shift *= 2
            cnt = jnp.minimum(w, pos + 1).astype(F32)
            means.append(s[POOL_HALO:] * (1.0 / cnt))
        _pool_project_route(h_ref[...], hn, means, wpool_ref, ps_ref, gf_ref, wr_ref, br_ref,
                            h_out, hn_out, gates_out, lr_out, lrt_out, cnt_out, tile)

    @pl.when(b == n_seq)
    def _():
        h_out[...] = jnp.zeros_like(h_out)
        hn_out[...] = jnp.zeros_like(hn_out)
        gates_out[...] = jnp.zeros_like(gates_out)
        lr_out[...] = jnp.full(lr_out.shape, -1, jnp.int32)
        lrt_out[...] = jnp.full(lrt_out.shape, -1, jnp.int32)
        cnt_out[...] = jnp.zeros_like(cnt_out)


def _pool_prompt(h, hn, wpool, ps, gf, wr, br, batch, seq, tm, tile, t_pad):
    ns = seq // tm
    pad_blk = batch * ns
    blk = lambda b, n: jnp.where(b < batch, b * ns + n, pad_blk)
    src = lambda b, n: (jnp.minimum(b * ns + n, pad_blk - 1), 0)
    consts = [wpool, ps, gf, wr, br]
    route_specs, route_shapes = _route_out_specs(t_pad, tm, tile, blk)
    row = lambda b, n: (blk(b, n), 0)
    return pl.pallas_call(
        functools.partial(_pool_prompt_kernel, tile=tile),
        grid=(batch + 1, ns),
        in_specs=[pl.BlockSpec((tm, D_MODEL), src), pl.BlockSpec((tm, D_MODEL), src)]
        + [_const_spec(c.shape) for c in consts],
        out_specs=[pl.BlockSpec((tm, D_MODEL), row), pl.BlockSpec((tm, D_MODEL), row)] + route_specs
        + [pl.BlockSpec((1, POOL_HALO, D_MODEL), lambda b, n: (jnp.minimum(b, batch - 1), 0, 0))],
        out_shape=[jax.ShapeDtypeStruct((t_pad, D_MODEL), F32), jax.ShapeDtypeStruct((t_pad, D_MODEL), BF16)]
        + route_shapes + [jax.ShapeDtypeStruct((batch, POOL_HALO, D_MODEL), F32)],
        scratch_shapes=[pltpu.VMEM((POOL_HALO, D_MODEL), F32),
                        pltpu.VMEM((tm + POOL_HALO, D_MODEL), F32)],
        compiler_params=_cparams(("arbitrary", "arbitrary")),
        name="pool_prompt",
    )(h, hn, *consts)


def _pool_sample_kernel(h_ref, hn_ref, hist_ref, wpool_ref, ps_ref, gf_ref, wr_ref, br_ref, *rest):
    outs = rest[len(rest) // 2:]
    hn = hn_ref[...]
    means = []
    for g, w in enumerate(POOL_WINDOWS):
        cols = slice(g * POOL_GROUP_DIM, (g + 1) * POOL_GROUP_DIM)
        s = hn[:, cols]
        for j in range(1, w):
            s = s + hist_ref[POOL_HIST - j, :, cols]
        means.append(s * (1.0 / min(w, PAST_LEN + 1)))
    _pool_project_route(h_ref[...], hn, means, wpool_ref, ps_ref, gf_ref, wr_ref, br_ref,
                        *outs, hn.shape[0])


def _pool_sample(h, hn, hist_t, wpool, ps, gf, wr, br, padded, row0, tile):
    t = h.shape[0]
    args = [h, hn, hist_t, wpool, ps, gf, wr, br]
    blk = row0 // t
    out_specs = [pl.BlockSpec((t, D_MODEL), lambda i: (blk, 0)), pl.BlockSpec((t, D_MODEL), lambda i: (blk, 0)),
                 pl.BlockSpec((t, N_EXPERTS), lambda i: (blk, 0)), pl.BlockSpec((t, N_EXPERTS), lambda i: (blk, 0)),
                 pl.BlockSpec((N_EXPERTS, t), lambda i: (0, blk)),
                 pl.BlockSpec((1, 1, N_EXPERTS), lambda i: (row0 // tile, 0, 0))]
    return pl.pallas_call(
        _pool_sample_kernel,
        grid=(1,),
        in_specs=[_const_spec(a.shape) for a in args] + [pl.BlockSpec(memory_space=pl.ANY)] * len(padded),
        out_specs=out_specs,
        out_shape=[jax.ShapeDtypeStruct(a.shape, a.dtype) for a in padded],
        input_output_aliases={len(args) + k: k for k in range(len(padded))},
        compiler_params=_cparams(("arbitrary",)),
        name="pool_sample",
    )(*args, *padded)


def _segment_copies(meta_ref, step, tile, vmem_of, hbm, sem_of, to_hbm):
    half = tile // 2
    out = []
    for e in range(N_EXPERTS):
        pos = meta_ref[0, step * N_EXPERTS + e]
        cnt = meta_ref[1, step * N_EXPERTS + e]
        for k, pred in ((0, cnt > 0), (1, cnt > half)):
            v = vmem_of(e).at[pl.ds(k * half, half)]
            h = hbm.at[pl.ds(pl.multiple_of(pos + k * half, SUBLANES), half)]
            copy = pltpu.make_async_copy(v, h, sem_of(e)) if to_hbm else pltpu.make_async_copy(h, v, sem_of(e))
            out.append((pred, copy))
    return out


def _for_slot(slot, fn):
    for s in range(2):
        @pl.when(slot == s)
        def _(s=s):
            fn(s)


def _start_all(copies):
    for pred, copy in copies:
        @pl.when(pred)
        def _(copy=copy):
            copy.start()


def _wait_all(copies):
    for pred, copy in copies:
        @pl.when(pred)
        def _(copy=copy):
            copy.wait()


def _dispatch_kernel(meta_ref, tail_ref, hn_ref, lrt_ref, xs_ref, seg, zeros, sem, zsem):
    i = pl.program_id(0)
    last = pl.num_programs(0) - 1
    tile = hn_ref.shape[0]
    slot = i % 2
    copies = lambda step, s: _segment_copies(meta_ref, step, tile, lambda e: seg.at[s, e], xs_ref,
                                             lambda e: sem.at[s, e], True)
    hn = hn_ref[...]
    half = tile // 2
    place = lax.broadcasted_iota(jnp.int32, (half, tile), 0)
    for e0 in range(0, N_EXPERTS, 4):
        first = jnp.concatenate([(place == lrt_ref[e:e + 1, :]).astype(BF16) for e in range(e0, e0 + 4)], axis=0)
        rows = _mm(first, hn)
        for k in range(4):
            seg[slot, e0 + k, 0:half] = rows[k * half:(k + 1) * half]
    for e in range(N_EXPERTS):
        @pl.when(meta_ref[1, i * N_EXPERTS + e] > half)
        def _(e=e):
            seg[slot, e, half:] = _mm((place + half == lrt_ref[e:e + 1, :]).astype(BF16), hn_ref[...])

    @pl.when(i > 0)
    def _():
        _for_slot(1 - slot, lambda s: _wait_all(copies(i - 1, s)))

    _for_slot(slot, lambda s: _start_all(copies(i, s)))

    @pl.when(i == last)
    def _():
        _for_slot(slot, lambda s: _wait_all(copies(i, s)))
        zeros[...] = jnp.zeros_like(zeros)
        tails = []
        for e in range(N_EXPERTS):
            start = tail_ref[0, e]
            cnt = tail_ref[1, e]
            for piece in TAIL_PIECES:
                off = pl.multiple_of(cnt & ~(2 * piece - 1), SUBLANES)
                copy = pltpu.make_async_copy(
                    zeros.at[pl.ds(0, piece)],
                    xs_ref.at[pl.ds(pl.multiple_of(start + off, SUBLANES), piece)],
                    zsem.at[e])
                tails.append(((cnt & piece) != 0, copy))
        _start_all(tails)
        _wait_all(tails)

        def slack_copy(c):
            row = pl.multiple_of(tail_ref[0, N_EXPERTS] + c * SLACK_CHUNK, SLACK_CHUNK)
            return pltpu.make_async_copy(zeros.at[pl.ds(0, SLACK_CHUNK)],
                                         xs_ref.at[pl.ds(row, SLACK_CHUNK)], zsem.at[0])

        n_slack = tail_ref[1, N_EXPERTS] // SLACK_CHUNK
        lax.fori_loop(0, n_slack, lambda c, _: slack_copy(c).start(), None)
        lax.fori_loop(0, n_slack, lambda c, _: slack_copy(c).wait(), None)


def _dispatch(hn, lr_t, meta, tail, rows, tile):
    t = hn.shape[0]
    return pl.pallas_call(
        _dispatch_kernel,
        grid_spec=pltpu.PrefetchScalarGridSpec(
            num_scalar_prefetch=2,
            grid=(t // tile,),
            in_specs=[pl.BlockSpec((tile, D_MODEL), lambda i, *_: (i, 0)),
                      pl.BlockSpec((N_EXPERTS, tile), lambda i, *_: (0, i))],
            out_specs=pl.BlockSpec(memory_space=pl.ANY),
            scratch_shapes=[pltpu.VMEM((2, N_EXPERTS, tile, D_MODEL), F32),
                            pltpu.VMEM((TAIL_PIECES[0], D_MODEL), F32),
                            pltpu.SemaphoreType.DMA((2, N_EXPERTS)),
                            pltpu.SemaphoreType.DMA((N_EXPERTS,))]),
        out_shape=jax.ShapeDtypeStruct((rows, D_MODEL), F32),
        compiler_params=_cparams(("arbitrary",)),
        name="expert_dispatch",
    )(meta, tail, hn, lr_t)


def _expert_kernel(te_ref, tr_ref, x_ref, wg_ref, wu_ref, wd_ref, y_ref, acc):
    r = pl.program_id(0)
    j = pl.program_id(1)

    @pl.when(j == 0)
    def _():
        acc[...] = jnp.zeros_like(acc)

    def ffn(m):
        x = x_ref[0:m].astype(BF16)
        a = (_silu(_mm(x, wg_ref[0].astype(BF16))) * _mm(x, wu_ref[0].astype(BF16))).astype(BF16)
        acc[0:m] += _mm(a, wd_ref[0].astype(BF16))

    rows = tr_ref[r]
    half = x_ref.shape[0] // 2

    @pl.when(rows > half)
    def _():
        ffn(2 * half)

    @pl.when((rows > 0) & (rows <= half))
    def _():
        ffn(half)

    @pl.when(j == pl.num_programs(1) - 1)
    def _():
        y_ref[...] = acc[...]


def _expert_ffn(xs, tile_e, tile_rows, wg, wu, wd, tmd):
    rows = xs.shape[0]
    nj = D_FF_EXPERT // EXPERT_FF_CHUNK
    live = lambda r, tr: tr[r] > 0
    jj = lambda r, j, tr: jnp.where(live(r, tr), j, nj - 1)
    return pl.pallas_call(
        _expert_kernel,
        grid_spec=pltpu.PrefetchScalarGridSpec(
            num_scalar_prefetch=2,
            grid=(rows // tmd, nj),
            in_specs=[
                pl.BlockSpec((tmd, D_MODEL), lambda r, j, te, tr: (jnp.where(live(r, tr), r, 0), 0)),
                pl.BlockSpec((1, D_MODEL, EXPERT_FF_CHUNK), lambda r, j, te, tr: (te[r], 0, jj(r, j, tr))),
                pl.BlockSpec((1, D_MODEL, EXPERT_FF_CHUNK), lambda r, j, te, tr: (te[r], 0, jj(r, j, tr))),
                pl.BlockSpec((1, EXPERT_FF_CHUNK, D_MODEL), lambda r, j, te, tr: (te[r], jj(r, j, tr), 0)),
            ],
            out_specs=pl.BlockSpec((tmd, D_MODEL), lambda r, j, te, tr: (r, 0)),
            scratch_shapes=[pltpu.VMEM((tmd, D_MODEL), F32)]),
        out_shape=jax.ShapeDtypeStruct((rows, D_MODEL), F32),
        compiler_params=_cparams(("parallel", "arbitrary")),
        name="expert_ffn",
    )(tile_e, tile_rows, xs, wg, wu, wd)


def _combine_kernel(meta_ref, h_ref, gates_ref, lr_ref, pp_ref, ps_ref, gp_ref, wpg_ref, bpg_ref, wp_ref,
                    gfin_ref, ys_ref, outp_ref, outs_ref, ybuf, f_ref, sem, *, n_prompt_tiles):
    i = pl.program_id(0)
    n = pl.num_programs(0)
    tile = h_ref.shape[0]
    slot = i % 2
    copies = lambda step, s: _segment_copies(meta_ref, step, tile, lambda e: ybuf.at[s, e], ys_ref,
                                             lambda e: sem.at[s, e], False)

    @pl.when(i == 0)
    def _():
        ybuf[...] = jnp.zeros_like(ybuf)
        _start_all(copies(0, 0))

    @pl.when(i + 1 < n)
    def _():
        _for_slot(1 - slot, lambda s: _start_all(copies(i + 1, s)))

    _for_slot(slot, lambda s: _wait_all(copies(i, s)))
    gates = gates_ref[...]
    lr = lr_ref[...]
    half = tile // 2
    place = lax.broadcasted_iota(jnp.int32, (tile, half), 1)

    def gathered(offset):
        f = jnp.zeros((tile, D_MODEL), F32)
        for e in range(N_EXPERTS):
            onehot = (place + offset == lr[:, e:e + 1]).astype(BF16)
            f = f + gates[:, e:e + 1] * _mm(onehot, ybuf[slot, e, offset:offset + half, :].astype(BF16))
        return f

    f_ref[...] = gathered(0)
    longest = meta_ref[1, i * N_EXPERTS]
    for e in range(1, N_EXPERTS):
        longest = jnp.maximum(longest, meta_ref[1, i * N_EXPERTS + e])

    @pl.when(longest > half)
    def _():
        f_ref[...] += gathered(half)

    h = h_ref[...] + f_ref[...]

    def head(h, p):
        h = _ple(h, p, gp_ref[...], wpg_ref[...], bpg_ref[...], wp_ref[...])
        return _rms(h, gfin_ref[...])

    @pl.when(i < n_prompt_tiles)
    def _():
        outp_ref[...] = head(h, pp_ref[0])

    @pl.when(i == n_prompt_tiles)
    def _():
        n_s = outs_ref.shape[0]
        outs_ref[...] = head(h[:n_s], ps_ref[0])


def _combine(h, gates, lr, pp, ps, gp, wpg, bpg, wp, gfin, ys, meta, tile, n_prompt):
    t = h.shape[0]
    n_pt = n_prompt // tile
    n_s = ps.shape[1]
    row = lambda i, *_: (i, 0)
    prow = lambda i, *_: (jnp.minimum(i, n_pt - 1), 0)
    consts = [gp, wpg, bpg, wp, gfin]
    return pl.pallas_call(
        functools.partial(_combine_kernel, n_prompt_tiles=n_pt),
        grid_spec=pltpu.PrefetchScalarGridSpec(
            num_scalar_prefetch=1,
            grid=(t // tile,),
            in_specs=[pl.BlockSpec((tile, D_MODEL), row), pl.BlockSpec((tile, N_EXPERTS), row),
                      pl.BlockSpec((tile, N_EXPERTS), row),
                      pl.BlockSpec((1, tile, PLE_DIM), lambda i, *_: (1,) + prow(i)),
                      pl.BlockSpec((1, n_s, PLE_DIM), lambda i, *_: (1, 0, 0))]
            + [_const_spec(c.shape) for c in consts] + [pl.BlockSpec(memory_space=pl.ANY)],
            out_specs=[pl.BlockSpec((tile, D_MODEL), prow),
                       pl.BlockSpec((n_s, D_MODEL), lambda i, *_: (0, 0))],
            scratch_shapes=[pltpu.VMEM((2, N_EXPERTS, tile, D_MODEL), F32),
                            pltpu.VMEM((tile, D_MODEL), F32),
                            pltpu.SemaphoreType.DMA((2, N_EXPERTS))]),
        out_shape=[jax.ShapeDtypeStruct((n_prompt, D_MODEL), F32),
                   jax.ShapeDtypeStruct((n_s, D_MODEL), F32)],
        compiler_params=_cparams(("arbitrary",)),
        name="combine",
    )(meta, h, gates, lr, pp, ps, *consts, ys)


def _round_up(x, m):
    return (x + m - 1) // m * m


def _route_plan(cnt, t, tile, tmd):
    n_tiles = t // tile
    guard = tile // 2
    cnt8 = _round_up(cnt.reshape(n_tiles, N_EXPERTS), SUBLANES)
    tile_base = jnp.cumsum(cnt8, axis=0) - cnt8
    total8 = cnt8.sum(axis=0)
    region = _round_up(total8 + guard, tmd)
    region_end = jnp.cumsum(region)
    off = region_end - region
    pos = off[None, :] + tile_base
    rows = _round_up(2 * t + (SUBLANES - 1) * N_EXPERTS * n_tiles + N_EXPERTS * (tmd + guard), tmd)
    tile_start = jnp.arange(rows // tmd, dtype=jnp.int32) * tmd
    te = jnp.sum(region_end[None, :] <= tile_start[:, None], axis=1).astype(jnp.int32)
    tec = jnp.minimum(te, N_EXPERTS - 1)
    tile_rows = jnp.where(te < N_EXPERTS, jnp.clip(total8[tec] - (tile_start - off[tec]), 0, tmd), 0)
    last_e = jnp.max(jnp.where(tile_rows > 0, tec, 0))
    tile_e = jnp.where(tile_rows > 0, tec, last_e).astype(jnp.int32)
    meta = jnp.stack([pos.reshape(-1), cnt8.reshape(-1)]).astype(jnp.int32)
    tail = jnp.stack([jnp.append(off + total8, region_end[-1]),
                      jnp.append(region - total8, rows - region_end[-1])]).astype(jnp.int32)
    return dict(meta=meta, tail=tail, tile_e=tile_e, tile_rows=tile_rows.astype(jnp.int32), rows=rows)


def _moe_and_head(h, hn, gates, lr, lr_t, cnt, pp, ps, w, tile, tmd, n_prompt):
    plan = _route_plan(cnt, h.shape[0], tile, tmd)
    xs = _dispatch(hn, lr_t, plan["meta"], plan["tail"], plan["rows"], tile)
    ys = _expert_ffn(xs, plan["tile_e"], plan["tile_rows"], w["exp_gate"], w["exp_up"], w["exp_down"], tmd)
    return _combine(h, gates, lr, pp, ps, w["norm_ple1"], w["ple_gate1"], w["b_ple_gate1"], w["ple1"],
                    w["norm_final"], ys, plan["meta"], tile, n_prompt)


def _rope_tables(pos):
    half = HEAD_DIM // 2
    inv = 1.0 / (ROPE_THETA ** (jnp.arange(half, dtype=F32) / half))
    ang = pos.astype(F32)[:, None] * inv[None, :]
    cos = jnp.tile(jnp.cos(ang), (1, LANES // half))
    sin = jnp.sin(ang)
    sin = jnp.tile(jnp.concatenate([-sin, sin], axis=-1), (1, LANES // HEAD_DIM))
    return cos, sin


def kernel(x_prompt, x_sample, cache_k, cache_v, state_pool, p_prompt, p_sample, norm_mix, norm_ffn, norm_ple, norm_final, w_qkv, b_qkv, w_o, b_o, sinks, w_pool, pool_scale, w_ff_gate, w_ff_up, w_ff_down, w_router, b_router, w_exp_gate, w_exp_up, w_exp_down, w_ple, w_ple_gate, b_ple_gate):
    batch, seq, _ = x_prompt.shape
    dec = x_sample.shape[0]
    row2 = lambda a: a.reshape(1, -1)
    w = dict(
        exp_gate=w_exp_gate[0], exp_up=w_exp_up[0], exp_down=w_exp_down[0],
        norm_ple1=row2(norm_ple[1]), ple_gate1=w_ple_gate[1].astype(BF16), b_ple_gate1=row2(b_ple_gate[1]),
        ple1=w_ple[1].astype(BF16), norm_final=row2(norm_final))
    wqkv = w_qkv[0].astype(BF16)
    l0 = [w_o[0].astype(BF16), row2(b_o[0]), row2(norm_ffn[0]), w_ff_gate[0].astype(BF16),
          w_ff_up[0].astype(BF16), w_ff_down[0].astype(BF16), row2(norm_ple[0]),
          w_ple_gate[0].astype(BF16), row2(b_ple_gate[0]), w_ple[0].astype(BF16), row2(norm_mix[1])]
    wr_hi = w_router[0].astype(BF16)
    wr_lo = (w_router[0] - wr_hi.astype(F32)).astype(BF16)
    pool_w = [w_pool[0].astype(BF16), row2(pool_scale[0]), row2(norm_ffn[1]),
              jnp.concatenate([wr_hi, wr_lo], axis=1), row2(b_router[0])]

    tm, tile = 512, 256
    n_prompt = batch * seq
    t_pad = n_prompt + tm

    xp = x_prompt.reshape(n_prompt, D_MODEL)
    pp = p_prompt.reshape(p_prompt.shape[0], n_prompt, PLE_DIM)
    cos_p, sin_p = _rope_tables(jnp.arange(seq, dtype=jnp.int32))
    q, k, v = _qkv_rope(xp, row2(norm_mix[0]), wqkv, row2(b_qkv[0]), cos_p, sin_p, tm=tm)
    o = _prompt_attention(q, k, v, sinks[0], batch, seq)
    h1, hn1 = _layer0_tail(xp, o, pp, *l0, tm=tm)
    *padded, last = _pool_prompt(h1, hn1, *pool_w, batch=batch, seq=seq, tm=tm, tile=tile, t_pad=t_pad)
    kv_shape = (1, batch, WINDOW, N_KV_HEADS, HEAD_DIM)
    new_k_prompt = k.reshape(batch, seq, KV_DIM)[:, -WINDOW:].reshape(kv_shape)
    new_v_prompt = v.reshape(batch, seq, KV_DIM)[:, -WINDOW:].reshape(kv_shape)
    new_pool_prompt = last[:, -POOL_HIST:][None]

    xs_ = x_sample.reshape(dec, D_MODEL)
    ps = p_sample.reshape(p_sample.shape[0], dec, PLE_DIM)
    cos_s, sin_s = _rope_tables(jnp.full((dec,), PAST_LEN, jnp.int32))
    qs, ks, vs = _qkv_rope(xs_, row2(norm_mix[0]), wqkv, row2(b_qkv[0]), cos_s, sin_s, tm=dec)
    n_hist = cache_k.shape[2]
    k_win = jnp.concatenate([cache_k[0].reshape(dec, n_hist, KV_DIM), ks[:, None]], axis=1)[:, -n_hist:]
    v_win = jnp.concatenate([cache_v[0].reshape(dec, n_hist, KV_DIM), vs[:, None]], axis=1)[:, -n_hist:]
    head_kv = jnp.arange(N_HEADS) // GROUP
    blk = (head_kv[:, None] == jnp.arange(N_KV_HEADS)[None, :]).astype(BF16)
    q_blk = (qs.reshape(dec, N_HEADS, 1, HEAD_DIM) * blk[None, :, :, None]).reshape(dec, N_HEADS, KV_DIM)
    o_blk = _sample_attention(q_blk, k_win, v_win, sinks[0])
    o_s = jnp.take_along_axis(o_blk.reshape(dec, N_HEADS, N_KV_HEADS, HEAD_DIM),
                              head_kv[None, :, None, None], axis=2).reshape(dec, N_HEADS * HEAD_DIM)
    h1s, hn1s = _layer0_tail(xs_, o_s.astype(BF16), ps, *l0, tm=dec)
    hist_t = jnp.swapaxes(state_pool[0], 0, 1)
    padded = _pool_sample(h1s, hn1s, hist_t, *pool_w, padded=padded, row0=n_prompt, tile=tile)
    new_k_sample = k_win.reshape(1, dec, n_hist, N_KV_HEADS, HEAD_DIM)
    new_v_sample = v_win.reshape(1, dec, n_hist, N_KV_HEADS, HEAD_DIM)
    new_pool_sample = jnp.concatenate([state_pool[0], hn1s[:, None]], axis=1)[:, -POOL_HIST:][None]

    y_prompt, y_sample = _moe_and_head(*padded, pp, ps, w, tile=tile, tmd=1024, n_prompt=n_prompt)

    return (y_prompt.reshape(batch, seq, D_MODEL), y_sample.reshape(dec, 1, D_MODEL),
            new_k_prompt, new_v_prompt, new_pool_prompt, new_k_sample, new_v_sample, new_pool_sample)
```

```python
import functools

import jax
import jax.numpy as jnp
from jax import lax
from jax.experimental import pallas as pl
from jax.experimental.pallas import tpu as pltpu

F32 = jnp.float32
BF16 = jnp.bfloat16

D_MODEL = 1024
HEAD_DIM = 64
N_HEADS = 16
N_KV_HEADS = 4
GROUP = N_HEADS // N_KV_HEADS
KV_DIM = N_KV_HEADS * HEAD_DIM
QKV_DIM = (N_HEADS + 2 * N_KV_HEADS) * HEAD_DIM
WINDOW = 128
ROPE_THETA = 10000.0
PAST_LEN = 16384
POOL_WINDOWS = (2, 4, 8, 16)
POOL_GROUP_DIM = D_MODEL // len(POOL_WINDOWS)
POOL_HIST = max(POOL_WINDOWS) - 1
POOL_HALO = 16
D_FF = 2816
N_EXPERTS = 8
D_FF_EXPERT = 3584
PLE_DIM = 256
EPS = 1e-6

LANES = 128
VMEM_LIMIT = 56 * 1024 * 1024

ATTN_TQ = 512
ATTN_STAGE = 2
FF_CHUNK = 1408
EXPERT_FF_CHUNK = 512
EXPERT_ROW_STEPS = 4
ROW_ALIGN = 16
TAIL_PIECES = (1024, 512, 256, 128, 64, 32, 16)
SLACK_CHUNK = 256


def _cparams(sem):
    return pltpu.CompilerParams(dimension_semantics=sem, vmem_limit_bytes=VMEM_LIMIT)


def _const_spec(shape):
    nd = len(shape)
    return pl.BlockSpec(shape, lambda *_: (0,) * nd, pipeline_mode=pl.Buffered(1))


def _rms(x, g):
    return x * lax.rsqrt(jnp.mean(x * x, axis=-1, keepdims=True) + EPS) * g


def _mm(a, b):
    return jnp.dot(a, b, preferred_element_type=F32)


def _silu(x):
    return x * jax.nn.sigmoid(x)


def _qkv_kernel(x_ref, g_ref, w_ref, b_ref, cos_ref, sin_ref, q_ref, k_ref, v_ref):
    hn = _rms(x_ref[...], g_ref[...])
    qkv = _mm(hn.astype(BF16), w_ref[...]) + b_ref[...]
    cos = cos_ref[...]
    sin = sin_ref[...]
    lane = lax.broadcasted_iota(jnp.int32, cos.shape, 1)
    first_half = (lane % HEAD_DIM) < (HEAD_DIM // 2)

    def rope(xb):
        partner = jnp.where(first_half, pltpu.roll(xb, LANES - HEAD_DIM // 2, 1),
                            pltpu.roll(xb, HEAD_DIM // 2, 1))
        return xb * cos + partner * sin

    scale = HEAD_DIM ** -0.5
    for c in range(N_HEADS * HEAD_DIM // LANES):
        sl = slice(c * LANES, (c + 1) * LANES)
        q_ref[:, sl] = (rope(qkv[:, sl]) * scale).astype(BF16)
    for c in range(KV_DIM // LANES):
        sl = slice(c * LANES, (c + 1) * LANES)
        k_ref[:, sl] = rope(qkv[:, N_HEADS * HEAD_DIM + c * LANES:N_HEADS * HEAD_DIM + (c + 1) * LANES])
    v_ref[...] = qkv[:, N_HEADS * HEAD_DIM + KV_DIM:]


def _qkv_rope(x, g, w, b, cos, sin, tm):
    t = x.shape[0]
    n_pos_tiles = cos.shape[0] // tm
    row = lambda i: (i, 0)
    return pl.pallas_call(
        _qkv_kernel,
        grid=(t // tm,),
        in_specs=[
            pl.BlockSpec((tm, D_MODEL), row),
            _const_spec((1, D_MODEL)),
            _const_spec((D_MODEL, QKV_DIM)),
            _const_spec((1, QKV_DIM)),
            pl.BlockSpec((tm, LANES), lambda i: (i % n_pos_tiles, 0)),
            pl.BlockSpec((tm, LANES), lambda i: (i % n_pos_tiles, 0)),
        ],
        out_specs=[
            pl.BlockSpec((tm, N_HEADS * HEAD_DIM), row),
            pl.BlockSpec((tm, KV_DIM), row),
            pl.BlockSpec((tm, KV_DIM), row),
        ],
        out_shape=[
            jax.ShapeDtypeStruct((t, N_HEADS * HEAD_DIM), BF16),
            jax.ShapeDtypeStruct((t, KV_DIM), F32),
            jax.ShapeDtypeStruct((t, KV_DIM), F32),
        ],
        compiler_params=_cparams(("parallel",)),
        name="qkv_rope",
    )(x, g, w, b, cos, sin)


V_AUG = 4 * HEAD_DIM


def _attn_kernel(sink_ref, q_ref, kc_ref, kp_ref, vc_ref, vp_ref, o_ref, kbuf, vbuf, bias):
    n = pl.program_id(1)
    kbuf[0:WINDOW] = kp_ref[...].astype(BF16)
    kbuf[WINDOW:] = kc_ref[...].astype(BF16)
    v_all = jnp.concatenate([vp_ref[...], vc_ref[...]], axis=0).astype(BF16)
    pad0 = jnp.zeros((v_all.shape[0], HEAD_DIM), BF16)
    pad1 = jnp.ones((v_all.shape[0], 2 * HEAD_DIM), BF16)
    for kv in range(N_KV_HEADS):
        vbuf[:, kv * V_AUG:(kv + 1) * V_AUG] = jnp.concatenate(
            [v_all[:, kv * HEAD_DIM:(kv + 1) * HEAD_DIM], pad0, pad1], axis=1)
    shape = (GROUP * WINDOW, 2 * WINDOW)
    qrow = lax.broadcasted_iota(jnp.int32, shape, 0) & (WINDOW - 1)
    col = lax.broadcasted_iota(jnp.int32, shape, 1)
    mask_cur = (col >= WINDOW) & (col - WINDOW <= qrow)
    mask_prev = (col < WINDOW) & (col > qrow)
    bias[0] = jnp.where(mask_cur | mask_prev, 0.0, -jnp.inf)
    bias[1] = jnp.where(mask_cur, 0.0, -jnp.inf)
    rows = lambda sb: slice(sb * WINDOW, (sb + 1) * WINDOW)
    window = lambda sb: slice(sb * WINDOW, (sb + 2) * WINDOW)
    for first in range(0, ATTN_TQ // WINDOW, ATTN_STAGE):
        blocks = range(first, first + ATTN_STAGE)
        scores, probs, sink_term, outs = {}, {}, {}, {}
        for sb in blocks:
            for kv in range(N_KV_HEADS):
                qg = jnp.concatenate([q_ref[rows(sb), h * HEAD_DIM:(h + 1) * HEAD_DIM]
                                      for h in range(kv * GROUP, (kv + 1) * GROUP)], axis=0)
                scores[sb, kv] = lax.dot_general(qg, kbuf[window(sb), kv * HEAD_DIM:(kv + 1) * HEAD_DIM],
                                                 (((1,), (1,)), ((), ())), preferred_element_type=F32)
        for sb in blocks:
            bias_sb = bias[jnp.where(n == 0, 1, 0)] if sb == 0 else bias[0]
            for kv in range(N_KV_HEADS):
                s = scores[sb, kv] + bias_sb
                p = []
                for g in range(GROUP):
                    h = kv * GROUP + g
                    sh = s[g * WINDOW:(g + 1) * WINDOW]
                    m = jnp.maximum(jnp.max(sh, axis=-1, keepdims=True), sink_ref[h])
                    p.append(jnp.exp(sh - m).astype(BF16))
                    sink_term[sb, h] = jnp.exp(sink_ref[h] - m)
                probs[sb, kv] = jnp.concatenate(p, axis=0)
        for sb in blocks:
            for kv in range(N_KV_HEADS):
                outs[sb, kv] = _mm(probs[sb, kv], vbuf[window(sb), kv * V_AUG:(kv + 1) * V_AUG])
        for sb in blocks:
            for h in range(N_HEADS):
                oh = outs[sb, h // GROUP][(h % GROUP) * WINDOW:(h % GROUP + 1) * WINDOW]
                o = oh[:, :2 * HEAD_DIM] / (oh[:, 2 * HEAD_DIM:] + sink_term[sb, h])
                o_ref[rows(sb), h * HEAD_DIM:(h + 1) * HEAD_DIM] = o[:, :HEAD_DIM].astype(BF16)


def _prompt_attention(q, k, v, sinks, batch, seq):
    nq = seq // ATTN_TQ
    per = ATTN_TQ // WINDOW
    cur = lambda b, n: (b * nq + n, 0)
    prev = lambda b, n: (jnp.maximum(b * nq * per + n * per - 1, b * nq * per), 0)
    return pl.pallas_call(
        _attn_kernel,
        grid=(batch, nq),
        in_specs=[
            pl.BlockSpec(memory_space=pltpu.SMEM),
            pl.BlockSpec((ATTN_TQ, N_HEADS * HEAD_DIM), cur),
            pl.BlockSpec((ATTN_TQ, KV_DIM), cur),
            pl.BlockSpec((WINDOW, KV_DIM), prev),
            pl.BlockSpec((ATTN_TQ, KV_DIM), cur),
            pl.BlockSpec((WINDOW, KV_DIM), prev),
        ],
        out_specs=pl.BlockSpec((ATTN_TQ, N_HEADS * HEAD_DIM), cur),
        out_shape=jax.ShapeDtypeStruct(q.shape, BF16),
        scratch_shapes=[pltpu.VMEM((ATTN_TQ + WINDOW, KV_DIM), BF16),
                        pltpu.VMEM((ATTN_TQ + WINDOW, N_KV_HEADS * V_AUG), BF16),
                        pltpu.VMEM((2, GROUP * WINDOW, 2 * WINDOW), F32)],
        compiler_params=_cparams(("parallel", "parallel")),
        name="prompt_attention",
    )(sinks, q, k, k, v, v)


def _attn_sample_kernel(sink_ref, q_ref, k_ref, v_ref, o_ref):
    s = jnp.einsum("bhc,blc->bhl", q_ref[...], k_ref[...].astype(BF16),
                   preferred_element_type=F32)
    sink = sink_ref[...]
    m = jnp.maximum(jnp.max(s, axis=-1, keepdims=True), sink)
    p = jnp.exp(s - m)
    denom = jnp.sum(p, axis=-1, keepdims=True) + jnp.exp(sink - m)
    o_ref[...] = jnp.einsum("bhl,blc->bhc", (p / denom).astype(BF16), v_ref[...].astype(BF16),
                            preferred_element_type=F32)


def _sample_attention(q_blk, k_win, v_win, sinks, bt=16):
    b = q_blk.shape[0]
    blk = lambda i: (i, 0, 0)
    return pl.pallas_call(
        _attn_sample_kernel,
        grid=(b // bt,),
        in_specs=[
            _const_spec((1, N_HEADS, 1)),
            pl.BlockSpec((bt, N_HEADS, KV_DIM), blk),
            pl.BlockSpec((bt, WINDOW, KV_DIM), blk),
            pl.BlockSpec((bt, WINDOW, KV_DIM), blk),
        ],
        out_specs=pl.BlockSpec((bt, N_HEADS, KV_DIM), blk),
        out_shape=jax.ShapeDtypeStruct((b, N_HEADS, KV_DIM), F32),
        compiler_params=_cparams(("parallel",)),
        name="sample_attention",
    )(sinks.reshape(1, N_HEADS, 1), q_blk, k_win, v_win)


def _ple(h, p, g, wg, bg, wp):
    gate = jax.nn.sigmoid(_mm(_rms(h, g).astype(BF16), wg) + bg)
    return h + gate * _mm(p.astype(BF16), wp)


def _layer0_tail_math(x_ref, o_ref, p, wo_ref, bo_ref, gf_ref, wg_ref, wu_ref, wd_ref,
                      gp_ref, wpg_ref, bpg_ref, wp_ref, gm_ref):
    h = x_ref[...] + _mm(o_ref[...], wo_ref[...]) + bo_ref[...]
    hn = _rms(h, gf_ref[...]).astype(BF16)
    f = None
    for c in range(0, D_FF, FF_CHUNK):
        a = (_silu(_mm(hn, wg_ref[:, c:c + FF_CHUNK])) * _mm(hn, wu_ref[:, c:c + FF_CHUNK])).astype(BF16)
        part = _mm(a, wd_ref[c:c + FF_CHUNK, :])
        f = part if f is None else f + part
    h = h + f
    h = _ple(h, p, gp_ref[...], wpg_ref[...], bpg_ref[...], wp_ref[...])
    return h, _rms(h, gm_ref[...])


def _layer0_tail_kernel(x_ref, o_ref, p_ref, wo_ref, bo_ref, gf_ref, wg_ref, wu_ref, wd_ref,
                        gp_ref, wpg_ref, bpg_ref, wp_ref, gm_ref, h_ref, hn_ref):
    h_ref[...], hn_ref[...] = _layer0_tail_math(x_ref, o_ref, p_ref[0], wo_ref, bo_ref, gf_ref, wg_ref,
                                                wu_ref, wd_ref, gp_ref, wpg_ref, bpg_ref, wp_ref, gm_ref)


def _layer0_tail(x, o, p, wo, bo, gf, wg, wu, wd, gp, wpg, bpg, wp, gm, tm):
    t = x.shape[0]
    row = lambda i: (i, 0)
    consts = [wo, bo, gf, wg, wu, wd, gp, wpg, bpg, wp, gm]
    return pl.pallas_call(
        _layer0_tail_kernel,
        grid=(t // tm,),
        in_specs=[pl.BlockSpec((tm, D_MODEL), row), pl.BlockSpec((tm, D_MODEL), row),
                  pl.BlockSpec((1, tm, PLE_DIM), lambda i: (0, i, 0))] + [_const_spec(c.shape) for c in consts],
        out_specs=[pl.BlockSpec((tm, D_MODEL), row), pl.BlockSpec((tm, D_MODEL), row)],
        out_shape=[jax.ShapeDtypeStruct((t, D_MODEL), F32), jax.ShapeDtypeStruct((t, D_MODEL), F32)],
        compiler_params=_cparams(("parallel",)),
        name="layer0_tail",
    )(x, o, p, *consts)


def _segment_ranks(sel, tile, lr_out, lrt_out, cnt_out):
    r = lax.broadcasted_iota(jnp.int32, (tile, tile), 0)
    c = lax.broadcasted_iota(jnp.int32, (tile, tile), 1)
    earlier = (c < r).astype(BF16)
    pad = jnp.full((tile, LANES - N_EXPERTS), -1.0, F32)
    for k in range(sel.shape[0] // tile):
        rows = slice(k * tile, (k + 1) * tile)
        sel_k = sel[rows]
        rank = _mm(earlier, sel_k.astype(BF16))
        lr = jnp.where(sel_k, rank, -1.0)
        lr_out[rows, :] = lr.astype(jnp.int32)
        lrt_out[:, rows] = jnp.concatenate([lr, pad], axis=1).T[:N_EXPERTS].astype(jnp.int32)
        cnt_out[k] = jnp.sum(sel_k.astype(F32), axis=0, keepdims=True).astype(jnp.int32)


def _pool_project_route(h, hn, pooled_sum_inv, wpool_ref, ps_ref, gf_ref, wr_ref, br_ref,
                        h_out, hn_out, gates_out, lr_out, lrt_out, cnt_out, tile):
    mixed = []
    for g in range(len(POOL_WINDOWS)):
        cols = slice(g * POOL_GROUP_DIM, (g + 1) * POOL_GROUP_DIM)
        pooled = pooled_sum_inv[g] - hn[:, cols]
        mixed.append(_mm(pooled.astype(BF16), wpool_ref[g]))
    h = h + jnp.concatenate(mixed, axis=-1) * ps_ref[...]
    h_out[...] = h
    hn2 = _rms(h, gf_ref[...])
    hi = hn2.astype(BF16)
    hn_out[...] = hi
    lo = (hn2 - hi.astype(F32)).astype(BF16)
    by_hi = _mm(hi, wr_ref[...])
    logits = (by_hi[:, :N_EXPERTS] + by_hi[:, N_EXPERTS:] + _mm(lo, wr_ref[:, :N_EXPERTS])) + br_ref[...]
    idx = lax.broadcasted_iota(jnp.int32, logits.shape, 1)
    m1 = jnp.max(logits, axis=-1, keepdims=True)
    i1 = jnp.min(jnp.where(logits == m1, idx, N_EXPERTS), axis=-1, keepdims=True)
    rest = jnp.where(idx == i1, -jnp.inf, logits)
    m2 = jnp.max(rest, axis=-1, keepdims=True)
    i2 = jnp.min(jnp.where(rest == m2, idx, N_EXPERTS), axis=-1, keepdims=True)
    e = jnp.exp(m2 - m1)
    w1 = 1.0 / (1.0 + e)
    w2 = e / (1.0 + e)
    gates_out[...] = jnp.where(idx == i1, w1, jnp.where(idx == i2, w2, 0.0))
    _segment_ranks((idx == i1) | (idx == i2), tile, lr_out, lrt_out, cnt_out)


def _route_out_specs(t, tm, tile, blk):
    specs = [pl.BlockSpec((tm, N_EXPERTS), lambda *g: (blk(*g), 0)),
             pl.BlockSpec((tm, N_EXPERTS), lambda *g: (blk(*g), 0)),
             pl.BlockSpec((N_EXPERTS, tm), lambda *g: (0, blk(*g))),
             pl.BlockSpec((tm // tile, 1, N_EXPERTS), lambda *g: (blk(*g), 0, 0))]
    shapes = [jax.ShapeDtypeStruct((t, N_EXPERTS), F32),
              jax.ShapeDtypeStruct((t, N_EXPERTS), jnp.int32),
              jax.ShapeDtypeStruct((N_EXPERTS, t), jnp.int32),
              jax.ShapeDtypeStruct((t // tile, 1, N_EXPERTS), jnp.int32)]
    return specs, shapes


def _pool_prompt_kernel(h_ref, hn_ref, wpool_ref, ps_ref, gf_ref, wr_ref, br_ref,
                        h_out, hn_out, gates_out, lr_out, lrt_out, cnt_out, last_out, carry, buf, *, tile):
    b = pl.program_id(0)
    n = pl.program_id(1)
    tm = h_ref.shape[0]
    n_seq = pl.num_programs(0) - 1

    @pl.when(b < n_seq)
    def _():
        @pl.when(n == 0)
        def _():
            carry[...] = jnp.zeros_like(carry)

        hn = hn_ref[...]
        buf[0:POOL_HALO] = carry[...]
        buf[POOL_HALO:] = hn
        carry[...] = hn[tm - POOL_HALO:]
        last_out[0] = hn[tm - POOL_HALO:]
        pos = n * tm + lax.broadcasted_iota(jnp.int32, (tm, 1), 0)
        means = []
        for g, w in enumerate(POOL_WINDOWS):
            s = buf[:, g * POOL_GROUP_DIM:(g + 1) * POOL_GROUP_DIM]
            shift = 1
            while shift < w:
                s = s + pltpu.roll(s, shift, 0)
                shift *= 2
            cnt = jnp.minimum(w, pos + 1).astype(F32)
            means.append(s[POOL_HALO:] * (1.0 / cnt))
        _pool_project_route(h_ref[...], hn, means, wpool_ref, ps_ref, gf_ref, wr_ref, br_ref,
                            h_out, hn_out, gates_out, lr_out, lrt_out, cnt_out, tile)

    @pl.when(b == n_seq)
    def _():
        h_out[...] = jnp.zeros_like(h_out)
        hn_out[...] = jnp.zeros_like(hn_out)
        gates_out[...] = jnp.zeros_like(gates_out)
        lr_out[...] = jnp.full(lr_out.shape, -1, jnp.int32)
        lrt_out[...] = jnp.full(lrt_out.shape, -1, jnp.int32)
        cnt_out[...] = jnp.zeros_like(cnt_out)


def _pool_prompt(h, hn, wpool, ps, gf, wr, br, batch, seq, tm, tile, t_pad):
    ns = seq // tm
    pad_blk = batch * ns
    blk = lambda b, n: jnp.where(b < batch, b * ns + n, pad_blk)
    src = lambda b, n: (jnp.minimum(b * ns + n, pad_blk - 1), 0)
    consts = [wpool, ps, gf, wr, br]
    route_specs, route_shapes = _route_out_specs(t_pad, tm, tile, blk)
    row = lambda b, n: (blk(b, n), 0)
    return pl.pallas_call(
        functools.partial(_pool_prompt_kernel, tile=tile),
        grid=(batch + 1, ns),
        in_specs=[pl.BlockSpec((tm, D_MODEL), src), pl.BlockSpec((tm, D_MODEL), src)]
        + [_const_spec(c.shape) for c in consts],
        out_specs=[pl.BlockSpec((tm, D_MODEL), row), pl.BlockSpec((tm, D_MODEL), row)] + route_specs
        + [pl.BlockSpec((1, POOL_HALO, D_MODEL), lambda b, n: (jnp.minimum(b, batch - 1), 0, 0))],
        out_shape=[jax.ShapeDtypeStruct((t_pad, D_MODEL), F32), jax.ShapeDtypeStruct((t_pad, D_MODEL), BF16)]
        + route_shapes + [jax.ShapeDtypeStruct((batch, POOL_HALO, D_MODEL), F32)],
        scratch_shapes=[pltpu.VMEM((POOL_HALO, D_MODEL), F32),
                        pltpu.VMEM((tm + POOL_HALO, D_MODEL), F32)],
        compiler_params=_cparams(("arbitrary", "arbitrary")),
        name="pool_prompt",
    )(h, hn, *consts)


def _pool_sample_kernel(h_ref, hn_ref, hist_ref, wpool_ref, ps_ref, gf_ref, wr_ref, br_ref, *rest):
    outs = rest[len(rest) // 2:]
    hn = hn_ref[...]
    means = []
    for g, w in enumerate(POOL_WINDOWS):
        cols = slice(g * POOL_GROUP_DIM, (g + 1) * POOL_GROUP_DIM)
        s = hn[:, cols]
        for j in range(1, w):
            s = s + hist_ref[POOL_HIST - j, :, cols]
        means.append(s * (1.0 / min(w, PAST_LEN + 1)))
    _pool_project_route(h_ref[...], hn, means, wpool_ref, ps_ref, gf_ref, wr_ref, br_ref,
                        *outs, hn.shape[0])


def _pool_sample(h, hn, hist_t, wpool, ps, gf, wr, br, padded, row0, tile):
    t = h.shape[0]
    args = [h, hn, hist_t, wpool, ps, gf, wr, br]
    blk = row0 // t
    out_specs = [pl.BlockSpec((t, D_MODEL), lambda i: (blk, 0)), pl.BlockSpec((t, D_MODEL), lambda i: (blk, 0)),
                 pl.BlockSpec((t, N_EXPERTS), lambda i: (blk, 0)), pl.BlockSpec((t, N_EXPERTS), lambda i: (blk, 0)),
                 pl.BlockSpec((N_EXPERTS, t), lambda i: (0, blk)),
                 pl.BlockSpec((1, 1, N_EXPERTS), lambda i: (row0 // tile, 0, 0))]
    return pl.pallas_call(
        _pool_sample_kernel,
        grid=(1,),
        in_specs=[_const_spec(a.shape) for a in args] + [pl.BlockSpec(memory_space=pl.ANY)] * len(padded),
        out_specs=out_specs,
        out_shape=[jax.ShapeDtypeStruct(a.shape, a.dtype) for a in padded],
        input_output_aliases={len(args) + k: k for k in range(len(padded))},
        compiler_params=_cparams(("arbitrary",)),
        name="pool_sample",
    )(*args, *padded)


def _segment_copies(meta_ref, step, tile, vmem_of, hbm, sem_of, to_hbm):
    half = tile // 2
    out = []
    for e in range(N_EXPERTS):
        pos = meta_ref[0, step * N_EXPERTS + e]
        cnt = meta_ref[1, step * N_EXPERTS + e]
        for k, pred in ((0, cnt > 0), (1, cnt > half)):
            v = vmem_of(e).at[pl.ds(k * half, half)]
            h = hbm.at[pl.ds(pl.multiple_of(pos + k * half, ROW_ALIGN), half)]
            copy = pltpu.make_async_copy(v, h, sem_of(e)) if to_hbm else pltpu.make_async_copy(h, v, sem_of(e))
            out.append((pred, copy))
    return out


def _for_slot(slot, fn):
    for s in range(2):
        @pl.when(slot == s)
        def _(s=s):
            fn(s)


def _start_all(copies):
    for pred, copy in copies:
        @pl.when(pred)
        def _(copy=copy):
            copy.start()


def _wait_all(copies):
    for pred, copy in copies:
        @pl.when(pred)
        def _(copy=copy):
            copy.wait()


def _dispatch_kernel(meta_ref, tail_ref, hn_ref, lrt_ref, xs_ref, seg, zeros, sem, zsem):
    i = pl.program_id(0)
    last = pl.num_programs(0) - 1
    tile = hn_ref.shape[0]
    slot = i % 2
    copies = lambda step, s: _segment_copies(meta_ref, step, tile, lambda e: seg.at[s, e], xs_ref,
                                             lambda e: sem.at[s, e], True)
    hn = hn_ref[...]
    half = tile // 2
    place = lax.broadcasted_iota(jnp.int32, (half, tile), 0)
    for e0 in range(0, N_EXPERTS, 4):
        first = jnp.concatenate([(place == lrt_ref[e:e + 1, :]).astype(BF16) for e in range(e0, e0 + 4)], axis=0)
        rows = _mm(first, hn).astype(BF16)
        for k in range(4):
            seg[slot, e0 + k, 0:half] = rows[k * half:(k + 1) * half]
    for e in range(N_EXPERTS):
        @pl.when(meta_ref[1, i * N_EXPERTS + e] > half)
        def _(e=e):
            seg[slot, e, half:] = _mm((place + half == lrt_ref[e:e + 1, :]).astype(BF16),
                                      hn_ref[...]).astype(BF16)

    @pl.when(i > 0)
    def _():
        _for_slot(1 - slot, lambda s: _wait_all(copies(i - 1, s)))

    _for_slot(slot, lambda s: _start_all(copies(i, s)))

    @pl.when(i == last)
    def _():
        _for_slot(slot, lambda s: _wait_all(copies(i, s)))
        zeros[...] = jnp.zeros_like(zeros)
        tails = []
        for e in range(N_EXPERTS):
            start = tail_ref[0, e]
            cnt = tail_ref[1, e]
            for piece in TAIL_PIECES:
                off = pl.multiple_of(cnt & ~(2 * piece - 1), ROW_ALIGN)
                copy = pltpu.make_async_copy(
                    zeros.at[pl.ds(0, piece)],
                    xs_ref.at[pl.ds(pl.multiple_of(start + off, ROW_ALIGN), piece)],
                    zsem.at[e])
                tails.append(((cnt & piece) != 0, copy))
        _start_all(tails)
        _wait_all(tails)

        def slack_copy(c):
            row = pl.multiple_of(tail_ref[0, N_EXPERTS] + c * SLACK_CHUNK, SLACK_CHUNK)
            return pltpu.make_async_copy(zeros.at[pl.ds(0, SLACK_CHUNK)],
                                         xs_ref.at[pl.ds(row, SLACK_CHUNK)], zsem.at[0])

        n_slack = tail_ref[1, N_EXPERTS] // SLACK_CHUNK
        lax.fori_loop(0, n_slack, lambda c, _: slack_copy(c).start(), None)
        lax.fori_loop(0, n_slack, lambda c, _: slack_copy(c).wait(), None)


def _dispatch(hn, lr_t, meta, tail, rows, tile):
    t = hn.shape[0]
    return pl.pallas_call(
        _dispatch_kernel,
        grid_spec=pltpu.PrefetchScalarGridSpec(
            num_scalar_prefetch=2,
            grid=(t // tile,),
            in_specs=[pl.BlockSpec((tile, D_MODEL), lambda i, *_: (i, 0)),
                      pl.BlockSpec((N_EXPERTS, tile), lambda i, *_: (0, i))],
            out_specs=pl.BlockSpec(memory_space=pl.ANY),
            scratch_shapes=[pltpu.VMEM((2, N_EXPERTS, tile, D_MODEL), BF16),
                            pltpu.VMEM((TAIL_PIECES[0], D_MODEL), BF16),
                            pltpu.SemaphoreType.DMA((2, N_EXPERTS)),
                            pltpu.SemaphoreType.DMA((N_EXPERTS,))]),
        out_shape=jax.ShapeDtypeStruct((rows, D_MODEL), BF16),
        compiler_params=_cparams(("arbitrary",)),
        name="expert_dispatch",
    )(meta, tail, hn, lr_t)


def _expert_kernel(te_ref, tr_ref, x_ref, wg_ref, wu_ref, wd_ref, y_ref, acc):
    r = pl.program_id(0)
    j = pl.program_id(1)

    @pl.when(j == 0)
    def _():
        acc[...] = jnp.zeros_like(acc)

    def ffn(m):
        x = x_ref[0:m]
        a = (_silu(_mm(x, wg_ref[0].astype(BF16))) * _mm(x, wu_ref[0].astype(BF16))).astype(BF16)
        acc[0:m] += _mm(a, wd_ref[0].astype(BF16))

    rows = tr_ref[r]
    step = x_ref.shape[0] // EXPERT_ROW_STEPS
    for k in range(1, EXPERT_ROW_STEPS + 1):
        @pl.when((rows > (k - 1) * step) & (rows <= k * step))
        def _(k=k):
            ffn(k * step)

    @pl.when(j == pl.num_programs(1) - 1)
    def _():
        y_ref[...] = acc[...].astype(BF16)


def _expert_ffn(xs, tile_e, tile_rows, wg, wu, wd, tmd):
    rows = xs.shape[0]
    nj = D_FF_EXPERT // EXPERT_FF_CHUNK
    live = lambda r, tr: tr[r] > 0
    jj = lambda r, j, tr: jnp.where(live(r, tr), j, nj - 1)
    return pl.pallas_call(
        _expert_kernel,
        grid_spec=pltpu.PrefetchScalarGridSpec(
            num_scalar_prefetch=2,
            grid=(rows // tmd, nj),
            in_specs=[
                pl.BlockSpec((tmd, D_MODEL), lambda r, j, te, tr: (jnp.where(live(r, tr), r, 0), 0)),
                pl.BlockSpec((1, D_MODEL, EXPERT_FF_CHUNK), lambda r, j, te, tr: (te[r], 0, jj(r, j, tr))),
                pl.BlockSpec((1, D_MODEL, EXPERT_FF_CHUNK), lambda r, j, te, tr: (te[r], 0, jj(r, j, tr))),
                pl.BlockSpec((1, EXPERT_FF_CHUNK, D_MODEL), lambda r, j, te, tr: (te[r], jj(r, j, tr), 0)),
            ],
            out_specs=pl.BlockSpec((tmd, D_MODEL), lambda r, j, te, tr: (r, 0)),
            scratch_shapes=[pltpu.VMEM((tmd, D_MODEL), F32)]),
        out_shape=jax.ShapeDtypeStruct((rows, D_MODEL), BF16),
        compiler_params=_cparams(("parallel", "arbitrary")),
        name="expert_ffn",
    )(tile_e, tile_rows, xs, wg, wu, wd)


def _combine_kernel(meta_ref, h_ref, gates_ref, lr_ref, pp_ref, ps_ref, gp_ref, wpg_ref, bpg_ref, wp_ref,
                    gfin_ref, ys_ref, outp_ref, outs_ref, ybuf, f_ref, sem, *, n_prompt_tiles):
    i = pl.program_id(0)
    n = pl.num_programs(0)
    tile = h_ref.shape[0]
    slot = i % 2
    copies = lambda step, s: _segment_copies(meta_ref, step, tile, lambda e: ybuf.at[s, e], ys_ref,
                                             lambda e: sem.at[s, e], False)

    @pl.when(i == 0)
    def _():
        ybuf[...] = jnp.zeros_like(ybuf)
        _start_all(copies(0, 0))

    @pl.when(i + 1 < n)
    def _():
        _for_slot(1 - slot, lambda s: _start_all(copies(i + 1, s)))

    _for_slot(slot, lambda s: _wait_all(copies(i, s)))
    gates = gates_ref[...]
    lr = lr_ref[...]
    half = tile // 2
    place = lax.broadcasted_iota(jnp.int32, (tile, half), 1)

    def gathered(offset):
        f = jnp.zeros((tile, D_MODEL), F32)
        for e in range(N_EXPERTS):
            onehot = (place + offset == lr[:, e:e + 1]).astype(BF16)
            f = f + gates[:, e:e + 1] * _mm(onehot, ybuf[slot, e, offset:offset + half, :])
        return f

    f_ref[...] = gathered(0)
    longest = meta_ref[1, i * N_EXPERTS]
    for e in range(1, N_EXPERTS):
        longest = jnp.maximum(longest, meta_ref[1, i * N_EXPERTS + e])

    @pl.when(longest > half)
    def _():
        f_ref[...] += gathered(half)

    h = h_ref[...] + f_ref[...]

    def head(h, p):
        h = _ple(h, p, gp_ref[...], wpg_ref[...], bpg_ref[...], wp_ref[...])
        return _rms(h, gfin_ref[...])

    @pl.when(i < n_prompt_tiles)
    def _():
        outp_ref[...] = head(h, pp_ref[0])

    @pl.when(i == n_prompt_tiles)
    def _():
        n_s = outs_ref.shape[0]
        outs_ref[...] = head(h[:n_s], ps_ref[0])


def _combine(h, gates, lr, pp, ps, gp, wpg, bpg, wp, gfin, ys, meta, tile, n_prompt):
    t = h.shape[0]
    n_pt = n_prompt // tile
    n_s = ps.shape[1]
    row = lambda i, *_: (i, 0)
    prow = lambda i, *_: (jnp.minimum(i, n_pt - 1), 0)
    consts = [gp, wpg, bpg, wp, gfin]
    return pl.pallas_call(
        functools.partial(_combine_kernel, n_prompt_tiles=n_pt),
        grid_spec=pltpu.PrefetchScalarGridSpec(
            num_scalar_prefetch=1,
            grid=(t // tile,),
            in_specs=[pl.BlockSpec((tile, D_MODEL), row), pl.BlockSpec((tile, N_EXPERTS), row),
                      pl.BlockSpec((tile, N_EXPERTS), row),
                      pl.BlockSpec((1, tile, PLE_DIM), lambda i, *_: (1,) + prow(i)),
                      pl.BlockSpec((1, n_s, PLE_DIM), lambda i, *_: (1, 0, 0))]
            + [_const_spec(c.shape) for c in consts] + [pl.BlockSpec(memory_space=pl.ANY)],
            out_specs=[pl.BlockSpec((tile, D_MODEL), prow),
                       pl.BlockSpec((n_s, D_MODEL), lambda i, *_: (0, 0))],
            scratch_shapes=[pltpu.VMEM((2, N_EXPERTS, tile, D_MODEL), BF16),
                            pltpu.VMEM((tile, D_MODEL), F32),
                            pltpu.SemaphoreType.DMA((2, N_EXPERTS))]),
        out_shape=[jax.ShapeDtypeStruct((n_prompt, D_MODEL), F32),
                   jax.ShapeDtypeStruct((n_s, D_MODEL), F32)],
        compiler_params=_cparams(("arbitrary",)),
        name="combine",
    )(meta, h, gates, lr, pp, ps, *consts, ys)


def _round_up(x, m):
    return (x + m - 1) // m * m


def _route_plan(cnt, t, tile, tmd):
    n_tiles = t // tile
    guard = tile // 2
    cnt8 = _round_up(cnt.reshape(n_tiles, N_EXPERTS), ROW_ALIGN)
    tile_base = jnp.cumsum(cnt8, axis=0) - cnt8
    total8 = cnt8.sum(axis=0)
    region = _round_up(total8 + guard, tmd)
    region_end = jnp.cumsum(region)
    off = region_end - region
    pos = off[None, :] + tile_base
    rows = _round_up(2 * t + (ROW_ALIGN - 1) * N_EXPERTS * n_tiles + N_EXPERTS * (tmd + guard), tmd)
    tile_start = jnp.arange(rows // tmd, dtype=jnp.int32) * tmd
    te = jnp.sum(region_end[None, :] <= tile_start[:, None], axis=1).astype(jnp.int32)
    tec = jnp.minimum(te, N_EXPERTS - 1)
    tile_rows = jnp.where(te < N_EXPERTS, jnp.clip(total8[tec] - (tile_start - off[tec]), 0, tmd), 0)
    last_e = jnp.max(jnp.where(tile_rows > 0, tec, 0))
    tile_e = jnp.where(tile_rows > 0, tec, last_e).astype(jnp.int32)
    meta = jnp.stack([pos.reshape(-1), cnt8.reshape(-1)]).astype(jnp.int32)
    tail = jnp.stack([jnp.append(off + total8, region_end[-1]),
                      jnp.append(region - total8, rows - region_end[-1])]).astype(jnp.int32)
    return dict(meta=meta, tail=tail, tile_e=tile_e, tile_rows=tile_rows.astype(jnp.int32), rows=rows)


def _moe_and_head(h, hn, gates, lr, lr_t, cnt, pp, ps, w, tile, tmd, n_prompt):
    plan = _route_plan(cnt, h.shape[0], tile, tmd)
    xs = _dispatch(hn, lr_t, plan["meta"], plan["tail"], plan["rows"], tile)
    ys = _expert_ffn(xs, plan["tile_e"], plan["tile_rows"], w["exp_gate"], w["exp_up"], w["exp_down"], tmd)
    return _combine(h, gates, lr, pp, ps, w["norm_ple1"], w["ple_gate1"], w["b_ple_gate1"], w["ple1"],
                    w["norm_final"], ys, plan["meta"], tile, n_prompt)


def _rope_tables(pos):
    half = HEAD_DIM // 2
    inv = 1.0 / (ROPE_THETA ** (jnp.arange(half, dtype=F32) / half))
    ang = pos.astype(F32)[:, None] * inv[None, :]
    cos = jnp.tile(jnp.cos(ang), (1, LANES // half))
    sin = jnp.sin(ang)
    sin = jnp.tile(jnp.concatenate([-sin, sin], axis=-1), (1, LANES // HEAD_DIM))
    return cos, sin


def kernel(x_prompt, x_sample, cache_k, cache_v, state_pool, p_prompt, p_sample, norm_mix, norm_ffn, norm_ple, norm_final, w_qkv, b_qkv, w_o, b_o, sinks, w_pool, pool_scale, w_ff_gate, w_ff_up, w_ff_down, w_router, b_router, w_exp_gate, w_exp_up, w_exp_down, w_ple, w_ple_gate, b_ple_gate):
    batch, seq, _ = x_prompt.shape
    dec = x_sample.shape[0]
    row2 = lambda a: a.reshape(1, -1)
    w = dict(
        exp_gate=w_exp_gate[0], exp_up=w_exp_up[0], exp_down=w_exp_down[0],
        norm_ple1=row2(norm_ple[1]), ple_gate1=w_ple_gate[1].astype(BF16), b_ple_gate1=row2(b_ple_gate[1]),
        ple1=w_ple[1].astype(BF16), norm_final=row2(norm_final))
    wqkv = w_qkv[0].astype(BF16)
    l0 = [w_o[0].astype(BF16), row2(b_o[0]), row2(norm_ffn[0]), w_ff_gate[0].astype(BF16),
          w_ff_up[0].astype(BF16), w_ff_down[0].astype(BF16), row2(norm_ple[0]),
          w_ple_gate[0].astype(BF16), row2(b_ple_gate[0]), w_ple[0].astype(BF16), row2(norm_mix[1])]
    wr_hi = w_router[0].astype(BF16)
    wr_lo = (w_router[0] - wr_hi.astype(F32)).astype(BF16)
    pool_w = [w_pool[0].astype(BF16), row2(pool_scale[0]), row2(norm_ffn[1]),
              jnp.concatenate([wr_hi, wr_lo], axis=1), row2(b_router[0])]

    tm, tile = 512, 512
    n_prompt = batch * seq
    t_pad = n_prompt + tm

    xp = x_prompt.reshape(n_prompt, D_MODEL)
    pp = p_prompt.reshape(p_prompt.shape[0], n_prompt, PLE_DIM)
    cos_p, sin_p = _rope_tables(jnp.arange(seq, dtype=jnp.int32))
    q, k, v = _qkv_rope(xp, row2(norm_mix[0]), wqkv, row2(b_qkv[0]), cos_p, sin_p, tm=tm)
    o = _prompt_attention(q, k, v, sinks[0], batch, seq)
    h1, hn1 = _layer0_tail(xp, o, pp, *l0, tm=tm)
    *padded, last = _pool_prompt(h1, hn1, *pool_w, batch=batch, seq=seq, tm=tm, tile=tile, t_pad=t_pad)
    kv_shape = (1, batch, WINDOW, N_KV_HEADS, HEAD_DIM)
    new_k_prompt = k.reshape(batch, seq, KV_DIM)[:, -WINDOW:].reshape(kv_shape)
    new_v_prompt = v.reshape(batch, seq, KV_DIM)[:, -WINDOW:].reshape(kv_shape)
    new_pool_prompt = last[:, -POOL_HIST:][None]

    xs_ = x_sample.reshape(dec, D_MODEL)
    ps = p_sample.reshape(p_sample.shape[0], dec, PLE_DIM)
    cos_s, sin_s = _rope_tables(jnp.full((dec,), PAST_LEN, jnp.int32))
    qs, ks, vs = _qkv_rope(xs_, row2(norm_mix[0]), wqkv, row2(b_qkv[0]), cos_s, sin_s, tm=dec)
    n_hist = cache_k.shape[2]
    k_win = jnp.concatenate([cache_k[0].reshape(dec, n_hist, KV_DIM), ks[:, None]], axis=1)[:, -n_hist:]
    v_win = jnp.concatenate([cache_v[0].reshape(dec, n_hist, KV_DIM), vs[:, None]], axis=1)[:, -n_hist:]
    head_kv = jnp.arange(N_HEADS) // GROUP
    blk = (head_kv[:, None] == jnp.arange(N_KV_HEADS)[None, :]).astype(BF16)
    q_blk = (qs.reshape(dec, N_HEADS, 1, HEAD_DIM) * blk[None, :, :, None]).reshape(dec, N_HEADS, KV_DIM)
    o_blk = _sample_attention(q_blk, k_win, v_win, sinks[0])
    o_s = jnp.take_along_axis(o_blk.reshape(dec, N_HEADS, N_KV_HEADS, HEAD_DIM),
                              head_kv[None, :, None, None], axis=2).reshape(dec, N_HEADS * HEAD_DIM)
    h1s, hn1s = _layer0_tail(xs_, o_s.astype(BF16), ps, *l0, tm=dec)
    hist_t = jnp.swapaxes(state_pool[0], 0, 1)
    padded = _pool_sample(h1s, hn1s, hist_t, *pool_w, padded=padded, row0=n_prompt, tile=tile)
    new_k_sample = k_win.reshape(1, dec, n_hist, N_KV_HEADS, HEAD_DIM)
    new_v_sample = v_win.reshape(1, dec, n_hist, N_KV_HEADS, HEAD_DIM)
    new_pool_sample = jnp.concatenate([state_pool[0], hn1s[:, None]], axis=1)[:, -POOL_HIST:][None]

    y_prompt, y_sample = _moe_and_head(*padded, pp, ps, w, tile=tile, tmd=1024, n_prompt=n_prompt)

    return (y_prompt.reshape(batch, seq, D_MODEL), y_sample.reshape(dec, 1, D_MODEL),
            new_k_prompt, new_v_prompt, new_pool_prompt, new_k_sample, new_v_sample, new_pool_sample)
```

```python
import functools

import jax
import jax.numpy as jnp
from jax import lax
from jax.experimental import pallas as pl
from jax.experimental.pallas import tpu as pltpu

F32 = jnp.float32
BF16 = jnp.bfloat16

D_MODEL = 1024
HEAD_DIM = 64
N_HEADS = 16
N_KV_HEADS = 4
GROUP = N_HEADS // N_KV_HEADS
KV_DIM = N_KV_HEADS * HEAD_DIM
QKV_DIM = (N_HEADS + 2 * N_KV_HEADS) * HEAD_DIM
WINDOW = 128
ROPE_THETA = 10000.0
PAST_LEN = 16384
POOL_WINDOWS = (2, 4, 8, 16)
POOL_GROUP_DIM = D_MODEL // len(POOL_WINDOWS)
POOL_HIST = max(POOL_WINDOWS) - 1
POOL_HALO = 16
D_FF = 2816
N_EXPERTS = 8
D_FF_EXPERT = 3584
PLE_DIM = 256
EPS = 1e-6

LANES = 128
VMEM_LIMIT = 56 * 1024 * 1024

ATTN_TQ = 512
ATTN_STAGE = 2
MXU_DIM = 256
FF_CHUNK_EDGES = (0, 6 * MXU_DIM, D_FF)
EXPERT_FF_CHUNK = 512
EXPERT_ROW_STEPS = 4
ROW_ALIGN = 16
TAIL_PIECES = (2048, 1024, 512, 256, 128, 64, 32, 16)
SLACK_CHUNK = 256


def _cparams(sem):
    return pltpu.CompilerParams(dimension_semantics=sem, vmem_limit_bytes=VMEM_LIMIT)


def _const_spec(shape):
    nd = len(shape)
    return pl.BlockSpec(shape, lambda *_: (0,) * nd, pipeline_mode=pl.Buffered(1))


def _rms(x, g):
    return x * lax.rsqrt(jnp.mean(x * x, axis=-1, keepdims=True) + EPS) * g


def _mm(a, b):
    return jnp.dot(a, b, preferred_element_type=F32)


def _silu(x):
    return x * jax.nn.sigmoid(x)


def _qkv_kernel(x_ref, g_ref, w_ref, b_ref, cos_ref, sin_ref, q_ref, k_ref, v_ref):
    hn = _rms(x_ref[...], g_ref[...])
    qkv = _mm(hn.astype(BF16), w_ref[...]) + b_ref[...]
    cos = cos_ref[...]
    sin = sin_ref[...]
    lane = lax.broadcasted_iota(jnp.int32, cos.shape, 1)
    first_half = (lane % HEAD_DIM) < (HEAD_DIM // 2)

    def rope(xb):
        partner = jnp.where(first_half, pltpu.roll(xb, LANES - HEAD_DIM // 2, 1),
                            pltpu.roll(xb, HEAD_DIM // 2, 1))
        return xb * cos + partner * sin

    scale = HEAD_DIM ** -0.5
    for c in range(N_HEADS * HEAD_DIM // LANES):
        sl = slice(c * LANES, (c + 1) * LANES)
        q_ref[:, sl] = (rope(qkv[:, sl]) * scale).astype(BF16)
    for c in range(KV_DIM // LANES):
        sl = slice(c * LANES, (c + 1) * LANES)
        k_ref[:, sl] = rope(qkv[:, N_HEADS * HEAD_DIM + c * LANES:N_HEADS * HEAD_DIM + (c + 1) * LANES])
    v_ref[...] = qkv[:, N_HEADS * HEAD_DIM + KV_DIM:]


def _qkv_rope(x, g, w, b, cos, sin, tm):
    t = x.shape[0]
    n_pos_tiles = cos.shape[0] // tm
    row = lambda i: (i, 0)
    return pl.pallas_call(
        _qkv_kernel,
        grid=(t // tm,),
        in_specs=[
            pl.BlockSpec((tm, D_MODEL), row),
            _const_spec((1, D_MODEL)),
            _const_spec((D_MODEL, QKV_DIM)),
            _const_spec((1, QKV_DIM)),
            pl.BlockSpec((tm, LANES), lambda i: (i % n_pos_tiles, 0)),
            pl.BlockSpec((tm, LANES), lambda i: (i % n_pos_tiles, 0)),
        ],
        out_specs=[
            pl.BlockSpec((tm, N_HEADS * HEAD_DIM), row),
            pl.BlockSpec((tm, KV_DIM), row),
            pl.BlockSpec((tm, KV_DIM), row),
        ],
        out_shape=[
            jax.ShapeDtypeStruct((t, N_HEADS * HEAD_DIM), BF16),
            jax.ShapeDtypeStruct((t, KV_DIM), F32),
            jax.ShapeDtypeStruct((t, KV_DIM), F32),
        ],
        compiler_params=_cparams(("parallel",)),
        name="qkv_rope",
    )(x, g, w, b, cos, sin)


V_AUG = 4 * HEAD_DIM


def _attn_kernel(sink_ref, q_ref, kc_ref, kp_ref, vc_ref, vp_ref, o_ref, kbuf, vbuf, bias):
    n = pl.program_id(1)
    kbuf[0:WINDOW] = kp_ref[...].astype(BF16)
    kbuf[WINDOW:] = kc_ref[...].astype(BF16)
    v_all = jnp.concatenate([vp_ref[...], vc_ref[...]], axis=0).astype(BF16)
    pad0 = jnp.zeros((v_all.shape[0], HEAD_DIM), BF16)
    pad1 = jnp.ones((v_all.shape[0], 2 * HEAD_DIM), BF16)
    for kv in range(N_KV_HEADS):
        vbuf[:, kv * V_AUG:(kv + 1) * V_AUG] = jnp.concatenate(
            [v_all[:, kv * HEAD_DIM:(kv + 1) * HEAD_DIM], pad0, pad1], axis=1)
    shape = (GROUP * WINDOW, 2 * WINDOW)
    qrow = lax.broadcasted_iota(jnp.int32, shape, 0) & (WINDOW - 1)
    col = lax.broadcasted_iota(jnp.int32, shape, 1)
    mask_cur = (col >= WINDOW) & (col - WINDOW <= qrow)
    mask_prev = (col < WINDOW) & (col > qrow)
    bias[0] = jnp.where(mask_cur | mask_prev, 0.0, -jnp.inf)
    bias[1] = jnp.where(mask_cur, 0.0, -jnp.inf)
    rows = lambda sb: slice(sb * WINDOW, (sb + 1) * WINDOW)
    window = lambda sb: slice(sb * WINDOW, (sb + 2) * WINDOW)
    for first in range(0, ATTN_TQ // WINDOW, ATTN_STAGE):
        blocks = range(first, first + ATTN_STAGE)
        scores, probs, sink_term, outs = {}, {}, {}, {}
        for sb in blocks:
            for kv in range(N_KV_HEADS):
                qg = jnp.concatenate([q_ref[rows(sb), h * HEAD_DIM:(h + 1) * HEAD_DIM]
                                      for h in range(kv * GROUP, (kv + 1) * GROUP)], axis=0)
                scores[sb, kv] = lax.dot_general(qg, kbuf[window(sb), kv * HEAD_DIM:(kv + 1) * HEAD_DIM],
                                                 (((1,), (1,)), ((), ())), preferred_element_type=F32)
        for sb in blocks:
            bias_sb = bias[jnp.where(n == 0, 1, 0)] if sb == 0 else bias[0]
            for kv in range(N_KV_HEADS):
                s = scores[sb, kv] + bias_sb
                p = []
                for g in range(GROUP):
                    h = kv * GROUP + g
                    sh = s[g * WINDOW:(g + 1) * WINDOW]
                    m = jnp.maximum(jnp.max(sh, axis=-1, keepdims=True), sink_ref[h])
                    p.append(jnp.exp(sh - m).astype(BF16))
                    sink_term[sb, h] = jnp.exp(sink_ref[h] - m)
                probs[sb, kv] = jnp.concatenate(p, axis=0)
        for sb in blocks:
            for kv in range(N_KV_HEADS):
                outs[sb, kv] = _mm(probs[sb, kv], vbuf[window(sb), kv * V_AUG:(kv + 1) * V_AUG])
        for sb in blocks:
            for h in range(N_HEADS):
                oh = outs[sb, h // GROUP][(h % GROUP) * WINDOW:(h % GROUP + 1) * WINDOW]
                o = oh[:, :2 * HEAD_DIM] / (oh[:, 2 * HEAD_DIM:] + sink_term[sb, h])
                o_ref[rows(sb), h * HEAD_DIM:(h + 1) * HEAD_DIM] = o[:, :HEAD_DIM].astype(BF16)


def _prompt_attention(q, k, v, sinks, batch, seq):
    nq = seq // ATTN_TQ
    per = ATTN_TQ // WINDOW
    cur = lambda b, n: (b * nq + n, 0)
    prev = lambda b, n: (jnp.maximum(b * nq * per + n * per - 1, b * nq * per), 0)
    return pl.pallas_call(
        _attn_kernel,
        grid=(batch, nq),
        in_specs=[
            pl.BlockSpec(memory_space=pltpu.SMEM),
            pl.BlockSpec((ATTN_TQ, N_HEADS * HEAD_DIM), cur),
            pl.BlockSpec((ATTN_TQ, KV_DIM), cur),
            pl.BlockSpec((WINDOW, KV_DIM), prev),
            pl.BlockSpec((ATTN_TQ, KV_DIM), cur),
            pl.BlockSpec((WINDOW, KV_DIM), prev),
        ],
        out_specs=pl.BlockSpec((ATTN_TQ, N_HEADS * HEAD_DIM), cur),
        out_shape=jax.ShapeDtypeStruct(q.shape, BF16),
        scratch_shapes=[pltpu.VMEM((ATTN_TQ + WINDOW, KV_DIM), BF16),
                        pltpu.VMEM((ATTN_TQ + WINDOW, N_KV_HEADS * V_AUG), BF16),
                        pltpu.VMEM((2, GROUP * WINDOW, 2 * WINDOW), F32)],
        compiler_params=_cparams(("parallel", "parallel")),
        name="prompt_attention",
    )(sinks, q, k, k, v, v)


def _attn_sample_kernel(sink_ref, q_ref, k_ref, v_ref, o_ref):
    s = jnp.einsum("bhc,blc->bhl", q_ref[...], k_ref[...].astype(BF16),
                   preferred_element_type=F32)
    sink = sink_ref[...]
    m = jnp.maximum(jnp.max(s, axis=-1, keepdims=True), sink)
    p = jnp.exp(s - m)
    denom = jnp.sum(p, axis=-1, keepdims=True) + jnp.exp(sink - m)
    o_ref[...] = jnp.einsum("bhl,blc->bhc", (p / denom).astype(BF16), v_ref[...].astype(BF16),
                            preferred_element_type=F32)


def _sample_attention(q_blk, k_win, v_win, sinks, bt=16):
    b = q_blk.shape[0]
    blk = lambda i: (i, 0, 0)
    return pl.pallas_call(
        _attn_sample_kernel,
        grid=(b // bt,),
        in_specs=[
            _const_spec((1, N_HEADS, 1)),
            pl.BlockSpec((bt, N_HEADS, KV_DIM), blk),
            pl.BlockSpec((bt, WINDOW, KV_DIM), blk),
            pl.BlockSpec((bt, WINDOW, KV_DIM), blk),
        ],
        out_specs=pl.BlockSpec((bt, N_HEADS, KV_DIM), blk),
        out_shape=jax.ShapeDtypeStruct((b, N_HEADS, KV_DIM), F32),
        compiler_params=_cparams(("parallel",)),
        name="sample_attention",
    )(sinks.reshape(1, N_HEADS, 1), q_blk, k_win, v_win)


def _ple(h, p, g, wg, bg, wp):
    gate = jax.nn.sigmoid(_mm(_rms(h, g).astype(BF16), wg) + bg)
    return h + gate * _mm(p.astype(BF16), wp)


def _layer0_tail_math(x_ref, o_ref, p, wo_ref, bo_ref, gf_ref, wg_ref, wu_ref, wd_ref,
                      gp_ref, wpg_ref, bpg_ref, wp_ref, gm_ref):
    h = x_ref[...] + _mm(o_ref[...], wo_ref[...]) + bo_ref[...]
    hn = _rms(h, gf_ref[...]).astype(BF16)
    f = None
    for c0, c1 in zip(FF_CHUNK_EDGES[:-1], FF_CHUNK_EDGES[1:]):
        a = (_silu(_mm(hn, wg_ref[:, c0:c1])) * _mm(hn, wu_ref[:, c0:c1])).astype(BF16)
        part = _mm(a, wd_ref[c0:c1, :])
        f = part if f is None else f + part
    h = h + f
    h = _ple(h, p, gp_ref[...], wpg_ref[...], bpg_ref[...], wp_ref[...])
    return h, _rms(h, gm_ref[...])


def _layer0_tail_kernel(x_ref, o_ref, p_ref, wo_ref, bo_ref, gf_ref, wg_ref, wu_ref, wd_ref,
                        gp_ref, wpg_ref, bpg_ref, wp_ref, gm_ref, h_ref, hn_ref):
    h_ref[...], hn_ref[...] = _layer0_tail_math(x_ref, o_ref, p_ref[0], wo_ref, bo_ref, gf_ref, wg_ref,
                                                wu_ref, wd_ref, gp_ref, wpg_ref, bpg_ref, wp_ref, gm_ref)


def _layer0_tail(x, o, p, wo, bo, gf, wg, wu, wd, gp, wpg, bpg, wp, gm, tm):
    t = x.shape[0]
    row = lambda i: (i, 0)
    consts = [wo, bo, gf, wg, wu, wd, gp, wpg, bpg, wp, gm]
    return pl.pallas_call(
        _layer0_tail_kernel,
        grid=(t // tm,),
        in_specs=[pl.BlockSpec((tm, D_MODEL), row), pl.BlockSpec((tm, D_MODEL), row),
                  pl.BlockSpec((1, tm, PLE_DIM), lambda i: (0, i, 0))] + [_const_spec(c.shape) for c in consts],
        out_specs=[pl.BlockSpec((tm, D_MODEL), row), pl.BlockSpec((tm, D_MODEL), row)],
        out_shape=[jax.ShapeDtypeStruct((t, D_MODEL), F32), jax.ShapeDtypeStruct((t, D_MODEL), F32)],
        compiler_params=_cparams(("parallel",)),
        name="layer0_tail",
    )(x, o, p, *consts)


def _segment_ranks(sel, tile, lr_out, lrt_out, cnt_out):
    r = lax.broadcasted_iota(jnp.int32, (tile, tile), 0)
    c = lax.broadcasted_iota(jnp.int32, (tile, tile), 1)
    earlier = (c < r).astype(BF16)
    pad = jnp.full((tile, LANES - N_EXPERTS), -1.0, F32)
    for k in range(sel.shape[0] // tile):
        rows = slice(k * tile, (k + 1) * tile)
        sel_k = sel[rows]
        rank = _mm(earlier, sel_k.astype(BF16))
        lr = jnp.where(sel_k, rank, -1.0)
        lr_out[rows, :] = lr.astype(jnp.int32)
        lrt_out[:, rows] = jnp.concatenate([lr, pad], axis=1).T[:N_EXPERTS].astype(jnp.int32)
        cnt_out[k] = jnp.sum(sel_k.astype(F32), axis=0, keepdims=True).astype(jnp.int32)


def _pool_project_route(h, hn, pooled_sum_inv, wpool_ref, ps_ref, gf_ref, wr_ref, br_ref,
                        h_out, hn_out, gates_out, lr_out, lrt_out, cnt_out, tile):
    mixed = []
    for g in range(len(POOL_WINDOWS)):
        cols = slice(g * POOL_GROUP_DIM, (g + 1) * POOL_GROUP_DIM)
        pooled = pooled_sum_inv[g] - hn[:, cols]
        mixed.append(_mm(pooled.astype(BF16), wpool_ref[g]))
    h = h + jnp.concatenate(mixed, axis=-1) * ps_ref[...]
    h_out[...] = h
    hn2 = _rms(h, gf_ref[...])
    hi = hn2.astype(BF16)
    hn_out[...] = hi
    lo = (hn2 - hi.astype(F32)).astype(BF16)
    by_hi = _mm(hi, wr_ref[...])
    logits = (by_hi[:, :N_EXPERTS] + by_hi[:, N_EXPERTS:] + _mm(lo, wr_ref[:, :N_EXPERTS])) + br_ref[...]
    idx = lax.broadcasted_iota(jnp.int32, logits.shape, 1)
    m1 = jnp.max(logits, axis=-1, keepdims=True)
    i1 = jnp.min(jnp.where(logits == m1, idx, N_EXPERTS), axis=-1, keepdims=True)
    rest = jnp.where(idx == i1, -jnp.inf, logits)
    m2 = jnp.max(rest, axis=-1, keepdims=True)
    i2 = jnp.min(jnp.where(rest == m2, idx, N_EXPERTS), axis=-1, keepdims=True)
    e = jnp.exp(m2 - m1)
    w1 = 1.0 / (1.0 + e)
    w2 = e / (1.0 + e)
    gates_out[...] = jnp.where(idx == i1, w1, jnp.where(idx == i2, w2, 0.0))
    _segment_ranks((idx == i1) | (idx == i2), tile, lr_out, lrt_out, cnt_out)


def _route_out_specs(t, tm, tile, blk):
    specs = [pl.BlockSpec((tm, N_EXPERTS), lambda *g: (blk(*g), 0)),
             pl.BlockSpec((tm, N_EXPERTS), lambda *g: (blk(*g), 0)),
             pl.BlockSpec((N_EXPERTS, tm), lambda *g: (0, blk(*g))),
             pl.BlockSpec((tm // tile, 1, N_EXPERTS), lambda *g: (blk(*g), 0, 0))]
    shapes = [jax.ShapeDtypeStruct((t, N_EXPERTS), F32),
              jax.ShapeDtypeStruct((t, N_EXPERTS), jnp.int32),
              jax.ShapeDtypeStruct((N_EXPERTS, t), jnp.int32),
              jax.ShapeDtypeStruct((t // tile, 1, N_EXPERTS), jnp.int32)]
    return specs, shapes


def _pool_prompt_kernel(h_ref, hn_ref, wpool_ref, ps_ref, gf_ref, wr_ref, br_ref,
                        h_out, hn_out, gates_out, lr_out, lrt_out, cnt_out, last_out, carry, buf, *, tile):
    b = pl.program_id(0)
    n = pl.program_id(1)
    tm = h_ref.shape[0]
    n_seq = pl.num_programs(0) - 1

    @pl.when(b < n_seq)
    def _():
        @pl.when(n == 0)
        def _():
            carry[...] = jnp.zeros_like(carry)

        hn = hn_ref[...]
        buf[0:POOL_HALO] = carry[...]
        buf[POOL_HALO:] = hn
        carry[...] = hn[tm - POOL_HALO:]
        last_out[0] = hn[tm - POOL_HALO:]
        pos = n * tm + lax.broadcasted_iota(jnp.int32, (tm, 1), 0)
        means = []
        for g, w in enumerate(POOL_WINDOWS):
            s = buf[:, g * POOL_GROUP_DIM:(g + 1) * POOL_GROUP_DIM]
            shift = 1
            while shift < w:
                s = s + pltpu.roll(s, shift, 0)
                shift *= 2
            cnt = jnp.minimum(w, pos + 1).astype(F32)
            means.append(s[POOL_HALO:] * (1.0 / cnt))
        _pool_project_route(h_ref[...], hn, means, wpool_ref, ps_ref, gf_ref, wr_ref, br_ref,
                            h_out, hn_out, gates_out, lr_out, lrt_out, cnt_out, tile)

    @pl.when(b == n_seq)
    def _():
        h_out[...] = jnp.zeros_like(h_out)
        hn_out[...] = jnp.zeros_like(hn_out)
        gates_out[...] = jnp.zeros_like(gates_out)
        lr_out[...] = jnp.full(lr_out.shape, -1, jnp.int32)
        lrt_out[...] = jnp.full(lrt_out.shape, -1, jnp.int32)
        cnt_out[...] = jnp.zeros_like(cnt_out)


def _pool_prompt(h, hn, wpool, ps, gf, wr, br, batch, seq, tm, tile, t_pad):
    ns = seq // tm
    pad_blk = batch * ns
    blk = lambda b, n: jnp.where(b < batch, b * ns + n, pad_blk)
    src = lambda b, n: (jnp.minimum(b * ns + n, pad_blk - 1), 0)
    consts = [wpool, ps, gf, wr, br]
    route_specs, route_shapes = _route_out_specs(t_pad, tm, tile, blk)
    row = lambda b, n: (blk(b, n), 0)
    return pl.pallas_call(
        functools.partial(_pool_prompt_kernel, tile=tile),
        grid=(batch + 1, ns),
        in_specs=[pl.BlockSpec((tm, D_MODEL), src), pl.BlockSpec((tm, D_MODEL), src)]
        + [_const_spec(c.shape) for c in consts],
        out_specs=[pl.BlockSpec((tm, D_MODEL), row), pl.BlockSpec((tm, D_MODEL), row)] + route_specs
        + [pl.BlockSpec((1, POOL_HALO, D_MODEL), lambda b, n: (jnp.minimum(b, batch - 1), 0, 0))],
        out_shape=[jax.ShapeDtypeStruct((t_pad, D_MODEL), F32), jax.ShapeDtypeStruct((t_pad, D_MODEL), BF16)]
        + route_shapes + [jax.ShapeDtypeStruct((batch, POOL_HALO, D_MODEL), F32)],
        scratch_shapes=[pltpu.VMEM((POOL_HALO, D_MODEL), F32),
                        pltpu.VMEM((tm + POOL_HALO, D_MODEL), F32)],
        compiler_params=_cparams(("arbitrary", "arbitrary")),
        name="pool_prompt",
    )(h, hn, *consts)


def _pool_sample_kernel(h_ref, hn_ref, hist_ref, wpool_ref, ps_ref, gf_ref, wr_ref, br_ref, *rest):
    outs = rest[len(rest) // 2:]
    hn = hn_ref[...]
    means = []
    for g, w in enumerate(POOL_WINDOWS):
        cols = slice(g * POOL_GROUP_DIM, (g + 1) * POOL_GROUP_DIM)
        s = hn[:, cols]
        for j in range(1, w):
            s = s + hist_ref[POOL_HIST - j, :, cols]
        means.append(s * (1.0 / min(w, PAST_LEN + 1)))
    _pool_project_route(h_ref[...], hn, means, wpool_ref, ps_ref, gf_ref, wr_ref, br_ref,
                        *outs, hn.shape[0])


def _pool_sample(h, hn, hist_t, wpool, ps, gf, wr, br, padded, row0, tile):
    t = h.shape[0]
    args = [h, hn, hist_t, wpool, ps, gf, wr, br]
    blk = row0 // t
    out_specs = [pl.BlockSpec((t, D_MODEL), lambda i: (blk, 0)), pl.BlockSpec((t, D_MODEL), lambda i: (blk, 0)),
                 pl.BlockSpec((t, N_EXPERTS), lambda i: (blk, 0)), pl.BlockSpec((t, N_EXPERTS), lambda i: (blk, 0)),
                 pl.BlockSpec((N_EXPERTS, t), lambda i: (0, blk)),
                 pl.BlockSpec((1, 1, N_EXPERTS), lambda i: (row0 // tile, 0, 0))]
    return pl.pallas_call(
        _pool_sample_kernel,
        grid=(1,),
        in_specs=[_const_spec(a.shape) for a in args] + [pl.BlockSpec(memory_space=pl.ANY)] * len(padded),
        out_specs=out_specs,
        out_shape=[jax.ShapeDtypeStruct(a.shape, a.dtype) for a in padded],
        input_output_aliases={len(args) + k: k for k in range(len(padded))},
        compiler_params=_cparams(("arbitrary",)),
        name="pool_sample",
    )(*args, *padded)


def _segment_copies(meta_ref, step, tile, vmem_of, hbm, sem_of, to_hbm):
    half = tile // 2
    out = []
    for e in range(N_EXPERTS):
        pos = meta_ref[0, step * N_EXPERTS + e]
        cnt = meta_ref[1, step * N_EXPERTS + e]
        for k, pred in ((0, cnt > 0), (1, cnt > half)):
            v = vmem_of(e).at[pl.ds(k * half, half)]
            h = hbm.at[pl.ds(pl.multiple_of(pos + k * half, ROW_ALIGN), half)]
            copy = pltpu.make_async_copy(v, h, sem_of(e)) if to_hbm else pltpu.make_async_copy(h, v, sem_of(e))
            out.append((pred, copy))
    return out


def _for_slot(slot, fn):
    for s in range(2):
        @pl.when(slot == s)
        def _(s=s):
            fn(s)


def _start_all(copies):
    for pred, copy in copies:
        @pl.when(pred)
        def _(copy=copy):
            copy.start()


def _wait_all(copies):
    for pred, copy in copies:
        @pl.when(pred)
        def _(copy=copy):
            copy.wait()


def _dispatch_kernel(meta_ref, tail_ref, hn_ref, lrt_ref, xs_ref, seg, zeros, sem, zsem):
    i = pl.program_id(0)
    last = pl.num_programs(0) - 1
    tile = hn_ref.shape[0]
    slot = i % 2
    copies = lambda step, s: _segment_copies(meta_ref, step, tile, lambda e: seg.at[s, e], xs_ref,
                                             lambda e: sem.at[s, e], True)
    hn = hn_ref[...]
    half = tile // 2
    place = lax.broadcasted_iota(jnp.int32, (half, tile), 0)
    for e0 in range(0, N_EXPERTS, 4):
        first = jnp.concatenate([(place == lrt_ref[e:e + 1, :]).astype(BF16) for e in range(e0, e0 + 4)], axis=0)
        rows = _mm(first, hn).astype(BF16)
        for k in range(4):
            seg[slot, e0 + k, 0:half] = rows[k * half:(k + 1) * half]
    for e in range(N_EXPERTS):
        @pl.when(meta_ref[1, i * N_EXPERTS + e] > half)
        def _(e=e):
            seg[slot, e, half:] = _mm((place + half == lrt_ref[e:e + 1, :]).astype(BF16),
                                      hn_ref[...]).astype(BF16)

    @pl.when(i > 0)
    def _():
        _for_slot(1 - slot, lambda s: _wait_all(copies(i - 1, s)))

    _for_slot(slot, lambda s: _start_all(copies(i, s)))

    @pl.when(i == last)
    def _():
        _for_slot(slot, lambda s: _wait_all(copies(i, s)))
        zeros[...] = jnp.zeros_like(zeros)
        tails = []
        for e in range(N_EXPERTS):
            start = tail_ref[0, e]
            cnt = tail_ref[1, e]
            for piece in TAIL_PIECES:
                off = pl.multiple_of(cnt & ~(2 * piece - 1), ROW_ALIGN)
                copy = pltpu.make_async_copy(
                    zeros.at[pl.ds(0, piece)],
                    xs_ref.at[pl.ds(pl.multiple_of(start + off, ROW_ALIGN), piece)],
                    zsem.at[e])
                tails.append(((cnt & piece) != 0, copy))
        _start_all(tails)
        _wait_all(tails)

        def slack_copy(c):
            row = pl.multiple_of(tail_ref[0, N_EXPERTS] + c * SLACK_CHUNK, SLACK_CHUNK)
            return pltpu.make_async_copy(zeros.at[pl.ds(0, SLACK_CHUNK)],
                                         xs_ref.at[pl.ds(row, SLACK_CHUNK)], zsem.at[0])

        n_slack = tail_ref[1, N_EXPERTS] // SLACK_CHUNK
        lax.fori_loop(0, n_slack, lambda c, _: slack_copy(c).start(), None)
        lax.fori_loop(0, n_slack, lambda c, _: slack_copy(c).wait(), None)


def _dispatch(hn, lr_t, meta, tail, rows, tile):
    t = hn.shape[0]
    return pl.pallas_call(
        _dispatch_kernel,
        grid_spec=pltpu.PrefetchScalarGridSpec(
            num_scalar_prefetch=2,
            grid=(t // tile,),
            in_specs=[pl.BlockSpec((tile, D_MODEL), lambda i, *_: (i, 0)),
                      pl.BlockSpec((N_EXPERTS, tile), lambda i, *_: (0, i))],
            out_specs=pl.BlockSpec(memory_space=pl.ANY),
            scratch_shapes=[pltpu.VMEM((2, N_EXPERTS, tile, D_MODEL), BF16),
                            pltpu.VMEM((TAIL_PIECES[0], D_MODEL), BF16),
                            pltpu.SemaphoreType.DMA((2, N_EXPERTS)),
                            pltpu.SemaphoreType.DMA((N_EXPERTS,))]),
        out_shape=jax.ShapeDtypeStruct((rows, D_MODEL), BF16),
        compiler_params=_cparams(("arbitrary",)),
        name="expert_dispatch",
    )(meta, tail, hn, lr_t)


def _expert_kernel(te_ref, tr_ref, x_ref, wg_ref, wu_ref, wd_ref, y_ref, acc):
    r = pl.program_id(0)
    j = pl.program_id(1)

    @pl.when(j == 0)
    def _():
        acc[...] = jnp.zeros_like(acc)

    def ffn(m):
        x = x_ref[0:m]
        a = (_silu(_mm(x, wg_ref[0].astype(BF16))) * _mm(x, wu_ref[0].astype(BF16))).astype(BF16)
        acc[0:m] += _mm(a, wd_ref[0].astype(BF16))

    rows = tr_ref[r]
    step = x_ref.shape[0] // EXPERT_ROW_STEPS
    for k in range(1, EXPERT_ROW_STEPS + 1):
        @pl.when((rows > (k - 1) * step) & (rows <= k * step))
        def _(k=k):
            ffn(k * step)

    @pl.when(j == pl.num_programs(1) - 1)
    def _():
        y_ref[...] = acc[...].astype(BF16)


def _expert_ffn(xs, tile_e, tile_rows, wg, wu, wd, tmd):
    rows = xs.shape[0]
    nj = D_FF_EXPERT // EXPERT_FF_CHUNK
    live = lambda r, tr: tr[r] > 0
    jj = lambda r, j, tr: jnp.where(live(r, tr), j, nj - 1)
    return pl.pallas_call(
        _expert_kernel,
        grid_spec=pltpu.PrefetchScalarGridSpec(
            num_scalar_prefetch=2,
            grid=(rows // tmd, nj),
            in_specs=[
                pl.BlockSpec((tmd, D_MODEL), lambda r, j, te, tr: (jnp.where(live(r, tr), r, 0), 0)),
                pl.BlockSpec((1, D_MODEL, EXPERT_FF_CHUNK), lambda r, j, te, tr: (te[r], 0, jj(r, j, tr))),
                pl.BlockSpec((1, D_MODEL, EXPERT_FF_CHUNK), lambda r, j, te, tr: (te[r], 0, jj(r, j, tr))),
                pl.BlockSpec((1, EXPERT_FF_CHUNK, D_MODEL), lambda r, j, te, tr: (te[r], jj(r, j, tr), 0)),
            ],
            out_specs=pl.BlockSpec((tmd, D_MODEL), lambda r, j, te, tr: (r, 0)),
            scratch_shapes=[pltpu.VMEM((tmd, D_MODEL), F32)]),
        out_shape=jax.ShapeDtypeStruct((rows, D_MODEL), BF16),
        compiler_params=_cparams(("parallel", "arbitrary")),
        name="expert_ffn",
    )(tile_e, tile_rows, xs, wg, wu, wd)


def _combine_kernel(meta_ref, h_ref, gates_ref, lr_ref, pp_ref, ps_ref, gp_ref, wpg_ref, bpg_ref, wp_ref,
                    gfin_ref, ys_ref, outp_ref, outs_ref, ybuf, f_ref, sem, *, n_prompt_tiles):
    i = pl.program_id(0)
    n = pl.num_programs(0)
    tile = h_ref.shape[0]
    slot = i % 2
    copies = lambda step, s: _segment_copies(meta_ref, step, tile, lambda e: ybuf.at[s, e], ys_ref,
                                             lambda e: sem.at[s, e], False)

    @pl.when(i == 0)
    def _():
        ybuf[...] = jnp.zeros_like(ybuf)
        _start_all(copies(0, 0))

    @pl.when(i + 1 < n)
    def _():
        _for_slot(1 - slot, lambda s: _start_all(copies(i + 1, s)))

    _for_slot(slot, lambda s: _wait_all(copies(i, s)))
    gates = gates_ref[...]
    lr = lr_ref[...]
    half = tile // 2
    place = lax.broadcasted_iota(jnp.int32, (tile, half), 1)

    def gathered(offset):
        f = jnp.zeros((tile, D_MODEL), F32)
        for e in range(N_EXPERTS):
            onehot = (place + offset == lr[:, e:e + 1]).astype(BF16)
            f = f + gates[:, e:e + 1] * _mm(onehot, ybuf[slot, e, offset:offset + half, :])
        return f

    f_ref[...] = gathered(0)
    longest = meta_ref[1, i * N_EXPERTS]
    for e in range(1, N_EXPERTS):
        longest = jnp.maximum(longest, meta_ref[1, i * N_EXPERTS + e])

    @pl.when(longest > half)
    def _():
        f_ref[...] += gathered(half)

    h = h_ref[...] + f_ref[...]

    def head(h, p):
        h = _ple(h, p, gp_ref[...], wpg_ref[...], bpg_ref[...], wp_ref[...])
        return _rms(h, gfin_ref[...])

    @pl.when(i < n_prompt_tiles)
    def _():
        outp_ref[...] = head(h, pp_ref[0])

    @pl.when(i == n_prompt_tiles)
    def _():
        n_s = outs_ref.shape[0]
        outs_ref[...] = head(h[:n_s], ps_ref[0])


def _combine(h, gates, lr, pp, ps, gp, wpg, bpg, wp, gfin, ys, meta, tile, n_prompt):
    t = h.shape[0]
    n_pt = n_prompt // tile
    n_s = ps.shape[1]
    row = lambda i, *_: (i, 0)
    prow = lambda i, *_: (jnp.minimum(i, n_pt - 1), 0)
    consts = [gp, wpg, bpg, wp, gfin]
    return pl.pallas_call(
        functools.partial(_combine_kernel, n_prompt_tiles=n_pt),
        grid_spec=pltpu.PrefetchScalarGridSpec(
            num_scalar_prefetch=1,
            grid=(t // tile,),
            in_specs=[pl.BlockSpec((tile, D_MODEL), row), pl.BlockSpec((tile, N_EXPERTS), row),
                      pl.BlockSpec((tile, N_EXPERTS), row),
                      pl.BlockSpec((1, tile, PLE_DIM), lambda i, *_: (1,) + prow(i)),
                      pl.BlockSpec((1, n_s, PLE_DIM), lambda i, *_: (1, 0, 0))]
            + [_const_spec(c.shape) for c in consts] + [pl.BlockSpec(memory_space=pl.ANY)],
            out_specs=[pl.BlockSpec((tile, D_MODEL), prow),
                       pl.BlockSpec((n_s, D_MODEL), lambda i, *_: (0, 0))],
            scratch_shapes=[pltpu.VMEM((2, N_EXPERTS, tile, D_MODEL), BF16),
                            pltpu.VMEM((tile, D_MODEL), F32),
                            pltpu.SemaphoreType.DMA((2, N_EXPERTS))]),
        out_shape=[jax.ShapeDtypeStruct((n_prompt, D_MODEL), F32),
                   jax.ShapeDtypeStruct((n_s, D_MODEL), F32)],
        compiler_params=_cparams(("arbitrary",)),
        name="combine",
    )(meta, h, gates, lr, pp, ps, *consts, ys)


def _round_up(x, m):
    return (x + m - 1) // m * m


def _route_plan(cnt, t, tile, tmd):
    n_tiles = t // tile
    guard = tile // 2
    cnt8 = _round_up(cnt.reshape(n_tiles, N_EXPERTS), ROW_ALIGN)
    tile_base = jnp.cumsum(cnt8, axis=0) - cnt8
    total8 = cnt8.sum(axis=0)
    region = _round_up(total8 + guard, tmd)
    region_end = jnp.cumsum(region)
    off = region_end - region
    pos = off[None, :] + tile_base
    rows = _round_up(2 * t + (ROW_ALIGN - 1) * N_EXPERTS * n_tiles + N_EXPERTS * (tmd + guard), tmd)
    tile_start = jnp.arange(rows // tmd, dtype=jnp.int32) * tmd
    te = jnp.sum(region_end[None, :] <= tile_start[:, None], axis=1).astype(jnp.int32)
    tec = jnp.minimum(te, N_EXPERTS - 1)
    tile_rows = jnp.where(te < N_EXPERTS, jnp.clip(total8[tec] - (tile_start - off[tec]), 0, tmd), 0)
    last_e = jnp.max(jnp.where(tile_rows > 0, tec, 0))
    tile_e = jnp.where(tile_rows > 0, tec, last_e).astype(jnp.int32)
    meta = jnp.stack([pos.reshape(-1), cnt8.reshape(-1)]).astype(jnp.int32)
    tail = jnp.stack([jnp.append(off + total8, region_end[-1]),
                      jnp.append(region - total8, rows - region_end[-1])]).astype(jnp.int32)
    return dict(meta=meta, tail=tail, tile_e=tile_e, tile_rows=tile_rows.astype(jnp.int32), rows=rows)


def _moe_and_head(h, hn, gates, lr, lr_t, cnt, pp, ps, w, tile, tmd, n_prompt):
    plan = _route_plan(cnt, h.shape[0], tile, tmd)
    xs = _dispatch(hn, lr_t, plan["meta"], plan["tail"], plan["rows"], tile)
    ys = _expert_ffn(xs, plan["tile_e"], plan["tile_rows"], w["exp_gate"], w["exp_up"], w["exp_down"], tmd)
    return _combine(h, gates, lr, pp, ps, w["norm_ple1"], w["ple_gate1"], w["b_ple_gate1"], w["ple1"],
                    w["norm_final"], ys, plan["meta"], tile, n_prompt)


def _rope_tables(pos):
    half = HEAD_DIM // 2
    inv = 1.0 / (ROPE_THETA ** (jnp.arange(half, dtype=F32) / half))
    ang = pos.astype(F32)[:, None] * inv[None, :]
    cos = jnp.tile(jnp.cos(ang), (1, LANES // half))
    sin = jnp.sin(ang)
    sin = jnp.tile(jnp.concatenate([-sin, sin], axis=-1), (1, LANES // HEAD_DIM))
    return cos, sin


def kernel(x_prompt, x_sample, cache_k, cache_v, state_pool, p_prompt, p_sample, norm_mix, norm_ffn, norm_ple, norm_final, w_qkv, b_qkv, w_o, b_o, sinks, w_pool, pool_scale, w_ff_gate, w_ff_up, w_ff_down, w_router, b_router, w_exp_gate, w_exp_up, w_exp_down, w_ple, w_ple_gate, b_ple_gate):
    batch, seq, _ = x_prompt.shape
    dec = x_sample.shape[0]
    row2 = lambda a: a.reshape(1, -1)
    w = dict(
        exp_gate=w_exp_gate[0], exp_up=w_exp_up[0], exp_down=w_exp_down[0],
        norm_ple1=row2(norm_ple[1]), ple_gate1=w_ple_gate[1].astype(BF16), b_ple_gate1=row2(b_ple_gate[1]),
        ple1=w_ple[1].astype(BF16), norm_final=row2(norm_final))
    wqkv = w_qkv[0].astype(BF16)
    l0 = [w_o[0].astype(BF16), row2(b_o[0]), row2(norm_ffn[0]), w_ff_gate[0].astype(BF16),
          w_ff_up[0].astype(BF16), w_ff_down[0].astype(BF16), row2(norm_ple[0]),
          w_ple_gate[0].astype(BF16), row2(b_ple_gate[0]), w_ple[0].astype(BF16), row2(norm_mix[1])]
    wr_hi = w_router[0].astype(BF16)
    wr_lo = (w_router[0] - wr_hi.astype(F32)).astype(BF16)
    pool_w = [w_pool[0].astype(BF16), row2(pool_scale[0]), row2(norm_ffn[1]),
              jnp.concatenate([wr_hi, wr_lo], axis=1), row2(b_router[0])]

    tm, tile = 512, 512
    n_prompt = batch * seq
    t_pad = n_prompt + tm

    xp = x_prompt.reshape(n_prompt, D_MODEL)
    pp = p_prompt.reshape(p_prompt.shape[0], n_prompt, PLE_DIM)
    cos_p, sin_p = _rope_tables(jnp.arange(seq, dtype=jnp.int32))
    q, k, v = _qkv_rope(xp, row2(norm_mix[0]), wqkv, row2(b_qkv[0]), cos_p, sin_p, tm=tm)
    o = _prompt_attention(q, k, v, sinks[0], batch, seq)
    h1, hn1 = _layer0_tail(xp, o, pp, *l0, tm=tm)
    *padded, last = _pool_prompt(h1, hn1, *pool_w, batch=batch, seq=seq, tm=tm, tile=tile, t_pad=t_pad)
    kv_shape = (1, batch, WINDOW, N_KV_HEADS, HEAD_DIM)
    new_k_prompt = k.reshape(batch, seq, KV_DIM)[:, -WINDOW:].reshape(kv_shape)
    new_v_prompt = v.reshape(batch, seq, KV_DIM)[:, -WINDOW:].reshape(kv_shape)
    new_pool_prompt = last[:, -POOL_HIST:][None]

    xs_ = x_sample.reshape(dec, D_MODEL)
    ps = p_sample.reshape(p_sample.shape[0], dec, PLE_DIM)
    cos_s, sin_s = _rope_tables(jnp.full((dec,), PAST_LEN, jnp.int32))
    qs, ks, vs = _qkv_rope(xs_, row2(norm_mix[0]), wqkv, row2(b_qkv[0]), cos_s, sin_s, tm=dec)
    n_hist = cache_k.shape[2]
    k_win = jnp.concatenate([cache_k[0].reshape(dec, n_hist, KV_DIM), ks[:, None]], axis=1)[:, -n_hist:]
    v_win = jnp.concatenate([cache_v[0].reshape(dec, n_hist, KV_DIM), vs[:, None]], axis=1)[:, -n_hist:]
    head_kv = jnp.arange(N_HEADS) // GROUP
    blk = (head_kv[:, None] == jnp.arange(N_KV_HEADS)[None, :]).astype(BF16)
    q_blk = (qs.reshape(dec, N_HEADS, 1, HEAD_DIM) * blk[None, :, :, None]).reshape(dec, N_HEADS, KV_DIM)
    o_blk = _sample_attention(q_blk, k_win, v_win, sinks[0])
    o_s = jnp.take_along_axis(o_blk.reshape(dec, N_HEADS, N_KV_HEADS, HEAD_DIM),
                              head_kv[None, :, None, None], axis=2).reshape(dec, N_HEADS * HEAD_DIM)
    h1s, hn1s = _layer0_tail(xs_, o_s.astype(BF16), ps, *l0, tm=dec)
    hist_t = jnp.swapaxes(state_pool[0], 0, 1)
    padded = _pool_sample(h1s, hn1s, hist_t, *pool_w, padded=padded, row0=n_prompt, tile=tile)
    new_k_sample = k_win.reshape(1, dec, n_hist, N_KV_HEADS, HEAD_DIM)
    new_v_sample = v_win.reshape(1, dec, n_hist, N_KV_HEADS, HEAD_DIM)
    new_pool_sample = jnp.concatenate([state_pool[0], hn1s[:, None]], axis=1)[:, -POOL_HIST:][None]

    y_prompt, y_sample = _moe_and_head(*padded, pp, ps, w, tile=tile, tmd=2048, n_prompt=n_prompt)

    return (y_prompt.reshape(batch, seq, D_MODEL), y_sample.reshape(dec, 1, D_MODEL),
            new_k_prompt, new_v_prompt, new_pool_prompt, new_k_sample, new_v_sample, new_pool_sample)
```

```python
import functools

import jax
import jax.numpy as jnp
from jax import lax
from jax.experimental import pallas as pl
from jax.experimental.pallas import tpu as pltpu

F32 = jnp.float32
BF16 = jnp.bfloat16

D_MODEL = 1024
HEAD_DIM = 64
N_HEADS = 16
N_KV_HEADS = 4
GROUP = N_HEADS // N_KV_HEADS
KV_DIM = N_KV_HEADS * HEAD_DIM
QKV_DIM = (N_HEADS + 2 * N_KV_HEADS) * HEAD_DIM
WINDOW = 128
ROPE_THETA = 10000.0
PAST_LEN = 16384
POOL_WINDOWS = (2, 4, 8, 16)
POOL_GROUP_DIM = D_MODEL // len(POOL_WINDOWS)
POOL_HIST = max(POOL_WINDOWS) - 1
POOL_HALO = 16
D_FF = 2816
N_EXPERTS = 8
D_FF_EXPERT = 3584
PLE_DIM = 256
EPS = 1e-6

LANES = 128
VMEM_LIMIT = 56 * 1024 * 1024

ATTN_TQ = 512
ATTN_STAGE = 2
MXU_DIM = 256
FF_CHUNK_EDGES = (0, 6 * MXU_DIM, D_FF)
EXPERT_FF_CHUNK = 512
EXPERT_ROW_STEPS = 8
ROW_ALIGN = 16
TAIL_PIECES = (2048, 1024, 512, 256, 128, 64, 32, 16)
SLACK_CHUNK = 256


def _cparams(sem):
    return pltpu.CompilerParams(dimension_semantics=sem, vmem_limit_bytes=VMEM_LIMIT)


def _const_spec(shape):
    nd = len(shape)
    return pl.BlockSpec(shape, lambda *_: (0,) * nd, pipeline_mode=pl.Buffered(1))


def _rms(x, g):
    return x * lax.rsqrt(jnp.mean(x * x, axis=-1, keepdims=True) + EPS) * g


def _mm(a, b):
    return jnp.dot(a, b, preferred_element_type=F32)


def _silu(x):
    return x * jax.nn.sigmoid(x)


def _qkv_kernel(x_ref, g_ref, w_ref, b_ref, cos_ref, sin_ref, q_ref, k_ref, v_ref):
    hn = _rms(x_ref[...], g_ref[...])
    qkv = _mm(hn.astype(BF16), w_ref[...]) + b_ref[...]
    cos = cos_ref[...]
    sin = sin_ref[...]
    lane = lax.broadcasted_iota(jnp.int32, cos.shape, 1)
    first_half = (lane % HEAD_DIM) < (HEAD_DIM // 2)

    def rope(xb):
        partner = jnp.where(first_half, pltpu.roll(xb, LANES - HEAD_DIM // 2, 1),
                            pltpu.roll(xb, HEAD_DIM // 2, 1))
        return xb * cos + partner * sin

    scale = HEAD_DIM ** -0.5
    for c in range(N_HEADS * HEAD_DIM // LANES):
        sl = slice(c * LANES, (c + 1) * LANES)
        q_ref[:, sl] = (rope(qkv[:, sl]) * scale).astype(BF16)
    for c in range(KV_DIM // LANES):
        sl = slice(c * LANES, (c + 1) * LANES)
        k_ref[:, sl] = rope(qkv[:, N_HEADS * HEAD_DIM + c * LANES:N_HEADS * HEAD_DIM + (c + 1) * LANES])
    v_ref[...] = qkv[:, N_HEADS * HEAD_DIM + KV_DIM:]


def _qkv_rope(x, g, w, b, cos, sin, tm):
    t = x.shape[0]
    n_pos_tiles = cos.shape[0] // tm
    row = lambda i: (i, 0)
    return pl.pallas_call(
        _qkv_kernel,
        grid=(t // tm,),
        in_specs=[
            pl.BlockSpec((tm, D_MODEL), row),
            _const_spec((1, D_MODEL)),
            _const_spec((D_MODEL, QKV_DIM)),
            _const_spec((1, QKV_DIM)),
            pl.BlockSpec((tm, LANES), lambda i: (i % n_pos_tiles, 0)),
            pl.BlockSpec((tm, LANES), lambda i: (i % n_pos_tiles, 0)),
        ],
        out_specs=[
            pl.BlockSpec((tm, N_HEADS * HEAD_DIM), row),
            pl.BlockSpec((tm, KV_DIM), row),
            pl.BlockSpec((tm, KV_DIM), row),
        ],
        out_shape=[
            jax.ShapeDtypeStruct((t, N_HEADS * HEAD_DIM), BF16),
            jax.ShapeDtypeStruct((t, KV_DIM), F32),
            jax.ShapeDtypeStruct((t, KV_DIM), F32),
        ],
        compiler_params=_cparams(("parallel",)),
        name="qkv_rope",
    )(x, g, w, b, cos, sin)


V_AUG = 4 * HEAD_DIM


def _attn_kernel(sink_ref, q_ref, kc_ref, kp_ref, vc_ref, vp_ref, o_ref, kbuf, vbuf, bias):
    n = pl.program_id(1)
    kbuf[0:WINDOW] = kp_ref[...].astype(BF16)
    kbuf[WINDOW:] = kc_ref[...].astype(BF16)
    v_all = jnp.concatenate([vp_ref[...], vc_ref[...]], axis=0).astype(BF16)
    pad0 = jnp.zeros((v_all.shape[0], HEAD_DIM), BF16)
    pad1 = jnp.ones((v_all.shape[0], 2 * HEAD_DIM), BF16)
    for kv in range(N_KV_HEADS):
        vbuf[:, kv * V_AUG:(kv + 1) * V_AUG] = jnp.concatenate(
            [v_all[:, kv * HEAD_DIM:(kv + 1) * HEAD_DIM], pad0, pad1], axis=1)
    shape = (GROUP * WINDOW, 2 * WINDOW)
    qrow = lax.broadcasted_iota(jnp.int32, shape, 0) & (WINDOW - 1)
    col = lax.broadcasted_iota(jnp.int32, shape, 1)
    mask_cur = (col >= WINDOW) & (col - WINDOW <= qrow)
    mask_prev = (col < WINDOW) & (col > qrow)
    bias[0] = jnp.where(mask_cur | mask_prev, 0.0, -jnp.inf)
    bias[1] = jnp.where(mask_cur, 0.0, -jnp.inf)
    rows = lambda sb: slice(sb * WINDOW, (sb + 1) * WINDOW)
    window = lambda sb: slice(sb * WINDOW, (sb + 2) * WINDOW)
    for first in range(0, ATTN_TQ // WINDOW, ATTN_STAGE):
        blocks = range(first, first + ATTN_STAGE)
        scores, probs, sink_term, outs = {}, {}, {}, {}
        for sb in blocks:
            for kv in range(N_KV_HEADS):
                qg = jnp.concatenate([q_ref[rows(sb), h * HEAD_DIM:(h + 1) * HEAD_DIM]
                                      for h in range(kv * GROUP, (kv + 1) * GROUP)], axis=0)
                scores[sb, kv] = lax.dot_general(qg, kbuf[window(sb), kv * HEAD_DIM:(kv + 1) * HEAD_DIM],
                                                 (((1,), (1,)), ((), ())), preferred_element_type=F32)
        for sb in blocks:
            bias_sb = bias[jnp.where(n == 0, 1, 0)] if sb == 0 else bias[0]
            for kv in range(N_KV_HEADS):
                s = scores[sb, kv] + bias_sb
                p = []
                for g in range(GROUP):
                    h = kv * GROUP + g
                    sh = s[g * WINDOW:(g + 1) * WINDOW]
                    m = jnp.maximum(jnp.max(sh, axis=-1, keepdims=True), sink_ref[h])
                    p.append(jnp.exp(sh - m).astype(BF16))
                    sink_term[sb, h] = jnp.exp(sink_ref[h] - m)
                probs[sb, kv] = jnp.concatenate(p, axis=0)
        for sb in blocks:
            for kv in range(N_KV_HEADS):
                outs[sb, kv] = _mm(probs[sb, kv], vbuf[window(sb), kv * V_AUG:(kv + 1) * V_AUG])
        for sb in blocks:
            for h in range(N_HEADS):
                oh = outs[sb, h // GROUP][(h % GROUP) * WINDOW:(h % GROUP + 1) * WINDOW]
                o = oh[:, :2 * HEAD_DIM] / (oh[:, 2 * HEAD_DIM:] + sink_term[sb, h])
                o_ref[rows(sb), h * HEAD_DIM:(h + 1) * HEAD_DIM] = o[:, :HEAD_DIM].astype(BF16)


def _prompt_attention(q, k, v, sinks, batch, seq):
    nq = seq // ATTN_TQ
    per = ATTN_TQ // WINDOW
    cur = lambda b, n: (b * nq + n, 0)
    prev = lambda b, n: (jnp.maximum(b * nq * per + n * per - 1, b * nq * per), 0)
    return pl.pallas_call(
        _attn_kernel,
        grid=(batch, nq),
        in_specs=[
            pl.BlockSpec(memory_space=pltpu.SMEM),
            pl.BlockSpec((ATTN_TQ, N_HEADS * HEAD_DIM), cur),
            pl.BlockSpec((ATTN_TQ, KV_DIM), cur),
            pl.BlockSpec((WINDOW, KV_DIM), prev),
            pl.BlockSpec((ATTN_TQ, KV_DIM), cur),
            pl.BlockSpec((WINDOW, KV_DIM), prev),
        ],
        out_specs=pl.BlockSpec((ATTN_TQ, N_HEADS * HEAD_DIM), cur),
        out_shape=jax.ShapeDtypeStruct(q.shape, BF16),
        scratch_shapes=[pltpu.VMEM((ATTN_TQ + WINDOW, KV_DIM), BF16),
                        pltpu.VMEM((ATTN_TQ + WINDOW, N_KV_HEADS * V_AUG), BF16),
                        pltpu.VMEM((2, GROUP * WINDOW, 2 * WINDOW), F32)],
        compiler_params=_cparams(("parallel", "parallel")),
        name="prompt_attention",
    )(sinks, q, k, k, v, v)


def _attn_sample_kernel(sink_ref, q_ref, ck_ref, cv_ref, kn_ref, vn_ref, o_ref, kw_ref, vw_ref):
    n = ck_ref.shape[1]
    for cache, new, win in ((ck_ref, kn_ref, kw_ref), (cv_ref, vn_ref, vw_ref)):
        win[:, 0:n - 1, :] = cache[:, 1:n, :]
        win[:, n - 1:n, :] = new[...]
    s = jnp.einsum("bhc,blc->bhl", q_ref[...], kw_ref[...].astype(BF16),
                   preferred_element_type=F32)
    sink = sink_ref[...]
    m = jnp.maximum(jnp.max(s, axis=-1, keepdims=True), sink)
    p = jnp.exp(s - m)
    denom = jnp.sum(p, axis=-1, keepdims=True) + jnp.exp(sink - m)
    o_ref[...] = jnp.einsum("bhl,blc->bhc", (p / denom).astype(BF16), vw_ref[...].astype(BF16),
                            preferred_element_type=F32)


def _sample_attention(q_blk, cache_k, cache_v, k_new, v_new, sinks, bt=16):
    b = q_blk.shape[0]
    blk = lambda i: (i, 0, 0)
    win = pl.BlockSpec((bt, WINDOW, KV_DIM), blk)
    row = pl.BlockSpec((bt, 1, KV_DIM), blk)
    return pl.pallas_call(
        _attn_sample_kernel,
        grid=(b // bt,),
        in_specs=[_const_spec((1, N_HEADS, 1)), pl.BlockSpec((bt, N_HEADS, KV_DIM), blk), win, win, row, row],
        out_specs=[pl.BlockSpec((bt, N_HEADS, KV_DIM), blk), win, win],
        out_shape=[jax.ShapeDtypeStruct((b, N_HEADS, KV_DIM), F32),
                   jax.ShapeDtypeStruct(cache_k.shape, F32), jax.ShapeDtypeStruct(cache_v.shape, F32)],
        compiler_params=_cparams(("parallel",)),
        name="sample_attention",
    )(sinks.reshape(1, N_HEADS, 1), q_blk, cache_k, cache_v, k_new, v_new)


def _ple(h, p, g, wg, bg, wp):
    gate = jax.nn.sigmoid(_mm(_rms(h, g).astype(BF16), wg) + bg)
    return h + gate * _mm(p.astype(BF16), wp)


def _layer0_tail_math(x_ref, o_ref, p, wo_ref, bo_ref, gf_ref, wg_ref, wu_ref, wd_ref,
                      gp_ref, wpg_ref, bpg_ref, wp_ref, gm_ref):
    h = x_ref[...] + _mm(o_ref[...], wo_ref[...]) + bo_ref[...]
    hn = _rms(h, gf_ref[...]).astype(BF16)
    f = None
    for c0, c1 in zip(FF_CHUNK_EDGES[:-1], FF_CHUNK_EDGES[1:]):
        a = (_silu(_mm(hn, wg_ref[:, c0:c1])) * _mm(hn, wu_ref[:, c0:c1])).astype(BF16)
        part = _mm(a, wd_ref[c0:c1, :])
        f = part if f is None else f + part
    h = h + f
    h = _ple(h, p, gp_ref[...], wpg_ref[...], bpg_ref[...], wp_ref[...])
    return h, _rms(h, gm_ref[...])


def _layer0_tail_kernel(x_ref, o_ref, p_ref, wo_ref, bo_ref, gf_ref, wg_ref, wu_ref, wd_ref,
                        gp_ref, wpg_ref, bpg_ref, wp_ref, gm_ref, h_ref, hn_ref):
    h_ref[...], hn_ref[...] = _layer0_tail_math(x_ref, o_ref, p_ref[0], wo_ref, bo_ref, gf_ref, wg_ref,
                                                wu_ref, wd_ref, gp_ref, wpg_ref, bpg_ref, wp_ref, gm_ref)


def _layer0_tail(x, o, p, wo, bo, gf, wg, wu, wd, gp, wpg, bpg, wp, gm, tm):
    t = x.shape[0]
    row = lambda i: (i, 0)
    consts = [wo, bo, gf, wg, wu, wd, gp, wpg, bpg, wp, gm]
    return pl.pallas_call(
        _layer0_tail_kernel,
        grid=(t // tm,),
        in_specs=[pl.BlockSpec((tm, D_MODEL), row), pl.BlockSpec((tm, D_MODEL), row),
                  pl.BlockSpec((1, tm, PLE_DIM), lambda i: (0, i, 0))] + [_const_spec(c.shape) for c in consts],
        out_specs=[pl.BlockSpec((tm, D_MODEL), row), pl.BlockSpec((tm, D_MODEL), row)],
        out_shape=[jax.ShapeDtypeStruct((t, D_MODEL), F32), jax.ShapeDtypeStruct((t, D_MODEL), F32)],
        compiler_params=_cparams(("parallel",)),
        name="layer0_tail",
    )(x, o, p, *consts)


def _segment_ranks(sel, tile, lr_out, lrt_out, cnt_out):
    r = lax.broadcasted_iota(jnp.int32, (tile, tile), 0)
    c = lax.broadcasted_iota(jnp.int32, (tile, tile), 1)
    earlier = (c < r).astype(BF16)
    pad = jnp.full((tile, LANES - N_EXPERTS), -1.0, F32)
    for k in range(sel.shape[0] // tile):
        rows = slice(k * tile, (k + 1) * tile)
        sel_k = sel[rows]
        rank = _mm(earlier, sel_k.astype(BF16))
        lr = jnp.where(sel_k, rank, -1.0)
        lr_out[rows, :] = lr.astype(jnp.int32)
        lrt_out[:, rows] = jnp.concatenate([lr, pad], axis=1).T[:N_EXPERTS].astype(jnp.int32)
        cnt_out[k] = jnp.sum(sel_k.astype(F32), axis=0, keepdims=True).astype(jnp.int32)


def _pool_project_route(h, hn, pooled_sum_inv, wpool_ref, ps_ref, gf_ref, wr_ref, br_ref,
                        h_out, hn_out, gates_out, lr_out, lrt_out, cnt_out, tile):
    mixed = []
    for g in range(len(POOL_WINDOWS)):
        cols = slice(g * POOL_GROUP_DIM, (g + 1) * POOL_GROUP_DIM)
        pooled = pooled_sum_inv[g] - hn[:, cols]
        mixed.append(_mm(pooled.astype(BF16), wpool_ref[g]))
    h = h + jnp.concatenate(mixed, axis=-1) * ps_ref[...]
    h_out[...] = h
    hn2 = _rms(h, gf_ref[...])
    hi = hn2.astype(BF16)
    hn_out[...] = hi
    lo = (hn2 - hi.astype(F32)).astype(BF16)
    by_hi = _mm(hi, wr_ref[...])
    logits = (by_hi[:, :N_EXPERTS] + by_hi[:, N_EXPERTS:] + _mm(lo, wr_ref[:, :N_EXPERTS])) + br_ref[...]
    idx = lax.broadcasted_iota(jnp.int32, logits.shape, 1)
    m1 = jnp.max(logits, axis=-1, keepdims=True)
    i1 = jnp.min(jnp.where(logits == m1, idx, N_EXPERTS), axis=-1, keepdims=True)
    rest = jnp.where(idx == i1, -jnp.inf, logits)
    m2 = jnp.max(rest, axis=-1, keepdims=True)
    i2 = jnp.min(jnp.where(rest == m2, idx, N_EXPERTS), axis=-1, keepdims=True)
    e = jnp.exp(m2 - m1)
    w1 = 1.0 / (1.0 + e)
    w2 = e / (1.0 + e)
    gates_out[...] = jnp.where(idx == i1, w1, jnp.where(idx == i2, w2, 0.0))
    _segment_ranks((idx == i1) | (idx == i2), tile, lr_out, lrt_out, cnt_out)


def _route_out_specs(t, tm, tile, blk):
    specs = [pl.BlockSpec((tm, N_EXPERTS), lambda *g: (blk(*g), 0)),
             pl.BlockSpec((tm, N_EXPERTS), lambda *g: (blk(*g), 0)),
             pl.BlockSpec((N_EXPERTS, tm), lambda *g: (0, blk(*g))),
             pl.BlockSpec((tm // tile, 1, N_EXPERTS), lambda *g: (blk(*g), 0, 0))]
    shapes = [jax.ShapeDtypeStruct((t, N_EXPERTS), F32),
              jax.ShapeDtypeStruct((t, N_EXPERTS), jnp.int32),
              jax.ShapeDtypeStruct((N_EXPERTS, t), jnp.int32),
              jax.ShapeDtypeStruct((t // tile, 1, N_EXPERTS), jnp.int32)]
    return specs, shapes


def _pool_prompt_kernel(h_ref, hn_ref, wpool_ref, ps_ref, gf_ref, wr_ref, br_ref,
                        h_out, hn_out, gates_out, lr_out, lrt_out, cnt_out, last_out, carry, buf, *, tile):
    b = pl.program_id(0)
    n = pl.program_id(1)
    tm = h_ref.shape[0]
    n_seq = pl.num_programs(0) - 1

    @pl.when(b < n_seq)
    def _():
        @pl.when(n == 0)
        def _():
            carry[...] = jnp.zeros_like(carry)

        hn = hn_ref[...]
        buf[0:POOL_HALO] = carry[...]
        buf[POOL_HALO:] = hn
        carry[...] = hn[tm - POOL_HALO:]
        last_out[0] = hn[tm - POOL_HALO:]
        pos = n * tm + lax.broadcasted_iota(jnp.int32, (tm, 1), 0)
        means = []
        for g, w in enumerate(POOL_WINDOWS):
            s = buf[:, g * POOL_GROUP_DIM:(g + 1) * POOL_GROUP_DIM]
            shift = 1
            while shift < w:
                s = s + pltpu.roll(s, shift, 0)
                shift *= 2
            cnt = jnp.minimum(w, pos + 1).astype(F32)
            means.append(s[POOL_HALO:] * (1.0 / cnt))
        _pool_project_route(h_ref[...], hn, means, wpool_ref, ps_ref, gf_ref, wr_ref, br_ref,
                            h_out, hn_out, gates_out, lr_out, lrt_out, cnt_out, tile)

    @pl.when(b == n_seq)
    def _():
        h_out[...] = jnp.zeros_like(h_out)
        hn_out[...] = jnp.zeros_like(hn_out)
        gates_out[...] = jnp.zeros_like(gates_out)
        lr_out[...] = jnp.full(lr_out.shape, -1, jnp.int32)
        lrt_out[...] = jnp.full(lrt_out.shape, -1, jnp.int32)
        cnt_out[...] = jnp.zeros_like(cnt_out)


def _pool_prompt(h, hn, wpool, ps, gf, wr, br, batch, seq, tm, tile, t_pad):
    ns = seq // tm
    pad_blk = batch * ns
    blk = lambda b, n: jnp.where(b < batch, b * ns + n, pad_blk)
    src = lambda b, n: (jnp.minimum(b * ns + n, pad_blk - 1), 0)
    consts = [wpool, ps, gf, wr, br]
    route_specs, route_shapes = _route_out_specs(t_pad, tm, tile, blk)
    row = lambda b, n: (blk(b, n), 0)
    return pl.pallas_call(
        functools.partial(_pool_prompt_kernel, tile=tile),
        grid=(batch + 1, ns),
        in_specs=[pl.BlockSpec((tm, D_MODEL), src), pl.BlockSpec((tm, D_MODEL), src)]
        + [_const_spec(c.shape) for c in consts],
        out_specs=[pl.BlockSpec((tm, D_MODEL), row), pl.BlockSpec((tm, D_MODEL), row)] + route_specs
        + [pl.BlockSpec((1, POOL_HALO, D_MODEL), lambda b, n: (jnp.minimum(b, batch - 1), 0, 0))],
        out_shape=[jax.ShapeDtypeStruct((t_pad, D_MODEL), F32), jax.ShapeDtypeStruct((t_pad, D_MODEL), BF16)]
        + route_shapes + [jax.ShapeDtypeStruct((batch, POOL_HALO, D_MODEL), F32)],
        scratch_shapes=[pltpu.VMEM((POOL_HALO, D_MODEL), F32),
                        pltpu.VMEM((tm + POOL_HALO, D_MODEL), F32)],
        compiler_params=_cparams(("arbitrary", "arbitrary")),
        name="pool_prompt",
    )(h, hn, *consts)


def _pool_sample_kernel(h_ref, hn_ref, hist_ref, wpool_ref, ps_ref, gf_ref, wr_ref, br_ref, *rest):
    outs = rest[len(rest) // 2:]
    hn = hn_ref[...]
    means = []
    for g, w in enumerate(POOL_WINDOWS):
        cols = slice(g * POOL_GROUP_DIM, (g + 1) * POOL_GROUP_DIM)
        s = hn[:, cols]
        for j in range(1, w):
            s = s + hist_ref[POOL_HIST - j, :, cols]
        means.append(s * (1.0 / min(w, PAST_LEN + 1)))
    _pool_project_route(h_ref[...], hn, means, wpool_ref, ps_ref, gf_ref, wr_ref, br_ref,
                        *outs, hn.shape[0])


def _pool_sample(h, hn, hist_t, wpool, ps, gf, wr, br, padded, row0, tile):
    t = h.shape[0]
    args = [h, hn, hist_t, wpool, ps, gf, wr, br]
    blk = row0 // t
    out_specs = [pl.BlockSpec((t, D_MODEL), lambda i: (blk, 0)), pl.BlockSpec((t, D_MODEL), lambda i: (blk, 0)),
                 pl.BlockSpec((t, N_EXPERTS), lambda i: (blk, 0)), pl.BlockSpec((t, N_EXPERTS), lambda i: (blk, 0)),
                 pl.BlockSpec((N_EXPERTS, t), lambda i: (0, blk)),
                 pl.BlockSpec((1, 1, N_EXPERTS), lambda i: (row0 // tile, 0, 0))]
    return pl.pallas_call(
        _pool_sample_kernel,
        grid=(1,),
        in_specs=[_const_spec(a.shape) for a in args] + [pl.BlockSpec(memory_space=pl.ANY)] * len(padded),
        out_specs=out_specs,
        out_shape=[jax.ShapeDtypeStruct(a.shape, a.dtype) for a in padded],
        input_output_aliases={len(args) + k: k for k in range(len(padded))},
        compiler_params=_cparams(("arbitrary",)),
        name="pool_sample",
    )(*args, *padded)


def _segment_copies(meta_ref, step, tile, vmem_of, hbm, sem_of, to_hbm):
    half = tile // 2
    out = []
    for e in range(N_EXPERTS):
        pos = meta_ref[0, step * N_EXPERTS + e]
        cnt = meta_ref[1, step * N_EXPERTS + e]
        for k, pred in ((0, cnt > 0), (1, cnt > half)):
            v = vmem_of(e).at[pl.ds(k * half, half)]
            h = hbm.at[pl.ds(pl.multiple_of(pos + k * half, ROW_ALIGN), half)]
            copy = pltpu.make_async_copy(v, h, sem_of(e)) if to_hbm else pltpu.make_async_copy(h, v, sem_of(e))
            out.append((pred, copy))
    return out


def _for_slot(slot, fn):
    for s in range(2):
        @pl.when(slot == s)
        def _(s=s):
            fn(s)


def _start_all(copies):
    for pred, copy in copies:
        @pl.when(pred)
        def _(copy=copy):
            copy.start()


def _wait_all(copies):
    for pred, copy in copies:
        @pl.when(pred)
        def _(copy=copy):
            copy.wait()


def _dispatch_kernel(meta_ref, tail_ref, hn_ref, lrt_ref, xs_ref, seg, zeros, sem, zsem):
    i = pl.program_id(0)
    last = pl.num_programs(0) - 1
    tile = hn_ref.shape[0]
    slot = i % 2
    copies = lambda step, s: _segment_copies(meta_ref, step, tile, lambda e: seg.at[s, e], xs_ref,
                                             lambda e: sem.at[s, e], True)
    hn = hn_ref[...]
    half = tile // 2
    place = lax.broadcasted_iota(jnp.int32, (half, tile), 0)
    for e0 in range(0, N_EXPERTS, 4):
        first = jnp.concatenate([(place == lrt_ref[e:e + 1, :]).astype(BF16) for e in range(e0, e0 + 4)], axis=0)
        rows = _mm(first, hn).astype(BF16)
        for k in range(4):
            seg[slot, e0 + k, 0:half] = rows[k * half:(k + 1) * half]
    for e in range(N_EXPERTS):
        @pl.when(meta_ref[1, i * N_EXPERTS + e] > half)
        def _(e=e):
            seg[slot, e, half:] = _mm((place + half == lrt_ref[e:e + 1, :]).astype(BF16),
                                      hn_ref[...]).astype(BF16)

    @pl.when(i > 0)
    def _():
        _for_slot(1 - slot, lambda s: _wait_all(copies(i - 1, s)))

    _for_slot(slot, lambda s: _start_all(copies(i, s)))

    @pl.when(i == last)
    def _():
        _for_slot(slot, lambda s: _wait_all(copies(i, s)))
        zeros[...] = jnp.zeros_like(zeros)
        tails = []
        for e in range(N_EXPERTS):
            start = tail_ref[0, e]
            cnt = tail_ref[1, e]
            for piece in TAIL_PIECES:
                off = pl.multiple_of(cnt & ~(2 * piece - 1), ROW_ALIGN)
                copy = pltpu.make_async_copy(
                    zeros.at[pl.ds(0, piece)],
                    xs_ref.at[pl.ds(pl.multiple_of(start + off, ROW_ALIGN), piece)],
                    zsem.at[e])
                tails.append(((cnt & piece) != 0, copy))
        _start_all(tails)
        _wait_all(tails)

        def slack_copy(c):
            row = pl.multiple_of(tail_ref[0, N_EXPERTS] + c * SLACK_CHUNK, SLACK_CHUNK)
            return pltpu.make_async_copy(zeros.at[pl.ds(0, SLACK_CHUNK)],
                                         xs_ref.at[pl.ds(row, SLACK_CHUNK)], zsem.at[0])

        n_slack = tail_ref[1, N_EXPERTS] // SLACK_CHUNK
        lax.fori_loop(0, n_slack, lambda c, _: slack_copy(c).start(), None)
        lax.fori_loop(0, n_slack, lambda c, _: slack_copy(c).wait(), None)


def _dispatch(hn, lr_t, meta, tail, rows, tile):
    t = hn.shape[0]
    return pl.pallas_call(
        _dispatch_kernel,
        grid_spec=pltpu.PrefetchScalarGridSpec(
            num_scalar_prefetch=2,
            grid=(t // tile,),
            in_specs=[pl.BlockSpec((tile, D_MODEL), lambda i, *_: (i, 0)),
                      pl.BlockSpec((N_EXPERTS, tile), lambda i, *_: (0, i))],
            out_specs=pl.BlockSpec(memory_space=pl.ANY),
            scratch_shapes=[pltpu.VMEM((2, N_EXPERTS, tile, D_MODEL), BF16),
                            pltpu.VMEM((TAIL_PIECES[0], D_MODEL), BF16),
                            pltpu.SemaphoreType.DMA((2, N_EXPERTS)),
                            pltpu.SemaphoreType.DMA((N_EXPERTS,))]),
        out_shape=jax.ShapeDtypeStruct((rows, D_MODEL), BF16),
        compiler_params=_cparams(("arbitrary",)),
        name="expert_dispatch",
    )(meta, tail, hn, lr_t)


def _expert_kernel(te_ref, tr_ref, x_ref, wg_ref, wu_ref, wd_ref, y_ref, acc):
    r = pl.program_id(0)
    j = pl.program_id(1)

    @pl.when(j == 0)
    def _():
        acc[...] = jnp.zeros_like(acc)

    def ffn(m):
        x = x_ref[0:m]
        a = (_silu(_mm(x, wg_ref[0].astype(BF16))) * _mm(x, wu_ref[0].astype(BF16))).astype(BF16)
        acc[0:m] += _mm(a, wd_ref[0].astype(BF16))

    rows = tr_ref[r]
    step = x_ref.shape[0] // EXPERT_ROW_STEPS
    for k in range(1, EXPERT_ROW_STEPS + 1):
        @pl.when((rows > (k - 1) * step) & (rows <= k * step))
        def _(k=k):
            ffn(k * step)

    @pl.when(j == pl.num_programs(1) - 1)
    def _():
        y_ref[...] = acc[...].astype(BF16)


def _expert_ffn(xs, tile_e, tile_rows, wg, wu, wd, tmd):
    rows = xs.shape[0]
    nj = D_FF_EXPERT // EXPERT_FF_CHUNK
    live = lambda r, tr: tr[r] > 0
    jj = lambda r, j, tr: jnp.where(live(r, tr), j, nj - 1)
    return pl.pallas_call(
        _expert_kernel,
        grid_spec=pltpu.PrefetchScalarGridSpec(
            num_scalar_prefetch=2,
            grid=(rows // tmd, nj),
            in_specs=[
                pl.BlockSpec((tmd, D_MODEL), lambda r, j, te, tr: (jnp.where(live(r, tr), r, 0), 0)),
                pl.BlockSpec((1, D_MODEL, EXPERT_FF_CHUNK), lambda r, j, te, tr: (te[r], 0, jj(r, j, tr))),
                pl.BlockSpec((1, D_MODEL, EXPERT_FF_CHUNK), lambda r, j, te, tr: (te[r], 0, jj(r, j, tr))),
                pl.BlockSpec((1, EXPERT_FF_CHUNK, D_MODEL), lambda r, j, te, tr: (te[r], jj(r, j, tr), 0)),
            ],
            out_specs=pl.BlockSpec((tmd, D_MODEL), lambda r, j, te, tr: (r, 0)),
            scratch_shapes=[pltpu.VMEM((tmd, D_MODEL), F32)]),
        out_shape=jax.ShapeDtypeStruct((rows, D_MODEL), BF16),
        compiler_params=_cparams(("parallel", "arbitrary")),
        name="expert_ffn",
    )(tile_e, tile_rows, xs, wg, wu, wd)


def _combine_kernel(meta_ref, h_ref, gates_ref, lr_ref, pp_ref, ps_ref, gp_ref, wpg_ref, bpg_ref, wp_ref,
                    gfin_ref, ys_ref, outp_ref, outs_ref, ybuf, f_ref, sem, *, n_prompt_tiles):
    i = pl.program_id(0)
    n = pl.num_programs(0)
    tile = h_ref.shape[0]
    slot = i % 2
    copies = lambda step, s: _segment_copies(meta_ref, step, tile, lambda e: ybuf.at[s, e], ys_ref,
                                             lambda e: sem.at[s, e], False)

    @pl.when(i == 0)
    def _():
        ybuf[...] = jnp.zeros_like(ybuf)
        _start_all(copies(0, 0))

    @pl.when(i + 1 < n)
    def _():
        _for_slot(1 - slot, lambda s: _start_all(copies(i + 1, s)))

    _for_slot(slot, lambda s: _wait_all(copies(i, s)))
    gates = gates_ref[...]
    lr = lr_ref[...]
    half = tile // 2
    place = lax.broadcasted_iota(jnp.int32, (tile, half), 1)

    def gathered(offset):
        f = jnp.zeros((tile, D_MODEL), F32)
        for e in range(N_EXPERTS):
            onehot = (place + offset == lr[:, e:e + 1]).astype(BF16)
            f = f + gates[:, e:e + 1] * _mm(onehot, ybuf[slot, e, offset:offset + half, :])
        return f

    f_ref[...] = gathered(0)
    longest = meta_ref[1, i * N_EXPERTS]
    for e in range(1, N_EXPERTS):
        longest = jnp.maximum(longest, meta_ref[1, i * N_EXPERTS + e])

    @pl.when(longest > half)
    def _():
        f_ref[...] += gathered(half)

    h = h_ref[...] + f_ref[...]

    def head(h, p):
        h = _ple(h, p, gp_ref[...], wpg_ref[...], bpg_ref[...], wp_ref[...])
        return _rms(h, gfin_ref[...])

    @pl.when(i < n_prompt_tiles)
    def _():
        outp_ref[...] = head(h, pp_ref[0])

    @pl.when(i == n_prompt_tiles)
    def _():
        n_s = outs_ref.shape[0]
        outs_ref[...] = head(h[:n_s], ps_ref[0])


def _combine(h, gates, lr, pp, ps, gp, wpg, bpg, wp, gfin, ys, meta, tile, n_prompt):
    t = h.shape[0]
    n_pt = n_prompt // tile
    n_s = ps.shape[1]
    row = lambda i, *_: (i, 0)
    prow = lambda i, *_: (jnp.minimum(i, n_pt - 1), 0)
    consts = [gp, wpg, bpg, wp, gfin]
    return pl.pallas_call(
        functools.partial(_combine_kernel, n_prompt_tiles=n_pt),
        grid_spec=pltpu.PrefetchScalarGridSpec(
            num_scalar_prefetch=1,
            grid=(t // tile,),
            in_specs=[pl.BlockSpec((tile, D_MODEL), row), pl.BlockSpec((tile, N_EXPERTS), row),
                      pl.BlockSpec((tile, N_EXPERTS), row),
                      pl.BlockSpec((1, tile, PLE_DIM), lambda i, *_: (1,) + prow(i)),
                      pl.BlockSpec((1, n_s, PLE_DIM), lambda i, *_: (1, 0, 0))]
            + [_const_spec(c.shape) for c in consts] + [pl.BlockSpec(memory_space=pl.ANY)],
            out_specs=[pl.BlockSpec((tile, D_MODEL), prow),
                       pl.BlockSpec((n_s, D_MODEL), lambda i, *_: (0, 0))],
            scratch_shapes=[pltpu.VMEM((2, N_EXPERTS, tile, D_MODEL), BF16),
                            pltpu.VMEM((tile, D_MODEL), F32),
                            pltpu.SemaphoreType.DMA((2, N_EXPERTS))]),
        out_shape=[jax.ShapeDtypeStruct((n_prompt, D_MODEL), F32),
                   jax.ShapeDtypeStruct((n_s, D_MODEL), F32)],
        compiler_params=_cparams(("arbitrary",)),
        name="combine",
    )(meta, h, gates, lr, pp, ps, *consts, ys)


def _round_up(x, m):
    return (x + m - 1) // m * m


def _route_plan(cnt, t, tile, tmd):
    n_tiles = t // tile
    guard = tile // 2
    cnt8 = _round_up(cnt.reshape(n_tiles, N_EXPERTS), ROW_ALIGN)
    tile_base = jnp.cumsum(cnt8, axis=0) - cnt8
    total8 = cnt8.sum(axis=0)
    region = _round_up(total8 + guard, tmd)
    region_end = jnp.cumsum(region)
    off = region_end - region
    pos = off[None, :] + tile_base
    rows = _round_up(2 * t + (ROW_ALIGN - 1) * N_EXPERTS * n_tiles + N_EXPERTS * (tmd + guard), tmd)
    tile_start = jnp.arange(rows // tmd, dtype=jnp.int32) * tmd
    te = jnp.sum(region_end[None, :] <= tile_start[:, None], axis=1).astype(jnp.int32)
    tec = jnp.minimum(te, N_EXPERTS - 1)
    tile_rows = jnp.where(te < N_EXPERTS, jnp.clip(total8[tec] - (tile_start - off[tec]), 0, tmd), 0)
    last_e = jnp.max(jnp.where(tile_rows > 0, tec, 0))
    tile_e = jnp.where(tile_rows > 0, tec, last_e).astype(jnp.int32)
    meta = jnp.stack([pos.reshape(-1), cnt8.reshape(-1)]).astype(jnp.int32)
    tail = jnp.stack([jnp.append(off + total8, region_end[-1]),
                      jnp.append(region - total8, rows - region_end[-1])]).astype(jnp.int32)
    return dict(meta=meta, tail=tail, tile_e=tile_e, tile_rows=tile_rows.astype(jnp.int32), rows=rows)


def _moe_and_head(h, hn, gates, lr, lr_t, cnt, pp, ps, w, tile, tmd, n_prompt):
    plan = _route_plan(cnt, h.shape[0], tile, tmd)
    xs = _dispatch(hn, lr_t, plan["meta"], plan["tail"], plan["rows"], tile)
    ys = _expert_ffn(xs, plan["tile_e"], plan["tile_rows"], w["exp_gate"], w["exp_up"], w["exp_down"], tmd)
    return _combine(h, gates, lr, pp, ps, w["norm_ple1"], w["ple_gate1"], w["b_ple_gate1"], w["ple1"],
                    w["norm_final"], ys, plan["meta"], tile, n_prompt)


def _rope_tables(pos):
    half = HEAD_DIM // 2
    inv = 1.0 / (ROPE_THETA ** (jnp.arange(half, dtype=F32) / half))
    ang = pos.astype(F32)[:, None] * inv[None, :]
    cos = jnp.tile(jnp.cos(ang), (1, LANES // half))
    sin = jnp.sin(ang)
    sin = jnp.tile(jnp.concatenate([-sin, sin], axis=-1), (1, LANES // HEAD_DIM))
    return cos, sin


def kernel(x_prompt, x_sample, cache_k, cache_v, state_pool, p_prompt, p_sample, norm_mix, norm_ffn, norm_ple, norm_final, w_qkv, b_qkv, w_o, b_o, sinks, w_pool, pool_scale, w_ff_gate, w_ff_up, w_ff_down, w_router, b_router, w_exp_gate, w_exp_up, w_exp_down, w_ple, w_ple_gate, b_ple_gate):
    batch, seq, _ = x_prompt.shape
    dec = x_sample.shape[0]
    row2 = lambda a: a.reshape(1, -1)
    w = dict(
        exp_gate=w_exp_gate[0], exp_up=w_exp_up[0], exp_down=w_exp_down[0],
        norm_ple1=row2(norm_ple[1]), ple_gate1=w_ple_gate[1].astype(BF16), b_ple_gate1=row2(b_ple_gate[1]),
        ple1=w_ple[1].astype(BF16), norm_final=row2(norm_final))
    wqkv = w_qkv[0].astype(BF16)
    l0 = [w_o[0].astype(BF16), row2(b_o[0]), row2(norm_ffn[0]), w_ff_gate[0].astype(BF16),
          w_ff_up[0].astype(BF16), w_ff_down[0].astype(BF16), row2(norm_ple[0]),
          w_ple_gate[0].astype(BF16), row2(b_ple_gate[0]), w_ple[0].astype(BF16), row2(norm_mix[1])]
    wr_hi = w_router[0].astype(BF16)
    wr_lo = (w_router[0] - wr_hi.astype(F32)).astype(BF16)
    pool_w = [w_pool[0].astype(BF16), row2(pool_scale[0]), row2(norm_ffn[1]),
              jnp.concatenate([wr_hi, wr_lo], axis=1), row2(b_router[0])]

    tm, tile = 512, 512
    n_prompt = batch * seq
    t_pad = n_prompt + tm

    xp = x_prompt.reshape(n_prompt, D_MODEL)
    pp = p_prompt.reshape(p_prompt.shape[0], n_prompt, PLE_DIM)
    cos_p, sin_p = _rope_tables(jnp.arange(seq, dtype=jnp.int32))
    q, k, v = _qkv_rope(xp, row2(norm_mix[0]), wqkv, row2(b_qkv[0]), cos_p, sin_p, tm=tm)
    o = _prompt_attention(q, k, v, sinks[0], batch, seq)
    h1, hn1 = _layer0_tail(xp, o, pp, *l0, tm=tm)
    *padded, last = _pool_prompt(h1, hn1, *pool_w, batch=batch, seq=seq, tm=tm, tile=tile, t_pad=t_pad)
    kv_shape = (1, batch, WINDOW, N_KV_HEADS, HEAD_DIM)
    new_k_prompt = k.reshape(batch, seq, KV_DIM)[:, -WINDOW:].reshape(kv_shape)
    new_v_prompt = v.reshape(batch, seq, KV_DIM)[:, -WINDOW:].reshape(kv_shape)
    new_pool_prompt = last[:, -POOL_HIST:][None]

    xs_ = x_sample.reshape(dec, D_MODEL)
    ps = p_sample.reshape(p_sample.shape[0], dec, PLE_DIM)
    cos_s, sin_s = _rope_tables(jnp.full((dec,), PAST_LEN, jnp.int32))
    qs, ks, vs = _qkv_rope(xs_, row2(norm_mix[0]), wqkv, row2(b_qkv[0]), cos_s, sin_s, tm=dec)
    n_hist = cache_k.shape[2]
    assert n_hist == WINDOW, "the cached window is expected to be full"
    head_kv = jnp.arange(N_HEADS) // GROUP
    blk = (head_kv[:, None] == jnp.arange(N_KV_HEADS)[None, :]).astype(BF16)
    q_blk = (qs.reshape(dec, N_HEADS, 1, HEAD_DIM) * blk[None, :, :, None]).reshape(dec, N_HEADS, KV_DIM)
    o_blk, k_win, v_win = _sample_attention(q_blk, cache_k[0].reshape(dec, n_hist, KV_DIM),
                                            cache_v[0].reshape(dec, n_hist, KV_DIM),
                                            ks[:, None], vs[:, None], sinks[0])
    o_s = jnp.take_along_axis(o_blk.reshape(dec, N_HEADS, N_KV_HEADS, HEAD_DIM),
                              head_kv[None, :, None, None], axis=2).reshape(dec, N_HEADS * HEAD_DIM)
    h1s, hn1s = _layer0_tail(xs_, o_s.astype(BF16), ps, *l0, tm=dec)
    hist_t = jnp.swapaxes(state_pool[0], 0, 1)
    padded = _pool_sample(h1s, hn1s, hist_t, *pool_w, padded=padded, row0=n_prompt, tile=tile)
    new_k_sample = k_win.reshape(1, dec, n_hist, N_KV_HEADS, HEAD_DIM)
    new_v_sample = v_win.reshape(1, dec, n_hist, N_KV_HEADS, HEAD_DIM)
    new_pool_sample = jnp.concatenate([state_pool[0], hn1s[:, None]], axis=1)[:, -POOL_HIST:][None]

    y_prompt, y_sample = _moe_and_head(*padded, pp, ps, w, tile=tile, tmd=2048, n_prompt=n_prompt)

    return (y_prompt.reshape(batch, seq, D_MODEL), y_sample.reshape(dec, 1, D_MODEL),
            new_k_prompt, new_v_prompt, new_pool_prompt, new_k_sample, new_v_sample, new_pool_sample)
```

```python
import functools

import jax
import jax.numpy as jnp
from jax import lax
from jax.experimental import pallas as pl
from jax.experimental.pallas import tpu as pltpu

F32 = jnp.float32
BF16 = jnp.bfloat16

D_MODEL = 1024
HEAD_DIM = 64
N_HEADS = 16
N_KV_HEADS = 4
GROUP = N_HEADS // N_KV_HEADS
KV_DIM = N_KV_HEADS * HEAD_DIM
QKV_DIM = (N_HEADS + 2 * N_KV_HEADS) * HEAD_DIM
WINDOW = 128
ROPE_THETA = 10000.0
PAST_LEN = 16384
POOL_WINDOWS = (2, 4, 8, 16)
POOL_GROUP_DIM = D_MODEL // len(POOL_WINDOWS)
POOL_HIST = max(POOL_WINDOWS) - 1
POOL_HALO = 16
D_FF = 2816
N_EXPERTS = 8
D_FF_EXPERT = 3584
PLE_DIM = 256
EPS = 1e-6

LANES = 128
VMEM_LIMIT = 56 * 1024 * 1024

ATTN_TQ = 512
ATTN_STAGE = 2
MXU_DIM = 256
FF_CHUNK_EDGES = (0, 6 * MXU_DIM, D_FF)
EXPERT_FF_CHUNK = 512
EXPERT_ROW_STEPS = 4
ROW_ALIGN = 16
TAIL_PIECES = (2048, 1024, 512, 256, 128, 64, 32, 16)
SLACK_CHUNK = 256


def _cparams(sem):
    return pltpu.CompilerParams(dimension_semantics=sem, vmem_limit_bytes=VMEM_LIMIT)


def _const_spec(shape):
    nd = len(shape)
    return pl.BlockSpec(shape, lambda *_: (0,) * nd, pipeline_mode=pl.Buffered(1))


def _rms(x, g):
    return x * lax.rsqrt(jnp.mean(x * x, axis=-1, keepdims=True) + EPS) * g


def _mm(a, b):
    return jnp.dot(a, b, preferred_element_type=F32)


def _silu(x):
    return x * jax.nn.sigmoid(x)


def _qkv_kernel(x_ref, g_ref, w_ref, b_ref, cos_ref, sin_ref, q_ref, k_ref, v_ref):
    hn = _rms(x_ref[...], g_ref[...])
    qkv = _mm(hn.astype(BF16), w_ref[...]) + b_ref[...]
    cos = cos_ref[...]
    sin = sin_ref[...]
    lane = lax.broadcasted_iota(jnp.int32, cos.shape, 1)
    first_half = (lane % HEAD_DIM) < (HEAD_DIM // 2)

    def rope(xb):
        partner = jnp.where(first_half, pltpu.roll(xb, LANES - HEAD_DIM // 2, 1),
                            pltpu.roll(xb, HEAD_DIM // 2, 1))
        return xb * cos + partner * sin

    scale = HEAD_DIM ** -0.5
    for c in range(N_HEADS * HEAD_DIM // LANES):
        sl = slice(c * LANES, (c + 1) * LANES)
        q_ref[:, sl] = (rope(qkv[:, sl]) * scale).astype(BF16)
    for c in range(KV_DIM // LANES):
        sl = slice(c * LANES, (c + 1) * LANES)
        k_ref[:, sl] = rope(qkv[:, N_HEADS * HEAD_DIM + c * LANES:N_HEADS * HEAD_DIM + (c + 1) * LANES])
    v_ref[...] = qkv[:, N_HEADS * HEAD_DIM + KV_DIM:]


def _qkv_rope(x, g, w, b, cos, sin, tm):
    t = x.shape[0]
    n_pos_tiles = cos.shape[0] // tm
    row = lambda i: (i, 0)
    return pl.pallas_call(
        _qkv_kernel,
        grid=(t // tm,),
        in_specs=[
            pl.BlockSpec((tm, D_MODEL), row),
            _const_spec((1, D_MODEL)),
            _const_spec((D_MODEL, QKV_DIM)),
            _const_spec((1, QKV_DIM)),
            pl.BlockSpec((tm, LANES), lambda i: (i % n_pos_tiles, 0)),
            pl.BlockSpec((tm, LANES), lambda i: (i % n_pos_tiles, 0)),
        ],
        out_specs=[
            pl.BlockSpec((tm, N_HEADS * HEAD_DIM), row),
            pl.BlockSpec((tm, KV_DIM), row),
            pl.BlockSpec((tm, KV_DIM), row),
        ],
        out_shape=[
            jax.ShapeDtypeStruct((t, N_HEADS * HEAD_DIM), BF16),
            jax.ShapeDtypeStruct((t, KV_DIM), F32),
            jax.ShapeDtypeStruct((t, KV_DIM), F32),
        ],
        compiler_params=_cparams(("parallel",)),
        name="qkv_rope",
    )(x, g, w, b, cos, sin)


V_AUG = 4 * HEAD_DIM


def _attn_kernel(sink_ref, q_ref, kc_ref, kp_ref, vc_ref, vp_ref, o_ref, kbuf, vbuf, bias):
    n = pl.program_id(1)
    kbuf[0:WINDOW] = kp_ref[...].astype(BF16)
    kbuf[WINDOW:] = kc_ref[...].astype(BF16)
    v_all = jnp.concatenate([vp_ref[...], vc_ref[...]], axis=0).astype(BF16)
    pad0 = jnp.zeros((v_all.shape[0], HEAD_DIM), BF16)
    pad1 = jnp.ones((v_all.shape[0], 2 * HEAD_DIM), BF16)
    for kv in range(N_KV_HEADS):
        vbuf[:, kv * V_AUG:(kv + 1) * V_AUG] = jnp.concatenate(
            [v_all[:, kv * HEAD_DIM:(kv + 1) * HEAD_DIM], pad0, pad1], axis=1)
    shape = (GROUP * WINDOW, 2 * WINDOW)
    qrow = lax.broadcasted_iota(jnp.int32, shape, 0) & (WINDOW - 1)
    col = lax.broadcasted_iota(jnp.int32, shape, 1)
    mask_cur = (col >= WINDOW) & (col - WINDOW <= qrow)
    mask_prev = (col < WINDOW) & (col > qrow)
    bias[0] = jnp.where(mask_cur | mask_prev, 0.0, -jnp.inf)
    bias[1] = jnp.where(mask_cur, 0.0, -jnp.inf)
    rows = lambda sb: slice(sb * WINDOW, (sb + 1) * WINDOW)
    window = lambda sb: slice(sb * WINDOW, (sb + 2) * WINDOW)
    for first in range(0, ATTN_TQ // WINDOW, ATTN_STAGE):
        blocks = range(first, first + ATTN_STAGE)
        scores, probs, sink_term, outs = {}, {}, {}, {}
        for sb in blocks:
            for kv in range(N_KV_HEADS):
                qg = jnp.concatenate([q_ref[rows(sb), h * HEAD_DIM:(h + 1) * HEAD_DIM]
                                      for h in range(kv * GROUP, (kv + 1) * GROUP)], axis=0)
                scores[sb, kv] = lax.dot_general(qg, kbuf[window(sb), kv * HEAD_DIM:(kv + 1) * HEAD_DIM],
                                                 (((1,), (1,)), ((), ())), preferred_element_type=F32)
        for sb in blocks:
            bias_sb = bias[jnp.where(n == 0, 1, 0)] if sb == 0 else bias[0]
            for kv in range(N_KV_HEADS):
                s = scores[sb, kv] + bias_sb
                p = []
                for g in range(GROUP):
                    h = kv * GROUP + g
                    sh = s[g * WINDOW:(g + 1) * WINDOW]
                    m = jnp.maximum(jnp.max(sh, axis=-1, keepdims=True), sink_ref[h])
                    p.append(jnp.exp(sh - m).astype(BF16))
                    sink_term[sb, h] = jnp.exp(sink_ref[h] - m)
                probs[sb, kv] = jnp.concatenate(p, axis=0)
        for sb in blocks:
            for kv in range(N_KV_HEADS):
                outs[sb, kv] = _mm(probs[sb, kv], vbuf[window(sb), kv * V_AUG:(kv + 1) * V_AUG])
        for sb in blocks:
            for h in range(N_HEADS):
                oh = outs[sb, h // GROUP][(h % GROUP) * WINDOW:(h % GROUP + 1) * WINDOW]
                o = oh[:, :2 * HEAD_DIM] / (oh[:, 2 * HEAD_DIM:] + sink_term[sb, h])
                o_ref[rows(sb), h * HEAD_DIM:(h + 1) * HEAD_DIM] = o[:, :HEAD_DIM].astype(BF16)


def _prompt_attention(q, k, v, sinks, batch, seq):
    nq = seq // ATTN_TQ
    per = ATTN_TQ // WINDOW
    cur = lambda b, n: (b * nq + n, 0)
    prev = lambda b, n: (jnp.maximum(b * nq * per + n * per - 1, b * nq * per), 0)
    return pl.pallas_call(
        _attn_kernel,
        grid=(batch, nq),
        in_specs=[
            pl.BlockSpec(memory_space=pltpu.SMEM),
            pl.BlockSpec((ATTN_TQ, N_HEADS * HEAD_DIM), cur),
            pl.BlockSpec((ATTN_TQ, KV_DIM), cur),
            pl.BlockSpec((WINDOW, KV_DIM), prev),
            pl.BlockSpec((ATTN_TQ, KV_DIM), cur),
            pl.BlockSpec((WINDOW, KV_DIM), prev),
        ],
        out_specs=pl.BlockSpec((ATTN_TQ, N_HEADS * HEAD_DIM), cur),
        out_shape=jax.ShapeDtypeStruct(q.shape, BF16),
        scratch_shapes=[pltpu.VMEM((ATTN_TQ + WINDOW, KV_DIM), BF16),
                        pltpu.VMEM((ATTN_TQ + WINDOW, N_KV_HEADS * V_AUG), BF16),
                        pltpu.VMEM((2, GROUP * WINDOW, 2 * WINDOW), F32)],
        compiler_params=_cparams(("parallel", "parallel")),
        name="prompt_attention",
    )(sinks, q, k, k, v, v)


def _attn_sample_kernel(sink_ref, q_ref, k_ref, v_ref, o_ref):
    s = jnp.einsum("bhc,blc->bhl", q_ref[...], k_ref[...].astype(BF16),
                   preferred_element_type=F32)
    sink = sink_ref[...]
    m = jnp.maximum(jnp.max(s, axis=-1, keepdims=True), sink)
    p = jnp.exp(s - m)
    denom = jnp.sum(p, axis=-1, keepdims=True) + jnp.exp(sink - m)
    o_ref[...] = jnp.einsum("bhl,blc->bhc", (p / denom).astype(BF16), v_ref[...].astype(BF16),
                            preferred_element_type=F32)


def _sample_attention(q_blk, k_win, v_win, sinks, bt=16):
    b = q_blk.shape[0]
    blk = lambda i: (i, 0, 0)
    return pl.pallas_call(
        _attn_sample_kernel,
        grid=(b // bt,),
        in_specs=[
            _const_spec((1, N_HEADS, 1)),
            pl.BlockSpec((bt, N_HEADS, KV_DIM), blk),
            pl.BlockSpec((bt, WINDOW, KV_DIM), blk),
            pl.BlockSpec((bt, WINDOW, KV_DIM), blk),
        ],
        out_specs=pl.BlockSpec((bt, N_HEADS, KV_DIM), blk),
        out_shape=jax.ShapeDtypeStruct((b, N_HEADS, KV_DIM), F32),
        compiler_params=_cparams(("parallel",)),
        name="sample_attention",
    )(sinks.reshape(1, N_HEADS, 1), q_blk, k_win, v_win)


def _ple(h, p, g, wg, bg, wp):
    gate = jax.nn.sigmoid(_mm(_rms(h, g).astype(BF16), wg) + bg)
    return h + gate * _mm(p.astype(BF16), wp)


def _layer0_tail_math(x_ref, o_ref, p, wo_ref, bo_ref, gf_ref, wg_ref, wu_ref, wd_ref,
                      gp_ref, wpg_ref, bpg_ref, wp_ref, gm_ref):
    h = x_ref[...] + _mm(o_ref[...], wo_ref[...]) + bo_ref[...]
    hn = _rms(h, gf_ref[...]).astype(BF16)
    f = None
    for c0, c1 in zip(FF_CHUNK_EDGES[:-1], FF_CHUNK_EDGES[1:]):
        a = (_silu(_mm(hn, wg_ref[:, c0:c1])) * _mm(hn, wu_ref[:, c0:c1])).astype(BF16)
        part = _mm(a, wd_ref[c0:c1, :])
        f = part if f is None else f + part
    h = h + f
    h = _ple(h, p, gp_ref[...], wpg_ref[...], bpg_ref[...], wp_ref[...])
    return h, _rms(h, gm_ref[...])


def _layer0_tail_kernel(x_ref, o_ref, p_ref, wo_ref, bo_ref, gf_ref, wg_ref, wu_ref, wd_ref,
                        gp_ref, wpg_ref, bpg_ref, wp_ref, gm_ref, h_ref, hn_ref):
    h_ref[...], hn_ref[...] = _layer0_tail_math(x_ref, o_ref, p_ref[0], wo_ref, bo_ref, gf_ref, wg_ref,
                                                wu_ref, wd_ref, gp_ref, wpg_ref, bpg_ref, wp_ref, gm_ref)


def _layer0_tail(x, o, p, wo, bo, gf, wg, wu, wd, gp, wpg, bpg, wp, gm, tm):
    t = x.shape[0]
    row = lambda i: (i, 0)
    consts = [wo, bo, gf, wg, wu, wd, gp, wpg, bpg, wp, gm]
    return pl.pallas_call(
        _layer0_tail_kernel,
        grid=(t // tm,),
        in_specs=[pl.BlockSpec((tm, D_MODEL), row), pl.BlockSpec((tm, D_MODEL), row),
                  pl.BlockSpec((1, tm, PLE_DIM), lambda i: (0, i, 0))] + [_const_spec(c.shape) for c in consts],
        out_specs=[pl.BlockSpec((tm, D_MODEL), row), pl.BlockSpec((tm, D_MODEL), row)],
        out_shape=[jax.ShapeDtypeStruct((t, D_MODEL), F32), jax.ShapeDtypeStruct((t, D_MODEL), F32)],
        compiler_params=_cparams(("parallel",)),
        name="layer0_tail",
    )(x, o, p, *consts)


def _segment_ranks(sel, tile, lr_out, lrt_out, cnt_out):
    r = lax.broadcasted_iota(jnp.int32, (tile, tile), 0)
    c = lax.broadcasted_iota(jnp.int32, (tile, tile), 1)
    earlier = (c < r).astype(BF16)
    pad = jnp.full((tile, LANES - N_EXPERTS), -1.0, F32)
    for k in range(sel.shape[0] // tile):
        rows = slice(k * tile, (k + 1) * tile)
        sel_k = sel[rows]
        rank = _mm(earlier, sel_k.astype(BF16))
        lr = jnp.where(sel_k, rank, -1.0)
        lr_out[rows, :] = lr.astype(jnp.int32)
        lrt_out[:, rows] = jnp.concatenate([lr, pad], axis=1).T[:N_EXPERTS].astype(jnp.int32)
        cnt_out[k] = jnp.sum(sel_k.astype(F32), axis=0, keepdims=True).astype(jnp.int32)


def _pool_project_route(h, hn, pooled_sum_inv, wpool_ref, ps_ref, gf_ref, wr_ref, br_ref,
                        h_out, hn_out, gates_out, lr_out, lrt_out, cnt_out, tile):
    mixed = []
    for g in range(len(POOL_WINDOWS)):
        cols = slice(g * POOL_GROUP_DIM, (g + 1) * POOL_GROUP_DIM)
        pooled = pooled_sum_inv[g] - hn[:, cols]
        mixed.append(_mm(pooled.astype(BF16), wpool_ref[g]))
    h = h + jnp.concatenate(mixed, axis=-1) * ps_ref[...]
    h_out[...] = h
    hn2 = _rms(h, gf_ref[...])
    hi = hn2.astype(BF16)
    hn_out[...] = hi
    lo = (hn2 - hi.astype(F32)).astype(BF16)
    by_hi = _mm(hi, wr_ref[...])
    logits = (by_hi[:, :N_EXPERTS] + by_hi[:, N_EXPERTS:] + _mm(lo, wr_ref[:, :N_EXPERTS])) + br_ref[...]
    idx = lax.broadcasted_iota(jnp.int32, logits.shape, 1)
    m1 = jnp.max(logits, axis=-1, keepdims=True)
    i1 = jnp.min(jnp.where(logits == m1, idx, N_EXPERTS), axis=-1, keepdims=True)
    rest = jnp.where(idx == i1, -jnp.inf, logits)
    m2 = jnp.max(rest, axis=-1, keepdims=True)
    i2 = jnp.min(jnp.where(rest == m2, idx, N_EXPERTS), axis=-1, keepdims=True)
    e = jnp.exp(m2 - m1)
    w1 = 1.0 / (1.0 + e)
    w2 = e / (1.0 + e)
    gates_out[...] = jnp.where(idx == i1, w1, jnp.where(idx == i2, w2, 0.0))
    _segment_ranks((idx == i1) | (idx == i2), tile, lr_out, lrt_out, cnt_out)


def _route_out_specs(t, tm, tile, blk):
    specs = [pl.BlockSpec((tm, N_EXPERTS), lambda *g: (blk(*g), 0)),
             pl.BlockSpec((tm, N_EXPERTS), lambda *g: (blk(*g), 0)),
             pl.BlockSpec((N_EXPERTS, tm), lambda *g: (0, blk(*g))),
             pl.BlockSpec((tm // tile, 1, N_EXPERTS), lambda *g: (blk(*g), 0, 0))]
    shapes = [jax.ShapeDtypeStruct((t, N_EXPERTS), F32),
              jax.ShapeDtypeStruct((t, N_EXPERTS), jnp.int32),
              jax.ShapeDtypeStruct((N_EXPERTS, t), jnp.int32),
              jax.ShapeDtypeStruct((t // tile, 1, N_EXPERTS), jnp.int32)]
    return specs, shapes


def _pool_prompt_kernel(h_ref, hn_ref, wpool_ref, ps_ref, gf_ref, wr_ref, br_ref,
                        h_out, hn_out, gates_out, lr_out, lrt_out, cnt_out, last_out, carry, buf, *, tile):
    b = pl.program_id(0)
    n = pl.program_id(1)
    tm = h_ref.shape[0]
    n_seq = pl.num_programs(0) - 1

    @pl.when(b < n_seq)
    def _():
        @pl.when(n == 0)
        def _():
            carry[...] = jnp.zeros_like(carry)

        hn = hn_ref[...]
        buf[0:POOL_HALO] = carry[...]
        buf[POOL_HALO:] = hn
        carry[...] = hn[tm - POOL_HALO:]
        last_out[0] = hn[tm - POOL_HALO:]
        pos = n * tm + lax.broadcasted_iota(jnp.int32, (tm, 1), 0)
        means = []
        for g, w in enumerate(POOL_WINDOWS):
            s = buf[:, g * POOL_GROUP_DIM:(g + 1) * POOL_GROUP_DIM]
            shift = 1
            while shift < w:
                s = s + pltpu.roll(s, shift, 0)
                shift *= 2
            cnt = jnp.minimum(w, pos + 1).astype(F32)
            means.append(s[POOL_HALO:] * (1.0 / cnt))
        _pool_project_route(h_ref[...], hn, means, wpool_ref, ps_ref, gf_ref, wr_ref, br_ref,
                            h_out, hn_out, gates_out, lr_out, lrt_out, cnt_out, tile)

    @pl.when(b == n_seq)
    def _():
        h_out[...] = jnp.zeros_like(h_out)
        hn_out[...] = jnp.zeros_like(hn_out)
        gates_out[...] = jnp.zeros_like(gates_out)
        lr_out[...] = jnp.full(lr_out.shape, -1, jnp.int32)
        lrt_out[...] = jnp.full(lrt_out.shape, -1, jnp.int32)
        cnt_out[...] = jnp.zeros_like(cnt_out)


def _pool_prompt(h, hn, wpool, ps, gf, wr, br, batch, seq, tm, tile, t_pad):
    ns = seq // tm
    pad_blk = batch * ns
    blk = lambda b, n: jnp.where(b < batch, b * ns + n, pad_blk)
    src = lambda b, n: (jnp.minimum(b * ns + n, pad_blk - 1), 0)
    consts = [wpool, ps, gf, wr, br]
    route_specs, route_shapes = _route_out_specs(t_pad, tm, tile, blk)
    row = lambda b, n: (blk(b, n), 0)
    return pl.pallas_call(
        functools.partial(_pool_prompt_kernel, tile=tile),
        grid=(batch + 1, ns),
        in_specs=[pl.BlockSpec((tm, D_MODEL), src), pl.BlockSpec((tm, D_MODEL), src)]
        + [_const_spec(c.shape) for c in consts],
        out_specs=[pl.BlockSpec((tm, D_MODEL), row), pl.BlockSpec((tm, D_MODEL), row)] + route_specs
        + [pl.BlockSpec((1, POOL_HALO, D_MODEL), lambda b, n: (jnp.minimum(b, batch - 1), 0, 0))],
        out_shape=[jax.ShapeDtypeStruct((t_pad, D_MODEL), F32), jax.ShapeDtypeStruct((t_pad, D_MODEL), BF16)]
        + route_shapes + [jax.ShapeDtypeStruct((batch, POOL_HALO, D_MODEL), F32)],
        scratch_shapes=[pltpu.VMEM((POOL_HALO, D_MODEL), F32),
                        pltpu.VMEM((tm + POOL_HALO, D_MODEL), F32)],
        compiler_params=_cparams(("arbitrary", "arbitrary")),
        name="pool_prompt",
    )(h, hn, *consts)


def _pool_sample_kernel(h_ref, hn_ref, hist_ref, wpool_ref, ps_ref, gf_ref, wr_ref, br_ref, *rest):
    outs = rest[len(rest) // 2:]
    hn = hn_ref[...]
    means = []
    for g, w in enumerate(POOL_WINDOWS):
        cols = slice(g * POOL_GROUP_DIM, (g + 1) * POOL_GROUP_DIM)
        s = hn[:, cols]
        for j in range(1, w):
            s = s + hist_ref[POOL_HIST - j, :, cols]
        means.append(s * (1.0 / min(w, PAST_LEN + 1)))
    _pool_project_route(h_ref[...], hn, means, wpool_ref, ps_ref, gf_ref, wr_ref, br_ref,
                        *outs, hn.shape[0])


def _pool_sample(h, hn, hist_t, wpool, ps, gf, wr, br, padded, row0, tile):
    t = h.shape[0]
    args = [h, hn, hist_t, wpool, ps, gf, wr, br]
    blk = row0 // t
    out_specs = [pl.BlockSpec((t, D_MODEL), lambda i: (blk, 0)), pl.BlockSpec((t, D_MODEL), lambda i: (blk, 0)),
                 pl.BlockSpec((t, N_EXPERTS), lambda i: (blk, 0)), pl.BlockSpec((t, N_EXPERTS), lambda i: (blk, 0)),
                 pl.BlockSpec((N_EXPERTS, t), lambda i: (0, blk)),
                 pl.BlockSpec((1, 1, N_EXPERTS), lambda i: (row0 // tile, 0, 0))]
    return pl.pallas_call(
        _pool_sample_kernel,
        grid=(1,),
        in_specs=[_const_spec(a.shape) for a in args] + [pl.BlockSpec(memory_space=pl.ANY)] * len(padded),
        out_specs=out_specs,
        out_shape=[jax.ShapeDtypeStruct(a.shape, a.dtype) for a in padded],
        input_output_aliases={len(args) + k: k for k in range(len(padded))},
        compiler_params=_cparams(("arbitrary",)),
        name="pool_sample",
    )(*args, *padded)


def _segment_rows(tile):
    return tile + ROW_ALIGN


def _segment_copies(meta_ref, step, tile, vmem_of, hbm, sem_of, to_hbm):
    half = tile // 2
    out = []
    for k, (start, size) in enumerate(((0, half), (half, _segment_rows(tile) - half))):
        for e in range(N_EXPERTS):
            pos = meta_ref[0, step * N_EXPERTS + e]
            pred = None if k == 0 else meta_ref[1, step * N_EXPERTS + e] > half
            v = vmem_of(e).at[pl.ds(start, size)]
            h = hbm.at[pl.ds(pl.multiple_of(pos + start, ROW_ALIGN), size)]
            copy = pltpu.make_async_copy(v, h, sem_of(e)) if to_hbm else pltpu.make_async_copy(h, v, sem_of(e))
            out.append((pred, copy))
    return out


def _segment_row(rank, shared):
    return jnp.where(rank >= 0, rank + shared, -1)


def _for_slot(slot, fn):
    for s in range(2):
        @pl.when(slot == s)
        def _(s=s):
            fn(s)


def _start_all(copies):
    for pred, copy in copies:
        if pred is None:
            copy.start()
        else:
            pl.when(pred)(copy.start)


def _wait_all(copies):
    for pred, copy in copies:
        if pred is None:
            copy.wait()
        else:
            pl.when(pred)(copy.wait)


def _dispatch_kernel(meta_ref, tail_ref, hn_ref, lrt_ref, xs_ref, seg, zeros, sem, zsem):
    i = pl.program_id(0)
    last = pl.num_programs(0) - 1
    tile = hn_ref.shape[0]
    slot = i % 2
    copies = lambda step, s: _segment_copies(meta_ref, step, tile, lambda e: seg.at[s, e], xs_ref,
                                             lambda e: sem.at[s, e], True)
    hn = hn_ref[...]
    half = tile // 2
    rest = _segment_rows(tile) - half
    shared = [meta_ref[2, i * N_EXPERTS + e] for e in range(N_EXPERTS)]
    row_of = lambda e: _segment_row(lrt_ref[e:e + 1, :], shared[e])
    place = lax.broadcasted_iota(jnp.int32, (half, tile), 0)
    for e0 in range(0, N_EXPERTS, 4):
        first = jnp.concatenate([(place == row_of(e)).astype(BF16) for e in range(e0, e0 + 4)], axis=0)
        rows = _mm(first, hn).astype(BF16)
        for k in range(4):
            seg[slot, e0 + k, 0:half] = rows[k * half:(k + 1) * half]
    place_rest = lax.broadcasted_iota(jnp.int32, (rest, tile), 0) + half
    for e in range(N_EXPERTS):
        @pl.when(meta_ref[1, i * N_EXPERTS + e] > half)
        def _(e=e):
            seg[slot, e, half:] = _mm((place_rest == row_of(e)).astype(BF16), hn_ref[...]).astype(BF16)

        @pl.when(shared[e] > 0)
        def _(e=e):
            prev_len = meta_ref[1, jnp.maximum(i - 1, 0) * N_EXPERTS + e]
            block = pl.ds(pl.multiple_of(prev_len - shared[e], ROW_ALIGN), ROW_ALIGN)
            seg[slot, e, 0:ROW_ALIGN] = seg[slot, e, 0:ROW_ALIGN] + seg[1 - slot, e, block]

    @pl.when(i > 0)
    def _():
        _for_slot(1 - slot, lambda s: _wait_all(copies(i - 1, s)))

    _for_slot(slot, lambda s: _start_all(copies(i, s)))

    @pl.when(i == last)
    def _():
        _for_slot(slot, lambda s: _wait_all(copies(i, s)))
        zeros[...] = jnp.zeros_like(zeros)
        tails = []
        for e in range(N_EXPERTS):
            start = tail_ref[0, e]
            cnt = tail_ref[1, e]
            for piece in TAIL_PIECES:
                off = pl.multiple_of(cnt & ~(2 * piece - 1), ROW_ALIGN)
                copy = pltpu.make_async_copy(
                    zeros.at[pl.ds(0, piece)],
                    xs_ref.at[pl.ds(pl.multiple_of(start + off, ROW_ALIGN), piece)],
                    zsem.at[e])
                tails.append(((cnt & piece) != 0, copy))
        _start_all(tails)
        _wait_all(tails)

        def slack_copy(c):
            row = pl.multiple_of(tail_ref[0, N_EXPERTS] + c * SLACK_CHUNK, SLACK_CHUNK)
            return pltpu.make_async_copy(zeros.at[pl.ds(0, SLACK_CHUNK)],
                                         xs_ref.at[pl.ds(row, SLACK_CHUNK)], zsem.at[0])

        n_slack = tail_ref[1, N_EXPERTS] // SLACK_CHUNK
        lax.fori_loop(0, n_slack, lambda c, _: slack_copy(c).start(), None)
        lax.fori_loop(0, n_slack, lambda c, _: slack_copy(c).wait(), None)


def _dispatch(hn, lr_t, meta, tail, rows, tile):
    t = hn.shape[0]
    return pl.pallas_call(
        _dispatch_kernel,
        grid_spec=pltpu.PrefetchScalarGridSpec(
            num_scalar_prefetch=2,
            grid=(t // tile,),
            in_specs=[pl.BlockSpec((tile, D_MODEL), lambda i, *_: (i, 0)),
                      pl.BlockSpec((N_EXPERTS, tile), lambda i, *_: (0, i))],
            out_specs=pl.BlockSpec(memory_space=pl.ANY),
            scratch_shapes=[pltpu.VMEM((2, N_EXPERTS, _segment_rows(tile), D_MODEL), BF16),
                            pltpu.VMEM((TAIL_PIECES[0], D_MODEL), BF16),
                            pltpu.SemaphoreType.DMA((2, N_EXPERTS)),
                            pltpu.SemaphoreType.DMA((N_EXPERTS,))]),
        out_shape=jax.ShapeDtypeStruct((rows, D_MODEL), BF16),
        compiler_params=_cparams(("arbitrary",)),
        name="expert_dispatch",
    )(meta, tail, hn, lr_t)


def _expert_kernel(te_ref, tr_ref, x_ref, wg_ref, wu_ref, wd_ref, y_ref, acc):
    r = pl.program_id(0)
    j = pl.program_id(1)

    @pl.when(j == 0)
    def _():
        acc[...] = jnp.zeros_like(acc)

    def ffn(m):
        x = x_ref[0:m]
        a = (_silu(_mm(x, wg_ref[0].astype(BF16))) * _mm(x, wu_ref[0].astype(BF16))).astype(BF16)
        acc[0:m] += _mm(a, wd_ref[0].astype(BF16))

    rows = tr_ref[r]
    step = x_ref.shape[0] // EXPERT_ROW_STEPS
    for k in range(1, EXPERT_ROW_STEPS + 1):
        @pl.when((rows > (k - 1) * step) & (rows <= k * step))
        def _(k=k):
            ffn(k * step)

    @pl.when(j == pl.num_programs(1) - 1)
    def _():
        y_ref[...] = acc[...].astype(BF16)


def _expert_ffn(xs, tile_e, tile_rows, wg, wu, wd, tmd):
    rows = xs.shape[0]
    nj = D_FF_EXPERT // EXPERT_FF_CHUNK
    live = lambda r, tr: tr[r] > 0
    jj = lambda r, j, tr: jnp.where(live(r, tr), j, nj - 1)
    return pl.pallas_call(
        _expert_kernel,
        grid_spec=pltpu.PrefetchScalarGridSpec(
            num_scalar_prefetch=2,
            grid=(rows // tmd, nj),
            in_specs=[
                pl.BlockSpec((tmd, D_MODEL), lambda r, j, te, tr: (jnp.where(live(r, tr), r, 0), 0)),
                pl.BlockSpec((1, D_MODEL, EXPERT_FF_CHUNK), lambda r, j, te, tr: (te[r], 0, jj(r, j, tr))),
                pl.BlockSpec((1, D_MODEL, EXPERT_FF_CHUNK), lambda r, j, te, tr: (te[r], 0, jj(r, j, tr))),
                pl.BlockSpec((1, EXPERT_FF_CHUNK, D_MODEL), lambda r, j, te, tr: (te[r], jj(r, j, tr), 0)),
            ],
            out_specs=pl.BlockSpec((tmd, D_MODEL), lambda r, j, te, tr: (r, 0)),
            scratch_shapes=[pltpu.VMEM((tmd, D_MODEL), F32)]),
        out_shape=jax.ShapeDtypeStruct((rows, D_MODEL), BF16),
        compiler_params=_cparams(("parallel", "arbitrary")),
        name="expert_ffn",
    )(tile_e, tile_rows, xs, wg, wu, wd)


def _combine_kernel(meta_ref, h_ref, gates_ref, lr_ref, pp_ref, ps_ref, gp_ref, wpg_ref, bpg_ref, wp_ref,
                    gfin_ref, ys_ref, outp_ref, outs_ref, ybuf, f_ref, sem, *, n_prompt_tiles):
    i = pl.program_id(0)
    n = pl.num_programs(0)
    tile = h_ref.shape[0]
    slot = i % 2
    copies = lambda step, s: _segment_copies(meta_ref, step, tile, lambda e: ybuf.at[s, e], ys_ref,
                                             lambda e: sem.at[s, e], False)

    @pl.when(i == 0)
    def _():
        ybuf[...] = jnp.zeros_like(ybuf)
        _start_all(copies(0, 0))

    @pl.when(i + 1 < n)
    def _():
        _for_slot(1 - slot, lambda s: _start_all(copies(i + 1, s)))

    _for_slot(slot, lambda s: _wait_all(copies(i, s)))
    gates = gates_ref[...]
    lr = lr_ref[...]
    half = tile // 2

    def gather(start, size):
        place = lax.broadcasted_iota(jnp.int32, (tile, size), 1) + start
        for e in range(N_EXPERTS):
            row = _segment_row(lr[:, e:e + 1], meta_ref[2, i * N_EXPERTS + e])
            onehot = (place == row).astype(BF16)
            f_ref[...] += gates[:, e:e + 1] * _mm(onehot, ybuf[slot, e, start:start + size, :])

    f_ref[...] = h_ref[...]
    gather(0, half)
    longest = meta_ref[1, i * N_EXPERTS]
    for e in range(1, N_EXPERTS):
        longest = jnp.maximum(longest, meta_ref[1, i * N_EXPERTS + e])

    @pl.when(longest > half)
    def _():
        gather(half, _segment_rows(tile) - half)

    h = f_ref[...]

    def head(h, p):
        h = _ple(h, p, gp_ref[...], wpg_ref[...], bpg_ref[...], wp_ref[...])
        return _rms(h, gfin_ref[...])

    @pl.when(i < n_prompt_tiles)
    def _():
        outp_ref[...] = head(h, pp_ref[0])

    @pl.when(i == n_prompt_tiles)
    def _():
        n_s = outs_ref.shape[0]
        outs_ref[...] = head(h[:n_s], ps_ref[0])


def _combine(h, gates, lr, pp, ps, gp, wpg, bpg, wp, gfin, ys, meta, tile, n_prompt):
    t = h.shape[0]
    n_pt = n_prompt // tile
    n_s = ps.shape[1]
    row = lambda i, *_: (i, 0)
    prow = lambda i, *_: (jnp.minimum(i, n_pt - 1), 0)
    consts = [gp, wpg, bpg, wp, gfin]
    return pl.pallas_call(
        functools.partial(_combine_kernel, n_prompt_tiles=n_pt),
        grid_spec=pltpu.PrefetchScalarGridSpec(
            num_scalar_prefetch=1,
            grid=(t // tile,),
            in_specs=[pl.BlockSpec((tile, D_MODEL), row), pl.BlockSpec((tile, N_EXPERTS), row),
                      pl.BlockSpec((tile, N_EXPERTS), row),
                      pl.BlockSpec((1, tile, PLE_DIM), lambda i, *_: (1,) + prow(i)),
                      pl.BlockSpec((1, n_s, PLE_DIM), lambda i, *_: (1, 0, 0))]
            + [_const_spec(c.shape) for c in consts] + [pl.BlockSpec(memory_space=pl.ANY)],
            out_specs=[pl.BlockSpec((tile, D_MODEL), prow),
                       pl.BlockSpec((n_s, D_MODEL), lambda i, *_: (0, 0))],
            scratch_shapes=[pltpu.VMEM((2, N_EXPERTS, _segment_rows(tile), D_MODEL), BF16),
                            pltpu.VMEM((tile, D_MODEL), F32),
                            pltpu.SemaphoreType.DMA((2, N_EXPERTS))]),
        out_shape=[jax.ShapeDtypeStruct((n_prompt, D_MODEL), F32),
                   jax.ShapeDtypeStruct((n_s, D_MODEL), F32)],
        compiler_params=_cparams(("arbitrary",)),
        name="combine",
    )(meta, h, gates, lr, pp, ps, *consts, ys)


def _round_up(x, m):
    return (x + m - 1) // m * m


def _route_plan(cnt, t, tile, tmd):
    n_tiles = t // tile
    cnt = cnt.reshape(n_tiles, N_EXPERTS)
    before = jnp.cumsum(cnt, axis=0) - cnt
    shared = before % ROW_ALIGN
    seg_len = shared + cnt
    total = _round_up(cnt.sum(axis=0), ROW_ALIGN)
    guard = _segment_rows(tile) - tile // 2
    region = _round_up(total + guard, tmd)
    region_end = jnp.cumsum(region)
    off = region_end - region
    pos = off[None, :] + before - shared
    rows = _round_up(2 * t + N_EXPERTS * (ROW_ALIGN + tmd + guard), tmd)
    tile_start = jnp.arange(rows // tmd, dtype=jnp.int32) * tmd
    te = jnp.sum(region_end[None, :] <= tile_start[:, None], axis=1).astype(jnp.int32)
    tec = jnp.minimum(te, N_EXPERTS - 1)
    tile_rows = jnp.where(te < N_EXPERTS, jnp.clip(total[tec] - (tile_start - off[tec]), 0, tmd), 0)
    last_e = jnp.max(jnp.where(tile_rows > 0, tec, 0))
    tile_e = jnp.where(tile_rows > 0, tec, last_e).astype(jnp.int32)
    meta = jnp.stack([pos.reshape(-1), seg_len.reshape(-1), shared.reshape(-1)]).astype(jnp.int32)
    tail = jnp.stack([jnp.append(off + total, region_end[-1]),
                      jnp.append(region - total, rows - region_end[-1])]).astype(jnp.int32)
    return dict(meta=meta, tail=tail, tile_e=tile_e, tile_rows=tile_rows.astype(jnp.int32), rows=rows)


def _moe_and_head(h, hn, gates, lr, lr_t, cnt, pp, ps, w, tile, tmd, n_prompt):
    plan = _route_plan(cnt, h.shape[0], tile, tmd)
    xs = _dispatch(hn, lr_t, plan["meta"], plan["tail"], plan["rows"], tile)
    ys = _expert_ffn(xs, plan["tile_e"], plan["tile_rows"], w["exp_gate"], w["exp_up"], w["exp_down"], tmd)
    return _combine(h, gates, lr, pp, ps, w["norm_ple1"], w["ple_gate1"], w["b_ple_gate1"], w["ple1"],
                    w["norm_final"], ys, plan["meta"], tile, n_prompt)


def _rope_tables(pos):
    half = HEAD_DIM // 2
    inv = 1.0 / (ROPE_THETA ** (jnp.arange(half, dtype=F32) / half))
    ang = pos.astype(F32)[:, None] * inv[None, :]
    cos = jnp.tile(jnp.cos(ang), (1, LANES // half))
    sin = jnp.sin(ang)
    sin = jnp.tile(jnp.concatenate([-sin, sin], axis=-1), (1, LANES // HEAD_DIM))
    return cos, sin


def kernel(x_prompt, x_sample, cache_k, cache_v, state_pool, p_prompt, p_sample, norm_mix, norm_ffn, norm_ple, norm_final, w_qkv, b_qkv, w_o, b_o, sinks, w_pool, pool_scale, w_ff_gate, w_ff_up, w_ff_down, w_router, b_router, w_exp_gate, w_exp_up, w_exp_down, w_ple, w_ple_gate, b_ple_gate):
    batch, seq, _ = x_prompt.shape
    dec = x_sample.shape[0]
    row2 = lambda a: a.reshape(1, -1)
    w = dict(
        exp_gate=w_exp_gate[0], exp_up=w_exp_up[0], exp_down=w_exp_down[0],
        norm_ple1=row2(norm_ple[1]), ple_gate1=w_ple_gate[1].astype(BF16), b_ple_gate1=row2(b_ple_gate[1]),
        ple1=w_ple[1].astype(BF16), norm_final=row2(norm_final))
    wqkv = w_qkv[0].astype(BF16)
    l0 = [w_o[0].astype(BF16), row2(b_o[0]), row2(norm_ffn[0]), w_ff_gate[0].astype(BF16),
          w_ff_up[0].astype(BF16), w_ff_down[0].astype(BF16), row2(norm_ple[0]),
          w_ple_gate[0].astype(BF16), row2(b_ple_gate[0]), w_ple[0].astype(BF16), row2(norm_mix[1])]
    wr_hi = w_router[0].astype(BF16)
    wr_lo = (w_router[0] - wr_hi.astype(F32)).astype(BF16)
    pool_w = [w_pool[0].astype(BF16), row2(pool_scale[0]), row2(norm_ffn[1]),
              jnp.concatenate([wr_hi, wr_lo], axis=1), row2(b_router[0])]

    tm, tile = 512, 512
    n_prompt = batch * seq
    t_pad = n_prompt + tm

    xp = x_prompt.reshape(n_prompt, D_MODEL)
    pp = p_prompt.reshape(p_prompt.shape[0], n_prompt, PLE_DIM)
    cos_p, sin_p = _rope_tables(jnp.arange(seq, dtype=jnp.int32))
    q, k, v = _qkv_rope(xp, row2(norm_mix[0]), wqkv, row2(b_qkv[0]), cos_p, sin_p, tm=tm)
    o = _prompt_attention(q, k, v, sinks[0], batch, seq)
    h1, hn1 = _layer0_tail(xp, o, pp, *l0, tm=tm)
    *padded, last = _pool_prompt(h1, hn1, *pool_w, batch=batch, seq=seq, tm=tm, tile=tile, t_pad=t_pad)
    kv_shape = (1, batch, WINDOW, N_KV_HEADS, HEAD_DIM)
    new_k_prompt = k.reshape(batch, seq, KV_DIM)[:, -WINDOW:].reshape(kv_shape)
    new_v_prompt = v.reshape(batch, seq, KV_DIM)[:, -WINDOW:].reshape(kv_shape)
    new_pool_prompt = last[:, -POOL_HIST:][None]

    xs_ = x_sample.reshape(dec, D_MODEL)
    ps = p_sample.reshape(p_sample.shape[0], dec, PLE_DIM)
    cos_s, sin_s = _rope_tables(jnp.full((dec,), PAST_LEN, jnp.int32))
    qs, ks, vs = _qkv_rope(xs_, row2(norm_mix[0]), wqkv, row2(b_qkv[0]), cos_s, sin_s, tm=dec)
    n_hist = cache_k.shape[2]
    k_win = jnp.concatenate([cache_k[0].reshape(dec, n_hist, KV_DIM), ks[:, None]], axis=1)[:, -n_hist:]
    v_win = jnp.concatenate([cache_v[0].reshape(dec, n_hist, KV_DIM), vs[:, None]], axis=1)[:, -n_hist:]
    head_kv = jnp.arange(N_HEADS) // GROUP
    blk = (head_kv[:, None] == jnp.arange(N_KV_HEADS)[None, :]).astype(BF16)
    q_blk = (qs.reshape(dec, N_HEADS, 1, HEAD_DIM) * blk[None, :, :, None]).reshape(dec, N_HEADS, KV_DIM)
    o_blk = _sample_attention(q_blk, k_win, v_win, sinks[0])
    o_s = jnp.take_along_axis(o_blk.reshape(dec, N_HEADS, N_KV_HEADS, HEAD_DIM),
                              head_kv[None, :, None, None], axis=2).reshape(dec, N_HEADS * HEAD_DIM)
    h1s, hn1s = _layer0_tail(xs_, o_s.astype(BF16), ps, *l0, tm=dec)
    hist_t = jnp.swapaxes(state_pool[0], 0, 1)
    padded = _pool_sample(h1s, hn1s, hist_t, *pool_w, padded=padded, row0=n_prompt, tile=tile)
    new_k_sample = k_win.reshape(1, dec, n_hist, N_KV_HEADS, HEAD_DIM)
    new_v_sample = v_win.reshape(1, dec, n_hist, N_KV_HEADS, HEAD_DIM)
    new_pool_sample = jnp.concatenate([state_pool[0], hn1s[:, None]], axis=1)[:, -POOL_HIST:][None]

    y_prompt, y_sample = _moe_and_head(*padded, pp, ps, w, tile=tile, tmd=2048, n_prompt=n_prompt)

    return (y_prompt.reshape(batch, seq, D_MODEL), y_sample.reshape(dec, 1, D_MODEL),
            new_k_prompt, new_v_prompt, new_pool_prompt, new_k_sample, new_v_sample, new_pool_sample)
```

```python
import functools

import jax
import jax.numpy as jnp
from jax import lax
from jax.experimental import pallas as pl
from jax.experimental.pallas import tpu as pltpu

F32 = jnp.float32
BF16 = jnp.bfloat16

D_MODEL = 1024
HEAD_DIM = 64
N_HEADS = 16
N_KV_HEADS = 4
GROUP = N_HEADS // N_KV_HEADS
KV_DIM = N_KV_HEADS * HEAD_DIM
QKV_DIM = (N_HEADS + 2 * N_KV_HEADS) * HEAD_DIM
WINDOW = 128
ROPE_THETA = 10000.0
PAST_LEN = 16384
POOL_WINDOWS = (2, 4, 8, 16)
POOL_GROUP_DIM = D_MODEL // len(POOL_WINDOWS)
POOL_HIST = max(POOL_WINDOWS) - 1
POOL_HALO = 16
D_FF = 2816
N_EXPERTS = 8
D_FF_EXPERT = 3584
PLE_DIM = 256
EPS = 1e-6

LANES = 128
VMEM_LIMIT = 56 * 1024 * 1024

ATTN_TQ = 512
ATTN_STAGE = 2
MXU_DIM = 256
FF_CHUNK_EDGES = (0, 6 * MXU_DIM, D_FF)
EXPERT_FF_CHUNK = 512
EXPERT_ROW_STEPS = 8
EXPERT_TILE_TARGET = 2048
EXPERT_TILE_SLACK = 0.05
TOP_K = 2
ROW_ALIGN = 16
TAIL_PIECES = (2048, 1024, 512, 256, 128, 64, 32, 16)


def _cparams(sem):
    return pltpu.CompilerParams(dimension_semantics=sem, vmem_limit_bytes=VMEM_LIMIT)


def _const_spec(shape):
    nd = len(shape)
    return pl.BlockSpec(shape, lambda *_: (0,) * nd, pipeline_mode=pl.Buffered(1))


def _rms(x, g):
    return x * lax.rsqrt(jnp.mean(x * x, axis=-1, keepdims=True) + EPS) * g


def _mm(a, b):
    return jnp.dot(a, b, preferred_element_type=F32)


def _silu(x):
    return x * jax.nn.sigmoid(x)


def _qkv_kernel(x_ref, g_ref, w_ref, b_ref, cos_ref, sin_ref, q_ref, k_ref, v_ref):
    hn = _rms(x_ref[...], g_ref[...])
    qkv = _mm(hn.astype(BF16), w_ref[...]) + b_ref[...]
    cos = cos_ref[...]
    sin = sin_ref[...]
    lane = lax.broadcasted_iota(jnp.int32, cos.shape, 1)
    first_half = (lane % HEAD_DIM) < (HEAD_DIM // 2)

    def rope(xb):
        partner = jnp.where(first_half, pltpu.roll(xb, LANES - HEAD_DIM // 2, 1),
                            pltpu.roll(xb, HEAD_DIM // 2, 1))
        return xb * cos + partner * sin

    scale = HEAD_DIM ** -0.5
    for c in range(N_HEADS * HEAD_DIM // LANES):
        sl = slice(c * LANES, (c + 1) * LANES)
        q_ref[:, sl] = (rope(qkv[:, sl]) * scale).astype(BF16)
    for c in range(KV_DIM // LANES):
        sl = slice(c * LANES, (c + 1) * LANES)
        k_ref[:, sl] = rope(qkv[:, N_HEADS * HEAD_DIM + c * LANES:N_HEADS * HEAD_DIM + (c + 1) * LANES])
    v_ref[...] = qkv[:, N_HEADS * HEAD_DIM + KV_DIM:]


def _qkv_rope(x, g, w, b, cos, sin, tm):
    t = x.shape[0]
    n_pos_tiles = cos.shape[0] // tm
    row = lambda i: (i, 0)
    return pl.pallas_call(
        _qkv_kernel,
        grid=(t // tm,),
        in_specs=[
            pl.BlockSpec((tm, D_MODEL), row),
            _const_spec((1, D_MODEL)),
            _const_spec((D_MODEL, QKV_DIM)),
            _const_spec((1, QKV_DIM)),
            pl.BlockSpec((tm, LANES), lambda i: (i % n_pos_tiles, 0)),
            pl.BlockSpec((tm, LANES), lambda i: (i % n_pos_tiles, 0)),
        ],
        out_specs=[
            pl.BlockSpec((tm, N_HEADS * HEAD_DIM), row),
            pl.BlockSpec((tm, KV_DIM), row),
            pl.BlockSpec((tm, KV_DIM), row),
        ],
        out_shape=[
            jax.ShapeDtypeStruct((t, N_HEADS * HEAD_DIM), BF16),
            jax.ShapeDtypeStruct((t, KV_DIM), F32),
            jax.ShapeDtypeStruct((t, KV_DIM), F32),
        ],
        compiler_params=_cparams(("parallel",)),
        name="qkv_rope",
    )(x, g, w, b, cos, sin)


V_AUG = 4 * HEAD_DIM


def _attn_kernel(sink_ref, q_ref, kc_ref, kp_ref, vc_ref, vp_ref, o_ref, kbuf, vbuf, bias):
    n = pl.program_id(1)
    kbuf[0:WINDOW] = kp_ref[...].astype(BF16)
    kbuf[WINDOW:] = kc_ref[...].astype(BF16)
    v_all = jnp.concatenate([vp_ref[...], vc_ref[...]], axis=0).astype(BF16)
    pad0 = jnp.zeros((v_all.shape[0], HEAD_DIM), BF16)
    pad1 = jnp.ones((v_all.shape[0], 2 * HEAD_DIM), BF16)
    for kv in range(N_KV_HEADS):
        vbuf[:, kv * V_AUG:(kv + 1) * V_AUG] = jnp.concatenate(
            [v_all[:, kv * HEAD_DIM:(kv + 1) * HEAD_DIM], pad0, pad1], axis=1)
    shape = (GROUP * WINDOW, 2 * WINDOW)
    qrow = lax.broadcasted_iota(jnp.int32, shape, 0) & (WINDOW - 1)
    col = lax.broadcasted_iota(jnp.int32, shape, 1)
    mask_cur = (col >= WINDOW) & (col - WINDOW <= qrow)
    mask_prev = (col < WINDOW) & (col > qrow)
    bias[0] = jnp.where(mask_cur | mask_prev, 0.0, -jnp.inf)
    bias[1] = jnp.where(mask_cur, 0.0, -jnp.inf)
    rows = lambda sb: slice(sb * WINDOW, (sb + 1) * WINDOW)
    window = lambda sb: slice(sb * WINDOW, (sb + 2) * WINDOW)
    for first in range(0, ATTN_TQ // WINDOW, ATTN_STAGE):
        blocks = range(first, first + ATTN_STAGE)
        scores, probs, sink_term, outs = {}, {}, {}, {}
        for sb in blocks:
            for kv in range(N_KV_HEADS):
                qg = jnp.concatenate([q_ref[rows(sb), h * HEAD_DIM:(h + 1) * HEAD_DIM]
                                      for h in range(kv * GROUP, (kv + 1) * GROUP)], axis=0)
                scores[sb, kv] = lax.dot_general(qg, kbuf[window(sb), kv * HEAD_DIM:(kv + 1) * HEAD_DIM],
                                                 (((1,), (1,)), ((), ())), preferred_element_type=F32)
        for sb in blocks:
            bias_sb = bias[jnp.where(n == 0, 1, 0)] if sb == 0 else bias[0]
            for kv in range(N_KV_HEADS):
                s = scores[sb, kv] + bias_sb
                p = []
                for g in range(GROUP):
                    h = kv * GROUP + g
                    sh = s[g * WINDOW:(g + 1) * WINDOW]
                    m = jnp.maximum(jnp.max(sh, axis=-1, keepdims=True), sink_ref[h])
                    p.append(jnp.exp(sh - m).astype(BF16))
                    sink_term[sb, h] = jnp.exp(sink_ref[h] - m)
                probs[sb, kv] = jnp.concatenate(p, axis=0)
        for sb in blocks:
            for kv in range(N_KV_HEADS):
                outs[sb, kv] = _mm(probs[sb, kv], vbuf[window(sb), kv * V_AUG:(kv + 1) * V_AUG])
        for sb in blocks:
            for h in range(N_HEADS):
                oh = outs[sb, h // GROUP][(h % GROUP) * WINDOW:(h % GROUP + 1) * WINDOW]
                o = oh[:, :2 * HEAD_DIM] / (oh[:, 2 * HEAD_DIM:] + sink_term[sb, h])
                o_ref[rows(sb), h * HEAD_DIM:(h + 1) * HEAD_DIM] = o[:, :HEAD_DIM].astype(BF16)


def _prompt_attention(q, k, v, sinks, batch, seq):
    nq = seq // ATTN_TQ
    per = ATTN_TQ // WINDOW
    cur = lambda b, n: (b * nq + n, 0)
    prev = lambda b, n: (jnp.maximum(b * nq * per + n * per - 1, b * nq * per), 0)
    return pl.pallas_call(
        _attn_kernel,
        grid=(batch, nq),
        in_specs=[
            pl.BlockSpec(memory_space=pltpu.SMEM),
            pl.BlockSpec((ATTN_TQ, N_HEADS * HEAD_DIM), cur),
            pl.BlockSpec((ATTN_TQ, KV_DIM), cur),
            pl.BlockSpec((WINDOW, KV_DIM), prev),
            pl.BlockSpec((ATTN_TQ, KV_DIM), cur),
            pl.BlockSpec((WINDOW, KV_DIM), prev),
        ],
        out_specs=pl.BlockSpec((ATTN_TQ, N_HEADS * HEAD_DIM), cur),
        out_shape=jax.ShapeDtypeStruct(q.shape, BF16),
        scratch_shapes=[pltpu.VMEM((ATTN_TQ + WINDOW, KV_DIM), BF16),
                        pltpu.VMEM((ATTN_TQ + WINDOW, N_KV_HEADS * V_AUG), BF16),
                        pltpu.VMEM((2, GROUP * WINDOW, 2 * WINDOW), F32)],
        compiler_params=_cparams(("parallel", "parallel")),
        name="prompt_attention",
    )(sinks, q, k, k, v, v)


def _attn_sample_kernel(sink_ref, q_ref, k_ref, v_ref, o_ref):
    s = jnp.einsum("bhc,blc->bhl", q_ref[...], k_ref[...].astype(BF16),
                   preferred_element_type=F32)
    sink = sink_ref[...]
    m = jnp.maximum(jnp.max(s, axis=-1, keepdims=True), sink)
    p = jnp.exp(s - m)
    denom = jnp.sum(p, axis=-1, keepdims=True) + jnp.exp(sink - m)
    o_ref[...] = jnp.einsum("bhl,blc->bhc", (p / denom).astype(BF16), v_ref[...].astype(BF16),
                            preferred_element_type=F32)


def _sample_attention(q_blk, k_win, v_win, sinks, bt=16):
    b = q_blk.shape[0]
    blk = lambda i: (i, 0, 0)
    return pl.pallas_call(
        _attn_sample_kernel,
        grid=(b // bt,),
        in_specs=[
            _const_spec((1, N_HEADS, 1)),
            pl.BlockSpec((bt, N_HEADS, KV_DIM), blk),
            pl.BlockSpec((bt, WINDOW, KV_DIM), blk),
            pl.BlockSpec((bt, WINDOW, KV_DIM), blk),
        ],
        out_specs=pl.BlockSpec((bt, N_HEADS, KV_DIM), blk),
        out_shape=jax.ShapeDtypeStruct((b, N_HEADS, KV_DIM), F32),
        compiler_params=_cparams(("parallel",)),
        name="sample_attention",
    )(sinks.reshape(1, N_HEADS, 1), q_blk, k_win, v_win)


def _ple(h, p, g, wg, bg, wp):
    gate = jax.nn.sigmoid(_mm(_rms(h, g).astype(BF16), wg) + bg)
    return h + gate * _mm(p.astype(BF16), wp)


def _layer0_tail_math(x_ref, o_ref, p, wo_ref, bo_ref, gf_ref, wg_ref, wu_ref, wd_ref,
                      gp_ref, wpg_ref, bpg_ref, wp_ref, gm_ref):
    h = x_ref[...] + _mm(o_ref[...], wo_ref[...]) + bo_ref[...]
    hn = _rms(h, gf_ref[...]).astype(BF16)
    f = None
    for c0, c1 in zip(FF_CHUNK_EDGES[:-1], FF_CHUNK_EDGES[1:]):
        a = (_silu(_mm(hn, wg_ref[:, c0:c1])) * _mm(hn, wu_ref[:, c0:c1])).astype(BF16)
        part = _mm(a, wd_ref[c0:c1, :])
        f = part if f is None else f + part
    h = h + f
    h = _ple(h, p, gp_ref[...], wpg_ref[...], bpg_ref[...], wp_ref[...])
    return h, _rms(h, gm_ref[...])


def _layer0_tail_kernel(x_ref, o_ref, p_ref, wo_ref, bo_ref, gf_ref, wg_ref, wu_ref, wd_ref,
                        gp_ref, wpg_ref, bpg_ref, wp_ref, gm_ref, h_ref, hn_ref):
    h_ref[...], hn_ref[...] = _layer0_tail_math(x_ref, o_ref, p_ref[0], wo_ref, bo_ref, gf_ref, wg_ref,
                                                wu_ref, wd_ref, gp_ref, wpg_ref, bpg_ref, wp_ref, gm_ref)


def _layer0_tail(x, o, p, wo, bo, gf, wg, wu, wd, gp, wpg, bpg, wp, gm, tm):
    t = x.shape[0]
    row = lambda i: (i, 0)
    consts = [wo, bo, gf, wg, wu, wd, gp, wpg, bpg, wp, gm]
    return pl.pallas_call(
        _layer0_tail_kernel,
        grid=(t // tm,),
        in_specs=[pl.BlockSpec((tm, D_MODEL), row), pl.BlockSpec((tm, D_MODEL), row),
                  pl.BlockSpec((1, tm, PLE_DIM), lambda i: (0, i, 0))] + [_const_spec(c.shape) for c in consts],
        out_specs=[pl.BlockSpec((tm, D_MODEL), row), pl.BlockSpec((tm, D_MODEL), row)],
        out_shape=[jax.ShapeDtypeStruct((t, D_MODEL), F32), jax.ShapeDtypeStruct((t, D_MODEL), F32)],
        compiler_params=_cparams(("parallel",)),
        name="layer0_tail",
    )(x, o, p, *consts)


def _segment_ranks(sel, tile, lr_out, lrt_out, cnt_out):
    r = lax.broadcasted_iota(jnp.int32, (tile, tile), 0)
    c = lax.broadcasted_iota(jnp.int32, (tile, tile), 1)
    earlier = (c < r).astype(BF16)
    pad = jnp.full((tile, LANES - N_EXPERTS), -1.0, F32)
    for k in range(sel.shape[0] // tile):
        rows = slice(k * tile, (k + 1) * tile)
        sel_k = sel[rows]
        rank = _mm(earlier, sel_k.astype(BF16))
        lr = jnp.where(sel_k, rank, -1.0)
        lr_out[rows, :] = lr.astype(jnp.int32)
        lrt_out[:, rows] = jnp.concatenate([lr, pad], axis=1).T[:N_EXPERTS].astype(jnp.int32)
        cnt_out[k] = jnp.sum(sel_k.astype(F32), axis=0, keepdims=True).astype(jnp.int32)


def _pool_project_route(h, hn, pooled_sum_inv, wpool_ref, ps_ref, gf_ref, wr_ref, br_ref,
                        h_out, hn_out, gates_out, lr_out, lrt_out, cnt_out, tile):
    mixed = []
    for g in range(len(POOL_WINDOWS)):
        cols = slice(g * POOL_GROUP_DIM, (g + 1) * POOL_GROUP_DIM)
        pooled = pooled_sum_inv[g] - hn[:, cols]
        mixed.append(_mm(pooled.astype(BF16), wpool_ref[g]))
    h = h + jnp.concatenate(mixed, axis=-1) * ps_ref[...]
    h_out[...] = h
    hn2 = _rms(h, gf_ref[...])
    hi = hn2.astype(BF16)
    hn_out[...] = hi
    lo = (hn2 - hi.astype(F32)).astype(BF16)
    by_hi = _mm(hi, wr_ref[...])
    logits = (by_hi[:, :N_EXPERTS] + by_hi[:, N_EXPERTS:] + _mm(lo, wr_ref[:, :N_EXPERTS])) + br_ref[...]
    idx = lax.broadcasted_iota(jnp.int32, logits.shape, 1)
    m1 = jnp.max(logits, axis=-1, keepdims=True)
    i1 = jnp.min(jnp.where(logits == m1, idx, N_EXPERTS), axis=-1, keepdims=True)
    rest = jnp.where(idx == i1, -jnp.inf, logits)
    m2 = jnp.max(rest, axis=-1, keepdims=True)
    i2 = jnp.min(jnp.where(rest == m2, idx, N_EXPERTS), axis=-1, keepdims=True)
    e = jnp.exp(m2 - m1)
    w1 = 1.0 / (1.0 + e)
    w2 = e / (1.0 + e)
    gates_out[...] = jnp.where(idx == i1, w1, jnp.where(idx == i2, w2, 0.0))
    _segment_ranks((idx == i1) | (idx == i2), tile, lr_out, lrt_out, cnt_out)


def _route_out_specs(t, tm, tile, blk):
    specs = [pl.BlockSpec((tm, N_EXPERTS), lambda *g: (blk(*g), 0)),
             pl.BlockSpec((tm, N_EXPERTS), lambda *g: (blk(*g), 0)),
             pl.BlockSpec((N_EXPERTS, tm), lambda *g: (0, blk(*g))),
             pl.BlockSpec((tm // tile, 1, N_EXPERTS), lambda *g: (blk(*g), 0, 0))]
    shapes = [jax.ShapeDtypeStruct((t, N_EXPERTS), F32),
              jax.ShapeDtypeStruct((t, N_EXPERTS), jnp.int32),
              jax.ShapeDtypeStruct((N_EXPERTS, t), jnp.int32),
              jax.ShapeDtypeStruct((t // tile, 1, N_EXPERTS), jnp.int32)]
    return specs, shapes


def _pool_prompt_kernel(h_ref, hn_ref, wpool_ref, ps_ref, gf_ref, wr_ref, br_ref,
                        h_out, hn_out, gates_out, lr_out, lrt_out, cnt_out, last_out, carry, buf, *, tile):
    b = pl.program_id(0)
    n = pl.program_id(1)
    tm = h_ref.shape[0]
    n_seq = pl.num_programs(0) - 1

    @pl.when(b < n_seq)
    def _():
        @pl.when(n == 0)
        def _():
            carry[...] = jnp.zeros_like(carry)

        hn = hn_ref[...]
        buf[0:POOL_HALO] = carry[...]
        buf[POOL_HALO:] = hn
        carry[...] = hn[tm - POOL_HALO:]
        last_out[0] = hn[tm - POOL_HALO:]
        pos = n * tm + lax.broadcasted_iota(jnp.int32, (tm, 1), 0)
        means = []
        for g, w in enumerate(POOL_WINDOWS):
            s = buf[:, g * POOL_GROUP_DIM:(g + 1) * POOL_GROUP_DIM]
            shift = 1
            while shift < w:
                s = s + pltpu.roll(s, shift, 0)
                shift *= 2
            cnt = jnp.minimum(w, pos + 1).astype(F32)
            means.append(s[POOL_HALO:] * (1.0 / cnt))
        _pool_project_route(h_ref[...], hn, means, wpool_ref, ps_ref, gf_ref, wr_ref, br_ref,
                            h_out, hn_out, gates_out, lr_out, lrt_out, cnt_out, tile)

    @pl.when(b == n_seq)
    def _():
        h_out[...] = jnp.zeros_like(h_out)
        hn_out[...] = jnp.zeros_like(hn_out)
        gates_out[...] = jnp.zeros_like(gates_out)
        lr_out[...] = jnp.full(lr_out.shape, -1, jnp.int32)
        lrt_out[...] = jnp.full(lrt_out.shape, -1, jnp.int32)
        cnt_out[...] = jnp.zeros_like(cnt_out)


def _pool_prompt(h, hn, wpool, ps, gf, wr, br, batch, seq, tm, tile, t_pad):
    ns = seq // tm
    pad_blk = batch * ns
    blk = lambda b, n: jnp.where(b < batch, b * ns + n, pad_blk)
    src = lambda b, n: (jnp.minimum(b * ns + n, pad_blk - 1), 0)
    consts = [wpool, ps, gf, wr, br]
    route_specs, route_shapes = _route_out_specs(t_pad, tm, tile, blk)
    row = lambda b, n: (blk(b, n), 0)
    return pl.pallas_call(
        functools.partial(_pool_prompt_kernel, tile=tile),
        grid=(batch + 1, ns),
        in_specs=[pl.BlockSpec((tm, D_MODEL), src), pl.BlockSpec((tm, D_MODEL), src)]
        + [_const_spec(c.shape) for c in consts],
        out_specs=[pl.BlockSpec((tm, D_MODEL), row), pl.BlockSpec((tm, D_MODEL), row)] + route_specs
        + [pl.BlockSpec((1, POOL_HALO, D_MODEL), lambda b, n: (jnp.minimum(b, batch - 1), 0, 0))],
        out_shape=[jax.ShapeDtypeStruct((t_pad, D_MODEL), F32), jax.ShapeDtypeStruct((t_pad, D_MODEL), BF16)]
        + route_shapes + [jax.ShapeDtypeStruct((batch, POOL_HALO, D_MODEL), F32)],
        scratch_shapes=[pltpu.VMEM((POOL_HALO, D_MODEL), F32),
                        pltpu.VMEM((tm + POOL_HALO, D_MODEL), F32)],
        compiler_params=_cparams(("arbitrary", "arbitrary")),
        name="pool_prompt",
    )(h, hn, *consts)


def _pool_sample_kernel(h_ref, hn_ref, hist_ref, wpool_ref, ps_ref, gf_ref, wr_ref, br_ref, *rest):
    outs = rest[len(rest) // 2:]
    hn = hn_ref[...]
    means = []
    for g, w in enumerate(POOL_WINDOWS):
        cols = slice(g * POOL_GROUP_DIM, (g + 1) * POOL_GROUP_DIM)
        s = hn[:, cols]
        for j in range(1, w):
            s = s + hist_ref[POOL_HIST - j, :, cols]
        means.append(s * (1.0 / min(w, PAST_LEN + 1)))
    _pool_project_route(h_ref[...], hn, means, wpool_ref, ps_ref, gf_ref, wr_ref, br_ref,
                        *outs, hn.shape[0])


def _pool_sample(h, hn, hist_t, wpool, ps, gf, wr, br, padded, row0, tile):
    t = h.shape[0]
    args = [h, hn, hist_t, wpool, ps, gf, wr, br]
    blk = row0 // t
    out_specs = [pl.BlockSpec((t, D_MODEL), lambda i: (blk, 0)), pl.BlockSpec((t, D_MODEL), lambda i: (blk, 0)),
                 pl.BlockSpec((t, N_EXPERTS), lambda i: (blk, 0)), pl.BlockSpec((t, N_EXPERTS), lambda i: (blk, 0)),
                 pl.BlockSpec((N_EXPERTS, t), lambda i: (0, blk)),
                 pl.BlockSpec((1, 1, N_EXPERTS), lambda i: (row0 // tile, 0, 0))]
    return pl.pallas_call(
        _pool_sample_kernel,
        grid=(1,),
        in_specs=[_const_spec(a.shape) for a in args] + [pl.BlockSpec(memory_space=pl.ANY)] * len(padded),
        out_specs=out_specs,
        out_shape=[jax.ShapeDtypeStruct(a.shape, a.dtype) for a in padded],
        input_output_aliases={len(args) + k: k for k in range(len(padded))},
        compiler_params=_cparams(("arbitrary",)),
        name="pool_sample",
    )(*args, *padded)


def _segment_rows(tile):
    return tile + ROW_ALIGN


def _segment_copies(meta_ref, step, tile, vmem_of, hbm, sem_of, to_hbm):
    half = tile // 2
    out = []
    for k, (start, size) in enumerate(((0, half), (half, _segment_rows(tile) - half))):
        for e in range(N_EXPERTS):
            pos = meta_ref[0, step * N_EXPERTS + e]
            pred = None if k == 0 else meta_ref[1, step * N_EXPERTS + e] > half
            v = vmem_of(e).at[pl.ds(start, size)]
            h = hbm.at[pl.ds(pl.multiple_of(pos + start, ROW_ALIGN), size)]
            copy = pltpu.make_async_copy(v, h, sem_of(e)) if to_hbm else pltpu.make_async_copy(h, v, sem_of(e))
            out.append((pred, copy))
    return out


def _segment_row(rank, shared):
    return jnp.where(rank >= 0, rank + shared, -1)


def _for_slot(slot, fn):
    for s in range(2):
        @pl.when(slot == s)
        def _(s=s):
            fn(s)


def _start_all(copies):
    for pred, copy in copies:
        if pred is None:
            copy.start()
        else:
            pl.when(pred)(copy.start)


def _wait_all(copies):
    for pred, copy in copies:
        if pred is None:
            copy.wait()
        else:
            pl.when(pred)(copy.wait)


def _dispatch_kernel(meta_ref, tail_ref, hn_ref, lrt_ref, xs_ref, seg, zeros, sem, zsem, *, slack_chunk):
    i = pl.program_id(0)
    last = pl.num_programs(0) - 1
    tile = hn_ref.shape[0]
    slot = i % 2
    copies = lambda step, s: _segment_copies(meta_ref, step, tile, lambda e: seg.at[s, e], xs_ref,
                                             lambda e: sem.at[s, e], True)
    hn = hn_ref[...]
    half = tile // 2
    rest = _segment_rows(tile) - half
    shared = [meta_ref[2, i * N_EXPERTS + e] for e in range(N_EXPERTS)]
    row_of = lambda e: _segment_row(lrt_ref[e:e + 1, :], shared[e])
    place = lax.broadcasted_iota(jnp.int32, (half, tile), 0)
    for e0 in range(0, N_EXPERTS, 4):
        first = jnp.concatenate([(place == row_of(e)).astype(BF16) for e in range(e0, e0 + 4)], axis=0)
        rows = _mm(first, hn).astype(BF16)
        for k in range(4):
            seg[slot, e0 + k, 0:half] = rows[k * half:(k + 1) * half]
    place_rest = lax.broadcasted_iota(jnp.int32, (rest, tile), 0) + half
    for e in range(N_EXPERTS):
        @pl.when(meta_ref[1, i * N_EXPERTS + e] > half)
        def _(e=e):
            seg[slot, e, half:] = _mm((place_rest == row_of(e)).astype(BF16), hn_ref[...]).astype(BF16)

        @pl.when(shared[e] > 0)
        def _(e=e):
            prev_len = meta_ref[1, jnp.maximum(i - 1, 0) * N_EXPERTS + e]
            block = pl.ds(pl.multiple_of(prev_len - shared[e], ROW_ALIGN), ROW_ALIGN)
            seg[slot, e, 0:ROW_ALIGN] = seg[slot, e, 0:ROW_ALIGN] + seg[1 - slot, e, block]

    @pl.when(i > 0)
    def _():
        _for_slot(1 - slot, lambda s: _wait_all(copies(i - 1, s)))

    _for_slot(slot, lambda s: _start_all(copies(i, s)))

    @pl.when(i == last)
    def _():
        _for_slot(slot, lambda s: _wait_all(copies(i, s)))
        zeros[...] = jnp.zeros_like(zeros)
        tails = []
        for e in range(N_EXPERTS):
            start = tail_ref[0, e]
            cnt = tail_ref[1, e]
            for piece in TAIL_PIECES:
                off = pl.multiple_of(cnt & ~(2 * piece - 1), ROW_ALIGN)
                copy = pltpu.make_async_copy(
                    zeros.at[pl.ds(0, piece)],
                    xs_ref.at[pl.ds(pl.multiple_of(start + off, ROW_ALIGN), piece)],
                    zsem.at[e])
                tails.append(((cnt & piece) != 0, copy))
        _start_all(tails)
        _wait_all(tails)

        def slack_copy(c):
            row = pl.multiple_of(tail_ref[0, N_EXPERTS] + c * slack_chunk, ROW_ALIGN)
            return pltpu.make_async_copy(zeros.at[pl.ds(0, slack_chunk)],
                                         xs_ref.at[pl.ds(row, slack_chunk)], zsem.at[0])

        n_slack = tail_ref[1, N_EXPERTS] // slack_chunk
        lax.fori_loop(0, n_slack, lambda c, _: slack_copy(c).start(), None)
        lax.fori_loop(0, n_slack, lambda c, _: slack_copy(c).wait(), None)


def _dispatch(hn, lr_t, meta, tail, rows, tile, slack_chunk):
    t = hn.shape[0]
    return pl.pallas_call(
        functools.partial(_dispatch_kernel, slack_chunk=slack_chunk),
        grid_spec=pltpu.PrefetchScalarGridSpec(
            num_scalar_prefetch=2,
            grid=(t // tile,),
            in_specs=[pl.BlockSpec((tile, D_MODEL), lambda i, *_: (i, 0)),
                      pl.BlockSpec((N_EXPERTS, tile), lambda i, *_: (0, i))],
            out_specs=pl.BlockSpec(memory_space=pl.ANY),
            scratch_shapes=[pltpu.VMEM((2, N_EXPERTS, _segment_rows(tile), D_MODEL), BF16),
                            pltpu.VMEM((TAIL_PIECES[0], D_MODEL), BF16),
                            pltpu.SemaphoreType.DMA((2, N_EXPERTS)),
                            pltpu.SemaphoreType.DMA((N_EXPERTS,))]),
        out_shape=jax.ShapeDtypeStruct((rows, D_MODEL), BF16),
        compiler_params=_cparams(("arbitrary",)),
        name="expert_dispatch",
    )(meta, tail, hn, lr_t)


def _expert_kernel(te_ref, tr_ref, x_ref, wg_ref, wu_ref, wd_ref, y_ref, acc):
    r = pl.program_id(0)
    j = pl.program_id(1)

    @pl.when(j == 0)
    def _():
        acc[...] = jnp.zeros_like(acc)

    def ffn(m):
        x = x_ref[0:m]
        a = (_silu(_mm(x, wg_ref[0].astype(BF16))) * _mm(x, wu_ref[0].astype(BF16))).astype(BF16)
        acc[0:m] += _mm(a, wd_ref[0].astype(BF16))

    rows = tr_ref[r]
    step = x_ref.shape[0] // EXPERT_ROW_STEPS
    for k in range(1, EXPERT_ROW_STEPS + 1):
        @pl.when((rows > (k - 1) * step) & (rows <= k * step))
        def _(k=k):
            ffn(k * step)

    @pl.when(j == pl.num_programs(1) - 1)
    def _():
        y_ref[...] = acc[...].astype(BF16)


def _expert_ffn(xs, tile_e, tile_rows, wg, wu, wd, tmd):
    rows = xs.shape[0]
    nj = D_FF_EXPERT // EXPERT_FF_CHUNK
    live = lambda r, tr: tr[r] > 0
    jj = lambda r, j, tr: jnp.where(live(r, tr), j, nj - 1)
    return pl.pallas_call(
        _expert_kernel,
        grid_spec=pltpu.PrefetchScalarGridSpec(
            num_scalar_prefetch=2,
            grid=(rows // tmd, nj),
            in_specs=[
                pl.BlockSpec((tmd, D_MODEL), lambda r, j, te, tr: (jnp.where(live(r, tr), r, 0), 0)),
                pl.BlockSpec((1, D_MODEL, EXPERT_FF_CHUNK), lambda r, j, te, tr: (te[r], 0, jj(r, j, tr))),
                pl.BlockSpec((1, D_MODEL, EXPERT_FF_CHUNK), lambda r, j, te, tr: (te[r], 0, jj(r, j, tr))),
                pl.BlockSpec((1, EXPERT_FF_CHUNK, D_MODEL), lambda r, j, te, tr: (te[r], jj(r, j, tr), 0)),
            ],
            out_specs=pl.BlockSpec((tmd, D_MODEL), lambda r, j, te, tr: (r, 0)),
            scratch_shapes=[pltpu.VMEM((tmd, D_MODEL), F32)]),
        out_shape=jax.ShapeDtypeStruct((rows, D_MODEL), BF16),
        compiler_params=_cparams(("parallel", "arbitrary")),
        name="expert_ffn",
    )(tile_e, tile_rows, xs, wg, wu, wd)


def _combine_kernel(meta_ref, h_ref, gates_ref, lr_ref, pp_ref, ps_ref, gp_ref, wpg_ref, bpg_ref, wp_ref,
                    gfin_ref, ys_ref, outp_ref, outs_ref, ybuf, f_ref, sem, *, n_prompt_tiles):
    i = pl.program_id(0)
    n = pl.num_programs(0)
    tile = h_ref.shape[0]
    slot = i % 2
    copies = lambda step, s: _segment_copies(meta_ref, step, tile, lambda e: ybuf.at[s, e], ys_ref,
                                             lambda e: sem.at[s, e], False)

    @pl.when(i == 0)
    def _():
        ybuf[...] = jnp.zeros_like(ybuf)
        _start_all(copies(0, 0))

    @pl.when(i + 1 < n)
    def _():
        _for_slot(1 - slot, lambda s: _start_all(copies(i + 1, s)))

    _for_slot(slot, lambda s: _wait_all(copies(i, s)))
    gates = gates_ref[...]
    lr = lr_ref[...]
    half = tile // 2

    def gather(start, size):
        place = lax.broadcasted_iota(jnp.int32, (tile, size), 1) + start
        for e in range(N_EXPERTS):
            row = _segment_row(lr[:, e:e + 1], meta_ref[2, i * N_EXPERTS + e])
            onehot = (place == row).astype(BF16)
            f_ref[...] += gates[:, e:e + 1] * _mm(onehot, ybuf[slot, e, start:start + size, :])

    f_ref[...] = h_ref[...]
    gather(0, half)
    longest = meta_ref[1, i * N_EXPERTS]
    for e in range(1, N_EXPERTS):
        longest = jnp.maximum(longest, meta_ref[1, i * N_EXPERTS + e])

    @pl.when(longest > half)
    def _():
        gather(half, _segment_rows(tile) - half)

    h = f_ref[...]

    def head(h, p):
        h = _ple(h, p, gp_ref[...], wpg_ref[...], bpg_ref[...], wp_ref[...])
        return _rms(h, gfin_ref[...])

    @pl.when(i < n_prompt_tiles)
    def _():
        outp_ref[...] = head(h, pp_ref[0])

    @pl.when(i == n_prompt_tiles)
    def _():
        n_s = outs_ref.shape[0]
        outs_ref[...] = head(h[:n_s], ps_ref[0])


def _combine(h, gates, lr, pp, ps, gp, wpg, bpg, wp, gfin, ys, meta, tile, n_prompt):
    t = h.shape[0]
    n_pt = n_prompt // tile
    n_s = ps.shape[1]
    row = lambda i, *_: (i, 0)
    prow = lambda i, *_: (jnp.minimum(i, n_pt - 1), 0)
    consts = [gp, wpg, bpg, wp, gfin]
    return pl.pallas_call(
        functools.partial(_combine_kernel, n_prompt_tiles=n_pt),
        grid_spec=pltpu.PrefetchScalarGridSpec(
            num_scalar_prefetch=1,
            grid=(t // tile,),
            in_specs=[pl.BlockSpec((tile, D_MODEL), row), pl.BlockSpec((tile, N_EXPERTS), row),
                      pl.BlockSpec((tile, N_EXPERTS), row),
                      pl.BlockSpec((1, tile, PLE_DIM), lambda i, *_: (1,) + prow(i)),
                      pl.BlockSpec((1, n_s, PLE_DIM), lambda i, *_: (1, 0, 0))]
            + [_const_spec(c.shape) for c in consts] + [pl.BlockSpec(memory_space=pl.ANY)],
            out_specs=[pl.BlockSpec((tile, D_MODEL), prow),
                       pl.BlockSpec((n_s, D_MODEL), lambda i, *_: (0, 0))],
            scratch_shapes=[pltpu.VMEM((2, N_EXPERTS, _segment_rows(tile), D_MODEL), BF16),
                            pltpu.VMEM((tile, D_MODEL), F32),
                            pltpu.SemaphoreType.DMA((2, N_EXPERTS))]),
        out_shape=[jax.ShapeDtypeStruct((n_prompt, D_MODEL), F32),
                   jax.ShapeDtypeStruct((n_s, D_MODEL), F32)],
        compiler_params=_cparams(("arbitrary",)),
        name="combine",
    )(meta, h, gates, lr, pp, ps, *consts, ys)


def _round_up(x, m):
    return (x + m - 1) // m * m


def _route_plan(cnt, t, tile, tmd):
    n_tiles = t // tile
    cnt = cnt.reshape(n_tiles, N_EXPERTS)
    before = jnp.cumsum(cnt, axis=0) - cnt
    shared = before % ROW_ALIGN
    seg_len = shared + cnt
    total = _round_up(cnt.sum(axis=0), ROW_ALIGN)
    guard = _segment_rows(tile) - tile // 2
    region = _round_up(total + guard, tmd)
    region_end = jnp.cumsum(region)
    off = region_end - region
    pos = off[None, :] + before - shared
    rows = _round_up(2 * t + N_EXPERTS * (ROW_ALIGN + tmd + guard), tmd)
    tile_start = jnp.arange(rows // tmd, dtype=jnp.int32) * tmd
    te = jnp.sum(region_end[None, :] <= tile_start[:, None], axis=1).astype(jnp.int32)
    tec = jnp.minimum(te, N_EXPERTS - 1)
    tile_rows = jnp.where(te < N_EXPERTS, jnp.clip(total[tec] - (tile_start - off[tec]), 0, tmd), 0)
    last_e = jnp.max(jnp.where(tile_rows > 0, tec, 0))
    tile_e = jnp.where(tile_rows > 0, tec, last_e).astype(jnp.int32)
    meta = jnp.stack([pos.reshape(-1), seg_len.reshape(-1), shared.reshape(-1)]).astype(jnp.int32)
    tail = jnp.stack([jnp.append(off + total, region_end[-1]),
                      jnp.append(region - total, rows - region_end[-1])]).astype(jnp.int32)
    return dict(meta=meta, tail=tail, tile_e=tile_e, tile_rows=tile_rows.astype(jnp.int32), rows=rows)


def _moe_and_head(h, hn, gates, lr, lr_t, cnt, pp, ps, w, tile, tmd, n_prompt):
    plan = _route_plan(cnt, h.shape[0], tile, tmd)
    xs = _dispatch(hn, lr_t, plan["meta"], plan["tail"], plan["rows"], tile, tmd // EXPERT_ROW_STEPS)
    ys = _expert_ffn(xs, plan["tile_e"], plan["tile_rows"], w["exp_gate"], w["exp_up"], w["exp_down"], tmd)
    return _combine(h, gates, lr, pp, ps, w["norm_ple1"], w["ple_gate1"], w["b_ple_gate1"], w["ple1"],
                    w["norm_final"], ys, plan["meta"], tile, n_prompt)


def _rope_tables(pos):
    half = HEAD_DIM // 2
    inv = 1.0 / (ROPE_THETA ** (jnp.arange(half, dtype=F32) / half))
    ang = pos.astype(F32)[:, None] * inv[None, :]
    cos = jnp.tile(jnp.cos(ang), (1, LANES // half))
    sin = jnp.sin(ang)
    sin = jnp.tile(jnp.concatenate([-sin, sin], axis=-1), (1, LANES // HEAD_DIM))
    return cos, sin


def kernel(x_prompt, x_sample, cache_k, cache_v, state_pool, p_prompt, p_sample, norm_mix, norm_ffn, norm_ple, norm_final, w_qkv, b_qkv, w_o, b_o, sinks, w_pool, pool_scale, w_ff_gate, w_ff_up, w_ff_down, w_router, b_router, w_exp_gate, w_exp_up, w_exp_down, w_ple, w_ple_gate, b_ple_gate):
    batch, seq, _ = x_prompt.shape
    dec = x_sample.shape[0]
    row2 = lambda a: a.reshape(1, -1)
    w = dict(
        exp_gate=w_exp_gate[0], exp_up=w_exp_up[0], exp_down=w_exp_down[0],
        norm_ple1=row2(norm_ple[1]), ple_gate1=w_ple_gate[1].astype(BF16), b_ple_gate1=row2(b_ple_gate[1]),
        ple1=w_ple[1].astype(BF16), norm_final=row2(norm_final))
    wqkv = w_qkv[0].astype(BF16)
    l0 = [w_o[0].astype(BF16), row2(b_o[0]), row2(norm_ffn[0]), w_ff_gate[0].astype(BF16),
          w_ff_up[0].astype(BF16), w_ff_down[0].astype(BF16), row2(norm_ple[0]),
          w_ple_gate[0].astype(BF16), row2(b_ple_gate[0]), w_ple[0].astype(BF16), row2(norm_mix[1])]
    wr_hi = w_router[0].astype(BF16)
    wr_lo = (w_router[0] - wr_hi.astype(F32)).astype(BF16)
    pool_w = [w_pool[0].astype(BF16), row2(pool_scale[0]), row2(norm_ffn[1]),
              jnp.concatenate([wr_hi, wr_lo], axis=1), row2(b_router[0])]

    tm, tile = 512, 512
    n_prompt = batch * seq
    t_pad = n_prompt + tm

    xp = x_prompt.reshape(n_prompt, D_MODEL)
    pp = p_prompt.reshape(p_prompt.shape[0], n_prompt, PLE_DIM)
    cos_p, sin_p = _rope_tables(jnp.arange(seq, dtype=jnp.int32))
    q, k, v = _qkv_rope(xp, row2(norm_mix[0]), wqkv, row2(b_qkv[0]), cos_p, sin_p, tm=tm)
    o = _prompt_attention(q, k, v, sinks[0], batch, seq)
    h1, hn1 = _layer0_tail(xp, o, pp, *l0, tm=tm)
    *padded, last = _pool_prompt(h1, hn1, *pool_w, batch=batch, seq=seq, tm=tm, tile=tile, t_pad=t_pad)
    kv_shape = (1, batch, WINDOW, N_KV_HEADS, HEAD_DIM)
    new_k_prompt = k.reshape(batch, seq, KV_DIM)[:, -WINDOW:].reshape(kv_shape)
    new_v_prompt = v.reshape(batch, seq, KV_DIM)[:, -WINDOW:].reshape(kv_shape)
    new_pool_prompt = last[:, -POOL_HIST:][None]

    xs_ = x_sample.reshape(dec, D_MODEL)
    ps = p_sample.reshape(p_sample.shape[0], dec, PLE_DIM)
    cos_s, sin_s = _rope_tables(jnp.full((dec,), PAST_LEN, jnp.int32))
    qs, ks, vs = _qkv_rope(xs_, row2(norm_mix[0]), wqkv, row2(b_qkv[0]), cos_s, sin_s, tm=dec)
    n_hist = cache_k.shape[2]
    k_win = jnp.concatenate([cache_k[0].reshape(dec, n_hist, KV_DIM), ks[:, None]], axis=1)[:, -n_hist:]
    v_win = jnp.concatenate([cache_v[0].reshape(dec, n_hist, KV_DIM), vs[:, None]], axis=1)[:, -n_hist:]
    head_kv = jnp.arange(N_HEADS) // GROUP
    blk = (head_kv[:, None] == jnp.arange(N_KV_HEADS)[None, :]).astype(BF16)
    q_blk = (qs.reshape(dec, N_HEADS, 1, HEAD_DIM) * blk[None, :, :, None]).reshape(dec, N_HEADS, KV_DIM)
    o_blk = _sample_attention(q_blk, k_win, v_win, sinks[0])
    o_s = jnp.take_along_axis(o_blk.reshape(dec, N_HEADS, N_KV_HEADS, HEAD_DIM),
                              head_kv[None, :, None, None], axis=2).reshape(dec, N_HEADS * HEAD_DIM)
    h1s, hn1s = _layer0_tail(xs_, o_s.astype(BF16), ps, *l0, tm=dec)
    hist_t = jnp.swapaxes(state_pool[0], 0, 1)
    padded = _pool_sample(h1s, hn1s, hist_t, *pool_w, padded=padded, row0=n_prompt, tile=tile)
    new_k_sample = k_win.reshape(1, dec, n_hist, N_KV_HEADS, HEAD_DIM)
    new_v_sample = v_win.reshape(1, dec, n_hist, N_KV_HEADS, HEAD_DIM)
    new_pool_sample = jnp.concatenate([state_pool[0], hn1s[:, None]], axis=1)[:, -POOL_HIST:][None]

    expected = TOP_K * (n_prompt + dec) / N_EXPERTS
    tiles_per_expert = max(1, round(expected / EXPERT_TILE_TARGET))
    tmd = _round_up(int(expected * (1 + EXPERT_TILE_SLACK) / tiles_per_expert), EXPERT_ROW_STEPS * ROW_ALIGN)
    y_prompt, y_sample = _moe_and_head(*padded, pp, ps, w, tile=tile, tmd=tmd, n_prompt=n_prompt)

    return (y_prompt.reshape(batch, seq, D_MODEL), y_sample.reshape(dec, 1, D_MODEL),
            new_k_prompt, new_v_prompt, new_pool_prompt, new_k_sample, new_v_sample, new_pool_sample)
```

```python
import functools

import jax
import jax.numpy as jnp
from jax import lax
from jax.experimental import pallas as pl
from jax.experimental.pallas import tpu as pltpu

F32 = jnp.float32
BF16 = jnp.bfloat16

D_MODEL = 1024
HEAD_DIM = 64
N_HEADS = 16
N_KV_HEADS = 4
GROUP = N_HEADS // N_KV_HEADS
KV_DIM = N_KV_HEADS * HEAD_DIM
QKV_DIM = (N_HEADS + 2 * N_KV_HEADS) * HEAD_DIM
WINDOW = 128
ROPE_THETA = 10000.0
PAST_LEN = 16384
POOL_WINDOWS = (2, 4, 8, 16)
POOL_GROUP_DIM = D_MODEL // len(POOL_WINDOWS)
POOL_HIST = max(POOL_WINDOWS) - 1
POOL_HALO = 16
D_FF = 2816
N_EXPERTS = 8
D_FF_EXPERT = 3584
PLE_DIM = 256
EPS = 1e-6

LANES = 128
VMEM_LIMIT = 56 * 1024 * 1024

ATTN_TQ = 1024
ATTN_STAGE = 2
MXU_DIM = 256
FF_CHUNK_EDGES = (0, 6 * MXU_DIM, D_FF)
EXPERT_FF_CHUNK = 512
EXPERT_ROW_STEPS = 8
EXPERT_TILE_TARGET = 2048
EXPERT_TILE_SLACK = 0.05
TOP_K = 2
ROW_ALIGN = 16
TAIL_PIECES = (2048, 1024, 512, 256, 128, 64, 32, 16)


def _cparams(sem):
    return pltpu.CompilerParams(dimension_semantics=sem, vmem_limit_bytes=VMEM_LIMIT)


def _const_spec(shape):
    nd = len(shape)
    return pl.BlockSpec(shape, lambda *_: (0,) * nd, pipeline_mode=pl.Buffered(1))


def _rms(x, g):
    return x * lax.rsqrt(jnp.mean(x * x, axis=-1, keepdims=True) + EPS) * g


def _mm(a, b):
    return jnp.dot(a, b, preferred_element_type=F32)


def _silu(x):
    return x * jax.nn.sigmoid(x)


def _qkv_kernel(x_ref, g_ref, w_ref, b_ref, cos_ref, sin_ref, q_ref, k_ref, v_ref):
    hn = _rms(x_ref[...], g_ref[...])
    qkv = _mm(hn.astype(BF16), w_ref[...]) + b_ref[...]
    cos = cos_ref[...]
    sin = sin_ref[...]
    lane = lax.broadcasted_iota(jnp.int32, cos.shape, 1)
    first_half = (lane % HEAD_DIM) < (HEAD_DIM // 2)

    def rope(xb):
        partner = jnp.where(first_half, pltpu.roll(xb, LANES - HEAD_DIM // 2, 1),
                            pltpu.roll(xb, HEAD_DIM // 2, 1))
        return xb * cos + partner * sin

    scale = HEAD_DIM ** -0.5
    for c in range(N_HEADS * HEAD_DIM // LANES):
        sl = slice(c * LANES, (c + 1) * LANES)
        q_ref[:, sl] = (rope(qkv[:, sl]) * scale).astype(BF16)
    for c in range(KV_DIM // LANES):
        sl = slice(c * LANES, (c + 1) * LANES)
        k_ref[:, sl] = rope(qkv[:, N_HEADS * HEAD_DIM + c * LANES:N_HEADS * HEAD_DIM + (c + 1) * LANES])
    v_ref[...] = qkv[:, N_HEADS * HEAD_DIM + KV_DIM:]


def _qkv_rope(x, g, w, b, cos, sin, tm):
    t = x.shape[0]
    n_pos_tiles = cos.shape[0] // tm
    row = lambda i: (i, 0)
    return pl.pallas_call(
        _qkv_kernel,
        grid=(t // tm,),
        in_specs=[
            pl.BlockSpec((tm, D_MODEL), row),
            _const_spec((1, D_MODEL)),
            _const_spec((D_MODEL, QKV_DIM)),
            _const_spec((1, QKV_DIM)),
            pl.BlockSpec((tm, LANES), lambda i: (i % n_pos_tiles, 0)),
            pl.BlockSpec((tm, LANES), lambda i: (i % n_pos_tiles, 0)),
        ],
        out_specs=[
            pl.BlockSpec((tm, N_HEADS * HEAD_DIM), row),
            pl.BlockSpec((tm, KV_DIM), row),
            pl.BlockSpec((tm, KV_DIM), row),
        ],
        out_shape=[
            jax.ShapeDtypeStruct((t, N_HEADS * HEAD_DIM), BF16),
            jax.ShapeDtypeStruct((t, KV_DIM), F32),
            jax.ShapeDtypeStruct((t, KV_DIM), F32),
        ],
        compiler_params=_cparams(("parallel",)),
        name="qkv_rope",
    )(x, g, w, b, cos, sin)


V_AUG = 4 * HEAD_DIM


def _attn_kernel(sink_ref, q_ref, kc_ref, kp_ref, vc_ref, vp_ref, o_ref, kbuf, vbuf, bias):
    n = pl.program_id(1)
    kbuf[0:WINDOW] = kp_ref[...].astype(BF16)
    kbuf[WINDOW:] = kc_ref[...].astype(BF16)
    v_all = jnp.concatenate([vp_ref[...], vc_ref[...]], axis=0).astype(BF16)
    pad0 = jnp.zeros((v_all.shape[0], HEAD_DIM), BF16)
    pad1 = jnp.ones((v_all.shape[0], 2 * HEAD_DIM), BF16)
    for kv in range(N_KV_HEADS):
        vbuf[:, kv * V_AUG:(kv + 1) * V_AUG] = jnp.concatenate(
            [v_all[:, kv * HEAD_DIM:(kv + 1) * HEAD_DIM], pad0, pad1], axis=1)
    shape = (GROUP * WINDOW, 2 * WINDOW)
    qrow = lax.broadcasted_iota(jnp.int32, shape, 0) & (WINDOW - 1)
    col = lax.broadcasted_iota(jnp.int32, shape, 1)
    mask_cur = (col >= WINDOW) & (col - WINDOW <= qrow)
    mask_prev = (col < WINDOW) & (col > qrow)
    bias[0] = jnp.where(mask_cur | mask_prev, 0.0, -jnp.inf)
    bias[1] = jnp.where(mask_cur, 0.0, -jnp.inf)
    rows = lambda sb: slice(sb * WINDOW, (sb + 1) * WINDOW)
    window = lambda sb: slice(sb * WINDOW, (sb + 2) * WINDOW)
    for first in range(0, ATTN_TQ // WINDOW, ATTN_STAGE):
        blocks = range(first, first + ATTN_STAGE)
        scores, probs, sink_term, outs = {}, {}, {}, {}
        for sb in blocks:
            for kv in range(N_KV_HEADS):
                qg = jnp.concatenate([q_ref[rows(sb), h * HEAD_DIM:(h + 1) * HEAD_DIM]
                                      for h in range(kv * GROUP, (kv + 1) * GROUP)], axis=0)
                scores[sb, kv] = lax.dot_general(qg, kbuf[window(sb), kv * HEAD_DIM:(kv + 1) * HEAD_DIM],
                                                 (((1,), (1,)), ((), ())), preferred_element_type=F32)
        for sb in blocks:
            bias_sb = bias[jnp.where(n == 0, 1, 0)] if sb == 0 else bias[0]
            for kv in range(N_KV_HEADS):
                s = scores[sb, kv] + bias_sb
                p = []
                for g in range(GROUP):
                    h = kv * GROUP + g
                    sh = s[g * WINDOW:(g + 1) * WINDOW]
                    m = jnp.maximum(jnp.max(sh, axis=-1, keepdims=True), sink_ref[h])
                    p.append(jnp.exp(sh - m).astype(BF16))
                    sink_term[sb, h] = jnp.exp(sink_ref[h] - m)
                probs[sb, kv] = jnp.concatenate(p, axis=0)
        for sb in blocks:
            for kv in range(N_KV_HEADS):
                outs[sb, kv] = _mm(probs[sb, kv], vbuf[window(sb), kv * V_AUG:(kv + 1) * V_AUG])
        for sb in blocks:
            for h in range(N_HEADS):
                oh = outs[sb, h // GROUP][(h % GROUP) * WINDOW:(h % GROUP + 1) * WINDOW]
                o = oh[:, :2 * HEAD_DIM] / (oh[:, 2 * HEAD_DIM:] + sink_term[sb, h])
                o_ref[rows(sb), h * HEAD_DIM:(h + 1) * HEAD_DIM] = o[:, :HEAD_DIM].astype(BF16)


def _prompt_attention(q, k, v, sinks, batch, seq):
    nq = seq // ATTN_TQ
    per = ATTN_TQ // WINDOW
    cur = lambda b, n: (b * nq + n, 0)
    prev = lambda b, n: (jnp.maximum(b * nq * per + n * per - 1, b * nq * per), 0)
    return pl.pallas_call(
        _attn_kernel,
        grid=(batch, nq),
        in_specs=[
            pl.BlockSpec(memory_space=pltpu.SMEM),
            pl.BlockSpec((ATTN_TQ, N_HEADS * HEAD_DIM), cur),
            pl.BlockSpec((ATTN_TQ, KV_DIM), cur),
            pl.BlockSpec((WINDOW, KV_DIM), prev),
            pl.BlockSpec((ATTN_TQ, KV_DIM), cur),
            pl.BlockSpec((WINDOW, KV_DIM), prev),
        ],
        out_specs=pl.BlockSpec((ATTN_TQ, N_HEADS * HEAD_DIM), cur),
        out_shape=jax.ShapeDtypeStruct(q.shape, BF16),
        scratch_shapes=[pltpu.VMEM((ATTN_TQ + WINDOW, KV_DIM), BF16),
                        pltpu.VMEM((ATTN_TQ + WINDOW, N_KV_HEADS * V_AUG), BF16),
                        pltpu.VMEM((2, GROUP * WINDOW, 2 * WINDOW), F32)],
        compiler_params=_cparams(("parallel", "parallel")),
        name="prompt_attention",
    )(sinks, q, k, k, v, v)


def _attn_sample_kernel(sink_ref, q_ref, k_ref, v_ref, o_ref):
    s = jnp.einsum("bhc,blc->bhl", q_ref[...], k_ref[...].astype(BF16),
                   preferred_element_type=F32)
    sink = sink_ref[...]
    m = jnp.maximum(jnp.max(s, axis=-1, keepdims=True), sink)
    p = jnp.exp(s - m)
    denom = jnp.sum(p, axis=-1, keepdims=True) + jnp.exp(sink - m)
    o_ref[...] = jnp.einsum("bhl,blc->bhc", (p / denom).astype(BF16), v_ref[...].astype(BF16),
                            preferred_element_type=F32)


def _sample_attention(q_blk, k_win, v_win, sinks, bt=16):
    b = q_blk.shape[0]
    blk = lambda i: (i, 0, 0)
    return pl.pallas_call(
        _attn_sample_kernel,
        grid=(b // bt,),
        in_specs=[
            _const_spec((1, N_HEADS, 1)),
            pl.BlockSpec((bt, N_HEADS, KV_DIM), blk),
            pl.BlockSpec((bt, WINDOW, KV_DIM), blk),
            pl.BlockSpec((bt, WINDOW, KV_DIM), blk),
        ],
        out_specs=pl.BlockSpec((bt, N_HEADS, KV_DIM), blk),
        out_shape=jax.ShapeDtypeStruct((b, N_HEADS, KV_DIM), F32),
        compiler_params=_cparams(("parallel",)),
        name="sample_attention",
    )(sinks.reshape(1, N_HEADS, 1), q_blk, k_win, v_win)


def _ple(h, p, g, wg, bg, wp):
    gate = jax.nn.sigmoid(_mm(_rms(h, g).astype(BF16), wg) + bg)
    return h + gate * _mm(p.astype(BF16), wp)


def _layer0_tail_math(x_ref, o_ref, p, wo_ref, bo_ref, gf_ref, wg_ref, wu_ref, wd_ref,
                      gp_ref, wpg_ref, bpg_ref, wp_ref, gm_ref):
    h = x_ref[...] + _mm(o_ref[...], wo_ref[...]) + bo_ref[...]
    hn = _rms(h, gf_ref[...]).astype(BF16)
    f = None
    for c0, c1 in zip(FF_CHUNK_EDGES[:-1], FF_CHUNK_EDGES[1:]):
        a = (_silu(_mm(hn, wg_ref[:, c0:c1])) * _mm(hn, wu_ref[:, c0:c1])).astype(BF16)
        part = _mm(a, wd_ref[c0:c1, :])
        f = part if f is None else f + part
    h = h + f
    h = _ple(h, p, gp_ref[...], wpg_ref[...], bpg_ref[...], wp_ref[...])
    return h, _rms(h, gm_ref[...])


def _layer0_tail_kernel(x_ref, o_ref, p_ref, wo_ref, bo_ref, gf_ref, wg_ref, wu_ref, wd_ref,
                        gp_ref, wpg_ref, bpg_ref, wp_ref, gm_ref, h_ref, hn_ref):
    h_ref[...], hn_ref[...] = _layer0_tail_math(x_ref, o_ref, p_ref[0], wo_ref, bo_ref, gf_ref, wg_ref,
                                                wu_ref, wd_ref, gp_ref, wpg_ref, bpg_ref, wp_ref, gm_ref)


def _layer0_tail(x, o, p, wo, bo, gf, wg, wu, wd, gp, wpg, bpg, wp, gm, tm):
    t = x.shape[0]
    row = lambda i: (i, 0)
    consts = [wo, bo, gf, wg, wu, wd, gp, wpg, bpg, wp, gm]
    return pl.pallas_call(
        _layer0_tail_kernel,
        grid=(t // tm,),
        in_specs=[pl.BlockSpec((tm, D_MODEL), row), pl.BlockSpec((tm, D_MODEL), row),
                  pl.BlockSpec((1, tm, PLE_DIM), lambda i: (0, i, 0))] + [_const_spec(c.shape) for c in consts],
        out_specs=[pl.BlockSpec((tm, D_MODEL), row), pl.BlockSpec((tm, D_MODEL), row)],
        out_shape=[jax.ShapeDtypeStruct((t, D_MODEL), F32), jax.ShapeDtypeStruct((t, D_MODEL), F32)],
        compiler_params=_cparams(("parallel",)),
        name="layer0_tail",
    )(x, o, p, *consts)


def _segment_ranks(sel, tile, lr_out, lrt_out, cnt_out):
    r = lax.broadcasted_iota(jnp.int32, (tile, tile), 0)
    c = lax.broadcasted_iota(jnp.int32, (tile, tile), 1)
    earlier = (c < r).astype(BF16)
    pad = jnp.full((tile, LANES - N_EXPERTS), -1.0, F32)
    for k in range(sel.shape[0] // tile):
        rows = slice(k * tile, (k + 1) * tile)
        sel_k = sel[rows]
        rank = _mm(earlier, sel_k.astype(BF16))
        lr = jnp.where(sel_k, rank, -1.0)
        lr_out[rows, :] = lr.astype(jnp.int32)
        lrt_out[:, rows] = jnp.concatenate([lr, pad], axis=1).T[:N_EXPERTS].astype(jnp.int32)
        cnt_out[k] = jnp.sum(sel_k.astype(F32), axis=0, keepdims=True).astype(jnp.int32)


def _pool_project_route(h, hn, pooled_sum_inv, wpool_ref, ps_ref, gf_ref, wr_ref, br_ref,
                        h_out, hn_out, gates_out, lr_out, lrt_out, cnt_out, tile):
    mixed = []
    for g in range(len(POOL_WINDOWS)):
        cols = slice(g * POOL_GROUP_DIM, (g + 1) * POOL_GROUP_DIM)
        pooled = pooled_sum_inv[g] - hn[:, cols]
        mixed.append(_mm(pooled.astype(BF16), wpool_ref[g]))
    h = h + jnp.concatenate(mixed, axis=-1) * ps_ref[...]
    h_out[...] = h
    hn2 = _rms(h, gf_ref[...])
    hi = hn2.astype(BF16)
    hn_out[...] = hi
    lo = (hn2 - hi.astype(F32)).astype(BF16)
    by_hi = _mm(hi, wr_ref[...])
    logits = (by_hi[:, :N_EXPERTS] + by_hi[:, N_EXPERTS:] + _mm(lo, wr_ref[:, :N_EXPERTS])) + br_ref[...]
    idx = lax.broadcasted_iota(jnp.int32, logits.shape, 1)
    m1 = jnp.max(logits, axis=-1, keepdims=True)
    i1 = jnp.min(jnp.where(logits == m1, idx, N_EXPERTS), axis=-1, keepdims=True)
    rest = jnp.where(idx == i1, -jnp.inf, logits)
    m2 = jnp.max(rest, axis=-1, keepdims=True)
    i2 = jnp.min(jnp.where(rest == m2, idx, N_EXPERTS), axis=-1, keepdims=True)
    e = jnp.exp(m2 - m1)
    w1 = 1.0 / (1.0 + e)
    w2 = e / (1.0 + e)
    gates_out[...] = jnp.where(idx == i1, w1, jnp.where(idx == i2, w2, 0.0))
    _segment_ranks((idx == i1) | (idx == i2), tile, lr_out, lrt_out, cnt_out)


def _route_out_specs(t, tm, tile, blk):
    specs = [pl.BlockSpec((tm, N_EXPERTS), lambda *g: (blk(*g), 0)),
             pl.BlockSpec((tm, N_EXPERTS), lambda *g: (blk(*g), 0)),
             pl.BlockSpec((N_EXPERTS, tm), lambda *g: (0, blk(*g))),
             pl.BlockSpec((tm // tile, 1, N_EXPERTS), lambda *g: (blk(*g), 0, 0))]
    shapes = [jax.ShapeDtypeStruct((t, N_EXPERTS), F32),
              jax.ShapeDtypeStruct((t, N_EXPERTS), jnp.int32),
              jax.ShapeDtypeStruct((N_EXPERTS, t), jnp.int32),
              jax.ShapeDtypeStruct((t // tile, 1, N_EXPERTS), jnp.int32)]
    return specs, shapes


def _pool_prompt_kernel(h_ref, hn_ref, wpool_ref, ps_ref, gf_ref, wr_ref, br_ref,
                        h_out, hn_out, gates_out, lr_out, lrt_out, cnt_out, last_out, carry, buf, *, tile):
    b = pl.program_id(0)
    n = pl.program_id(1)
    tm = h_ref.shape[0]
    n_seq = pl.num_programs(0) - 1

    @pl.when(b < n_seq)
    def _():
        @pl.when(n == 0)
        def _():
            carry[...] = jnp.zeros_like(carry)

        hn = hn_ref[...]
        buf[0:POOL_HALO] = carry[...]
        buf[POOL_HALO:] = hn
        carry[...] = hn[tm - POOL_HALO:]
        last_out[0] = hn[tm - POOL_HALO:]
        pos = n * tm + lax.broadcasted_iota(jnp.int32, (tm, 1), 0)
        means = []
        for g, w in enumerate(POOL_WINDOWS):
            s = buf[:, g * POOL_GROUP_DIM:(g + 1) * POOL_GROUP_DIM]
            shift = 1
            while shift < w:
                s = s + pltpu.roll(s, shift, 0)
                shift *= 2
            cnt = jnp.minimum(w, pos + 1).astype(F32)
            means.append(s[POOL_HALO:] * (1.0 / cnt))
        _pool_project_route(h_ref[...], hn, means, wpool_ref, ps_ref, gf_ref, wr_ref, br_ref,
                            h_out, hn_out, gates_out, lr_out, lrt_out, cnt_out, tile)

    @pl.when(b == n_seq)
    def _():
        h_out[...] = jnp.zeros_like(h_out)
        hn_out[...] = jnp.zeros_like(hn_out)
        gates_out[...] = jnp.zeros_like(gates_out)
        lr_out[...] = jnp.full(lr_out.shape, -1, jnp.int32)
        lrt_out[...] = jnp.full(lrt_out.shape, -1, jnp.int32)
        cnt_out[...] = jnp.zeros_like(cnt_out)


def _pool_prompt(h, hn, wpool, ps, gf, wr, br, batch, seq, tm, tile, t_pad):
    ns = seq // tm
    pad_blk = batch * ns
    blk = lambda b, n: jnp.where(b < batch, b * ns + n, pad_blk)
    src = lambda b, n: (jnp.minimum(b * ns + n, pad_blk - 1), 0)
    consts = [wpool, ps, gf, wr, br]
    route_specs, route_shapes = _route_out_specs(t_pad, tm, tile, blk)
    row = lambda b, n: (blk(b, n), 0)
    return pl.pallas_call(
        functools.partial(_pool_prompt_kernel, tile=tile),
        grid=(batch + 1, ns),
        in_specs=[pl.BlockSpec((tm, D_MODEL), src), pl.BlockSpec((tm, D_MODEL), src)]
        + [_const_spec(c.shape) for c in consts],
        out_specs=[pl.BlockSpec((tm, D_MODEL), row), pl.BlockSpec((tm, D_MODEL), row)] + route_specs
        + [pl.BlockSpec((1, POOL_HALO, D_MODEL), lambda b, n: (jnp.minimum(b, batch - 1), 0, 0))],
        out_shape=[jax.ShapeDtypeStruct((t_pad, D_MODEL), F32), jax.ShapeDtypeStruct((t_pad, D_MODEL), BF16)]
        + route_shapes + [jax.ShapeDtypeStruct((batch, POOL_HALO, D_MODEL), F32)],
        scratch_shapes=[pltpu.VMEM((POOL_HALO, D_MODEL), F32),
                        pltpu.VMEM((tm + POOL_HALO, D_MODEL), F32)],
        compiler_params=_cparams(("arbitrary", "arbitrary")),
        name="pool_prompt",
    )(h, hn, *consts)


def _pool_sample_kernel(h_ref, hn_ref, hist_ref, wpool_ref, ps_ref, gf_ref, wr_ref, br_ref, *rest):
    outs = rest[len(rest) // 2:]
    hn = hn_ref[...]
    means = []
    for g, w in enumerate(POOL_WINDOWS):
        cols = slice(g * POOL_GROUP_DIM, (g + 1) * POOL_GROUP_DIM)
        s = hn[:, cols]
        for j in range(1, w):
            s = s + hist_ref[POOL_HIST - j, :, cols]
        means.append(s * (1.0 / min(w, PAST_LEN + 1)))
    _pool_project_route(h_ref[...], hn, means, wpool_ref, ps_ref, gf_ref, wr_ref, br_ref,
                        *outs, hn.shape[0])


def _pool_sample(h, hn, hist_t, wpool, ps, gf, wr, br, padded, row0, tile):
    t = h.shape[0]
    args = [h, hn, hist_t, wpool, ps, gf, wr, br]
    blk = row0 // t
    out_specs = [pl.BlockSpec((t, D_MODEL), lambda i: (blk, 0)), pl.BlockSpec((t, D_MODEL), lambda i: (blk, 0)),
                 pl.BlockSpec((t, N_EXPERTS), lambda i: (blk, 0)), pl.BlockSpec((t, N_EXPERTS), lambda i: (blk, 0)),
                 pl.BlockSpec((N_EXPERTS, t), lambda i: (0, blk)),
                 pl.BlockSpec((1, 1, N_EXPERTS), lambda i: (row0 // tile, 0, 0))]
    return pl.pallas_call(
        _pool_sample_kernel,
        grid=(1,),
        in_specs=[_const_spec(a.shape) for a in args] + [pl.BlockSpec(memory_space=pl.ANY)] * len(padded),
        out_specs=out_specs,
        out_shape=[jax.ShapeDtypeStruct(a.shape, a.dtype) for a in padded],
        input_output_aliases={len(args) + k: k for k in range(len(padded))},
        compiler_params=_cparams(("arbitrary",)),
        name="pool_sample",
    )(*args, *padded)


def _segment_rows(tile):
    return tile + ROW_ALIGN


def _segment_copies(meta_ref, step, tile, vmem_of, hbm, sem_of, to_hbm):
    half = tile // 2
    out = []
    for k, (start, size) in enumerate(((0, half), (half, _segment_rows(tile) - half))):
        for e in range(N_EXPERTS):
            pos = meta_ref[0, step * N_EXPERTS + e]
            pred = None if k == 0 else meta_ref[1, step * N_EXPERTS + e] > half
            v = vmem_of(e).at[pl.ds(start, size)]
            h = hbm.at[pl.ds(pl.multiple_of(pos + start, ROW_ALIGN), size)]
            copy = pltpu.make_async_copy(v, h, sem_of(e)) if to_hbm else pltpu.make_async_copy(h, v, sem_of(e))
            out.append((pred, copy))
    return out


def _segment_row(rank, shared):
    return jnp.where(rank >= 0, rank + shared, -1)


def _for_slot(slot, fn):
    for s in range(2):
        @pl.when(slot == s)
        def _(s=s):
            fn(s)


def _start_all(copies):
    for pred, copy in copies:
        if pred is None:
            copy.start()
        else:
            pl.when(pred)(copy.start)


def _wait_all(copies):
    for pred, copy in copies:
        if pred is None:
            copy.wait()
        else:
            pl.when(pred)(copy.wait)


def _dispatch_kernel(meta_ref, tail_ref, hn_ref, lrt_ref, xs_ref, seg, zeros, sem, zsem, *, slack_chunk):
    i = pl.program_id(0)
    last = pl.num_programs(0) - 1
    tile = hn_ref.shape[0]
    slot = i % 2
    copies = lambda step, s: _segment_copies(meta_ref, step, tile, lambda e: seg.at[s, e], xs_ref,
                                             lambda e: sem.at[s, e], True)
    hn = hn_ref[...]
    half = tile // 2
    rest = _segment_rows(tile) - half
    shared = [meta_ref[2, i * N_EXPERTS + e] for e in range(N_EXPERTS)]
    row_of = lambda e: _segment_row(lrt_ref[e:e + 1, :], shared[e])
    place = lax.broadcasted_iota(jnp.int32, (half, tile), 0)
    for e0 in range(0, N_EXPERTS, 4):
        first = jnp.concatenate([(place == row_of(e)).astype(BF16) for e in range(e0, e0 + 4)], axis=0)
        rows = _mm(first, hn).astype(BF16)
        for k in range(4):
            seg[slot, e0 + k, 0:half] = rows[k * half:(k + 1) * half]
    place_rest = lax.broadcasted_iota(jnp.int32, (rest, tile), 0) + half
    for e in range(N_EXPERTS):
        @pl.when(meta_ref[1, i * N_EXPERTS + e] > half)
        def _(e=e):
            seg[slot, e, half:] = _mm((place_rest == row_of(e)).astype(BF16), hn_ref[...]).astype(BF16)

        @pl.when(shared[e] > 0)
        def _(e=e):
            prev_len = meta_ref[1, jnp.maximum(i - 1, 0) * N_EXPERTS + e]
            block = pl.ds(pl.multiple_of(prev_len - shared[e], ROW_ALIGN), ROW_ALIGN)
            seg[slot, e, 0:ROW_ALIGN] = seg[slot, e, 0:ROW_ALIGN] + seg[1 - slot, e, block]

    @pl.when(i > 0)
    def _():
        _for_slot(1 - slot, lambda s: _wait_all(copies(i - 1, s)))

    _for_slot(slot, lambda s: _start_all(copies(i, s)))

    @pl.when(i == last)
    def _():
        _for_slot(slot, lambda s: _wait_all(copies(i, s)))
        zeros[...] = jnp.zeros_like(zeros)
        tails = []
        for e in range(N_EXPERTS):
            start = tail_ref[0, e]
            cnt = tail_ref[1, e]
            for piece in TAIL_PIECES:
                off = pl.multiple_of(cnt & ~(2 * piece - 1), ROW_ALIGN)
                copy = pltpu.make_async_copy(
                    zeros.at[pl.ds(0, piece)],
                    xs_ref.at[pl.ds(pl.multiple_of(start + off, ROW_ALIGN), piece)],
                    zsem.at[e])
                tails.append(((cnt & piece) != 0, copy))
        _start_all(tails)
        _wait_all(tails)

        def slack_copy(c):
            row = pl.multiple_of(tail_ref[0, N_EXPERTS] + c * slack_chunk, ROW_ALIGN)
            return pltpu.make_async_copy(zeros.at[pl.ds(0, slack_chunk)],
                                         xs_ref.at[pl.ds(row, slack_chunk)], zsem.at[0])

        n_slack = tail_ref[1, N_EXPERTS] // slack_chunk
        lax.fori_loop(0, n_slack, lambda c, _: slack_copy(c).start(), None)
        lax.fori_loop(0, n_slack, lambda c, _: slack_copy(c).wait(), None)


def _dispatch(hn, lr_t, meta, tail, rows, tile, slack_chunk):
    t = hn.shape[0]
    return pl.pallas_call(
        functools.partial(_dispatch_kernel, slack_chunk=slack_chunk),
        grid_spec=pltpu.PrefetchScalarGridSpec(
            num_scalar_prefetch=2,
            grid=(t // tile,),
            in_specs=[pl.BlockSpec((tile, D_MODEL), lambda i, *_: (i, 0)),
                      pl.BlockSpec((N_EXPERTS, tile), lambda i, *_: (0, i))],
            out_specs=pl.BlockSpec(memory_space=pl.ANY),
            scratch_shapes=[pltpu.VMEM((2, N_EXPERTS, _segment_rows(tile), D_MODEL), BF16),
                            pltpu.VMEM((TAIL_PIECES[0], D_MODEL), BF16),
                            pltpu.SemaphoreType.DMA((2, N_EXPERTS)),
                            pltpu.SemaphoreType.DMA((N_EXPERTS,))]),
        out_shape=jax.ShapeDtypeStruct((rows, D_MODEL), BF16),
        compiler_params=_cparams(("arbitrary",)),
        name="expert_dispatch",
    )(meta, tail, hn, lr_t)


def _expert_kernel(te_ref, tr_ref, x_ref, wg_ref, wu_ref, wd_ref, y_ref, acc):
    r = pl.program_id(0)
    j = pl.program_id(1)

    @pl.when(j == 0)
    def _():
        acc[...] = jnp.zeros_like(acc)

    def ffn(m):
        x = x_ref[0:m]
        a = (_silu(_mm(x, wg_ref[0].astype(BF16))) * _mm(x, wu_ref[0].astype(BF16))).astype(BF16)
        acc[0:m] += _mm(a, wd_ref[0].astype(BF16))

    rows = tr_ref[r]
    step = x_ref.shape[0] // EXPERT_ROW_STEPS
    for k in range(1, EXPERT_ROW_STEPS + 1):
        @pl.when((rows > (k - 1) * step) & (rows <= k * step))
        def _(k=k):
            ffn(k * step)

    @pl.when(j == pl.num_programs(1) - 1)
    def _():
        y_ref[...] = acc[...].astype(BF16)


def _expert_ffn(xs, tile_e, tile_rows, wg, wu, wd, tmd):
    rows = xs.shape[0]
    nj = D_FF_EXPERT // EXPERT_FF_CHUNK
    live = lambda r, tr: tr[r] > 0
    jj = lambda r, j, tr: jnp.where(live(r, tr), j, nj - 1)
    return pl.pallas_call(
        _expert_kernel,
        grid_spec=pltpu.PrefetchScalarGridSpec(
            num_scalar_prefetch=2,
            grid=(rows // tmd, nj),
            in_specs=[
                pl.BlockSpec((tmd, D_MODEL), lambda r, j, te, tr: (jnp.where(live(r, tr), r, 0), 0)),
                pl.BlockSpec((1, D_MODEL, EXPERT_FF_CHUNK), lambda r, j, te, tr: (te[r], 0, jj(r, j, tr))),
                pl.BlockSpec((1, D_MODEL, EXPERT_FF_CHUNK), lambda r, j, te, tr: (te[r], 0, jj(r, j, tr))),
                pl.BlockSpec((1, EXPERT_FF_CHUNK, D_MODEL), lambda r, j, te, tr: (te[r], jj(r, j, tr), 0)),
            ],
            out_specs=pl.BlockSpec((tmd, D_MODEL), lambda r, j, te, tr: (r, 0)),
            scratch_shapes=[pltpu.VMEM((tmd, D_MODEL), F32)]),
        out_shape=jax.ShapeDtypeStruct((rows, D_MODEL), BF16),
        compiler_params=_cparams(("parallel", "arbitrary")),
        name="expert_ffn",
    )(tile_e, tile_rows, xs, wg, wu, wd)


def _combine_kernel(meta_ref, h_ref, gates_ref, lr_ref, pp_ref, ps_ref, gp_ref, wpg_ref, bpg_ref, wp_ref,
                    gfin_ref, ys_ref, outp_ref, outs_ref, ybuf, f_ref, sem, *, n_prompt_tiles):
    i = pl.program_id(0)
    n = pl.num_programs(0)
    tile = h_ref.shape[0]
    slot = i % 2
    copies = lambda step, s: _segment_copies(meta_ref, step, tile, lambda e: ybuf.at[s, e], ys_ref,
                                             lambda e: sem.at[s, e], False)

    @pl.when(i == 0)
    def _():
        ybuf[...] = jnp.zeros_like(ybuf)
        _start_all(copies(0, 0))

    @pl.when(i + 1 < n)
    def _():
        _for_slot(1 - slot, lambda s: _start_all(copies(i + 1, s)))

    _for_slot(slot, lambda s: _wait_all(copies(i, s)))
    gates = gates_ref[...]
    lr = lr_ref[...]
    half = tile // 2

    def gather(start, size):
        place = lax.broadcasted_iota(jnp.int32, (tile, size), 1) + start
        for e in range(N_EXPERTS):
            row = _segment_row(lr[:, e:e + 1], meta_ref[2, i * N_EXPERTS + e])
            onehot = (place == row).astype(BF16)
            f_ref[...] += gates[:, e:e + 1] * _mm(onehot, ybuf[slot, e, start:start + size, :])

    f_ref[...] = h_ref[...]
    gather(0, half)
    longest = meta_ref[1, i * N_EXPERTS]
    for e in range(1, N_EXPERTS):
        longest = jnp.maximum(longest, meta_ref[1, i * N_EXPERTS + e])

    @pl.when(longest > half)
    def _():
        gather(half, _segment_rows(tile) - half)

    h = f_ref[...]

    def head(h, p):
        h = _ple(h, p, gp_ref[...], wpg_ref[...], bpg_ref[...], wp_ref[...])
        return _rms(h, gfin_ref[...])

    @pl.when(i < n_prompt_tiles)
    def _():
        outp_ref[...] = head(h, pp_ref[0])

    @pl.when(i == n_prompt_tiles)
    def _():
        n_s = outs_ref.shape[0]
        outs_ref[...] = head(h[:n_s], ps_ref[0])


def _combine(h, gates, lr, pp, ps, gp, wpg, bpg, wp, gfin, ys, meta, tile, n_prompt):
    t = h.shape[0]
    n_pt = n_prompt // tile
    n_s = ps.shape[1]
    row = lambda i, *_: (i, 0)
    prow = lambda i, *_: (jnp.minimum(i, n_pt - 1), 0)
    consts = [gp, wpg, bpg, wp, gfin]
    return pl.pallas_call(
        functools.partial(_combine_kernel, n_prompt_tiles=n_pt),
        grid_spec=pltpu.PrefetchScalarGridSpec(
            num_scalar_prefetch=1,
            grid=(t // tile,),
            in_specs=[pl.BlockSpec((tile, D_MODEL), row), pl.BlockSpec((tile, N_EXPERTS), row),
                      pl.BlockSpec((tile, N_EXPERTS), row),
                      pl.BlockSpec((1, tile, PLE_DIM), lambda i, *_: (1,) + prow(i)),
                      pl.BlockSpec((1, n_s, PLE_DIM), lambda i, *_: (1, 0, 0))]
            + [_const_spec(c.shape) for c in consts] + [pl.BlockSpec(memory_space=pl.ANY)],
            out_specs=[pl.BlockSpec((tile, D_MODEL), prow),
                       pl.BlockSpec((n_s, D_MODEL), lambda i, *_: (0, 0))],
            scratch_shapes=[pltpu.VMEM((2, N_EXPERTS, _segment_rows(tile), D_MODEL), BF16),
                            pltpu.VMEM((tile, D_MODEL), F32),
                            pltpu.SemaphoreType.DMA((2, N_EXPERTS))]),
        out_shape=[jax.ShapeDtypeStruct((n_prompt, D_MODEL), F32),
                   jax.ShapeDtypeStruct((n_s, D_MODEL), F32)],
        compiler_params=_cparams(("arbitrary",)),
        name="combine",
    )(meta, h, gates, lr, pp, ps, *consts, ys)


def _round_up(x, m):
    return (x + m - 1) // m * m


def _route_plan(cnt, t, tile, tmd):
    n_tiles = t // tile
    cnt = cnt.reshape(n_tiles, N_EXPERTS)
    before = jnp.cumsum(cnt, axis=0) - cnt
    shared = before % ROW_ALIGN
    seg_len = shared + cnt
    total = _round_up(cnt.sum(axis=0), ROW_ALIGN)
    guard = _segment_rows(tile) - tile // 2
    region = _round_up(total + guard, tmd)
    region_end = jnp.cumsum(region)
    off = region_end - region
    pos = off[None, :] + before - shared
    rows = _round_up(2 * t + N_EXPERTS * (ROW_ALIGN + tmd + guard), tmd)
    tile_start = jnp.arange(rows // tmd, dtype=jnp.int32) * tmd
    te = jnp.sum(region_end[None, :] <= tile_start[:, None], axis=1).astype(jnp.int32)
    tec = jnp.minimum(te, N_EXPERTS - 1)
    tile_rows = jnp.where(te < N_EXPERTS, jnp.clip(total[tec] - (tile_start - off[tec]), 0, tmd), 0)
    last_e = jnp.max(jnp.where(tile_rows > 0, tec, 0))
    tile_e = jnp.where(tile_rows > 0, tec, last_e).astype(jnp.int32)
    meta = jnp.stack([pos.reshape(-1), seg_len.reshape(-1), shared.reshape(-1)]).astype(jnp.int32)
    tail = jnp.stack([jnp.append(off + total, region_end[-1]),
                      jnp.append(region - total, rows - region_end[-1])]).astype(jnp.int32)
    return dict(meta=meta, tail=tail, tile_e=tile_e, tile_rows=tile_rows.astype(jnp.int32), rows=rows)


def _moe_and_head(h, hn, gates, lr, lr_t, cnt, pp, ps, w, tile, tmd, n_prompt):
    plan = _route_plan(cnt, h.shape[0], tile, tmd)
    xs = _dispatch(hn, lr_t, plan["meta"], plan["tail"], plan["rows"], tile, tmd // EXPERT_ROW_STEPS)
    ys = _expert_ffn(xs, plan["tile_e"], plan["tile_rows"], w["exp_gate"], w["exp_up"], w["exp_down"], tmd)
    return _combine(h, gates, lr, pp, ps, w["norm_ple1"], w["ple_gate1"], w["b_ple_gate1"], w["ple1"],
                    w["norm_final"], ys, plan["meta"], tile, n_prompt)


def _rope_tables(pos):
    half = HEAD_DIM // 2
    inv = 1.0 / (ROPE_THETA ** (jnp.arange(half, dtype=F32) / half))
    ang = pos.astype(F32)[:, None] * inv[None, :]
    cos = jnp.tile(jnp.cos(ang), (1, LANES // half))
    sin = jnp.sin(ang)
    sin = jnp.tile(jnp.concatenate([-sin, sin], axis=-1), (1, LANES // HEAD_DIM))
    return cos, sin


def kernel(x_prompt, x_sample, cache_k, cache_v, state_pool, p_prompt, p_sample, norm_mix, norm_ffn, norm_ple, norm_final, w_qkv, b_qkv, w_o, b_o, sinks, w_pool, pool_scale, w_ff_gate, w_ff_up, w_ff_down, w_router, b_router, w_exp_gate, w_exp_up, w_exp_down, w_ple, w_ple_gate, b_ple_gate):
    batch, seq, _ = x_prompt.shape
    dec = x_sample.shape[0]
    row2 = lambda a: a.reshape(1, -1)
    w = dict(
        exp_gate=w_exp_gate[0], exp_up=w_exp_up[0], exp_down=w_exp_down[0],
        norm_ple1=row2(norm_ple[1]), ple_gate1=w_ple_gate[1].astype(BF16), b_ple_gate1=row2(b_ple_gate[1]),
        ple1=w_ple[1].astype(BF16), norm_final=row2(norm_final))
    wqkv = w_qkv[0].astype(BF16)
    l0 = [w_o[0].astype(BF16), row2(b_o[0]), row2(norm_ffn[0]), w_ff_gate[0].astype(BF16),
          w_ff_up[0].astype(BF16), w_ff_down[0].astype(BF16), row2(norm_ple[0]),
          w_ple_gate[0].astype(BF16), row2(b_ple_gate[0]), w_ple[0].astype(BF16), row2(norm_mix[1])]
    wr_hi = w_router[0].astype(BF16)
    wr_lo = (w_router[0] - wr_hi.astype(F32)).astype(BF16)
    pool_w = [w_pool[0].astype(BF16), row2(pool_scale[0]), row2(norm_ffn[1]),
              jnp.concatenate([wr_hi, wr_lo], axis=1), row2(b_router[0])]

    tm, tile = 512, 512
    n_prompt = batch * seq
    t_pad = n_prompt + tm

    xp = x_prompt.reshape(n_prompt, D_MODEL)
    pp = p_prompt.reshape(p_prompt.shape[0], n_prompt, PLE_DIM)
    cos_p, sin_p = _rope_tables(jnp.arange(seq, dtype=jnp.int32))
    q, k, v = _qkv_rope(xp, row2(norm_mix[0]), wqkv, row2(b_qkv[0]), cos_p, sin_p, tm=2 * tm)
    o = _prompt_attention(q, k, v, sinks[0], batch, seq)
    h1, hn1 = _layer0_tail(xp, o, pp, *l0, tm=tm)
    *padded, last = _pool_prompt(h1, hn1, *pool_w, batch=batch, seq=seq, tm=tm, tile=tile, t_pad=t_pad)
    kv_shape = (1, batch, WINDOW, N_KV_HEADS, HEAD_DIM)
    new_k_prompt = k.reshape(batch, seq, KV_DIM)[:, -WINDOW:].reshape(kv_shape)
    new_v_prompt = v.reshape(batch, seq, KV_DIM)[:, -WINDOW:].reshape(kv_shape)
    new_pool_prompt = last[:, -POOL_HIST:][None]

    xs_ = x_sample.reshape(dec, D_MODEL)
    ps = p_sample.reshape(p_sample.shape[0], dec, PLE_DIM)
    cos_s, sin_s = _rope_tables(jnp.full((dec,), PAST_LEN, jnp.int32))
    qs, ks, vs = _qkv_rope(xs_, row2(norm_mix[0]), wqkv, row2(b_qkv[0]), cos_s, sin_s, tm=dec)
    n_hist = cache_k.shape[2]
    k_win = jnp.concatenate([cache_k[0].reshape(dec, n_hist, KV_DIM), ks[:, None]], axis=1)[:, -n_hist:]
    v_win = jnp.concatenate([cache_v[0].reshape(dec, n_hist, KV_DIM), vs[:, None]], axis=1)[:, -n_hist:]
    head_kv = jnp.arange(N_HEADS) // GROUP
    blk = (head_kv[:, None] == jnp.arange(N_KV_HEADS)[None, :]).astype(BF16)
    q_blk = (qs.reshape(dec, N_HEADS, 1, HEAD_DIM) * blk[None, :, :, None]).reshape(dec, N_HEADS, KV_DIM)
    o_blk = _sample_attention(q_blk, k_win, v_win, sinks[0])
    o_s = jnp.take_along_axis(o_blk.reshape(dec, N_HEADS, N_KV_HEADS, HEAD_DIM),
                              head_kv[None, :, None, None], axis=2).reshape(dec, N_HEADS * HEAD_DIM)
    h1s, hn1s = _layer0_tail(xs_, o_s.astype(BF16), ps, *l0, tm=dec)
    hist_t = jnp.swapaxes(state_pool[0], 0, 1)
    padded = _pool_sample(h1s, hn1s, hist_t, *pool_w, padded=padded, row0=n_prompt, tile=tile)
    new_k_sample = k_win.reshape(1, dec, n_hist, N_KV_HEADS, HEAD_DIM)
    new_v_sample = v_win.reshape(1, dec, n_hist, N_KV_HEADS, HEAD_DIM)
    new_pool_sample = jnp.concatenate([state_pool[0], hn1s[:, None]], axis=1)[:, -POOL_HIST:][None]

    expected = TOP_K * (n_prompt + dec) / N_EXPERTS
    tiles_per_expert = max(1, round(expected / EXPERT_TILE_TARGET))
    tmd = _round_up(int(expected * (1 + EXPERT_TILE_SLACK) / tiles_per_expert), EXPERT_ROW_STEPS * ROW_ALIGN)
    y_prompt, y_sample = _moe_and_head(*padded, pp, ps, w, tile=tile, tmd=tmd, n_prompt=n_prompt)

    return (y_prompt.reshape(batch, seq, D_MODEL), y_sample.reshape(dec, 1, D_MODEL),
            new_k_prompt, new_v_prompt, new_pool_prompt, new_k_sample, new_v_sample, new_pool_sample)
```

```python
import functools

import jax
import jax.numpy as jnp
from jax import lax
from jax.experimental import pallas as pl
from jax.experimental.pallas import tpu as pltpu

F32 = jnp.float32
BF16 = jnp.bfloat16

D_MODEL = 1024
HEAD_DIM = 64
N_HEADS = 16
N_KV_HEADS = 4
GROUP = N_HEADS // N_KV_HEADS
KV_DIM = N_KV_HEADS * HEAD_DIM
QKV_DIM = (N_HEADS + 2 * N_KV_HEADS) * HEAD_DIM
WINDOW = 128
ROPE_THETA = 10000.0
PAST_LEN = 16384
POOL_WINDOWS = (2, 4, 8, 16)
POOL_GROUP_DIM = D_MODEL // len(POOL_WINDOWS)
POOL_HIST = max(POOL_WINDOWS) - 1
POOL_HALO = 16
D_FF = 2816
N_EXPERTS = 8
D_FF_EXPERT = 3584
PLE_DIM = 256
EPS = 1e-6

LANES = 128
VMEM_LIMIT = 56 * 1024 * 1024

ATTN_TQ = 1024
ATTN_STAGE = 2
MXU_DIM = 256
FF_CHUNK_EDGES = (0, 6 * MXU_DIM, D_FF)
EXPERT_FF_CHUNK = 512
EXPERT_ROW_STEPS = 8
EXPERT_TILE_TARGET = 2048
EXPERT_TILE_SLACK = 0.05
TOP_K = 2
ROW_ALIGN = 16
TAIL_PIECES = (2048, 1024, 512, 256, 128, 64, 32, 16)


def _cparams(sem):
    return pltpu.CompilerParams(dimension_semantics=sem, vmem_limit_bytes=VMEM_LIMIT)


def _const_spec(shape):
    nd = len(shape)
    return pl.BlockSpec(shape, lambda *_: (0,) * nd, pipeline_mode=pl.Buffered(1))


def _rms(x, g):
    return x * lax.rsqrt(jnp.mean(x * x, axis=-1, keepdims=True) + EPS) * g


def _mm(a, b):
    return jnp.dot(a, b, preferred_element_type=F32)


def _silu(x):
    return x * jax.nn.sigmoid(x)


def _qkv_kernel(x_ref, g_ref, w_ref, b_ref, cos_ref, sin_ref, q_ref, k_ref, v_ref):
    hn = _rms(x_ref[...], g_ref[...])
    qkv = _mm(hn.astype(BF16), w_ref[...]) + b_ref[...]
    cos = cos_ref[...]
    sin = sin_ref[...]
    lane = lax.broadcasted_iota(jnp.int32, cos.shape, 1)
    first_half = (lane % HEAD_DIM) < (HEAD_DIM // 2)

    def rope(xb):
        partner = jnp.where(first_half, pltpu.roll(xb, LANES - HEAD_DIM // 2, 1),
                            pltpu.roll(xb, HEAD_DIM // 2, 1))
        return xb * cos + partner * sin

    scale = HEAD_DIM ** -0.5
    for c in range(N_HEADS * HEAD_DIM // LANES):
        sl = slice(c * LANES, (c + 1) * LANES)
        q_ref[:, sl] = (rope(qkv[:, sl]) * scale).astype(BF16)
    for c in range(KV_DIM // LANES):
        sl = slice(c * LANES, (c + 1) * LANES)
        k_ref[:, sl] = rope(qkv[:, N_HEADS * HEAD_DIM + c * LANES:N_HEADS * HEAD_DIM + (c + 1) * LANES])
    v_ref[...] = qkv[:, N_HEADS * HEAD_DIM + KV_DIM:]


def _qkv_rope(x, g, w, b, cos, sin, tm):
    t = x.shape[0]
    n_pos_tiles = cos.shape[0] // tm
    row = lambda i: (i, 0)
    return pl.pallas_call(
        _qkv_kernel,
        grid=(t // tm,),
        in_specs=[
            pl.BlockSpec((tm, D_MODEL), row),
            _const_spec((1, D_MODEL)),
            _const_spec((D_MODEL, QKV_DIM)),
            _const_spec((1, QKV_DIM)),
            pl.BlockSpec((tm, LANES), lambda i: (i % n_pos_tiles, 0)),
            pl.BlockSpec((tm, LANES), lambda i: (i % n_pos_tiles, 0)),
        ],
        out_specs=[
            pl.BlockSpec((tm, N_HEADS * HEAD_DIM), row),
            pl.BlockSpec((tm, KV_DIM), row),
            pl.BlockSpec((tm, KV_DIM), row),
        ],
        out_shape=[
            jax.ShapeDtypeStruct((t, N_HEADS * HEAD_DIM), BF16),
            jax.ShapeDtypeStruct((t, KV_DIM), F32),
            jax.ShapeDtypeStruct((t, KV_DIM), F32),
        ],
        compiler_params=_cparams(("parallel",)),
        name="qkv_rope",
    )(x, g, w, b, cos, sin)


V_AUG = 4 * HEAD_DIM


def _attn_kernel(sink_ref, q_ref, kc_ref, kp_ref, vc_ref, vp_ref, o_ref, kbuf, vbuf, bias):
    n = pl.program_id(1)
    kbuf[0:WINDOW] = kp_ref[...].astype(BF16)
    kbuf[WINDOW:] = kc_ref[...].astype(BF16)
    v_all = jnp.concatenate([vp_ref[...], vc_ref[...]], axis=0).astype(BF16)
    pad0 = jnp.zeros((v_all.shape[0], HEAD_DIM), BF16)
    pad1 = jnp.ones((v_all.shape[0], 2 * HEAD_DIM), BF16)
    for kv in range(N_KV_HEADS):
        vbuf[:, kv * V_AUG:(kv + 1) * V_AUG] = jnp.concatenate(
            [v_all[:, kv * HEAD_DIM:(kv + 1) * HEAD_DIM], pad0, pad1], axis=1)
    shape = (GROUP * WINDOW, 2 * WINDOW)
    qrow = lax.broadcasted_iota(jnp.int32, shape, 0) & (WINDOW - 1)
    col = lax.broadcasted_iota(jnp.int32, shape, 1)
    mask_cur = (col >= WINDOW) & (col - WINDOW <= qrow)
    mask_prev = (col < WINDOW) & (col > qrow)
    bias[0] = jnp.where(mask_cur | mask_prev, 0.0, -jnp.inf)
    bias[1] = jnp.where(mask_cur, 0.0, -jnp.inf)
    rows = lambda sb: slice(sb * WINDOW, (sb + 1) * WINDOW)
    window = lambda sb: slice(sb * WINDOW, (sb + 2) * WINDOW)
    for first in range(0, ATTN_TQ // WINDOW, ATTN_STAGE):
        blocks = range(first, first + ATTN_STAGE)
        scores, probs, sink_term, outs = {}, {}, {}, {}
        for sb in blocks:
            for kv in range(N_KV_HEADS):
                qg = jnp.concatenate([q_ref[rows(sb), h * HEAD_DIM:(h + 1) * HEAD_DIM]
                                      for h in range(kv * GROUP, (kv + 1) * GROUP)], axis=0)
                scores[sb, kv] = lax.dot_general(qg, kbuf[window(sb), kv * HEAD_DIM:(kv + 1) * HEAD_DIM],
                                                 (((1,), (1,)), ((), ())), preferred_element_type=F32)
        for sb in blocks:
            bias_sb = bias[jnp.where(n == 0, 1, 0)] if sb == 0 else bias[0]
            for kv in range(N_KV_HEADS):
                s = scores[sb, kv] + bias_sb
                p = []
                for g in range(GROUP):
                    h = kv * GROUP + g
                    sh = s[g * WINDOW:(g + 1) * WINDOW]
                    m = jnp.maximum(jnp.max(sh, axis=-1, keepdims=True), sink_ref[h])
                    p.append(jnp.exp(sh - m).astype(BF16))
                    sink_term[sb, h] = jnp.exp(sink_ref[h] - m)
                probs[sb, kv] = jnp.concatenate(p, axis=0)
        for sb in blocks:
            for kv in range(N_KV_HEADS):
                outs[sb, kv] = _mm(probs[sb, kv], vbuf[window(sb), kv * V_AUG:(kv + 1) * V_AUG])
        for sb in blocks:
            for h in range(N_HEADS):
                oh = outs[sb, h // GROUP][(h % GROUP) * WINDOW:(h % GROUP + 1) * WINDOW]
                o = oh[:, :2 * HEAD_DIM] / (oh[:, 2 * HEAD_DIM:] + sink_term[sb, h])
                o_ref[rows(sb), h * HEAD_DIM:(h + 1) * HEAD_DIM] = o[:, :HEAD_DIM].astype(BF16)


def _prompt_attention(q, k, v, sinks, batch, seq):
    nq = seq // ATTN_TQ
    per = ATTN_TQ // WINDOW
    cur = lambda b, n: (b * nq + n, 0)
    prev = lambda b, n: (jnp.maximum(b * nq * per + n * per - 1, b * nq * per), 0)
    return pl.pallas_call(
        _attn_kernel,
        grid=(batch, nq),
        in_specs=[
            pl.BlockSpec(memory_space=pltpu.SMEM),
            pl.BlockSpec((ATTN_TQ, N_HEADS * HEAD_DIM), cur),
            pl.BlockSpec((ATTN_TQ, KV_DIM), cur),
            pl.BlockSpec((WINDOW, KV_DIM), prev),
            pl.BlockSpec((ATTN_TQ, KV_DIM), cur),
            pl.BlockSpec((WINDOW, KV_DIM), prev),
        ],
        out_specs=pl.BlockSpec((ATTN_TQ, N_HEADS * HEAD_DIM), cur),
        out_shape=jax.ShapeDtypeStruct(q.shape, BF16),
        scratch_shapes=[pltpu.VMEM((ATTN_TQ + WINDOW, KV_DIM), BF16),
                        pltpu.VMEM((ATTN_TQ + WINDOW, N_KV_HEADS * V_AUG), BF16),
                        pltpu.VMEM((2, GROUP * WINDOW, 2 * WINDOW), F32)],
        compiler_params=_cparams(("parallel", "parallel")),
        name="prompt_attention",
    )(sinks, q, k, k, v, v)


def _attn_sample_kernel(sink_ref, q_ref, k_ref, v_ref, o_ref):
    s = jnp.einsum("bhc,blc->bhl", q_ref[...], k_ref[...].astype(BF16),
                   preferred_element_type=F32)
    sink = sink_ref[...]
    m = jnp.maximum(jnp.max(s, axis=-1, keepdims=True), sink)
    p = jnp.exp(s - m)
    denom = jnp.sum(p, axis=-1, keepdims=True) + jnp.exp(sink - m)
    o_ref[...] = jnp.einsum("bhl,blc->bhc", (p / denom).astype(BF16), v_ref[...].astype(BF16),
                            preferred_element_type=F32)


def _sample_attention(q_blk, k_win, v_win, sinks, bt=16):
    b = q_blk.shape[0]
    blk = lambda i: (i, 0, 0)
    return pl.pallas_call(
        _attn_sample_kernel,
        grid=(b // bt,),
        in_specs=[
            _const_spec((1, N_HEADS, 1)),
            pl.BlockSpec((bt, N_HEADS, KV_DIM), blk),
            pl.BlockSpec((bt, WINDOW, KV_DIM), blk),
            pl.BlockSpec((bt, WINDOW, KV_DIM), blk),
        ],
        out_specs=pl.BlockSpec((bt, N_HEADS, KV_DIM), blk),
        out_shape=jax.ShapeDtypeStruct((b, N_HEADS, KV_DIM), F32),
        compiler_params=_cparams(("parallel",)),
        name="sample_attention",
    )(sinks.reshape(1, N_HEADS, 1), q_blk, k_win, v_win)


def _ple(h, p, g, wg, bg, wp):
    gate = jax.nn.sigmoid(_mm(_rms(h, g).astype(BF16), wg) + bg)
    return h + gate * _mm(p.astype(BF16), wp)


def _layer0_tail_math(x_ref, o_ref, p, wo_ref, bo_ref, gf_ref, wg_ref, wu_ref, wd_ref,
                      gp_ref, wpg_ref, bpg_ref, wp_ref, gm_ref):
    h = x_ref[...] + _mm(o_ref[...], wo_ref[...]) + bo_ref[...]
    hn = _rms(h, gf_ref[...]).astype(BF16)
    f = None
    for c0, c1 in zip(FF_CHUNK_EDGES[:-1], FF_CHUNK_EDGES[1:]):
        a = (_silu(_mm(hn, wg_ref[:, c0:c1])) * _mm(hn, wu_ref[:, c0:c1])).astype(BF16)
        part = _mm(a, wd_ref[c0:c1, :])
        f = part if f is None else f + part
    h = h + f
    h = _ple(h, p, gp_ref[...], wpg_ref[...], bpg_ref[...], wp_ref[...])
    return h, _rms(h, gm_ref[...])


def _layer0_tail_kernel(x_ref, o_ref, p_ref, wo_ref, bo_ref, gf_ref, wg_ref, wu_ref, wd_ref,
                        gp_ref, wpg_ref, bpg_ref, wp_ref, gm_ref, h_ref, hn_ref):
    h_ref[...], hn_ref[...] = _layer0_tail_math(x_ref, o_ref, p_ref[0], wo_ref, bo_ref, gf_ref, wg_ref,
                                                wu_ref, wd_ref, gp_ref, wpg_ref, bpg_ref, wp_ref, gm_ref)


def _layer0_tail(x, o, p, wo, bo, gf, wg, wu, wd, gp, wpg, bpg, wp, gm, tm):
    t = x.shape[0]
    row = lambda i: (i, 0)
    consts = [wo, bo, gf, wg, wu, wd, gp, wpg, bpg, wp, gm]
    return pl.pallas_call(
        _layer0_tail_kernel,
        grid=(t // tm,),
        in_specs=[pl.BlockSpec((tm, D_MODEL), row), pl.BlockSpec((tm, D_MODEL), row),
                  pl.BlockSpec((1, tm, PLE_DIM), lambda i: (0, i, 0))] + [_const_spec(c.shape) for c in consts],
        out_specs=[pl.BlockSpec((tm, D_MODEL), row), pl.BlockSpec((tm, D_MODEL), row)],
        out_shape=[jax.ShapeDtypeStruct((t, D_MODEL), F32), jax.ShapeDtypeStruct((t, D_MODEL), F32)],
        compiler_params=_cparams(("parallel",)),
        name="layer0_tail",
    )(x, o, p, *consts)


def _segment_ranks(sel, tile, lr_out, lrt_out, cnt_out):
    r = lax.broadcasted_iota(jnp.int32, (tile, tile), 0)
    c = lax.broadcasted_iota(jnp.int32, (tile, tile), 1)
    earlier = (c < r).astype(BF16)
    pad = jnp.full((tile, LANES - N_EXPERTS), -1.0, F32)
    for k in range(sel.shape[0] // tile):
        rows = slice(k * tile, (k + 1) * tile)
        sel_k = sel[rows]
        rank = _mm(earlier, sel_k.astype(BF16))
        lr = jnp.where(sel_k, rank, -1.0)
        lr_out[rows, :] = lr.astype(jnp.int32)
        lrt_out[:, rows] = jnp.concatenate([lr, pad], axis=1).T[:N_EXPERTS].astype(jnp.int32)
        cnt_out[k] = jnp.sum(sel_k.astype(F32), axis=0, keepdims=True).astype(jnp.int32)


def _pool_project_route(h, hn, pooled_sum_inv, wpool_ref, ps_ref, gf_ref, wr_ref, br_ref,
                        h_out, hn_out, gates_out, lr_out, lrt_out, cnt_out, tile):
    mixed = []
    for g in range(len(POOL_WINDOWS)):
        cols = slice(g * POOL_GROUP_DIM, (g + 1) * POOL_GROUP_DIM)
        pooled = pooled_sum_inv[g] - hn[:, cols]
        mixed.append(_mm(pooled.astype(BF16), wpool_ref[g]))
    h = h + jnp.concatenate(mixed, axis=-1) * ps_ref[...]
    h_out[...] = h
    hn2 = _rms(h, gf_ref[...])
    hi = hn2.astype(BF16)
    hn_out[...] = hi
    lo = (hn2 - hi.astype(F32)).astype(BF16)
    by_hi = _mm(hi, wr_ref[...])
    logits = (by_hi[:, :N_EXPERTS] + by_hi[:, N_EXPERTS:] + _mm(lo, wr_ref[:, :N_EXPERTS])) + br_ref[...]
    idx = lax.broadcasted_iota(jnp.int32, logits.shape, 1)
    m1 = jnp.max(logits, axis=-1, keepdims=True)
    i1 = jnp.min(jnp.where(logits == m1, idx, N_EXPERTS), axis=-1, keepdims=True)
    rest = jnp.where(idx == i1, -jnp.inf, logits)
    m2 = jnp.max(rest, axis=-1, keepdims=True)
    i2 = jnp.min(jnp.where(rest == m2, idx, N_EXPERTS), axis=-1, keepdims=True)
    e = jnp.exp(m2 - m1)
    w1 = 1.0 / (1.0 + e)
    w2 = e / (1.0 + e)
    gates_out[...] = jnp.where(idx == i1, w1, jnp.where(idx == i2, w2, 0.0))
    _segment_ranks((idx == i1) | (idx == i2), tile, lr_out, lrt_out, cnt_out)


def _route_out_specs(t, tm, tile, blk):
    specs = [pl.BlockSpec((tm, N_EXPERTS), lambda *g: (blk(*g), 0)),
             pl.BlockSpec((tm, N_EXPERTS), lambda *g: (blk(*g), 0)),
             pl.BlockSpec((N_EXPERTS, tm), lambda *g: (0, blk(*g))),
             pl.BlockSpec((tm // tile, 1, N_EXPERTS), lambda *g: (blk(*g), 0, 0))]
    shapes = [jax.ShapeDtypeStruct((t, N_EXPERTS), F32),
              jax.ShapeDtypeStruct((t, N_EXPERTS), jnp.int32),
              jax.ShapeDtypeStruct((N_EXPERTS, t), jnp.int32),
              jax.ShapeDtypeStruct((t // tile, 1, N_EXPERTS), jnp.int32)]
    return specs, shapes


def _pool_prompt_kernel(h_ref, hn_ref, wpool_ref, ps_ref, gf_ref, wr_ref, br_ref,
                        h_out, hn_out, gates_out, lr_out, lrt_out, cnt_out, last_out, carry, buf, *, tile):
    b = pl.program_id(0)
    n = pl.program_id(1)
    tm = h_ref.shape[0]
    n_seq = pl.num_programs(0) - 1

    @pl.when(b < n_seq)
    def _():
        @pl.when(n == 0)
        def _():
            carry[...] = jnp.zeros_like(carry)

        hn = hn_ref[...]
        buf[0:POOL_HALO] = carry[...]
        buf[POOL_HALO:] = hn
        carry[...] = hn[tm - POOL_HALO:]
        last_out[0] = hn[tm - POOL_HALO:]
        pos = n * tm + lax.broadcasted_iota(jnp.int32, (tm, 1), 0)
        means = []
        for g, w in enumerate(POOL_WINDOWS):
            s = buf[:, g * POOL_GROUP_DIM:(g + 1) * POOL_GROUP_DIM]
            shift = 1
            while shift < w:
                s = s + pltpu.roll(s, shift, 0)
                shift *= 2
            cnt = jnp.minimum(w, pos + 1).astype(F32)
            means.append(s[POOL_HALO:] * (1.0 / cnt))
        _pool_project_route(h_ref[...], hn, means, wpool_ref, ps_ref, gf_ref, wr_ref, br_ref,
                            h_out, hn_out, gates_out, lr_out, lrt_out, cnt_out, tile)

    @pl.when(b == n_seq)
    def _():
        h_out[...] = jnp.zeros_like(h_out)
        hn_out[...] = jnp.zeros_like(hn_out)
        gates_out[...] = jnp.zeros_like(gates_out)
        lr_out[...] = jnp.full(lr_out.shape, -1, jnp.int32)
        lrt_out[...] = jnp.full(lrt_out.shape, -1, jnp.int32)
        cnt_out[...] = jnp.zeros_like(cnt_out)


def _pool_prompt(h, hn, wpool, ps, gf, wr, br, batch, seq, tm, tile, t_pad):
    ns = seq // tm
    pad_blk = batch * ns
    blk = lambda b, n: jnp.where(b < batch, b * ns + n, pad_blk)
    src = lambda b, n: (jnp.minimum(b * ns + n, pad_blk - 1), 0)
    consts = [wpool, ps, gf, wr, br]
    route_specs, route_shapes = _route_out_specs(t_pad, tm, tile, blk)
    row = lambda b, n: (blk(b, n), 0)
    return pl.pallas_call(
        functools.partial(_pool_prompt_kernel, tile=tile),
        grid=(batch + 1, ns),
        in_specs=[pl.BlockSpec((tm, D_MODEL), src), pl.BlockSpec((tm, D_MODEL), src)]
        + [_const_spec(c.shape) for c in consts],
        out_specs=[pl.BlockSpec((tm, D_MODEL), row), pl.BlockSpec((tm, D_MODEL), row)] + route_specs
        + [pl.BlockSpec((1, POOL_HALO, D_MODEL), lambda b, n: (jnp.minimum(b, batch - 1), 0, 0))],
        out_shape=[jax.ShapeDtypeStruct((t_pad, D_MODEL), F32), jax.ShapeDtypeStruct((t_pad, D_MODEL), BF16)]
        + route_shapes + [jax.ShapeDtypeStruct((batch, POOL_HALO, D_MODEL), F32)],
        scratch_shapes=[pltpu.VMEM((POOL_HALO, D_MODEL), F32),
                        pltpu.VMEM((tm + POOL_HALO, D_MODEL), F32)],
        compiler_params=_cparams(("arbitrary", "arbitrary")),
        name="pool_prompt",
    )(h, hn, *consts)


def _pool_sample_kernel(h_ref, hn_ref, hist_ref, wpool_ref, ps_ref, gf_ref, wr_ref, br_ref, *rest):
    outs = rest[len(rest) // 2:]
    hn = hn_ref[...]
    means = []
    for g, w in enumerate(POOL_WINDOWS):
        cols = slice(g * POOL_GROUP_DIM, (g + 1) * POOL_GROUP_DIM)
        s = hn[:, cols]
        for j in range(1, w):
            s = s + hist_ref[POOL_HIST - j, :, cols]
        means.append(s * (1.0 / min(w, PAST_LEN + 1)))
    _pool_project_route(h_ref[...], hn, means, wpool_ref, ps_ref, gf_ref, wr_ref, br_ref,
                        *outs, hn.shape[0])


def _pool_sample(h, hn, hist_t, wpool, ps, gf, wr, br, padded, row0, tile):
    t = h.shape[0]
    args = [h, hn, hist_t, wpool, ps, gf, wr, br]
    blk = row0 // t
    out_specs = [pl.BlockSpec((t, D_MODEL), lambda i: (blk, 0)), pl.BlockSpec((t, D_MODEL), lambda i: (blk, 0)),
                 pl.BlockSpec((t, N_EXPERTS), lambda i: (blk, 0)), pl.BlockSpec((t, N_EXPERTS), lambda i: (blk, 0)),
                 pl.BlockSpec((N_EXPERTS, t), lambda i: (0, blk)),
                 pl.BlockSpec((1, 1, N_EXPERTS), lambda i: (row0 // tile, 0, 0))]
    return pl.pallas_call(
        _pool_sample_kernel,
        grid=(1,),
        in_specs=[_const_spec(a.shape) for a in args] + [pl.BlockSpec(memory_space=pl.ANY)] * len(padded),
        out_specs=out_specs,
        out_shape=[jax.ShapeDtypeStruct(a.shape, a.dtype) for a in padded],
        input_output_aliases={len(args) + k: k for k in range(len(padded))},
        compiler_params=_cparams(("arbitrary",)),
        name="pool_sample",
    )(*args, *padded)


def _segment_rows(tile):
    return tile + ROW_ALIGN


def _segment_copies(meta_ref, step, tile, vmem_of, hbm, sem_of, to_hbm):
    half = tile // 2
    out = []
    for k, (start, size) in enumerate(((0, half), (half, _segment_rows(tile) - half))):
        for e in range(N_EXPERTS):
            pos = meta_ref[0, step * N_EXPERTS + e]
            pred = None if k == 0 else meta_ref[1, step * N_EXPERTS + e] > half
            v = vmem_of(e).at[pl.ds(start, size)]
            h = hbm.at[pl.ds(pl.multiple_of(pos + start, ROW_ALIGN), size)]
            copy = pltpu.make_async_copy(v, h, sem_of(e)) if to_hbm else pltpu.make_async_copy(h, v, sem_of(e))
            out.append((pred, copy))
    return out


def _segment_row(rank, shared):
    return jnp.where(rank >= 0, rank + shared, -1)


def _for_slot(slot, fn):
    for s in range(2):
        @pl.when(slot == s)
        def _(s=s):
            fn(s)


def _start_all(copies):
    for pred, copy in copies:
        if pred is None:
            copy.start()
        else:
            pl.when(pred)(copy.start)


def _wait_all(copies):
    for pred, copy in copies:
        if pred is None:
            copy.wait()
        else:
            pl.when(pred)(copy.wait)


def _dispatch_kernel(meta_ref, tail_ref, hn_ref, lrt_ref, xs_ref, seg, zeros, sem, zsem, *, slack_chunk):
    i = pl.program_id(0)
    last = pl.num_programs(0) - 1
    tile = hn_ref.shape[0]
    slot = i % 2
    copies = lambda step, s: _segment_copies(meta_ref, step, tile, lambda e: seg.at[s, e], xs_ref,
                                             lambda e: sem.at[s, e], True)
    hn = hn_ref[...]
    half = tile // 2
    rest = _segment_rows(tile) - half
    shared = [meta_ref[2, i * N_EXPERTS + e] for e in range(N_EXPERTS)]
    row_of = lambda e: _segment_row(lrt_ref[e:e + 1, :], shared[e])
    place = lax.broadcasted_iota(jnp.int32, (half, tile), 0)
    for e0 in range(0, N_EXPERTS, 4):
        first = jnp.concatenate([(place == row_of(e)).astype(BF16) for e in range(e0, e0 + 4)], axis=0)
        rows = _mm(first, hn).astype(BF16)
        for k in range(4):
            seg[slot, e0 + k, 0:half] = rows[k * half:(k + 1) * half]
    place_rest = lax.broadcasted_iota(jnp.int32, (rest, tile), 0) + half
    for e in range(N_EXPERTS):
        @pl.when(meta_ref[1, i * N_EXPERTS + e] > half)
        def _(e=e):
            seg[slot, e, half:] = _mm((place_rest == row_of(e)).astype(BF16), hn_ref[...]).astype(BF16)

        @pl.when(shared[e] > 0)
        def _(e=e):
            prev_len = meta_ref[1, jnp.maximum(i - 1, 0) * N_EXPERTS + e]
            block = pl.ds(pl.multiple_of(prev_len - shared[e], ROW_ALIGN), ROW_ALIGN)
            seg[slot, e, 0:ROW_ALIGN] = seg[slot, e, 0:ROW_ALIGN] + seg[1 - slot, e, block]

    @pl.when(i > 0)
    def _():
        _for_slot(1 - slot, lambda s: _wait_all(copies(i - 1, s)))

    _for_slot(slot, lambda s: _start_all(copies(i, s)))

    @pl.when(i == last)
    def _():
        _for_slot(slot, lambda s: _wait_all(copies(i, s)))
        zeros[...] = jnp.zeros_like(zeros)
        tails = []
        for e in range(N_EXPERTS):
            start = tail_ref[0, e]
            cnt = tail_ref[1, e]
            for piece in TAIL_PIECES:
                off = pl.multiple_of(cnt & ~(2 * piece - 1), ROW_ALIGN)
                copy = pltpu.make_async_copy(
                    zeros.at[pl.ds(0, piece)],
                    xs_ref.at[pl.ds(pl.multiple_of(start + off, ROW_ALIGN), piece)],
                    zsem.at[e])
                tails.append(((cnt & piece) != 0, copy))
        _start_all(tails)
        _wait_all(tails)

        def slack_copy(c):
            row = pl.multiple_of(tail_ref[0, N_EXPERTS] + c * slack_chunk, ROW_ALIGN)
            return pltpu.make_async_copy(zeros.at[pl.ds(0, slack_chunk)],
                                         xs_ref.at[pl.ds(row, slack_chunk)], zsem.at[0])

        n_slack = tail_ref[1, N_EXPERTS] // slack_chunk
        lax.fori_loop(0, n_slack, lambda c, _: slack_copy(c).start(), None)
        lax.fori_loop(0, n_slack, lambda c, _: slack_copy(c).wait(), None)


def _dispatch(hn, lr_t, meta, tail, rows, tile, slack_chunk):
    t = hn.shape[0]
    return pl.pallas_call(
        functools.partial(_dispatch_kernel, slack_chunk=slack_chunk),
        grid_spec=pltpu.PrefetchScalarGridSpec(
            num_scalar_prefetch=2,
            grid=(t // tile,),
            in_specs=[pl.BlockSpec((tile, D_MODEL), lambda i, *_: (i, 0)),
                      pl.BlockSpec((N_EXPERTS, tile), lambda i, *_: (0, i))],
            out_specs=pl.BlockSpec(memory_space=pl.ANY),
            scratch_shapes=[pltpu.VMEM((2, N_EXPERTS, _segment_rows(tile), D_MODEL), BF16),
                            pltpu.VMEM((TAIL_PIECES[0], D_MODEL), BF16),
                            pltpu.SemaphoreType.DMA((2, N_EXPERTS)),
                            pltpu.SemaphoreType.DMA((N_EXPERTS,))]),
        out_shape=jax.ShapeDtypeStruct((rows, D_MODEL), BF16),
        compiler_params=_cparams(("arbitrary",)),
        name="expert_dispatch",
    )(meta, tail, hn, lr_t)


def _expert_kernel(te_ref, tr_ref, x_ref, wg_ref, wu_ref, wd_ref, y_ref, acc):
    r = pl.program_id(0)
    j = pl.program_id(1)

    @pl.when(j == 0)
    def _():
        acc[...] = jnp.zeros_like(acc)

    def ffn(m):
        x = x_ref[0:m]
        a = (_silu(_mm(x, wg_ref[0].astype(BF16))) * _mm(x, wu_ref[0].astype(BF16))).astype(BF16)
        acc[0:m] += _mm(a, wd_ref[0].astype(BF16))

    rows = tr_ref[r]
    step = x_ref.shape[0] // EXPERT_ROW_STEPS
    for k in range(1, EXPERT_ROW_STEPS + 1):
        @pl.when((rows > (k - 1) * step) & (rows <= k * step))
        def _(k=k):
            ffn(k * step)

    @pl.when(j == pl.num_programs(1) - 1)
    def _():
        y_ref[...] = acc[...].astype(BF16)


def _expert_ffn(xs, tile_e, tile_rows, wg, wu, wd, tmd):
    rows = xs.shape[0]
    nj = D_FF_EXPERT // EXPERT_FF_CHUNK
    live = lambda r, tr: tr[r] > 0
    jj = lambda r, j, tr: jnp.where(live(r, tr), j, nj - 1)
    return pl.pallas_call(
        _expert_kernel,
        grid_spec=pltpu.PrefetchScalarGridSpec(
            num_scalar_prefetch=2,
            grid=(rows // tmd, nj),
            in_specs=[
                pl.BlockSpec((tmd, D_MODEL), lambda r, j, te, tr: (jnp.where(live(r, tr), r, 0), 0)),
                pl.BlockSpec((1, D_MODEL, EXPERT_FF_CHUNK), lambda r, j, te, tr: (te[r], 0, jj(r, j, tr))),
                pl.BlockSpec((1, D_MODEL, EXPERT_FF_CHUNK), lambda r, j, te, tr: (te[r], 0, jj(r, j, tr))),
                pl.BlockSpec((1, EXPERT_FF_CHUNK, D_MODEL), lambda r, j, te, tr: (te[r], jj(r, j, tr), 0)),
            ],
            out_specs=pl.BlockSpec((tmd, D_MODEL), lambda r, j, te, tr: (r, 0)),
            scratch_shapes=[pltpu.VMEM((tmd, D_MODEL), F32)]),
        out_shape=jax.ShapeDtypeStruct((rows, D_MODEL), BF16),
        compiler_params=_cparams(("parallel", "arbitrary")),
        name="expert_ffn",
    )(tile_e, tile_rows, xs, wg, wu, wd)


def _combine_kernel(meta_ref, h_ref, gates_ref, lr_ref, pp_ref, ps_ref, gp_ref, wpg_ref, bpg_ref, wp_ref,
                    gfin_ref, ys_ref, outp_ref, outs_ref, ybuf, f_ref, sem, *, n_prompt_tiles):
    i = pl.program_id(0)
    n = pl.num_programs(0)
    tile = h_ref.shape[0]
    slot = i % 2
    copies = lambda step, s: _segment_copies(meta_ref, step, tile, lambda e: ybuf.at[s, e], ys_ref,
                                             lambda e: sem.at[s, e], False)

    @pl.when(i == 0)
    def _():
        ybuf[...] = jnp.zeros_like(ybuf)
        _start_all(copies(0, 0))

    @pl.when(i + 1 < n)
    def _():
        _for_slot(1 - slot, lambda s: _start_all(copies(i + 1, s)))

    _for_slot(slot, lambda s: _wait_all(copies(i, s)))
    gates = gates_ref[...]
    lr = lr_ref[...]
    half = tile // 2

    def gather(start, size):
        place = lax.broadcasted_iota(jnp.int32, (tile, size), 1) + start
        for e in range(N_EXPERTS):
            row = _segment_row(lr[:, e:e + 1], meta_ref[2, i * N_EXPERTS + e])
            onehot = (place == row).astype(BF16)
            f_ref[...] += gates[:, e:e + 1] * _mm(onehot, ybuf[slot, e, start:start + size, :])

    f_ref[...] = h_ref[...]
    gather(0, half)
    longest = meta_ref[1, i * N_EXPERTS]
    for e in range(1, N_EXPERTS):
        longest = jnp.maximum(longest, meta_ref[1, i * N_EXPERTS + e])

    @pl.when(longest > half)
    def _():
        gather(half, _segment_rows(tile) - half)

    h = f_ref[...]

    def head(h, p):
        h = _ple(h, p, gp_ref[...], wpg_ref[...], bpg_ref[...], wp_ref[...])
        return _rms(h, gfin_ref[...])

    @pl.when(i < n_prompt_tiles)
    def _():
        outp_ref[...] = head(h, pp_ref[0])

    @pl.when(i == n_prompt_tiles)
    def _():
        n_s = outs_ref.shape[0]
        outs_ref[...] = head(h[:n_s], ps_ref[0])


def _combine(h, gates, lr, pp, ps, gp, wpg, bpg, wp, gfin, ys, meta, tile, n_prompt):
    t = h.shape[0]
    n_pt = n_prompt // tile
    n_s = ps.shape[1]
    row = lambda i, *_: (i, 0)
    prow = lambda i, *_: (jnp.minimum(i, n_pt - 1), 0)
    consts = [gp, wpg, bpg, wp, gfin]
    return pl.pallas_call(
        functools.partial(_combine_kernel, n_prompt_tiles=n_pt),
        grid_spec=pltpu.PrefetchScalarGridSpec(
            num_scalar_prefetch=1,
            grid=(t // tile,),
            in_specs=[pl.BlockSpec((tile, D_MODEL), row), pl.BlockSpec((tile, N_EXPERTS), row),
                      pl.BlockSpec((tile, N_EXPERTS), row),
                      pl.BlockSpec((1, tile, PLE_DIM), lambda i, *_: (1,) + prow(i)),
                      pl.BlockSpec((1, n_s, PLE_DIM), lambda i, *_: (1, 0, 0))]
            + [_const_spec(c.shape) for c in consts] + [pl.BlockSpec(memory_space=pl.ANY)],
            out_specs=[pl.BlockSpec((tile, D_MODEL), prow),
                       pl.BlockSpec((n_s, D_MODEL), lambda i, *_: (0, 0))],
            scratch_shapes=[pltpu.VMEM((2, N_EXPERTS, _segment_rows(tile), D_MODEL), BF16),
                            pltpu.VMEM((tile, D_MODEL), F32),
                            pltpu.SemaphoreType.DMA((2, N_EXPERTS))]),
        out_shape=[jax.ShapeDtypeStruct((n_prompt, D_MODEL), F32),
                   jax.ShapeDtypeStruct((n_s, D_MODEL), F32)],
        compiler_params=_cparams(("arbitrary",)),
        name="combine",
    )(meta, h, gates, lr, pp, ps, *consts, ys)


def _round_up(x, m):
    return (x + m - 1) // m * m


def _route_plan(cnt, t, tile, tmd):
    n_tiles = t // tile
    cnt = cnt.reshape(n_tiles, N_EXPERTS)
    before = jnp.cumsum(cnt, axis=0) - cnt
    shared = before % ROW_ALIGN
    seg_len = shared + cnt
    total = _round_up(cnt.sum(axis=0), ROW_ALIGN)
    guard = _segment_rows(tile) - tile // 2
    region = _round_up(total + guard, tmd)
    region_end = jnp.cumsum(region)
    off = region_end - region
    pos = off[None, :] + before - shared
    rows = _round_up(2 * t + N_EXPERTS * (ROW_ALIGN + tmd + guard), tmd)
    tile_start = jnp.arange(rows // tmd, dtype=jnp.int32) * tmd
    te = jnp.sum(region_end[None, :] <= tile_start[:, None], axis=1).astype(jnp.int32)
    tec = jnp.minimum(te, N_EXPERTS - 1)
    tile_rows = jnp.where(te < N_EXPERTS, jnp.clip(total[tec] - (tile_start - off[tec]), 0, tmd), 0)
    last_e = jnp.max(jnp.where(tile_rows > 0, tec, 0))
    tile_e = jnp.where(tile_rows > 0, tec, last_e).astype(jnp.int32)
    meta = jnp.stack([pos.reshape(-1), seg_len.reshape(-1), shared.reshape(-1)]).astype(jnp.int32)
    tail = jnp.stack([jnp.append(off + total, region_end[-1]),
                      jnp.append(region - total, rows - region_end[-1])]).astype(jnp.int32)
    return dict(meta=meta, tail=tail, tile_e=tile_e, tile_rows=tile_rows.astype(jnp.int32), rows=rows)


def _moe_and_head(h, hn, gates, lr, lr_t, cnt, pp, ps, w, tile, tmd, n_prompt):
    plan = _route_plan(cnt, h.shape[0], tile, tmd)
    xs = _dispatch(hn, lr_t, plan["meta"], plan["tail"], plan["rows"], tile, tmd // EXPERT_ROW_STEPS)
    ys = _expert_ffn(xs, plan["tile_e"], plan["tile_rows"], w["exp_gate"], w["exp_up"], w["exp_down"], tmd)
    return _combine(h, gates, lr, pp, ps, w["norm_ple1"], w["ple_gate1"], w["b_ple_gate1"], w["ple1"],
                    w["norm_final"], ys, plan["meta"], tile, n_prompt)


def _rope_tables(pos):
    half = HEAD_DIM // 2
    inv = 1.0 / (ROPE_THETA ** (jnp.arange(half, dtype=F32) / half))
    ang = pos.astype(F32)[:, None] * inv[None, :]
    cos = jnp.tile(jnp.cos(ang), (1, LANES // half))
    sin = jnp.sin(ang)
    sin = jnp.tile(jnp.concatenate([-sin, sin], axis=-1), (1, LANES // HEAD_DIM))
    return cos, sin


def kernel(x_prompt, x_sample, cache_k, cache_v, state_pool, p_prompt, p_sample, norm_mix, norm_ffn, norm_ple, norm_final, w_qkv, b_qkv, w_o, b_o, sinks, w_pool, pool_scale, w_ff_gate, w_ff_up, w_ff_down, w_router, b_router, w_exp_gate, w_exp_up, w_exp_down, w_ple, w_ple_gate, b_ple_gate):
    batch, seq, _ = x_prompt.shape
    dec = x_sample.shape[0]
    row2 = lambda a: a.reshape(1, -1)
    w = dict(
        exp_gate=w_exp_gate[0], exp_up=w_exp_up[0], exp_down=w_exp_down[0],
        norm_ple1=row2(norm_ple[1]), ple_gate1=w_ple_gate[1].astype(BF16), b_ple_gate1=row2(b_ple_gate[1]),
        ple1=w_ple[1].astype(BF16), norm_final=row2(norm_final))
    wqkv = w_qkv[0].astype(BF16)
    l0 = [w_o[0].astype(BF16), row2(b_o[0]), row2(norm_ffn[0]), w_ff_gate[0].astype(BF16),
          w_ff_up[0].astype(BF16), w_ff_down[0].astype(BF16), row2(norm_ple[0]),
          w_ple_gate[0].astype(BF16), row2(b_ple_gate[0]), w_ple[0].astype(BF16), row2(norm_mix[1])]
    wr_hi = w_router[0].astype(BF16)
    wr_lo = (w_router[0] - wr_hi.astype(F32)).astype(BF16)
    pool_w = [w_pool[0].astype(BF16), row2(pool_scale[0]), row2(norm_ffn[1]),
              jnp.concatenate([wr_hi, wr_lo], axis=1), row2(b_router[0])]

    tm, tile = 512, 512
    n_prompt = batch * seq
    t_pad = n_prompt + 2 * tm

    xp = x_prompt.reshape(n_prompt, D_MODEL)
    pp = p_prompt.reshape(p_prompt.shape[0], n_prompt, PLE_DIM)
    cos_p, sin_p = _rope_tables(jnp.arange(seq, dtype=jnp.int32))
    q, k, v = _qkv_rope(xp, row2(norm_mix[0]), wqkv, row2(b_qkv[0]), cos_p, sin_p, tm=2 * tm)
    o = _prompt_attention(q, k, v, sinks[0], batch, seq)
    h1, hn1 = _layer0_tail(xp, o, pp, *l0, tm=tm)
    *padded, last = _pool_prompt(h1, hn1, *pool_w, batch=batch, seq=seq, tm=2 * tm, tile=tile, t_pad=t_pad)
    kv_shape = (1, batch, WINDOW, N_KV_HEADS, HEAD_DIM)
    new_k_prompt = k.reshape(batch, seq, KV_DIM)[:, -WINDOW:].reshape(kv_shape)
    new_v_prompt = v.reshape(batch, seq, KV_DIM)[:, -WINDOW:].reshape(kv_shape)
    new_pool_prompt = last[:, -POOL_HIST:][None]

    xs_ = x_sample.reshape(dec, D_MODEL)
    ps = p_sample.reshape(p_sample.shape[0], dec, PLE_DIM)
    cos_s, sin_s = _rope_tables(jnp.full((dec,), PAST_LEN, jnp.int32))
    qs, ks, vs = _qkv_rope(xs_, row2(norm_mix[0]), wqkv, row2(b_qkv[0]), cos_s, sin_s, tm=dec)
    n_hist = cache_k.shape[2]
    k_win = jnp.concatenate([cache_k[0].reshape(dec, n_hist, KV_DIM), ks[:, None]], axis=1)[:, -n_hist:]
    v_win = jnp.concatenate([cache_v[0].reshape(dec, n_hist, KV_DIM), vs[:, None]], axis=1)[:, -n_hist:]
    head_kv = jnp.arange(N_HEADS) // GROUP
    blk = (head_kv[:, None] == jnp.arange(N_KV_HEADS)[None, :]).astype(BF16)
    q_blk = (qs.reshape(dec, N_HEADS, 1, HEAD_DIM) * blk[None, :, :, None]).reshape(dec, N_HEADS, KV_DIM)
    o_blk = _sample_attention(q_blk, k_win, v_win, sinks[0])
    o_s = jnp.take_along_axis(o_blk.reshape(dec, N_HEADS, N_KV_HEADS, HEAD_DIM),
                              head_kv[None, :, None, None], axis=2).reshape(dec, N_HEADS * HEAD_DIM)
    h1s, hn1s = _layer0_tail(xs_, o_s.astype(BF16), ps, *l0, tm=dec)
    hist_t = jnp.swapaxes(state_pool[0], 0, 1)
    padded = _pool_sample(h1s, hn1s, hist_t, *pool_w, padded=padded, row0=n_prompt, tile=tile)
    new_k_sample = k_win.reshape(1, dec, n_hist, N_KV_HEADS, HEAD_DIM)
    new_v_sample = v_win.reshape(1, dec, n_hist, N_KV_HEADS, HEAD_DIM)
    new_pool_sample = jnp.concatenate([state_pool[0], hn1s[:, None]], axis=1)[:, -POOL_HIST:][None]

    expected = TOP_K * (n_prompt + dec) / N_EXPERTS
    tiles_per_expert = max(1, round(expected / EXPERT_TILE_TARGET))
    tmd = _round_up(int(expected * (1 + EXPERT_TILE_SLACK) / tiles_per_expert), EXPERT_ROW_STEPS * ROW_ALIGN)
    y_prompt, y_sample = _moe_and_head(*padded, pp, ps, w, tile=tile, tmd=tmd, n_prompt=n_prompt)

    return (y_prompt.reshape(batch, seq, D_MODEL), y_sample.reshape(dec, 1, D_MODEL),
            new_k_prompt, new_v_prompt, new_pool_prompt, new_k_sample, new_v_sample, new_pool_sample)
```

```python
import functools

import jax
import jax.numpy as jnp
from jax import lax
from jax.experimental import pallas as pl
from jax.experimental.pallas import tpu as pltpu

F32 = jnp.float32
BF16 = jnp.bfloat16

D_MODEL = 1024
HEAD_DIM = 64
N_HEADS = 16
N_KV_HEADS = 4
GROUP = N_HEADS // N_KV_HEADS
KV_DIM = N_KV_HEADS * HEAD_DIM
QKV_DIM = (N_HEADS + 2 * N_KV_HEADS) * HEAD_DIM
WINDOW = 128
ROPE_THETA = 10000.0
PAST_LEN = 16384
POOL_WINDOWS = (2, 4, 8, 16)
POOL_GROUP_DIM = D_MODEL // len(POOL_WINDOWS)
POOL_HIST = max(POOL_WINDOWS) - 1
POOL_HALO = 16
D_FF = 2816
N_EXPERTS = 8
D_FF_EXPERT = 3584
PLE_DIM = 256
EPS = 1e-6

LANES = 128
VMEM_LIMIT = 56 * 1024 * 1024

ATTN_TQ = 1024
ATTN_STAGE = 2
MXU_DIM = 256
FF_CHUNK_EDGES = (0, 6 * MXU_DIM, D_FF)
EXPERT_FF_CHUNK = 512
EXPERT_ROW_STEPS = 8
EXPERT_TILE_TARGET = 2048
EXPERT_TILE_SLACK = 0.05
TOP_K = 2
ROW_ALIGN = 16
TAIL_PIECES = (2048, 1024, 512, 256, 128, 64, 32, 16)


def _cparams(sem):
    return pltpu.CompilerParams(dimension_semantics=sem, vmem_limit_bytes=VMEM_LIMIT)


def _const_spec(shape):
    nd = len(shape)
    return pl.BlockSpec(shape, lambda *_: (0,) * nd, pipeline_mode=pl.Buffered(1))


def _rms(x, g):
    return x * lax.rsqrt(jnp.mean(x * x, axis=-1, keepdims=True) + EPS) * g


def _mm(a, b):
    return jnp.dot(a, b, preferred_element_type=F32)


def _silu(x):
    return x * jax.nn.sigmoid(x)


def _qkv_kernel(x_ref, g_ref, w_ref, b_ref, cos_ref, sin_ref, q_ref, k_ref, v_ref):
    hn = _rms(x_ref[...], g_ref[...])
    qkv = _mm(hn.astype(BF16), w_ref[...]) + b_ref[...]
    cos = cos_ref[...]
    sin = sin_ref[...]
    lane = lax.broadcasted_iota(jnp.int32, cos.shape, 1)
    first_half = (lane % HEAD_DIM) < (HEAD_DIM // 2)

    def rope(xb):
        partner = jnp.where(first_half, pltpu.roll(xb, LANES - HEAD_DIM // 2, 1),
                            pltpu.roll(xb, HEAD_DIM // 2, 1))
        return xb * cos + partner * sin

    scale = HEAD_DIM ** -0.5
    for c in range(N_HEADS * HEAD_DIM // LANES):
        sl = slice(c * LANES, (c + 1) * LANES)
        q_ref[:, sl] = (rope(qkv[:, sl]) * scale).astype(BF16)
    for c in range(KV_DIM // LANES):
        sl = slice(c * LANES, (c + 1) * LANES)
        k_ref[:, sl] = rope(qkv[:, N_HEADS * HEAD_DIM + c * LANES:N_HEADS * HEAD_DIM + (c + 1) * LANES])
    v_ref[...] = qkv[:, N_HEADS * HEAD_DIM + KV_DIM:]


def _qkv_rope(x, g, w, b, cos, sin, tm):
    t = x.shape[0]
    n_pos_tiles = cos.shape[0] // tm
    row = lambda i: (i, 0)
    return pl.pallas_call(
        _qkv_kernel,
        grid=(t // tm,),
        in_specs=[
            pl.BlockSpec((tm, D_MODEL), row),
            _const_spec((1, D_MODEL)),
            _const_spec((D_MODEL, QKV_DIM)),
            _const_spec((1, QKV_DIM)),
            pl.BlockSpec((tm, LANES), lambda i: (i % n_pos_tiles, 0)),
            pl.BlockSpec((tm, LANES), lambda i: (i % n_pos_tiles, 0)),
        ],
        out_specs=[
            pl.BlockSpec((tm, N_HEADS * HEAD_DIM), row),
            pl.BlockSpec((tm, KV_DIM), row),
            pl.BlockSpec((tm, KV_DIM), row),
        ],
        out_shape=[
            jax.ShapeDtypeStruct((t, N_HEADS * HEAD_DIM), BF16),
            jax.ShapeDtypeStruct((t, KV_DIM), F32),
            jax.ShapeDtypeStruct((t, KV_DIM), F32),
        ],
        compiler_params=_cparams(("parallel",)),
        name="qkv_rope",
    )(x, g, w, b, cos, sin)


V_AUG = 4 * HEAD_DIM


def _attn_kernel(sink_ref, q_ref, kc_ref, kp_ref, vc_ref, vp_ref, o_ref, kbuf, vbuf, bias):
    n = pl.program_id(1)
    kbuf[0:WINDOW] = kp_ref[...].astype(BF16)
    kbuf[WINDOW:] = kc_ref[...].astype(BF16)
    v_all = jnp.concatenate([vp_ref[...], vc_ref[...]], axis=0).astype(BF16)
    pad0 = jnp.zeros((v_all.shape[0], HEAD_DIM), BF16)
    pad1 = jnp.ones((v_all.shape[0], 2 * HEAD_DIM), BF16)
    for kv in range(N_KV_HEADS):
        vbuf[:, kv * V_AUG:(kv + 1) * V_AUG] = jnp.concatenate(
            [v_all[:, kv * HEAD_DIM:(kv + 1) * HEAD_DIM], pad0, pad1], axis=1)
    shape = (GROUP * WINDOW, 2 * WINDOW)
    qrow = lax.broadcasted_iota(jnp.int32, shape, 0) & (WINDOW - 1)
    col = lax.broadcasted_iota(jnp.int32, shape, 1)
    mask_cur = (col >= WINDOW) & (col - WINDOW <= qrow)
    mask_prev = (col < WINDOW) & (col > qrow)
    bias[0] = jnp.where(mask_cur | mask_prev, 0.0, -jnp.inf)
    bias[1] = jnp.where(mask_cur, 0.0, -jnp.inf)
    rows = lambda sb: slice(sb * WINDOW, (sb + 1) * WINDOW)
    window = lambda sb: slice(sb * WINDOW, (sb + 2) * WINDOW)
    for first in range(0, ATTN_TQ // WINDOW, ATTN_STAGE):
        blocks = range(first, first + ATTN_STAGE)
        scores, probs, sink_term, outs = {}, {}, {}, {}
        for sb in blocks:
            for kv in range(N_KV_HEADS):
                qg = jnp.concatenate([q_ref[rows(sb), h * HEAD_DIM:(h + 1) * HEAD_DIM]
                                      for h in range(kv * GROUP, (kv + 1) * GROUP)], axis=0)
                scores[sb, kv] = lax.dot_general(qg, kbuf[window(sb), kv * HEAD_DIM:(kv + 1) * HEAD_DIM],
                                                 (((1,), (1,)), ((), ())), preferred_element_type=F32)
        for sb in blocks:
            bias_sb = bias[jnp.where(n == 0, 1, 0)] if sb == 0 else bias[0]
            for kv in range(N_KV_HEADS):
                s = scores[sb, kv] + bias_sb
                p = []
                for g in range(GROUP):
                    h = kv * GROUP + g
                    sh = s[g * WINDOW:(g + 1) * WINDOW]
                    m = jnp.maximum(jnp.max(sh, axis=-1, keepdims=True), sink_ref[h])
                    p.append(jnp.exp(sh - m).astype(BF16))
                    sink_term[sb, h] = jnp.exp(sink_ref[h] - m)
                probs[sb, kv] = jnp.concatenate(p, axis=0)
        for sb in blocks:
            for kv in range(N_KV_HEADS):
                outs[sb, kv] = _mm(probs[sb, kv], vbuf[window(sb), kv * V_AUG:(kv + 1) * V_AUG])
        for sb in blocks:
            for h in range(N_HEADS):
                oh = outs[sb, h // GROUP][(h % GROUP) * WINDOW:(h % GROUP + 1) * WINDOW]
                o = oh[:, :2 * HEAD_DIM] / (oh[:, 2 * HEAD_DIM:] + sink_term[sb, h])
                o_ref[rows(sb), h * HEAD_DIM:(h + 1) * HEAD_DIM] = o[:, :HEAD_DIM].astype(BF16)


def _prompt_attention(q, k, v, sinks, batch, seq):
    nq = seq // ATTN_TQ
    per = ATTN_TQ // WINDOW
    cur = lambda b, n: (b * nq + n, 0)
    prev = lambda b, n: (jnp.maximum(b * nq * per + n * per - 1, b * nq * per), 0)
    return pl.pallas_call(
        _attn_kernel,
        grid=(batch, nq),
        in_specs=[
            pl.BlockSpec(memory_space=pltpu.SMEM),
            pl.BlockSpec((ATTN_TQ, N_HEADS * HEAD_DIM), cur),
            pl.BlockSpec((ATTN_TQ, KV_DIM), cur),
            pl.BlockSpec((WINDOW, KV_DIM), prev),
            pl.BlockSpec((ATTN_TQ, KV_DIM), cur),
            pl.BlockSpec((WINDOW, KV_DIM), prev),
        ],
        out_specs=pl.BlockSpec((ATTN_TQ, N_HEADS * HEAD_DIM), cur),
        out_shape=jax.ShapeDtypeStruct(q.shape, BF16),
        scratch_shapes=[pltpu.VMEM((ATTN_TQ + WINDOW, KV_DIM), BF16),
                        pltpu.VMEM((ATTN_TQ + WINDOW, N_KV_HEADS * V_AUG), BF16),
                        pltpu.VMEM((2, GROUP * WINDOW, 2 * WINDOW), F32)],
        compiler_params=_cparams(("parallel", "parallel")),
        name="prompt_attention",
    )(sinks, q, k, k, v, v)


def _attn_sample_kernel(sink_ref, q_ref, k_ref, v_ref, o_ref):
    s = jnp.einsum("bhc,blc->bhl", q_ref[...], k_ref[...].astype(BF16),
                   preferred_element_type=F32)
    sink = sink_ref[...]
    m = jnp.maximum(jnp.max(s, axis=-1, keepdims=True), sink)
    p = jnp.exp(s - m)
    denom = jnp.sum(p, axis=-1, keepdims=True) + jnp.exp(sink - m)
    o_ref[...] = jnp.einsum("bhl,blc->bhc", (p / denom).astype(BF16), v_ref[...].astype(BF16),
                            preferred_element_type=F32)


def _sample_attention(q_blk, k_win, v_win, sinks, bt=16):
    b = q_blk.shape[0]
    blk = lambda i: (i, 0, 0)
    return pl.pallas_call(
        _attn_sample_kernel,
        grid=(b // bt,),
        in_specs=[
            _const_spec((1, N_HEADS, 1)),
            pl.BlockSpec((bt, N_HEADS, KV_DIM), blk),
            pl.BlockSpec((bt, WINDOW, KV_DIM), blk),
            pl.BlockSpec((bt, WINDOW, KV_DIM), blk),
        ],
        out_specs=pl.BlockSpec((bt, N_HEADS, KV_DIM), blk),
        out_shape=jax.ShapeDtypeStruct((b, N_HEADS, KV_DIM), F32),
        compiler_params=_cparams(("parallel",)),
        name="sample_attention",
    )(sinks.reshape(1, N_HEADS, 1), q_blk, k_win, v_win)


def _ple_gate(h, g, wg, bg):
    return jax.nn.sigmoid(_mm(_rms(h, g).astype(BF16), wg) + bg)


def _ple(h, p, g, wg, bg, wp):
    return h + _ple_gate(h, g, wg, bg) * _mm(p.astype(BF16), wp)


def _layer0_tail_math(x_ref, o_ref, p, wo_ref, bo_ref, gf_ref, wg_ref, wu_ref, wd_ref,
                      gp_ref, wpg_ref, bpg_ref, wp_ref, gm_ref):
    h = x_ref[...] + _mm(o_ref[...], wo_ref[...]) + bo_ref[...]
    hn = _rms(h, gf_ref[...]).astype(BF16)
    f = None
    for c0, c1 in zip(FF_CHUNK_EDGES[:-1], FF_CHUNK_EDGES[1:]):
        a = (_silu(_mm(hn, wg_ref[:, c0:c1])) * _mm(hn, wu_ref[:, c0:c1])).astype(BF16)
        part = _mm(a, wd_ref[c0:c1, :])
        f = part if f is None else f + part
    h = h + f
    h = _ple(h, p, gp_ref[...], wpg_ref[...], bpg_ref[...], wp_ref[...])
    return h, _rms(h, gm_ref[...])


def _layer0_tail_kernel(x_ref, o_ref, p_ref, wo_ref, bo_ref, gf_ref, wg_ref, wu_ref, wd_ref,
                        gp_ref, wpg_ref, bpg_ref, wp_ref, gm_ref, h_ref, hn_ref):
    h_ref[...], hn_ref[...] = _layer0_tail_math(x_ref, o_ref, p_ref[0], wo_ref, bo_ref, gf_ref, wg_ref,
                                                wu_ref, wd_ref, gp_ref, wpg_ref, bpg_ref, wp_ref, gm_ref)


def _layer0_tail(x, o, p, wo, bo, gf, wg, wu, wd, gp, wpg, bpg, wp, gm, tm):
    t = x.shape[0]
    row = lambda i: (i, 0)
    consts = [wo, bo, gf, wg, wu, wd, gp, wpg, bpg, wp, gm]
    return pl.pallas_call(
        _layer0_tail_kernel,
        grid=(t // tm,),
        in_specs=[pl.BlockSpec((tm, D_MODEL), row), pl.BlockSpec((tm, D_MODEL), row),
                  pl.BlockSpec((1, tm, PLE_DIM), lambda i: (0, i, 0))] + [_const_spec(c.shape) for c in consts],
        out_specs=[pl.BlockSpec((tm, D_MODEL), row), pl.BlockSpec((tm, D_MODEL), row)],
        out_shape=[jax.ShapeDtypeStruct((t, D_MODEL), F32), jax.ShapeDtypeStruct((t, D_MODEL), F32)],
        compiler_params=_cparams(("parallel",)),
        name="layer0_tail",
    )(x, o, p, *consts)


def _segment_ranks(sel, tile, lr_out, lrt_out, cnt_out):
    r = lax.broadcasted_iota(jnp.int32, (tile, tile), 0)
    c = lax.broadcasted_iota(jnp.int32, (tile, tile), 1)
    earlier = (c < r).astype(BF16)
    pad = jnp.full((tile, LANES - N_EXPERTS), -1.0, F32)
    for k in range(sel.shape[0] // tile):
        rows = slice(k * tile, (k + 1) * tile)
        sel_k = sel[rows]
        rank = _mm(earlier, sel_k.astype(BF16))
        lr = jnp.where(sel_k, rank, -1.0)
        lr_out[rows, :] = lr.astype(jnp.int32)
        lrt_out[:, rows] = jnp.concatenate([lr, pad], axis=1).T[:N_EXPERTS].astype(jnp.int32)
        cnt_out[k] = jnp.sum(sel_k.astype(F32), axis=0, keepdims=True).astype(jnp.int32)


def _pool_project_route(h, hn, pooled_sum_inv, wpool_ref, ps_ref, gf_ref, wr_ref, br_ref,
                        h_out, hn_out, gates_out, lr_out, lrt_out, cnt_out, tile):
    mixed = []
    for g in range(len(POOL_WINDOWS)):
        cols = slice(g * POOL_GROUP_DIM, (g + 1) * POOL_GROUP_DIM)
        pooled = pooled_sum_inv[g] - hn[:, cols]
        mixed.append(_mm(pooled.astype(BF16), wpool_ref[g]))
    h = h + jnp.concatenate(mixed, axis=-1) * ps_ref[...]
    h_out[...] = h
    hn2 = _rms(h, gf_ref[...])
    hi = hn2.astype(BF16)
    hn_out[...] = hi
    lo = (hn2 - hi.astype(F32)).astype(BF16)
    by_hi = _mm(hi, wr_ref[...])
    logits = (by_hi[:, :N_EXPERTS] + by_hi[:, N_EXPERTS:] + _mm(lo, wr_ref[:, :N_EXPERTS])) + br_ref[...]
    idx = lax.broadcasted_iota(jnp.int32, logits.shape, 1)
    m1 = jnp.max(logits, axis=-1, keepdims=True)
    i1 = jnp.min(jnp.where(logits == m1, idx, N_EXPERTS), axis=-1, keepdims=True)
    rest = jnp.where(idx == i1, -jnp.inf, logits)
    m2 = jnp.max(rest, axis=-1, keepdims=True)
    i2 = jnp.min(jnp.where(rest == m2, idx, N_EXPERTS), axis=-1, keepdims=True)
    e = jnp.exp(m2 - m1)
    w1 = 1.0 / (1.0 + e)
    w2 = e / (1.0 + e)
    gates_out[...] = jnp.where(idx == i1, w1, jnp.where(idx == i2, w2, 0.0))
    _segment_ranks((idx == i1) | (idx == i2), tile, lr_out, lrt_out, cnt_out)


def _route_out_specs(t, tm, tile, blk):
    specs = [pl.BlockSpec((tm, N_EXPERTS), lambda *g: (blk(*g), 0)),
             pl.BlockSpec((tm, N_EXPERTS), lambda *g: (blk(*g), 0)),
             pl.BlockSpec((N_EXPERTS, tm), lambda *g: (0, blk(*g))),
             pl.BlockSpec((tm // tile, 1, N_EXPERTS), lambda *g: (blk(*g), 0, 0))]
    shapes = [jax.ShapeDtypeStruct((t, N_EXPERTS), F32),
              jax.ShapeDtypeStruct((t, N_EXPERTS), jnp.int32),
              jax.ShapeDtypeStruct((N_EXPERTS, t), jnp.int32),
              jax.ShapeDtypeStruct((t // tile, 1, N_EXPERTS), jnp.int32)]
    return specs, shapes


def _pool_prompt_kernel(h_ref, hn_ref, wpool_ref, ps_ref, gf_ref, wr_ref, br_ref,
                        h_out, hn_out, gates_out, lr_out, lrt_out, cnt_out, last_out, carry, buf, *, tile):
    b = pl.program_id(0)
    n = pl.program_id(1)
    tm = h_ref.shape[0]
    n_seq = pl.num_programs(0) - 1

    @pl.when(b < n_seq)
    def _():
        @pl.when(n == 0)
        def _():
            carry[...] = jnp.zeros_like(carry)

        hn = hn_ref[...]
        buf[0:POOL_HALO] = carry[...]
        buf[POOL_HALO:] = hn
        carry[...] = hn[tm - POOL_HALO:]
        last_out[0] = hn[tm - POOL_HALO:]
        pos = n * tm + lax.broadcasted_iota(jnp.int32, (tm, 1), 0)
        means = []
        for g, w in enumerate(POOL_WINDOWS):
            s = buf[:, g * POOL_GROUP_DIM:(g + 1) * POOL_GROUP_DIM]
            shift = 1
            while shift < w:
                s = s + pltpu.roll(s, shift, 0)
                shift *= 2
            cnt = jnp.minimum(w, pos + 1).astype(F32)
            means.append(s[POOL_HALO:] * (1.0 / cnt))
        _pool_project_route(h_ref[...], hn, means, wpool_ref, ps_ref, gf_ref, wr_ref, br_ref,
                            h_out, hn_out, gates_out, lr_out, lrt_out, cnt_out, tile)

    @pl.when(b == n_seq)
    def _():
        h_out[...] = jnp.zeros_like(h_out)
        hn_out[...] = jnp.zeros_like(hn_out)
        gates_out[...] = jnp.zeros_like(gates_out)
        lr_out[...] = jnp.full(lr_out.shape, -1, jnp.int32)
        lrt_out[...] = jnp.full(lrt_out.shape, -1, jnp.int32)
        cnt_out[...] = jnp.zeros_like(cnt_out)


def _pool_prompt(h, hn, wpool, ps, gf, wr, br, batch, seq, tm, tile, t_pad):
    ns = seq // tm
    pad_blk = batch * ns
    blk = lambda b, n: jnp.where(b < batch, b * ns + n, pad_blk)
    src = lambda b, n: (jnp.minimum(b * ns + n, pad_blk - 1), 0)
    consts = [wpool, ps, gf, wr, br]
    route_specs, route_shapes = _route_out_specs(t_pad, tm, tile, blk)
    row = lambda b, n: (blk(b, n), 0)
    return pl.pallas_call(
        functools.partial(_pool_prompt_kernel, tile=tile),
        grid=(batch + 1, ns),
        in_specs=[pl.BlockSpec((tm, D_MODEL), src), pl.BlockSpec((tm, D_MODEL), src)]
        + [_const_spec(c.shape) for c in consts],
        out_specs=[pl.BlockSpec((tm, D_MODEL), row), pl.BlockSpec((tm, D_MODEL), row)] + route_specs
        + [pl.BlockSpec((1, POOL_HALO, D_MODEL), lambda b, n: (jnp.minimum(b, batch - 1), 0, 0))],
        out_shape=[jax.ShapeDtypeStruct((t_pad, D_MODEL), F32), jax.ShapeDtypeStruct((t_pad, D_MODEL), BF16)]
        + route_shapes + [jax.ShapeDtypeStruct((batch, POOL_HALO, D_MODEL), F32)],
        scratch_shapes=[pltpu.VMEM((POOL_HALO, D_MODEL), F32),
                        pltpu.VMEM((tm + POOL_HALO, D_MODEL), F32)],
        compiler_params=_cparams(("arbitrary", "arbitrary")),
        name="pool_prompt",
    )(h, hn, *consts)


def _pool_sample_kernel(h_ref, hn_ref, hist_ref, wpool_ref, ps_ref, gf_ref, wr_ref, br_ref, *rest):
    outs = rest[len(rest) // 2:]
    hn = hn_ref[...]
    means = []
    for g, w in enumerate(POOL_WINDOWS):
        cols = slice(g * POOL_GROUP_DIM, (g + 1) * POOL_GROUP_DIM)
        s = hn[:, cols]
        for j in range(1, w):
            s = s + hist_ref[POOL_HIST - j, :, cols]
        means.append(s * (1.0 / min(w, PAST_LEN + 1)))
    _pool_project_route(h_ref[...], hn, means, wpool_ref, ps_ref, gf_ref, wr_ref, br_ref,
                        *outs, hn.shape[0])


def _pool_sample(h, hn, hist_t, wpool, ps, gf, wr, br, padded, row0, tile):
    t = h.shape[0]
    args = [h, hn, hist_t, wpool, ps, gf, wr, br]
    blk = row0 // t
    out_specs = [pl.BlockSpec((t, D_MODEL), lambda i: (blk, 0)), pl.BlockSpec((t, D_MODEL), lambda i: (blk, 0)),
                 pl.BlockSpec((t, N_EXPERTS), lambda i: (blk, 0)), pl.BlockSpec((t, N_EXPERTS), lambda i: (blk, 0)),
                 pl.BlockSpec((N_EXPERTS, t), lambda i: (0, blk)),
                 pl.BlockSpec((1, 1, N_EXPERTS), lambda i: (row0 // tile, 0, 0))]
    return pl.pallas_call(
        _pool_sample_kernel,
        grid=(1,),
        in_specs=[_const_spec(a.shape) for a in args] + [pl.BlockSpec(memory_space=pl.ANY)] * len(padded),
        out_specs=out_specs,
        out_shape=[jax.ShapeDtypeStruct(a.shape, a.dtype) for a in padded],
        input_output_aliases={len(args) + k: k for k in range(len(padded))},
        compiler_params=_cparams(("arbitrary",)),
        name="pool_sample",
    )(*args, *padded)


def _segment_rows(tile):
    return tile + ROW_ALIGN


def _segment_copies(meta_ref, step, tile, vmem_of, hbm, sem_of, to_hbm):
    half = tile // 2
    out = []
    for k, (start, size) in enumerate(((0, half), (half, _segment_rows(tile) - half))):
        for e in range(N_EXPERTS):
            pos = meta_ref[0, step * N_EXPERTS + e]
            pred = None if k == 0 else meta_ref[1, step * N_EXPERTS + e] > half
            v = vmem_of(e).at[pl.ds(start, size)]
            h = hbm.at[pl.ds(pl.multiple_of(pos + start, ROW_ALIGN), size)]
            copy = pltpu.make_async_copy(v, h, sem_of(e)) if to_hbm else pltpu.make_async_copy(h, v, sem_of(e))
            out.append((pred, copy))
    return out


def _segment_row(rank, shared):
    return jnp.where(rank >= 0, rank + shared, -1)


def _for_slot(slot, fn):
    for s in range(2):
        @pl.when(slot == s)
        def _(s=s):
            fn(s)


def _start_all(copies):
    for pred, copy in copies:
        if pred is None:
            copy.start()
        else:
            pl.when(pred)(copy.start)


def _wait_all(copies):
    for pred, copy in copies:
        if pred is None:
            copy.wait()
        else:
            pl.when(pred)(copy.wait)


def _dispatch_kernel(meta_ref, tail_ref, hn_ref, lrt_ref, xs_ref, seg, zeros, sem, zsem, *, slack_chunk):
    i = pl.program_id(0)
    last = pl.num_programs(0) - 1
    tile = hn_ref.shape[0]
    slot = i % 2
    copies = lambda step, s: _segment_copies(meta_ref, step, tile, lambda e: seg.at[s, e], xs_ref,
                                             lambda e: sem.at[s, e], True)
    hn = hn_ref[...]
    half = tile // 2
    rest = _segment_rows(tile) - half
    shared = [meta_ref[2, i * N_EXPERTS + e] for e in range(N_EXPERTS)]
    row_of = lambda e: _segment_row(lrt_ref[e:e + 1, :], shared[e])
    place = lax.broadcasted_iota(jnp.int32, (half, tile), 0)
    for e0 in range(0, N_EXPERTS, 4):
        first = jnp.concatenate([(place == row_of(e)).astype(BF16) for e in range(e0, e0 + 4)], axis=0)
        rows = _mm(first, hn).astype(BF16)
        for k in range(4):
            seg[slot, e0 + k, 0:half] = rows[k * half:(k + 1) * half]
    place_rest = lax.broadcasted_iota(jnp.int32, (rest, tile), 0) + half
    for e in range(N_EXPERTS):
        @pl.when(meta_ref[1, i * N_EXPERTS + e] > half)
        def _(e=e):
            seg[slot, e, half:] = _mm((place_rest == row_of(e)).astype(BF16), hn_ref[...]).astype(BF16)

        @pl.when(shared[e] > 0)
        def _(e=e):
            prev_len = meta_ref[1, jnp.maximum(i - 1, 0) * N_EXPERTS + e]
            block = pl.ds(pl.multiple_of(prev_len - shared[e], ROW_ALIGN), ROW_ALIGN)
            seg[slot, e, 0:ROW_ALIGN] = seg[slot, e, 0:ROW_ALIGN] + seg[1 - slot, e, block]

    @pl.when(i > 0)
    def _():
        _for_slot(1 - slot, lambda s: _wait_all(copies(i - 1, s)))

    _for_slot(slot, lambda s: _start_all(copies(i, s)))

    @pl.when(i == last)
    def _():
        _for_slot(slot, lambda s: _wait_all(copies(i, s)))
        zeros[...] = jnp.zeros_like(zeros)
        tails = []
        for e in range(N_EXPERTS):
            start = tail_ref[0, e]
            cnt = tail_ref[1, e]
            for piece in TAIL_PIECES:
                off = pl.multiple_of(cnt & ~(2 * piece - 1), ROW_ALIGN)
                copy = pltpu.make_async_copy(
                    zeros.at[pl.ds(0, piece)],
                    xs_ref.at[pl.ds(pl.multiple_of(start + off, ROW_ALIGN), piece)],
                    zsem.at[e])
                tails.append(((cnt & piece) != 0, copy))
        _start_all(tails)
        _wait_all(tails)

        def slack_copy(c):
            row = pl.multiple_of(tail_ref[0, N_EXPERTS] + c * slack_chunk, ROW_ALIGN)
            return pltpu.make_async_copy(zeros.at[pl.ds(0, slack_chunk)],
                                         xs_ref.at[pl.ds(row, slack_chunk)], zsem.at[0])

        n_slack = tail_ref[1, N_EXPERTS] // slack_chunk
        lax.fori_loop(0, n_slack, lambda c, _: slack_copy(c).start(), None)
        lax.fori_loop(0, n_slack, lambda c, _: slack_copy(c).wait(), None)


def _dispatch(hn, lr_t, meta, tail, rows, tile, slack_chunk):
    t = hn.shape[0]
    return pl.pallas_call(
        functools.partial(_dispatch_kernel, slack_chunk=slack_chunk),
        grid_spec=pltpu.PrefetchScalarGridSpec(
            num_scalar_prefetch=2,
            grid=(t // tile,),
            in_specs=[pl.BlockSpec((tile, D_MODEL), lambda i, *_: (i, 0)),
                      pl.BlockSpec((N_EXPERTS, tile), lambda i, *_: (0, i))],
            out_specs=pl.BlockSpec(memory_space=pl.ANY),
            scratch_shapes=[pltpu.VMEM((2, N_EXPERTS, _segment_rows(tile), D_MODEL), BF16),
                            pltpu.VMEM((TAIL_PIECES[0], D_MODEL), BF16),
                            pltpu.SemaphoreType.DMA((2, N_EXPERTS)),
                            pltpu.SemaphoreType.DMA((N_EXPERTS,))]),
        out_shape=jax.ShapeDtypeStruct((rows, D_MODEL), BF16),
        compiler_params=_cparams(("arbitrary",)),
        name="expert_dispatch",
    )(meta, tail, hn, lr_t)


def _expert_kernel(te_ref, tr_ref, x_ref, wg_ref, wu_ref, wd_ref, y_ref, acc):
    r = pl.program_id(0)
    j = pl.program_id(1)

    @pl.when(j == 0)
    def _():
        acc[...] = jnp.zeros_like(acc)

    def ffn(m):
        x = x_ref[0:m]
        a = (_silu(_mm(x, wg_ref[0].astype(BF16))) * _mm(x, wu_ref[0].astype(BF16))).astype(BF16)
        acc[0:m] += _mm(a, wd_ref[0].astype(BF16))

    rows = tr_ref[r]
    step = x_ref.shape[0] // EXPERT_ROW_STEPS
    for k in range(1, EXPERT_ROW_STEPS + 1):
        @pl.when((rows > (k - 1) * step) & (rows <= k * step))
        def _(k=k):
            ffn(k * step)

    @pl.when(j == pl.num_programs(1) - 1)
    def _():
        y_ref[...] = acc[...].astype(BF16)


def _expert_ffn(xs, tile_e, tile_rows, wg, wu, wd, tmd):
    rows = xs.shape[0]
    nj = D_FF_EXPERT // EXPERT_FF_CHUNK
    live = lambda r, tr: tr[r] > 0
    jj = lambda r, j, tr: jnp.where(live(r, tr), j, nj - 1)
    return pl.pallas_call(
        _expert_kernel,
        grid_spec=pltpu.PrefetchScalarGridSpec(
            num_scalar_prefetch=2,
            grid=(rows // tmd, nj),
            in_specs=[
                pl.BlockSpec((tmd, D_MODEL), lambda r, j, te, tr: (jnp.where(live(r, tr), r, 0), 0)),
                pl.BlockSpec((1, D_MODEL, EXPERT_FF_CHUNK), lambda r, j, te, tr: (te[r], 0, jj(r, j, tr))),
                pl.BlockSpec((1, D_MODEL, EXPERT_FF_CHUNK), lambda r, j, te, tr: (te[r], 0, jj(r, j, tr))),
                pl.BlockSpec((1, EXPERT_FF_CHUNK, D_MODEL), lambda r, j, te, tr: (te[r], jj(r, j, tr), 0)),
            ],
            out_specs=pl.BlockSpec((tmd, D_MODEL), lambda r, j, te, tr: (r, 0)),
            scratch_shapes=[pltpu.VMEM((tmd, D_MODEL), F32)]),
        out_shape=jax.ShapeDtypeStruct((rows, D_MODEL), BF16),
        compiler_params=_cparams(("parallel", "arbitrary")),
        name="expert_ffn",
    )(tile_e, tile_rows, xs, wg, wu, wd)


def _combine_kernel(meta_ref, h_ref, gates_ref, lr_ref, pp_ref, ps_ref, gp_ref, wpg_ref, bpg_ref, wp_ref,
                    gfin_ref, ys_ref, outp_ref, outs_ref, ybuf, f_ref, sem, *, n_prompt_tiles):
    i = pl.program_id(0)
    n = pl.num_programs(0)
    tile = h_ref.shape[0]
    slot = i % 2
    copies = lambda step, s: _segment_copies(meta_ref, step, tile, lambda e: ybuf.at[s, e], ys_ref,
                                             lambda e: sem.at[s, e], False)

    @pl.when(i == 0)
    def _():
        ybuf[...] = jnp.zeros_like(ybuf)
        _start_all(copies(0, 0))

    @pl.when(i + 1 < n)
    def _():
        _for_slot(1 - slot, lambda s: _start_all(copies(i + 1, s)))

    f_ref[...] = h_ref[...]
    gates = gates_ref[...]
    lr = lr_ref[...]
    pw = _mm(pp_ref[0].astype(BF16), wp_ref[...])
    _for_slot(slot, lambda s: _wait_all(copies(i, s)))
    half = tile // 2

    def gather(start, size):
        place = lax.broadcasted_iota(jnp.int32, (tile, size), 1) + start
        for e in range(N_EXPERTS):
            row = _segment_row(lr[:, e:e + 1], meta_ref[2, i * N_EXPERTS + e])
            onehot = (place == row).astype(BF16)
            f_ref[...] += gates[:, e:e + 1] * _mm(onehot, ybuf[slot, e, start:start + size, :])

    gather(0, half)
    longest = meta_ref[1, i * N_EXPERTS]
    for e in range(1, N_EXPERTS):
        longest = jnp.maximum(longest, meta_ref[1, i * N_EXPERTS + e])

    @pl.when(longest > half)
    def _():
        gather(half, _segment_rows(tile) - half)

    h = f_ref[...]

    def head(h, pw):
        h = h + _ple_gate(h, gp_ref[...], wpg_ref[...], bpg_ref[...]) * pw
        return _rms(h, gfin_ref[...])

    @pl.when(i < n_prompt_tiles)
    def _():
        outp_ref[...] = head(h, pw)

    @pl.when(i == n_prompt_tiles)
    def _():
        n_s = outs_ref.shape[0]
        outs_ref[...] = head(h[:n_s], _mm(ps_ref[0].astype(BF16), wp_ref[...]))


def _combine(h, gates, lr, pp, ps, gp, wpg, bpg, wp, gfin, ys, meta, tile, n_prompt):
    t = h.shape[0]
    n_pt = n_prompt // tile
    n_s = ps.shape[1]
    row = lambda i, *_: (i, 0)
    prow = lambda i, *_: (jnp.minimum(i, n_pt - 1), 0)
    consts = [gp, wpg, bpg, wp, gfin]
    return pl.pallas_call(
        functools.partial(_combine_kernel, n_prompt_tiles=n_pt),
        grid_spec=pltpu.PrefetchScalarGridSpec(
            num_scalar_prefetch=1,
            grid=(t // tile,),
            in_specs=[pl.BlockSpec((tile, D_MODEL), row), pl.BlockSpec((tile, N_EXPERTS), row),
                      pl.BlockSpec((tile, N_EXPERTS), row),
                      pl.BlockSpec((1, tile, PLE_DIM), lambda i, *_: (1,) + prow(i)),
                      pl.BlockSpec((1, n_s, PLE_DIM), lambda i, *_: (1, 0, 0))]
            + [_const_spec(c.shape) for c in consts] + [pl.BlockSpec(memory_space=pl.ANY)],
            out_specs=[pl.BlockSpec((tile, D_MODEL), prow),
                       pl.BlockSpec((n_s, D_MODEL), lambda i, *_: (0, 0))],
            scratch_shapes=[pltpu.VMEM((2, N_EXPERTS, _segment_rows(tile), D_MODEL), BF16),
                            pltpu.VMEM((tile, D_MODEL), F32),
                            pltpu.SemaphoreType.DMA((2, N_EXPERTS))]),
        out_shape=[jax.ShapeDtypeStruct((n_prompt, D_MODEL), F32),
                   jax.ShapeDtypeStruct((n_s, D_MODEL), F32)],
        compiler_params=_cparams(("arbitrary",)),
        name="combine",
    )(meta, h, gates, lr, pp, ps, *consts, ys)


def _round_up(x, m):
    return (x + m - 1) // m * m


def _route_plan(cnt, t, tile, tmd):
    n_tiles = t // tile
    cnt = cnt.reshape(n_tiles, N_EXPERTS)
    before = jnp.cumsum(cnt, axis=0) - cnt
    shared = before % ROW_ALIGN
    seg_len = shared + cnt
    total = _round_up(cnt.sum(axis=0), ROW_ALIGN)
    guard = _segment_rows(tile) - tile // 2
    region = _round_up(total + guard, tmd)
    region_end = jnp.cumsum(region)
    off = region_end - region
    pos = off[None, :] + before - shared
    rows = _round_up(2 * t + N_EXPERTS * (ROW_ALIGN + tmd + guard), tmd)
    tile_start = jnp.arange(rows // tmd, dtype=jnp.int32) * tmd
    te = jnp.sum(region_end[None, :] <= tile_start[:, None], axis=1).astype(jnp.int32)
    tec = jnp.minimum(te, N_EXPERTS - 1)
    tile_rows = jnp.where(te < N_EXPERTS, jnp.clip(total[tec] - (tile_start - off[tec]), 0, tmd), 0)
    last_e = jnp.max(jnp.where(tile_rows > 0, tec, 0))
    tile_e = jnp.where(tile_rows > 0, tec, last_e).astype(jnp.int32)
    meta = jnp.stack([pos.reshape(-1), seg_len.reshape(-1), shared.reshape(-1)]).astype(jnp.int32)
    tail = jnp.stack([jnp.append(off + total, region_end[-1]),
                      jnp.append(region - total, rows - region_end[-1])]).astype(jnp.int32)
    return dict(meta=meta, tail=tail, tile_e=tile_e, tile_rows=tile_rows.astype(jnp.int32), rows=rows)


def _moe_and_head(h, hn, gates, lr, lr_t, cnt, pp, ps, w, tile, tmd, n_prompt):
    plan = _route_plan(cnt, h.shape[0], tile, tmd)
    xs = _dispatch(hn, lr_t, plan["meta"], plan["tail"], plan["rows"], tile, tmd // EXPERT_ROW_STEPS)
    ys = _expert_ffn(xs, plan["tile_e"], plan["tile_rows"], w["exp_gate"], w["exp_up"], w["exp_down"], tmd)
    return _combine(h, gates, lr, pp, ps, w["norm_ple1"], w["ple_gate1"], w["b_ple_gate1"], w["ple1"],
                    w["norm_final"], ys, plan["meta"], tile, n_prompt)


def _rope_tables(pos):
    half = HEAD_DIM // 2
    inv = 1.0 / (ROPE_THETA ** (jnp.arange(half, dtype=F32) / half))
    ang = pos.astype(F32)[:, None] * inv[None, :]
    cos = jnp.tile(jnp.cos(ang), (1, LANES // half))
    sin = jnp.sin(ang)
    sin = jnp.tile(jnp.concatenate([-sin, sin], axis=-1), (1, LANES // HEAD_DIM))
    return cos, sin


def kernel(x_prompt, x_sample, cache_k, cache_v, state_pool, p_prompt, p_sample, norm_mix, norm_ffn, norm_ple, norm_final, w_qkv, b_qkv, w_o, b_o, sinks, w_pool, pool_scale, w_ff_gate, w_ff_up, w_ff_down, w_router, b_router, w_exp_gate, w_exp_up, w_exp_down, w_ple, w_ple_gate, b_ple_gate):
    batch, seq, _ = x_prompt.shape
    dec = x_sample.shape[0]
    row2 = lambda a: a.reshape(1, -1)
    w = dict(
        exp_gate=w_exp_gate[0], exp_up=w_exp_up[0], exp_down=w_exp_down[0],
        norm_ple1=row2(norm_ple[1]), ple_gate1=w_ple_gate[1].astype(BF16), b_ple_gate1=row2(b_ple_gate[1]),
        ple1=w_ple[1].astype(BF16), norm_final=row2(norm_final))
    wqkv = w_qkv[0].astype(BF16)
    l0 = [w_o[0].astype(BF16), row2(b_o[0]), row2(norm_ffn[0]), w_ff_gate[0].astype(BF16),
          w_ff_up[0].astype(BF16), w_ff_down[0].astype(BF16), row2(norm_ple[0]),
          w_ple_gate[0].astype(BF16), row2(b_ple_gate[0]), w_ple[0].astype(BF16), row2(norm_mix[1])]
    wr_hi = w_router[0].astype(BF16)
    wr_lo = (w_router[0] - wr_hi.astype(F32)).astype(BF16)
    pool_w = [w_pool[0].astype(BF16), row2(pool_scale[0]), row2(norm_ffn[1]),
              jnp.concatenate([wr_hi, wr_lo], axis=1), row2(b_router[0])]

    tm, tile = 512, 512
    n_prompt = batch * seq
    t_pad = n_prompt + 2 * tm

    xp = x_prompt.reshape(n_prompt, D_MODEL)
    pp = p_prompt.reshape(p_prompt.shape[0], n_prompt, PLE_DIM)
    cos_p, sin_p = _rope_tables(jnp.arange(seq, dtype=jnp.int32))
    q, k, v = _qkv_rope(xp, row2(norm_mix[0]), wqkv, row2(b_qkv[0]), cos_p, sin_p, tm=2 * tm)
    o = _prompt_attention(q, k, v, sinks[0], batch, seq)
    h1, hn1 = _layer0_tail(xp, o, pp, *l0, tm=tm)
    *padded, last = _pool_prompt(h1, hn1, *pool_w, batch=batch, seq=seq, tm=2 * tm, tile=tile, t_pad=t_pad)
    kv_shape = (1, batch, WINDOW, N_KV_HEADS, HEAD_DIM)
    new_k_prompt = k.reshape(batch, seq, KV_DIM)[:, -WINDOW:].reshape(kv_shape)
    new_v_prompt = v.reshape(batch, seq, KV_DIM)[:, -WINDOW:].reshape(kv_shape)
    new_pool_prompt = last[:, -POOL_HIST:][None]

    xs_ = x_sample.reshape(dec, D_MODEL)
    ps = p_sample.reshape(p_sample.shape[0], dec, PLE_DIM)
    cos_s, sin_s = _rope_tables(jnp.full((dec,), PAST_LEN, jnp.int32))
    qs, ks, vs = _qkv_rope(xs_, row2(norm_mix[0]), wqkv, row2(b_qkv[0]), cos_s, sin_s, tm=dec)
    n_hist = cache_k.shape[2]
    k_win = jnp.concatenate([cache_k[0].reshape(dec, n_hist, KV_DIM), ks[:, None]], axis=1)[:, -n_hist:]
    v_win = jnp.concatenate([cache_v[0].reshape(dec, n_hist, KV_DIM), vs[:, None]], axis=1)[:, -n_hist:]
    head_kv = jnp.arange(N_HEADS) // GROUP
    blk = (head_kv[:, None] == jnp.arange(N_KV_HEADS)[None, :]).astype(BF16)
    q_blk = (qs.reshape(dec, N_HEADS, 1, HEAD_DIM) * blk[None, :, :, None]).reshape(dec, N_HEADS, KV_DIM)
    o_blk = _sample_attention(q_blk, k_win, v_win, sinks[0])
    o_s = jnp.take_along_axis(o_blk.reshape(dec, N_HEADS, N_KV_HEADS, HEAD_DIM),
                              head_kv[None, :, None, None], axis=2).reshape(dec, N_HEADS * HEAD_DIM)
    h1s, hn1s = _layer0_tail(xs_, o_s.astype(BF16), ps, *l0, tm=dec)
    hist_t = jnp.swapaxes(state_pool[0], 0, 1)
    padded = _pool_sample(h1s, hn1s, hist_t, *pool_w, padded=padded, row0=n_prompt, tile=tile)
    new_k_sample = k_win.reshape(1, dec, n_hist, N_KV_HEADS, HEAD_DIM)
    new_v_sample = v_win.reshape(1, dec, n_hist, N_KV_HEADS, HEAD_DIM)
    new_pool_sample = jnp.concatenate([state_pool[0], hn1s[:, None]], axis=1)[:, -POOL_HIST:][None]

    expected = TOP_K * (n_prompt + dec) / N_EXPERTS
    tiles_per_expert = max(1, round(expected / EXPERT_TILE_TARGET))
    tmd = _round_up(int(expected * (1 + EXPERT_TILE_SLACK) / tiles_per_expert), EXPERT_ROW_STEPS * ROW_ALIGN)
    y_prompt, y_sample = _moe_and_head(*padded, pp, ps, w, tile=tile, tmd=tmd, n_prompt=n_prompt)

    return (y_prompt.reshape(batch, seq, D_MODEL), y_sample.reshape(dec, 1, D_MODEL),
            new_k_prompt, new_v_prompt, new_pool_prompt, new_k_sample, new_v_sample, new_pool_sample)
```

```python
import functools

import jax
import jax.numpy as jnp
from jax import lax
from jax.experimental import pallas as pl
from jax.experimental.pallas import tpu as pltpu

F32 = jnp.float32
BF16 = jnp.bfloat16

D_MODEL = 1024
HEAD_DIM = 64
N_HEADS = 16
N_KV_HEADS = 4
GROUP = N_HEADS // N_KV_HEADS
KV_DIM = N_KV_HEADS * HEAD_DIM
QKV_DIM = (N_HEADS + 2 * N_KV_HEADS) * HEAD_DIM
WINDOW = 128
ROPE_THETA = 10000.0
PAST_LEN = 16384
POOL_WINDOWS = (2, 4, 8, 16)
POOL_GROUP_DIM = D_MODEL // len(POOL_WINDOWS)
POOL_HIST = max(POOL_WINDOWS) - 1
POOL_HALO = 16
D_FF = 2816
N_EXPERTS = 8
D_FF_EXPERT = 3584
PLE_DIM = 256
EPS = 1e-6

LANES = 128
VMEM_LIMIT = 56 * 1024 * 1024

ATTN_TQ = 1024
ATTN_STAGE = 2
MXU_DIM = 256
FF_CHUNK_EDGES = (0, 6 * MXU_DIM, D_FF)
EXPERT_FF_CHUNK = 512
EXPERT_ROW_STEPS = 8
EXPERT_TILE_TARGET = 2048
EXPERT_TILE_SLACK = 0.05
TOP_K = 2
ROW_ALIGN = 16
TAIL_PIECES = (2048, 1024, 512, 256, 128, 64, 32, 16)


def _cparams(sem, fuse_inputs=None):
    return pltpu.CompilerParams(dimension_semantics=sem, vmem_limit_bytes=VMEM_LIMIT,
                                allow_input_fusion=fuse_inputs)


def _const_spec(shape):
    nd = len(shape)
    return pl.BlockSpec(shape, lambda *_: (0,) * nd, pipeline_mode=pl.Buffered(1))


def _rms(x, g):
    return x * lax.rsqrt(jnp.mean(x * x, axis=-1, keepdims=True) + EPS) * g


def _mm(a, b):
    return jnp.dot(a, b, preferred_element_type=F32)


def _silu(x):
    return x * jax.nn.sigmoid(x)


def _qkv_kernel(x_ref, g_ref, w_ref, b_ref, cos_ref, sin_ref, q_ref, k_ref, v_ref):
    hn = _rms(x_ref[...], g_ref[...])
    qkv = _mm(hn.astype(BF16), w_ref[...]) + b_ref[...]
    cos = cos_ref[...]
    sin = sin_ref[...]
    lane = lax.broadcasted_iota(jnp.int32, cos.shape, 1)
    first_half = (lane % HEAD_DIM) < (HEAD_DIM // 2)

    def rope(xb):
        partner = jnp.where(first_half, pltpu.roll(xb, LANES - HEAD_DIM // 2, 1),
                            pltpu.roll(xb, HEAD_DIM // 2, 1))
        return xb * cos + partner * sin

    scale = HEAD_DIM ** -0.5
    for c in range(N_HEADS * HEAD_DIM // LANES):
        sl = slice(c * LANES, (c + 1) * LANES)
        q_ref[:, sl] = (rope(qkv[:, sl]) * scale).astype(BF16)
    for c in range(KV_DIM // LANES):
        sl = slice(c * LANES, (c + 1) * LANES)
        k_ref[:, sl] = rope(qkv[:, N_HEADS * HEAD_DIM + c * LANES:N_HEADS * HEAD_DIM + (c + 1) * LANES])
    v_ref[...] = qkv[:, N_HEADS * HEAD_DIM + KV_DIM:]


def _qkv_rope(x, g, w, b, cos, sin, tm):
    t = x.shape[0]
    n_pos_tiles = cos.shape[0] // tm
    row = lambda i: (i, 0)
    return pl.pallas_call(
        _qkv_kernel,
        grid=(t // tm,),
        in_specs=[
            pl.BlockSpec((tm, D_MODEL), row),
            _const_spec((1, D_MODEL)),
            _const_spec((D_MODEL, QKV_DIM)),
            _const_spec((1, QKV_DIM)),
            pl.BlockSpec((tm, LANES), lambda i: (i % n_pos_tiles, 0)),
            pl.BlockSpec((tm, LANES), lambda i: (i % n_pos_tiles, 0)),
        ],
        out_specs=[
            pl.BlockSpec((tm, N_HEADS * HEAD_DIM), row),
            pl.BlockSpec((tm, KV_DIM), row),
            pl.BlockSpec((tm, KV_DIM), row),
        ],
        out_shape=[
            jax.ShapeDtypeStruct((t, N_HEADS * HEAD_DIM), BF16),
            jax.ShapeDtypeStruct((t, KV_DIM), F32),
            jax.ShapeDtypeStruct((t, KV_DIM), F32),
        ],
        compiler_params=_cparams(("parallel",), [False, True, True, True, False, False]),
        name="qkv_rope",
    )(x, g, w, b, cos, sin)


V_AUG = 4 * HEAD_DIM


def _attn_kernel(sink_ref, q_ref, kc_ref, kp_ref, vc_ref, vp_ref, o_ref, kbuf, vbuf, bias):
    n = pl.program_id(1)
    kbuf[0:WINDOW] = kp_ref[...].astype(BF16)
    kbuf[WINDOW:] = kc_ref[...].astype(BF16)
    v_all = jnp.concatenate([vp_ref[...], vc_ref[...]], axis=0).astype(BF16)
    pad0 = jnp.zeros((v_all.shape[0], HEAD_DIM), BF16)
    pad1 = jnp.ones((v_all.shape[0], 2 * HEAD_DIM), BF16)
    for kv in range(N_KV_HEADS):
        vbuf[:, kv * V_AUG:(kv + 1) * V_AUG] = jnp.concatenate(
            [v_all[:, kv * HEAD_DIM:(kv + 1) * HEAD_DIM], pad0, pad1], axis=1)
    shape = (GROUP * WINDOW, 2 * WINDOW)
    qrow = lax.broadcasted_iota(jnp.int32, shape, 0) & (WINDOW - 1)
    col = lax.broadcasted_iota(jnp.int32, shape, 1)
    mask_cur = (col >= WINDOW) & (col - WINDOW <= qrow)
    mask_prev = (col < WINDOW) & (col > qrow)
    bias[0] = jnp.where(mask_cur | mask_prev, 0.0, -jnp.inf)
    bias[1] = jnp.where(mask_cur, 0.0, -jnp.inf)
    rows = lambda sb: slice(sb * WINDOW, (sb + 1) * WINDOW)
    window = lambda sb: slice(sb * WINDOW, (sb + 2) * WINDOW)
    for first in range(0, ATTN_TQ // WINDOW, ATTN_STAGE):
        blocks = range(first, first + ATTN_STAGE)
        scores, probs, sink_term, outs = {}, {}, {}, {}
        for sb in blocks:
            for kv in range(N_KV_HEADS):
                qg = jnp.concatenate([q_ref[rows(sb), h * HEAD_DIM:(h + 1) * HEAD_DIM]
                                      for h in range(kv * GROUP, (kv + 1) * GROUP)], axis=0)
                scores[sb, kv] = lax.dot_general(qg, kbuf[window(sb), kv * HEAD_DIM:(kv + 1) * HEAD_DIM],
                                                 (((1,), (1,)), ((), ())), preferred_element_type=F32)
        for sb in blocks:
            bias_sb = bias[jnp.where(n == 0, 1, 0)] if sb == 0 else bias[0]
            for kv in range(N_KV_HEADS):
                s = scores[sb, kv] + bias_sb
                p = []
                for g in range(GROUP):
                    h = kv * GROUP + g
                    sh = s[g * WINDOW:(g + 1) * WINDOW]
                    m = jnp.maximum(jnp.max(sh, axis=-1, keepdims=True), sink_ref[h])
                    p.append(jnp.exp(sh - m).astype(BF16))
                    sink_term[sb, h] = jnp.exp(sink_ref[h] - m)
                probs[sb, kv] = jnp.concatenate(p, axis=0)
        for sb in blocks:
            for kv in range(N_KV_HEADS):
                outs[sb, kv] = _mm(probs[sb, kv], vbuf[window(sb), kv * V_AUG:(kv + 1) * V_AUG])
        for sb in blocks:
            for h in range(N_HEADS):
                oh = outs[sb, h // GROUP][(h % GROUP) * WINDOW:(h % GROUP + 1) * WINDOW]
                o = oh[:, :2 * HEAD_DIM] / (oh[:, 2 * HEAD_DIM:] + sink_term[sb, h])
                o_ref[rows(sb), h * HEAD_DIM:(h + 1) * HEAD_DIM] = o[:, :HEAD_DIM].astype(BF16)


def _prompt_attention(q, k, v, sinks, batch, seq):
    nq = seq // ATTN_TQ
    per = ATTN_TQ // WINDOW
    cur = lambda b, n: (b * nq + n, 0)
    prev = lambda b, n: (jnp.maximum(b * nq * per + n * per - 1, b * nq * per), 0)
    return pl.pallas_call(
        _attn_kernel,
        grid=(batch, nq),
        in_specs=[
            pl.BlockSpec(memory_space=pltpu.SMEM),
            pl.BlockSpec((ATTN_TQ, N_HEADS * HEAD_DIM), cur),
            pl.BlockSpec((ATTN_TQ, KV_DIM), cur),
            pl.BlockSpec((WINDOW, KV_DIM), prev),
            pl.BlockSpec((ATTN_TQ, KV_DIM), cur),
            pl.BlockSpec((WINDOW, KV_DIM), prev),
        ],
        out_specs=pl.BlockSpec((ATTN_TQ, N_HEADS * HEAD_DIM), cur),
        out_shape=jax.ShapeDtypeStruct(q.shape, BF16),
        scratch_shapes=[pltpu.VMEM((ATTN_TQ + WINDOW, KV_DIM), BF16),
                        pltpu.VMEM((ATTN_TQ + WINDOW, N_KV_HEADS * V_AUG), BF16),
                        pltpu.VMEM((2, GROUP * WINDOW, 2 * WINDOW), F32)],
        compiler_params=_cparams(("parallel", "parallel")),
        name="prompt_attention",
    )(sinks, q, k, k, v, v)


def _attn_sample_kernel(sink_ref, q_ref, k_ref, v_ref, o_ref):
    s = jnp.einsum("bhc,blc->bhl", q_ref[...], k_ref[...].astype(BF16),
                   preferred_element_type=F32)
    sink = sink_ref[...]
    m = jnp.maximum(jnp.max(s, axis=-1, keepdims=True), sink)
    p = jnp.exp(s - m)
    denom = jnp.sum(p, axis=-1, keepdims=True) + jnp.exp(sink - m)
    o_ref[...] = jnp.einsum("bhl,blc->bhc", (p / denom).astype(BF16), v_ref[...].astype(BF16),
                            preferred_element_type=F32)


def _sample_attention(q_blk, k_win, v_win, sinks, bt=16):
    b = q_blk.shape[0]
    blk = lambda i: (i, 0, 0)
    return pl.pallas_call(
        _attn_sample_kernel,
        grid=(b // bt,),
        in_specs=[
            _const_spec((1, N_HEADS, 1)),
            pl.BlockSpec((bt, N_HEADS, KV_DIM), blk),
            pl.BlockSpec((bt, WINDOW, KV_DIM), blk),
            pl.BlockSpec((bt, WINDOW, KV_DIM), blk),
        ],
        out_specs=pl.BlockSpec((bt, N_HEADS, KV_DIM), blk),
        out_shape=jax.ShapeDtypeStruct((b, N_HEADS, KV_DIM), F32),
        compiler_params=_cparams(("parallel",)),
        name="sample_attention",
    )(sinks.reshape(1, N_HEADS, 1), q_blk, k_win, v_win)


def _ple(h, p, g, wg, bg, wp):
    gate = jax.nn.sigmoid(_mm(_rms(h, g).astype(BF16), wg) + bg)
    return h + gate * _mm(p.astype(BF16), wp)


def _layer0_tail_math(x_ref, o_ref, p, wo_ref, bo_ref, gf_ref, wg_ref, wu_ref, wd_ref,
                      gp_ref, wpg_ref, bpg_ref, wp_ref, gm_ref):
    h = x_ref[...] + _mm(o_ref[...], wo_ref[...]) + bo_ref[...]
    hn = _rms(h, gf_ref[...]).astype(BF16)
    f = None
    for c0, c1 in zip(FF_CHUNK_EDGES[:-1], FF_CHUNK_EDGES[1:]):
        a = (_silu(_mm(hn, wg_ref[:, c0:c1])) * _mm(hn, wu_ref[:, c0:c1])).astype(BF16)
        part = _mm(a, wd_ref[c0:c1, :])
        f = part if f is None else f + part
    h = h + f
    h = _ple(h, p, gp_ref[...], wpg_ref[...], bpg_ref[...], wp_ref[...])
    return h, _rms(h, gm_ref[...])


def _layer0_tail_kernel(x_ref, o_ref, p_ref, wo_ref, bo_ref, gf_ref, wg_ref, wu_ref, wd_ref,
                        gp_ref, wpg_ref, bpg_ref, wp_ref, gm_ref, h_ref, hn_ref):
    h_ref[...], hn_ref[...] = _layer0_tail_math(x_ref, o_ref, p_ref[0], wo_ref, bo_ref, gf_ref, wg_ref,
                                                wu_ref, wd_ref, gp_ref, wpg_ref, bpg_ref, wp_ref, gm_ref)


def _layer0_tail(x, o, p, wo, bo, gf, wg, wu, wd, gp, wpg, bpg, wp, gm, tm):
    t = x.shape[0]
    row = lambda i: (i, 0)
    consts = [wo, bo, gf, wg, wu, wd, gp, wpg, bpg, wp, gm]
    return pl.pallas_call(
        _layer0_tail_kernel,
        grid=(t // tm,),
        in_specs=[pl.BlockSpec((tm, D_MODEL), row), pl.BlockSpec((tm, D_MODEL), row),
                  pl.BlockSpec((1, tm, PLE_DIM), lambda i: (0, i, 0))] + [_const_spec(c.shape) for c in consts],
        out_specs=[pl.BlockSpec((tm, D_MODEL), row), pl.BlockSpec((tm, D_MODEL), row)],
        out_shape=[jax.ShapeDtypeStruct((t, D_MODEL), F32), jax.ShapeDtypeStruct((t, D_MODEL), F32)],
        compiler_params=_cparams(("parallel",), [False] * 3 + [True] * len(consts)),
        name="layer0_tail",
    )(x, o, p, *consts)


def _segment_ranks(sel, tile, lr_out, lrt_out, cnt_out):
    r = lax.broadcasted_iota(jnp.int32, (tile, tile), 0)
    c = lax.broadcasted_iota(jnp.int32, (tile, tile), 1)
    earlier = (c < r).astype(BF16)
    pad = jnp.full((tile, LANES - N_EXPERTS), -1.0, F32)
    for k in range(sel.shape[0] // tile):
        rows = slice(k * tile, (k + 1) * tile)
        sel_k = sel[rows]
        rank = _mm(earlier, sel_k.astype(BF16))
        lr = jnp.where(sel_k, rank, -1.0)
        lr_out[rows, :] = lr.astype(jnp.int32)
        lrt_out[:, rows] = jnp.concatenate([lr, pad], axis=1).T[:N_EXPERTS].astype(jnp.int32)
        cnt_out[k] = jnp.sum(sel_k.astype(F32), axis=0, keepdims=True).astype(jnp.int32)


def _pool_project_route(h, hn, pooled_sum_inv, wpool_ref, ps_ref, gf_ref, wr_ref, br_ref,
                        h_out, hn_out, gates_out, lr_out, lrt_out, cnt_out, tile):
    mixed = []
    for g in range(len(POOL_WINDOWS)):
        cols = slice(g * POOL_GROUP_DIM, (g + 1) * POOL_GROUP_DIM)
        pooled = pooled_sum_inv[g] - hn[:, cols]
        mixed.append(_mm(pooled.astype(BF16), wpool_ref[g]))
    h = h + jnp.concatenate(mixed, axis=-1) * ps_ref[...]
    h_out[...] = h
    hn2 = _rms(h, gf_ref[...])
    hi = hn2.astype(BF16)
    hn_out[...] = hi
    lo = (hn2 - hi.astype(F32)).astype(BF16)
    by_hi = _mm(hi, wr_ref[...])
    logits = (by_hi[:, :N_EXPERTS] + by_hi[:, N_EXPERTS:] + _mm(lo, wr_ref[:, :N_EXPERTS])) + br_ref[...]
    idx = lax.broadcasted_iota(jnp.int32, logits.shape, 1)
    m1 = jnp.max(logits, axis=-1, keepdims=True)
    i1 = jnp.min(jnp.where(logits == m1, idx, N_EXPERTS), axis=-1, keepdims=True)
    rest = jnp.where(idx == i1, -jnp.inf, logits)
    m2 = jnp.max(rest, axis=-1, keepdims=True)
    i2 = jnp.min(jnp.where(rest == m2, idx, N_EXPERTS), axis=-1, keepdims=True)
    e = jnp.exp(m2 - m1)
    w1 = 1.0 / (1.0 + e)
    w2 = e / (1.0 + e)
    gates_out[...] = jnp.where(idx == i1, w1, jnp.where(idx == i2, w2, 0.0))
    _segment_ranks((idx == i1) | (idx == i2), tile, lr_out, lrt_out, cnt_out)


def _route_out_specs(t, tm, tile, blk):
    specs = [pl.BlockSpec((tm, N_EXPERTS), lambda *g: (blk(*g), 0)),
             pl.BlockSpec((tm, N_EXPERTS), lambda *g: (blk(*g), 0)),
             pl.BlockSpec((N_EXPERTS, tm), lambda *g: (0, blk(*g))),
             pl.BlockSpec((tm // tile, 1, N_EXPERTS), lambda *g: (blk(*g), 0, 0))]
    shapes = [jax.ShapeDtypeStruct((t, N_EXPERTS), F32),
              jax.ShapeDtypeStruct((t, N_EXPERTS), jnp.int32),
              jax.ShapeDtypeStruct((N_EXPERTS, t), jnp.int32),
              jax.ShapeDtypeStruct((t // tile, 1, N_EXPERTS), jnp.int32)]
    return specs, shapes


def _pool_prompt_kernel(h_ref, hn_ref, wpool_ref, ps_ref, gf_ref, wr_ref, br_ref,
                        h_out, hn_out, gates_out, lr_out, lrt_out, cnt_out, last_out, carry, buf, *, tile):
    b = pl.program_id(0)
    n = pl.program_id(1)
    tm = h_ref.shape[0]
    n_seq = pl.num_programs(0) - 1

    @pl.when(b < n_seq)
    def _():
        @pl.when(n == 0)
        def _():
            carry[...] = jnp.zeros_like(carry)

        hn = hn_ref[...]
        buf[0:POOL_HALO] = carry[...]
        buf[POOL_HALO:] = hn
        carry[...] = hn[tm - POOL_HALO:]
        last_out[0] = hn[tm - POOL_HALO:]
        pos = n * tm + lax.broadcasted_iota(jnp.int32, (tm, 1), 0)
        means = []
        for g, w in enumerate(POOL_WINDOWS):
            s = buf[:, g * POOL_GROUP_DIM:(g + 1) * POOL_GROUP_DIM]
            shift = 1
            while shift < w:
                s = s + pltpu.roll(s, shift, 0)
                shift *= 2
            cnt = jnp.minimum(w, pos + 1).astype(F32)
            means.append(s[POOL_HALO:] * (1.0 / cnt))
        _pool_project_route(h_ref[...], hn, means, wpool_ref, ps_ref, gf_ref, wr_ref, br_ref,
                            h_out, hn_out, gates_out, lr_out, lrt_out, cnt_out, tile)

    @pl.when(b == n_seq)
    def _():
        h_out[...] = jnp.zeros_like(h_out)
        hn_out[...] = jnp.zeros_like(hn_out)
        gates_out[...] = jnp.zeros_like(gates_out)
        lr_out[...] = jnp.full(lr_out.shape, -1, jnp.int32)
        lrt_out[...] = jnp.full(lrt_out.shape, -1, jnp.int32)
        cnt_out[...] = jnp.zeros_like(cnt_out)


def _pool_prompt(h, hn, wpool, ps, gf, wr, br, batch, seq, tm, tile, t_pad):
    ns = seq // tm
    pad_blk = batch * ns
    blk = lambda b, n: jnp.where(b < batch, b * ns + n, pad_blk)
    src = lambda b, n: (jnp.minimum(b * ns + n, pad_blk - 1), 0)
    consts = [wpool, ps, gf, wr, br]
    route_specs, route_shapes = _route_out_specs(t_pad, tm, tile, blk)
    row = lambda b, n: (blk(b, n), 0)
    return pl.pallas_call(
        functools.partial(_pool_prompt_kernel, tile=tile),
        grid=(batch + 1, ns),
        in_specs=[pl.BlockSpec((tm, D_MODEL), src), pl.BlockSpec((tm, D_MODEL), src)]
        + [_const_spec(c.shape) for c in consts],
        out_specs=[pl.BlockSpec((tm, D_MODEL), row), pl.BlockSpec((tm, D_MODEL), row)] + route_specs
        + [pl.BlockSpec((1, POOL_HALO, D_MODEL), lambda b, n: (jnp.minimum(b, batch - 1), 0, 0))],
        out_shape=[jax.ShapeDtypeStruct((t_pad, D_MODEL), F32), jax.ShapeDtypeStruct((t_pad, D_MODEL), BF16)]
        + route_shapes + [jax.ShapeDtypeStruct((batch, POOL_HALO, D_MODEL), F32)],
        scratch_shapes=[pltpu.VMEM((POOL_HALO, D_MODEL), F32),
                        pltpu.VMEM((tm + POOL_HALO, D_MODEL), F32)],
        compiler_params=_cparams(("arbitrary", "arbitrary")),
        name="pool_prompt",
    )(h, hn, *consts)


def _pool_sample_kernel(h_ref, hn_ref, hist_ref, wpool_ref, ps_ref, gf_ref, wr_ref, br_ref, *rest):
    outs = rest[len(rest) // 2:]
    hn = hn_ref[...]
    means = []
    for g, w in enumerate(POOL_WINDOWS):
        cols = slice(g * POOL_GROUP_DIM, (g + 1) * POOL_GROUP_DIM)
        s = hn[:, cols]
        for j in range(1, w):
            s = s + hist_ref[POOL_HIST - j, :, cols]
        means.append(s * (1.0 / min(w, PAST_LEN + 1)))
    _pool_project_route(h_ref[...], hn, means, wpool_ref, ps_ref, gf_ref, wr_ref, br_ref,
                        *outs, hn.shape[0])


def _pool_sample(h, hn, hist_t, wpool, ps, gf, wr, br, padded, row0, tile):
    t = h.shape[0]
    args = [h, hn, hist_t, wpool, ps, gf, wr, br]
    blk = row0 // t
    out_specs = [pl.BlockSpec((t, D_MODEL), lambda i: (blk, 0)), pl.BlockSpec((t, D_MODEL), lambda i: (blk, 0)),
                 pl.BlockSpec((t, N_EXPERTS), lambda i: (blk, 0)), pl.BlockSpec((t, N_EXPERTS), lambda i: (blk, 0)),
                 pl.BlockSpec((N_EXPERTS, t), lambda i: (0, blk)),
                 pl.BlockSpec((1, 1, N_EXPERTS), lambda i: (row0 // tile, 0, 0))]
    return pl.pallas_call(
        _pool_sample_kernel,
        grid=(1,),
        in_specs=[_const_spec(a.shape) for a in args] + [pl.BlockSpec(memory_space=pl.ANY)] * len(padded),
        out_specs=out_specs,
        out_shape=[jax.ShapeDtypeStruct(a.shape, a.dtype) for a in padded],
        input_output_aliases={len(args) + k: k for k in range(len(padded))},
        compiler_params=_cparams(("arbitrary",)),
        name="pool_sample",
    )(*args, *padded)


def _segment_rows(tile):
    return tile + ROW_ALIGN


def _segment_copies(meta_ref, step, tile, vmem_of, hbm, sem_of, to_hbm):
    half = tile // 2
    out = []
    for k, (start, size) in enumerate(((0, half), (half, _segment_rows(tile) - half))):
        for e in range(N_EXPERTS):
            pos = meta_ref[0, step * N_EXPERTS + e]
            pred = None if k == 0 else meta_ref[1, step * N_EXPERTS + e] > half
            v = vmem_of(e).at[pl.ds(start, size)]
            h = hbm.at[pl.ds(pl.multiple_of(pos + start, ROW_ALIGN), size)]
            copy = pltpu.make_async_copy(v, h, sem_of(e)) if to_hbm else pltpu.make_async_copy(h, v, sem_of(e))
            out.append((pred, copy))
    return out


def _segment_row(rank, shared):
    return jnp.where(rank >= 0, rank + shared, -1)


def _for_slot(slot, fn):
    for s in range(2):
        @pl.when(slot == s)
        def _(s=s):
            fn(s)


def _start_all(copies):
    for pred, copy in copies:
        if pred is None:
            copy.start()
        else:
            pl.when(pred)(copy.start)


def _wait_all(copies):
    for pred, copy in copies:
        if pred is None:
            copy.wait()
        else:
            pl.when(pred)(copy.wait)


def _dispatch_kernel(meta_ref, tail_ref, hn_ref, lrt_ref, xs_ref, seg, zeros, sem, zsem, *, slack_chunk):
    i = pl.program_id(0)
    last = pl.num_programs(0) - 1
    tile = hn_ref.shape[0]
    slot = i % 2
    copies = lambda step, s: _segment_copies(meta_ref, step, tile, lambda e: seg.at[s, e], xs_ref,
                                             lambda e: sem.at[s, e], True)
    hn = hn_ref[...]
    half = tile // 2
    rest = _segment_rows(tile) - half
    shared = [meta_ref[2, i * N_EXPERTS + e] for e in range(N_EXPERTS)]
    row_of = lambda e: _segment_row(lrt_ref[e:e + 1, :], shared[e])
    place = lax.broadcasted_iota(jnp.int32, (half, tile), 0)
    for e0 in range(0, N_EXPERTS, 4):
        first = jnp.concatenate([(place == row_of(e)).astype(BF16) for e in range(e0, e0 + 4)], axis=0)
        rows = _mm(first, hn).astype(BF16)
        for k in range(4):
            seg[slot, e0 + k, 0:half] = rows[k * half:(k + 1) * half]
    place_rest = lax.broadcasted_iota(jnp.int32, (rest, tile), 0) + half
    for e in range(N_EXPERTS):
        @pl.when(meta_ref[1, i * N_EXPERTS + e] > half)
        def _(e=e):
            seg[slot, e, half:] = _mm((place_rest == row_of(e)).astype(BF16), hn_ref[...]).astype(BF16)

        @pl.when(shared[e] > 0)
        def _(e=e):
            prev_len = meta_ref[1, jnp.maximum(i - 1, 0) * N_EXPERTS + e]
            block = pl.ds(pl.multiple_of(prev_len - shared[e], ROW_ALIGN), ROW_ALIGN)
            seg[slot, e, 0:ROW_ALIGN] = seg[slot, e, 0:ROW_ALIGN] + seg[1 - slot, e, block]

    @pl.when(i > 0)
    def _():
        _for_slot(1 - slot, lambda s: _wait_all(copies(i - 1, s)))

    _for_slot(slot, lambda s: _start_all(copies(i, s)))

    @pl.when(i == last)
    def _():
        _for_slot(slot, lambda s: _wait_all(copies(i, s)))
        zeros[...] = jnp.zeros_like(zeros)
        tails = []
        for e in range(N_EXPERTS):
            start = tail_ref[0, e]
            cnt = tail_ref[1, e]
            for piece in TAIL_PIECES:
                off = pl.multiple_of(cnt & ~(2 * piece - 1), ROW_ALIGN)
                copy = pltpu.make_async_copy(
                    zeros.at[pl.ds(0, piece)],
                    xs_ref.at[pl.ds(pl.multiple_of(start + off, ROW_ALIGN), piece)],
                    zsem.at[e])
                tails.append(((cnt & piece) != 0, copy))
        _start_all(tails)
        _wait_all(tails)

        def slack_copy(c):
            row = pl.multiple_of(tail_ref[0, N_EXPERTS] + c * slack_chunk, ROW_ALIGN)
            return pltpu.make_async_copy(zeros.at[pl.ds(0, slack_chunk)],
                                         xs_ref.at[pl.ds(row, slack_chunk)], zsem.at[0])

        n_slack = tail_ref[1, N_EXPERTS] // slack_chunk
        lax.fori_loop(0, n_slack, lambda c, _: slack_copy(c).start(), None)
        lax.fori_loop(0, n_slack, lambda c, _: slack_copy(c).wait(), None)


def _dispatch(hn, lr_t, meta, tail, rows, tile, slack_chunk):
    t = hn.shape[0]
    return pl.pallas_call(
        functools.partial(_dispatch_kernel, slack_chunk=slack_chunk),
        grid_spec=pltpu.PrefetchScalarGridSpec(
            num_scalar_prefetch=2,
            grid=(t // tile,),
            in_specs=[pl.BlockSpec((tile, D_MODEL), lambda i, *_: (i, 0)),
                      pl.BlockSpec((N_EXPERTS, tile), lambda i, *_: (0, i))],
            out_specs=pl.BlockSpec(memory_space=pl.ANY),
            scratch_shapes=[pltpu.VMEM((2, N_EXPERTS, _segment_rows(tile), D_MODEL), BF16),
                            pltpu.VMEM((TAIL_PIECES[0], D_MODEL), BF16),
                            pltpu.SemaphoreType.DMA((2, N_EXPERTS)),
                            pltpu.SemaphoreType.DMA((N_EXPERTS,))]),
        out_shape=jax.ShapeDtypeStruct((rows, D_MODEL), BF16),
        compiler_params=_cparams(("arbitrary",)),
        name="expert_dispatch",
    )(meta, tail, hn, lr_t)


def _expert_kernel(te_ref, tr_ref, x_ref, wg_ref, wu_ref, wd_ref, y_ref, acc):
    r = pl.program_id(0)
    j = pl.program_id(1)

    @pl.when(j == 0)
    def _():
        acc[...] = jnp.zeros_like(acc)

    def ffn(m):
        x = x_ref[0:m]
        a = (_silu(_mm(x, wg_ref[0].astype(BF16))) * _mm(x, wu_ref[0].astype(BF16))).astype(BF16)
        acc[0:m] += _mm(a, wd_ref[0].astype(BF16))

    rows = tr_ref[r]
    step = x_ref.shape[0] // EXPERT_ROW_STEPS
    for k in range(1, EXPERT_ROW_STEPS + 1):
        @pl.when((rows > (k - 1) * step) & (rows <= k * step))
        def _(k=k):
            ffn(k * step)

    @pl.when(j == pl.num_programs(1) - 1)
    def _():
        y_ref[...] = acc[...].astype(BF16)


def _expert_ffn(xs, tile_e, tile_rows, wg, wu, wd, tmd):
    rows = xs.shape[0]
    nj = D_FF_EXPERT // EXPERT_FF_CHUNK
    live = lambda r, tr: tr[r] > 0
    jj = lambda r, j, tr: jnp.where(live(r, tr), j, nj - 1)
    return pl.pallas_call(
        _expert_kernel,
        grid_spec=pltpu.PrefetchScalarGridSpec(
            num_scalar_prefetch=2,
            grid=(rows // tmd, nj),
            in_specs=[
                pl.BlockSpec((tmd, D_MODEL), lambda r, j, te, tr: (jnp.where(live(r, tr), r, 0), 0)),
                pl.BlockSpec((1, D_MODEL, EXPERT_FF_CHUNK), lambda r, j, te, tr: (te[r], 0, jj(r, j, tr))),
                pl.BlockSpec((1, D_MODEL, EXPERT_FF_CHUNK), lambda r, j, te, tr: (te[r], 0, jj(r, j, tr))),
                pl.BlockSpec((1, EXPERT_FF_CHUNK, D_MODEL), lambda r, j, te, tr: (te[r], jj(r, j, tr), 0)),
            ],
            out_specs=pl.BlockSpec((tmd, D_MODEL), lambda r, j, te, tr: (r, 0)),
            scratch_shapes=[pltpu.VMEM((tmd, D_MODEL), F32)]),
        out_shape=jax.ShapeDtypeStruct((rows, D_MODEL), BF16),
        compiler_params=_cparams(("parallel", "arbitrary")),
        name="expert_ffn",
    )(tile_e, tile_rows, xs, wg, wu, wd)


def _combine_kernel(meta_ref, h_ref, gates_ref, lr_ref, pp_ref, ps_ref, gp_ref, wpg_ref, bpg_ref, wp_ref,
                    gfin_ref, ys_ref, outp_ref, outs_ref, ybuf, f_ref, sem, *, n_prompt_tiles):
    i = pl.program_id(0)
    n = pl.num_programs(0)
    tile = h_ref.shape[0]
    slot = i % 2
    copies = lambda step, s: _segment_copies(meta_ref, step, tile, lambda e: ybuf.at[s, e], ys_ref,
                                             lambda e: sem.at[s, e], False)

    @pl.when(i == 0)
    def _():
        ybuf[...] = jnp.zeros_like(ybuf)
        _start_all(copies(0, 0))

    @pl.when(i + 1 < n)
    def _():
        _for_slot(1 - slot, lambda s: _start_all(copies(i + 1, s)))

    _for_slot(slot, lambda s: _wait_all(copies(i, s)))
    gates = gates_ref[...]
    lr = lr_ref[...]
    half = tile // 2

    def gather(start, size):
        place = lax.broadcasted_iota(jnp.int32, (tile, size), 1) + start
        for e in range(N_EXPERTS):
            row = _segment_row(lr[:, e:e + 1], meta_ref[2, i * N_EXPERTS + e])
            onehot = (place == row).astype(BF16)
            f_ref[...] += gates[:, e:e + 1] * _mm(onehot, ybuf[slot, e, start:start + size, :])

    f_ref[...] = h_ref[...]
    gather(0, half)
    longest = meta_ref[1, i * N_EXPERTS]
    for e in range(1, N_EXPERTS):
        longest = jnp.maximum(longest, meta_ref[1, i * N_EXPERTS + e])

    @pl.when(longest > half)
    def _():
        gather(half, _segment_rows(tile) - half)

    h = f_ref[...]

    def head(h, p):
        h = _ple(h, p, gp_ref[...], wpg_ref[...], bpg_ref[...], wp_ref[...])
        return _rms(h, gfin_ref[...])

    @pl.when(i < n_prompt_tiles)
    def _():
        outp_ref[...] = head(h, pp_ref[0])

    @pl.when(i == n_prompt_tiles)
    def _():
        n_s = outs_ref.shape[0]
        outs_ref[...] = head(h[:n_s], ps_ref[0])


def _combine(h, gates, lr, pp, ps, gp, wpg, bpg, wp, gfin, ys, meta, tile, n_prompt):
    t = h.shape[0]
    n_pt = n_prompt // tile
    n_s = ps.shape[1]
    row = lambda i, *_: (i, 0)
    prow = lambda i, *_: (jnp.minimum(i, n_pt - 1), 0)
    consts = [gp, wpg, bpg, wp, gfin]
    return pl.pallas_call(
        functools.partial(_combine_kernel, n_prompt_tiles=n_pt),
        grid_spec=pltpu.PrefetchScalarGridSpec(
            num_scalar_prefetch=1,
            grid=(t // tile,),
            in_specs=[pl.BlockSpec((tile, D_MODEL), row), pl.BlockSpec((tile, N_EXPERTS), row),
                      pl.BlockSpec((tile, N_EXPERTS), row),
                      pl.BlockSpec((1, tile, PLE_DIM), lambda i, *_: (1,) + prow(i)),
                      pl.BlockSpec((1, n_s, PLE_DIM), lambda i, *_: (1, 0, 0))]
            + [_const_spec(c.shape) for c in consts] + [pl.BlockSpec(memory_space=pl.ANY)],
            out_specs=[pl.BlockSpec((tile, D_MODEL), prow),
                       pl.BlockSpec((n_s, D_MODEL), lambda i, *_: (0, 0))],
            scratch_shapes=[pltpu.VMEM((2, N_EXPERTS, _segment_rows(tile), D_MODEL), BF16),
                            pltpu.VMEM((tile, D_MODEL), F32),
                            pltpu.SemaphoreType.DMA((2, N_EXPERTS))]),
        out_shape=[jax.ShapeDtypeStruct((n_prompt, D_MODEL), F32),
                   jax.ShapeDtypeStruct((n_s, D_MODEL), F32)],
        compiler_params=_cparams(("arbitrary",)),
        name="combine",
    )(meta, h, gates, lr, pp, ps, *consts, ys)


def _round_up(x, m):
    return (x + m - 1) // m * m


def _route_plan(cnt, t, tile, tmd):
    n_tiles = t // tile
    cnt = cnt.reshape(n_tiles, N_EXPERTS)
    before = jnp.cumsum(cnt, axis=0) - cnt
    shared = before % ROW_ALIGN
    seg_len = shared + cnt
    total = _round_up(cnt.sum(axis=0), ROW_ALIGN)
    guard = _segment_rows(tile) - tile // 2
    region = _round_up(total + guard, tmd)
    region_end = jnp.cumsum(region)
    off = region_end - region
    pos = off[None, :] + before - shared
    rows = _round_up(2 * t + N_EXPERTS * (ROW_ALIGN + tmd + guard), tmd)
    tile_start = jnp.arange(rows // tmd, dtype=jnp.int32) * tmd
    te = jnp.sum(region_end[None, :] <= tile_start[:, None], axis=1).astype(jnp.int32)
    tec = jnp.minimum(te, N_EXPERTS - 1)
    tile_rows = jnp.where(te < N_EXPERTS, jnp.clip(total[tec] - (tile_start - off[tec]), 0, tmd), 0)
    last_e = jnp.max(jnp.where(tile_rows > 0, tec, 0))
    tile_e = jnp.where(tile_rows > 0, tec, last_e).astype(jnp.int32)
    meta = jnp.stack([pos.reshape(-1), seg_len.reshape(-1), shared.reshape(-1)]).astype(jnp.int32)
    tail = jnp.stack([jnp.append(off + total, region_end[-1]),
                      jnp.append(region - total, rows - region_end[-1])]).astype(jnp.int32)
    return dict(meta=meta, tail=tail, tile_e=tile_e, tile_rows=tile_rows.astype(jnp.int32), rows=rows)


def _moe_and_head(h, hn, gates, lr, lr_t, cnt, pp, ps, w, tile, tmd, n_prompt):
    plan = _route_plan(cnt, h.shape[0], tile, tmd)
    xs = _dispatch(hn, lr_t, plan["meta"], plan["tail"], plan["rows"], tile, tmd // EXPERT_ROW_STEPS)
    ys = _expert_ffn(xs, plan["tile_e"], plan["tile_rows"], w["exp_gate"], w["exp_up"], w["exp_down"], tmd)
    return _combine(h, gates, lr, pp, ps, w["norm_ple1"], w["ple_gate1"], w["b_ple_gate1"], w["ple1"],
                    w["norm_final"], ys, plan["meta"], tile, n_prompt)


def _rope_tables(pos):
    half = HEAD_DIM // 2
    inv = 1.0 / (ROPE_THETA ** (jnp.arange(half, dtype=F32) / half))
    ang = pos.astype(F32)[:, None] * inv[None, :]
    cos = jnp.tile(jnp.cos(ang), (1, LANES // half))
    sin = jnp.sin(ang)
    sin = jnp.tile(jnp.concatenate([-sin, sin], axis=-1), (1, LANES // HEAD_DIM))
    return cos, sin


def kernel(x_prompt, x_sample, cache_k, cache_v, state_pool, p_prompt, p_sample, norm_mix, norm_ffn, norm_ple, norm_final, w_qkv, b_qkv, w_o, b_o, sinks, w_pool, pool_scale, w_ff_gate, w_ff_up, w_ff_down, w_router, b_router, w_exp_gate, w_exp_up, w_exp_down, w_ple, w_ple_gate, b_ple_gate):
    batch, seq, _ = x_prompt.shape
    dec = x_sample.shape[0]
    row2 = lambda a: a.reshape(1, -1)
    w = dict(
        exp_gate=w_exp_gate[0], exp_up=w_exp_up[0], exp_down=w_exp_down[0],
        norm_ple1=row2(norm_ple[1]), ple_gate1=w_ple_gate[1].astype(BF16), b_ple_gate1=row2(b_ple_gate[1]),
        ple1=w_ple[1].astype(BF16), norm_final=row2(norm_final))
    wqkv = w_qkv[0].astype(BF16)
    l0 = [w_o[0].astype(BF16), row2(b_o[0]), row2(norm_ffn[0]), w_ff_gate[0].astype(BF16),
          w_ff_up[0].astype(BF16), w_ff_down[0].astype(BF16), row2(norm_ple[0]),
          w_ple_gate[0].astype(BF16), row2(b_ple_gate[0]), w_ple[0].astype(BF16), row2(norm_mix[1])]
    wr_hi = w_router[0].astype(BF16)
    wr_lo = (w_router[0] - wr_hi.astype(F32)).astype(BF16)
    pool_w = [w_pool[0].astype(BF16), row2(pool_scale[0]), row2(norm_ffn[1]),
              jnp.concatenate([wr_hi, wr_lo], axis=1), row2(b_router[0])]

    tm, tile = 512, 512
    n_prompt = batch * seq
    t_pad = n_prompt + 2 * tm

    xp = x_prompt.reshape(n_prompt, D_MODEL)
    pp = p_prompt.reshape(p_prompt.shape[0], n_prompt, PLE_DIM)
    cos_p, sin_p = _rope_tables(jnp.arange(seq, dtype=jnp.int32))
    q, k, v = _qkv_rope(xp, row2(norm_mix[0]), wqkv, row2(b_qkv[0]), cos_p, sin_p, tm=2 * tm)
    o = _prompt_attention(q, k, v, sinks[0], batch, seq)
    h1, hn1 = _layer0_tail(xp, o, pp, *l0, tm=tm)
    *padded, last = _pool_prompt(h1, hn1, *pool_w, batch=batch, seq=seq, tm=2 * tm, tile=tile, t_pad=t_pad)
    kv_shape = (1, batch, WINDOW, N_KV_HEADS, HEAD_DIM)
    new_k_prompt = k.reshape(batch, seq, KV_DIM)[:, -WINDOW:].reshape(kv_shape)
    new_v_prompt = v.reshape(batch, seq, KV_DIM)[:, -WINDOW:].reshape(kv_shape)
    new_pool_prompt = last[:, -POOL_HIST:][None]

    xs_ = x_sample.reshape(dec, D_MODEL)
    ps = p_sample.reshape(p_sample.shape[0], dec, PLE_DIM)
    cos_s, sin_s = _rope_tables(jnp.full((dec,), PAST_LEN, jnp.int32))
    qs, ks, vs = _qkv_rope(xs_, row2(norm_mix[0]), wqkv, row2(b_qkv[0]), cos_s, sin_s, tm=dec)
    n_hist = cache_k.shape[2]
    k_win = jnp.concatenate([cache_k[0].reshape(dec, n_hist, KV_DIM), ks[:, None]], axis=1)[:, -n_hist:]
    v_win = jnp.concatenate([cache_v[0].reshape(dec, n_hist, KV_DIM), vs[:, None]], axis=1)[:, -n_hist:]
    head_kv = jnp.arange(N_HEADS) // GROUP
    blk = (head_kv[:, None] == jnp.arange(N_KV_HEADS)[None, :]).astype(BF16)
    q_blk = (qs.reshape(dec, N_HEADS, 1, HEAD_DIM) * blk[None, :, :, None]).reshape(dec, N_HEADS, KV_DIM)
    o_blk = _sample_attention(q_blk, k_win, v_win, sinks[0])
    o_s = jnp.take_along_axis(o_blk.reshape(dec, N_HEADS, N_KV_HEADS, HEAD_DIM),
                              head_kv[None, :, None, None], axis=2).reshape(dec, N_HEADS * HEAD_DIM)
    h1s, hn1s = _layer0_tail(xs_, o_s.astype(BF16), ps, *l0, tm=dec)
    hist_t = jnp.swapaxes(state_pool[0], 0, 1)
    padded = _pool_sample(h1s, hn1s, hist_t, *pool_w, padded=padded, row0=n_prompt, tile=tile)
    new_k_sample = k_win.reshape(1, dec, n_hist, N_KV_HEADS, HEAD_DIM)
    new_v_sample = v_win.reshape(1, dec, n_hist, N_KV_HEADS, HEAD_DIM)
    new_pool_sample = jnp.concatenate([state_pool[0], hn1s[:, None]], axis=1)[:, -POOL_HIST:][None]

    expected = TOP_K * (n_prompt + dec) / N_EXPERTS
    tiles_per_expert = max(1, round(expected / EXPERT_TILE_TARGET))
    tmd = _round_up(int(expected * (1 + EXPERT_TILE_SLACK) / tiles_per_expert), EXPERT_ROW_STEPS * ROW_ALIGN)
    y_prompt, y_sample = _moe_and_head(*padded, pp, ps, w, tile=tile, tmd=tmd, n_prompt=n_prompt)

    return (y_prompt.reshape(batch, seq, D_MODEL), y_sample.reshape(dec, 1, D_MODEL),
            new_k_prompt, new_v_prompt, new_pool_prompt, new_k_sample, new_v_sample, new_pool_sample)
```
